```python
import jax, jax.numpy as jnp
from jax import lax
import numpy as np

D_MODEL = 1024
BATCH = 8
SEQ = 4096
DEPTH = 4

HEAD_DIM = 64
HG_HEADS = 4
HG_KEY = 64
HG_VAL = 64
SB_HEADS = 6
FOX_HEADS = 6
HG_KW = HG_HEADS * HG_KEY
HG_WIDTH = HG_HEADS * HG_VAL
SB_WIDTH = SB_HEADS * HEAD_DIM
FOX_WIDTH = FOX_HEADS * HEAD_DIM
MIX_WIDTH = HG_WIDTH + SB_WIDTH + FOX_WIDTH
SPLIT_SIZES = (HG_KW, HG_KW, HG_WIDTH, HG_WIDTH, SB_WIDTH, SB_WIDTH, SB_WIDTH,
               FOX_WIDTH, FOX_WIDTH, FOX_WIDTH, FOX_HEADS)
IN_COLS = sum(SPLIT_SIZES)
D_FF = 4 * D_MODEL
HG_CHUNK = 64
BLOCK_Q = 128
EPS = 1e-6
LB_FLOOR = 1e-30
NEG_BIG = -1e30

kernel_name = "hybrid_hgrn2_stickbreak_fox_block"


def rms_norm(x, g):
    x32 = x.astype(jnp.float32)
    y = x32 * lax.rsqrt(jnp.mean(x32 * x32, axis=-1, keepdims=True) + EPS)
    return y * g.astype(jnp.float32)


def hgrn2_mixer(q, f_logit, i, g, lb, norm_g):
    B, T, _ = q.shape
    dt = i.dtype
    C, H, N = HG_CHUNK, HG_HEADS, T // HG_CHUNK
    lb = lb.astype(jnp.float32)
    fl = f_logit.astype(jnp.float32)
    log_f = jnp.logaddexp(jax.nn.log_sigmoid(fl),
                          jnp.log(jnp.maximum(lb, LB_FLOOR)) + jax.nn.log_sigmoid(-fl))
    k = (1.0 - lb) * jax.nn.sigmoid(-fl)

    def to_chunks(a, d):
        return a.astype(jnp.float32).reshape(B, N, C, H, d).transpose(1, 0, 3, 2, 4)

    xs = (to_chunks(q, HG_KEY), to_chunks(k, HG_KEY), to_chunks(i, HG_VAL), to_chunks(log_f, HG_KEY))
    causal = jnp.tril(jnp.ones((C, C), bool))[None, None, :, :, None]

    def step(S, inp):
        qc, kc, vc, lfc = inp
        b = jnp.cumsum(lfc, axis=2)
        diff = b[:, :, :, None, :] - b[:, :, None, :, :]
        decay = jnp.where(causal, jnp.exp(jnp.where(causal, diff, 0.0)), 0.0)
        A = jnp.einsum('bhtk,bhsk,bhtsk->bhts', qc, kc, decay)
        o = (jnp.einsum('bhtk,bhkv->bhtv', qc * jnp.exp(b), S)
             + jnp.einsum('bhts,bhsv->bhtv', A, vc))
        b_end = b[:, :, -1:, :]
        S_new = (jnp.exp(b_end[:, :, 0, :])[..., None] * S
                 + jnp.einsum('bhsk,bhsv->bhkv', kc * jnp.exp(b_end - b), vc))
        return S_new, o

    S0 = jnp.zeros((B, H, HG_KEY, HG_VAL), jnp.float32)
    _, o = lax.scan(step, S0, xs)
    o = o.transpose(1, 0, 3, 2, 4).reshape(B, T, H, HG_VAL)
    o = rms_norm(o, norm_g).reshape(B, T, HG_WIDTH).astype(dt)
    return o * jax.nn.silu(g)


def to_heads(a, n):
    B, T, _ = a.shape
    return a.reshape(B, T, n, HEAD_DIM).transpose(0, 2, 1, 3)


def from_blocks(o):
    nb, B, H, Q, D = o.shape
    return o.transpose(1, 0, 3, 2, 4).reshape(B, nb * Q, H * D)


def stick_breaking_attn(q, k, v):
    B, H, T, D = q.shape
    nb = T // BLOCK_Q
    scale = jnp.float32(1.0 / np.sqrt(HEAD_DIM))
    qb = q.reshape(B, H, nb, BLOCK_Q, D).transpose(2, 0, 1, 3, 4)
    kpos = jnp.arange(T)

    def block(args):
        qi, idx = args
        qpos = idx * BLOCK_Q + jnp.arange(BLOCK_Q)
        z = jnp.einsum('bhqd,bhsd->bhqs', qi, k) * scale
        mask = kpos[None, :] < qpos[:, None]
        log_1mb = jnp.where(mask, jax.nn.log_sigmoid(-z), 0.0)
        tail = lax.cumsum(log_1mb, axis=3, reverse=True) - log_1mb
        A = jnp.where(mask, jnp.exp(jnp.where(mask, jax.nn.log_sigmoid(z) + tail, 0.0)), 0.0)
        return jnp.einsum('bhqs,bhsd->bhqd', A.astype(v.dtype), v)

    return from_blocks(lax.map(block, (qb, jnp.arange(nb))))


def forgetting_attn(q, k, v, c):
    B, H, T, D = q.shape
    nb = T // BLOCK_Q
    scale = jnp.float32(1.0 / np.sqrt(HEAD_DIM))
    qb = q.reshape(B, H, nb, BLOCK_Q, D).transpose(2, 0, 1, 3, 4)
    cb = c.reshape(B, H, nb, BLOCK_Q).transpose(2, 0, 1, 3)
    kpos = jnp.arange(T)

    def block(args):
        qi, ci, idx = args
        qpos = idx * BLOCK_Q + jnp.arange(BLOCK_Q)
        mask = kpos[None, :] <= qpos[:, None]
        bias = jnp.where(mask, ci[..., :, None] - c[..., None, :], 0.0)
        s = jnp.einsum('bhqd,bhsd->bhqs', qi, k) * scale + bias
        p = jax.nn.softmax(jnp.where(mask, s, NEG_BIG), axis=-1)
        return jnp.einsum('bhqs,bhsd->bhqd', p.astype(v.dtype), v)

    return from_blocks(lax.map(block, (qb, cb, jnp.arange(nb))))


def setup_inputs(seed: int = 0) -> dict:
    key = jax.random.key(seed)
    ks = jax.random.split(key, 16)
    f32 = jnp.float32
    nrm = lambda k, shape, s: jax.random.normal(k, shape, f32) * s
    return {
        "x": jax.random.normal(ks[0], (BATCH, SEQ, D_MODEL), f32),
        "lb_logits": nrm(ks[1], (DEPTH, HG_KW), 0.1),
        "norm1_g": 1.0 + nrm(ks[2], (DEPTH, D_MODEL), 0.02),
        "w_in": nrm(ks[3], (DEPTH, D_MODEL, IN_COLS), D_MODEL ** -0.5),
        "hg_norm_g": 1.0 + nrm(ks[4], (DEPTH, HG_VAL), 0.02),
        "sb_q_norm_g": 1.0 + nrm(ks[5], (DEPTH, HEAD_DIM), 0.02),
        "sb_k_norm_g": 1.0 + nrm(ks[6], (DEPTH, HEAD_DIM), 0.02),
        "fox_q_norm_g": 1.0 + nrm(ks[7], (DEPTH, HEAD_DIM), 0.02),
        "fox_k_norm_g": 1.0 + nrm(ks[8], (DEPTH, HEAD_DIM), 0.02),
        "fox_f_bias": 3.0 + nrm(ks[9], (DEPTH, FOX_HEADS), 0.1),
        "w_out": nrm(ks[10], (DEPTH, MIX_WIDTH, D_MODEL), MIX_WIDTH ** -0.5),
        "norm2_g": 1.0 + nrm(ks[11], (DEPTH, D_MODEL), 0.02),
        "w_ff1": nrm(ks[12], (DEPTH, D_MODEL, D_FF), D_MODEL ** -0.5),
        "w_ff2": nrm(ks[13], (DEPTH, D_FF, D_MODEL), D_FF ** -0.5),
    }


def reference(x, lb_logits, norm1_g, w_in, hg_norm_g, sb_q_norm_g, sb_k_norm_g,
              fox_q_norm_g, fox_k_norm_g, fox_f_bias, w_out, norm2_g, w_ff1, w_ff2):
    dt = x.dtype
    p_lb = jax.nn.softmax(lb_logits.astype(jnp.float32), axis=0)
    lower_bounds = jnp.cumsum(p_lb, axis=0) - p_lb[0:1]
    offsets = [int(o) for o in np.cumsum(SPLIT_SIZES)[:-1]]

    for l in range(DEPTH):
        h = rms_norm(x, norm1_g[l]).astype(dt)
        proj = h @ w_in[l]
        (hq, hf, hi, hg, sq, sk, sv, fq, fk, fv, ff) = jnp.split(proj, offsets, axis=-1)

        o_hg = hgrn2_mixer(hq, hf, hi, hg, lower_bounds[l], hg_norm_g[l])

        o_sb = stick_breaking_attn(rms_norm(to_heads(sq, SB_HEADS), sb_q_norm_g[l]),
                                   rms_norm(to_heads(sk, SB_HEADS), sb_k_norm_g[l]),
                                   to_heads(sv, SB_HEADS)).astype(dt)

        log_fg = jax.nn.log_sigmoid(ff.astype(jnp.float32) + fox_f_bias[l].astype(jnp.float32))
        c = jnp.cumsum(log_fg, axis=1).transpose(0, 2, 1)
        o_fox = forgetting_attn(rms_norm(to_heads(fq, FOX_HEADS), fox_q_norm_g[l]),
                                rms_norm(to_heads(fk, FOX_HEADS), fox_k_norm_g[l]),
                                to_heads(fv, FOX_HEADS), c).astype(dt)

        mix = jnp.concatenate([o_hg, o_sb, o_fox], axis=-1)
        x = x + mix @ w_out[l]

        h2 = rms_norm(x, norm2_g[l]).astype(dt)
        x = x + jnp.square(jax.nn.relu(h2 @ w_ff1[l])) @ w_ff2[l]
    return x
```

```python
import functools
import math

import jax
import jax.numpy as jnp
from jax import lax
from jax.experimental import pallas as pl
from jax.experimental.pallas import tpu as pltpu

F32 = jnp.float32
BF16 = jnp.bfloat16

D_MODEL = 1024
DEPTH = 4
HEAD_DIM = 64
HG_HEADS = 4
HG_KW = 256
HG_WIDTH = 256
SB_WIDTH = 384
FOX_WIDTH = 384
FOX_HEADS = 6
IN_COLS = 4 * 256 + 3 * 384 + 3 * 384 + FOX_HEADS
D_FF = 4 * D_MODEL
EPS = 1e-6
LB_FLOOR = 1e-30
NEG_BIG = -1e30

LANES = 128
IN_COLS_PAD = 27 * LANES
SB_Q_BLK, SB_K_BLK, SB_V_BLK = 8, 11, 14
FOX_Q_BLK, FOX_K_BLK, FOX_V_BLK, FOX_F_BLK = 17, 20, 23, 26

SUB = 16
HG_ROWS = 128
VMEM_LIMIT = 56 * 1024 * 1024


def _nt_dot(a, b):
    return lax.dot_general(a, b, (((1,), (1,)), ((), ())), preferred_element_type=F32)


def _dot(a, b):
    return jnp.dot(a, b, preferred_element_type=F32)


def _split2(x):
    hi = x.astype(BF16)
    lo = (x - hi.astype(F32)).astype(BF16)
    return hi, lo


def _split3(x):
    hi = x.astype(BF16)
    r1 = x - hi.astype(F32)
    mid = r1.astype(BF16)
    lo = (r1 - mid.astype(F32)).astype(BF16)
    return hi, mid, lo


def _softplus(z):
    return jnp.maximum(z, 0.0) + jnp.log(1.0 + jnp.exp(-jnp.abs(z)))


def _proj_kernel(x_ref, g_ref, w_ref, o_ref):
    x = x_ref[...]
    ms = jnp.mean(x * x, axis=-1, keepdims=True)
    h = (x * lax.rsqrt(ms + EPS) * g_ref[...]).astype(BF16)
    o_ref[...] = _dot(h, w_ref[...])


def _proj_call(x, g, w, tm):
    m, d = x.shape
    n = w.shape[1]
    return pl.pallas_call(
        _proj_kernel,
        grid=(m // tm,),
        in_specs=[
            pl.BlockSpec((tm, d), lambda i: (i, 0)),
            pl.BlockSpec((1, d), lambda i: (0, 0)),
            pl.BlockSpec((d, n), lambda i: (0, 0)),
        ],
        out_specs=pl.BlockSpec((tm, n), lambda i: (i, 0)),
        out_shape=jax.ShapeDtypeStruct((m, n), F32),
        compiler_params=pltpu.CompilerParams(
            dimension_semantics=("arbitrary",), vmem_limit_bytes=VMEM_LIMIT),
        name="proj",
    )(x, g, w)


def _hgrn_kernel(lbl_ref, q_ref, f_ref, i_ref, g_ref, gn_ref, o_ref, st_scr, oi_scr, *, layer, tb):
    @pl.when(pl.program_id(1) == 0)
    def _():
        st_scr[...] = jnp.zeros_like(st_scr)

    rows = [lbl_ref[j:j + 1, :] for j in range(DEPTH)]
    mx = functools.reduce(jnp.maximum, rows)
    ex = [jnp.exp(r - mx) for r in rows]
    den = functools.reduce(lambda a, b: a + b, ex)
    lb = jnp.zeros_like(mx)
    for j in range(1, layer + 1):
        lb = lb + ex[j] / den
    log_lb = jnp.log(jnp.maximum(lb, LB_FLOOR))
    one_m_lb = 1.0 - lb
    gn = gn_ref[...]

    r_i = lax.broadcasted_iota(jnp.int32, (HG_KW, HG_KW), 0)
    c_i = lax.broadcasted_iota(jnp.int32, (HG_KW, HG_KW), 1)
    same_head = (r_i >> 6) == (c_i >> 6)
    ones_bd = jnp.where(same_head, 1.0, 0.0).astype(BF16)
    t_mod = lax.broadcasted_iota(jnp.int32, (HG_ROWS, HG_KW), 0) & (SUB - 1)
    lane_sub = lax.broadcasted_iota(jnp.int32, (HG_KW, HG_ROWS), 1) >> 4

    def chunk(c, carry):
        rs = pl.ds(pl.multiple_of(c * HG_ROWS, HG_ROWS), HG_ROWS)
        q = q_ref[rs, :]
        fl = f_ref[rs, :]
        v = i_ref[rs, :]
        g = g_ref[rs, :]

        sp = _softplus(fl)
        lf = jnp.maximum(fl, log_lb) + jnp.log(1.0 + jnp.exp(-jnp.abs(fl - log_lb))) - sp
        kk = one_m_lb * jnp.exp(-sp)

        b = lf
        suf = lf
        s = 1
        while s < SUB:
            b = b + jnp.where(t_mod >= s, pltpu.roll(b, s, 0), 0.0)
            suf = suf + jnp.where(t_mod < SUB - s, pltpu.roll(suf, HG_ROWS - s, 0), 0.0)
            s *= 2
        rest = suf - lf
        beta = b + rest

        acc = jnp.zeros((HG_ROWS, HG_WIDTH), F32)
        for d in range(SUB):
            if d == 0:
                p = q * kk
                vd = v
            else:
                kd = pltpu.roll(kk, d, 0)
                bd = pltpu.roll(b, d, 0)
                vd = pltpu.roll(v, d, 0)
                p = jnp.where(t_mod >= d, q * kd * jnp.exp(jnp.minimum(b - bd, 0.0)), 0.0)
            acc = acc + _dot(p.astype(BF16), ones_bd) * vd

        qt = (q * jnp.exp(b)).astype(BF16)
        kt = (kk * jnp.exp(rest)).astype(BF16)
        v_t = v.T
        for j in range(HG_ROWS // SUB):
            st = st_scr[...]
            oi_scr[j * SUB:(j + 1) * SUB, :] = _nt_dot(qt[j * SUB:(j + 1) * SUB, :], st.astype(BF16))
            vm = jnp.where(lane_sub == j, v_t, 0.0).astype(BF16)
            ut = _dot(vm, kt)
            dec = jnp.exp(beta[j * SUB:j * SUB + 1, :])
            st_scr[...] = st * dec + jnp.where(same_head, ut, 0.0)

        o = acc + oi_scr[...]
        hi, lo = _split2(o * o)
        msq = (_dot(hi, ones_bd) + _dot(lo, ones_bd)) * (1.0 / HEAD_DIM)
        y = o * lax.rsqrt(msq + EPS) * gn
        o_ref[rs, :] = (y * (g / (1.0 + jnp.exp(-g)))).astype(o_ref.dtype)
        return carry

    lax.fori_loop(0, tb // HG_ROWS, chunk, 0)


def _hgrn_call(layer, lb_logits, proj, gn, batch, seq, tb):
    m = proj.shape[0]
    nt = seq // tb

    def col(j):
        return pl.BlockSpec((tb, HG_KW), lambda b, t, j=j: (b * nt + t, j))

    return pl.pallas_call(
        functools.partial(_hgrn_kernel, layer=layer, tb=tb),
        grid=(batch, nt),
        in_specs=[pl.BlockSpec((DEPTH, HG_KW), lambda b, t: (0, 0)),
                  col(0), col(1), col(2), col(3),
                  pl.BlockSpec((1, HG_WIDTH), lambda b, t: (0, 0))],
        out_specs=pl.BlockSpec((tb, HG_WIDTH), lambda b, t: (b * nt + t, 0)),
        out_shape=jax.ShapeDtypeStruct((m, HG_WIDTH), BF16),
        scratch_shapes=[pltpu.VMEM((HG_WIDTH, HG_KW), F32),
                        pltpu.VMEM((HG_ROWS, HG_WIDTH), F32)],
        compiler_params=pltpu.CompilerParams(
            dimension_semantics=("arbitrary", "arbitrary"), vmem_limit_bytes=VMEM_LIMIT),
        name="hgrn2",
    )(lb_logits, proj, proj, proj, proj, gn)


def _pair_rms(x, gain, lane_lo):
    x2 = x * x
    s0 = jnp.sum(jnp.where(lane_lo, x2, 0.0), axis=1, keepdims=True)
    s1 = jnp.sum(jnp.where(lane_lo, 0.0, x2), axis=1, keepdims=True)
    ms = jnp.where(lane_lo, s0, s1) * (1.0 / HEAD_DIM)
    return x * lax.rsqrt(ms + EPS) * gain


PRO_ROWS = 512


def _sb_kernel(q_ref, k_ref, v_ref, gq_ref, gk_ref, o_ref, kn_scr, vb_scr, *, bq, seq):
    i = pl.program_id(2)

    @pl.when(i == 0)
    def _():
        lane_lo = lax.broadcasted_iota(jnp.int32, (PRO_ROWS, LANES), 1) < HEAD_DIM

        def body(c, carry):
            rs = pl.ds(pl.multiple_of(c * PRO_ROWS, PRO_ROWS), PRO_ROWS)
            kn_scr[rs, :] = _pair_rms(k_ref[rs, :], gk_ref[...], lane_lo).astype(BF16)
            vb_scr[rs, :] = v_ref[rs, :].astype(BF16)
            return carry

        lax.fori_loop(0, seq // PRO_ROWS, body, 0)

    lane_lo = lax.broadcasted_iota(jnp.int32, (bq, LANES), 1) < HEAD_DIM
    scale = 1.0 / math.sqrt(HEAD_DIM)
    qn = _pair_rms(q_ref[...], gq_ref[...], lane_lo) * scale
    r_i = lax.broadcasted_iota(jnp.int32, (bq, bq), 0)
    c_i = lax.broadcasted_iota(jnp.int32, (bq, bq), 1)
    before = c_i < r_i
    later = jnp.where(r_i > c_i, 1.0, 0.0).astype(BF16)

    def tile(qm, kb, carry, acc, mask):
        rs = pl.ds(pl.multiple_of(kb * bq, bq), bq)
        z = _nt_dot(qm, kn_scr[rs, :])
        sp = _softplus(z)
        l = -sp if mask is None else jnp.where(mask, -sp, 0.0)
        hi, lo = _split2(l)
        tail = _dot(hi, later) + _dot(lo, later) + carry
        a = jnp.exp(z - sp + tail)
        if mask is not None:
            a = jnp.where(mask, a, 0.0)
        acc = acc + _dot(a.astype(BF16), vb_scr[rs, :])
        carry = carry + jnp.sum(l, axis=1, keepdims=True)
        return carry, acc

    outs = []
    for h in range(2):
        qm = jnp.where(lane_lo if h == 0 else jnp.logical_not(lane_lo), qn, 0.0).astype(BF16)
        carry, acc = tile(qm, i, jnp.zeros((bq, 1), F32), jnp.zeros((bq, LANES), F32), before)

        def body(n, ca, qm=qm):
            return tile(qm, i - 1 - n, ca[0], ca[1], None)

        carry, acc = lax.fori_loop(0, i, body, (carry, acc))
        outs.append(acc)
    o_ref[...] = jnp.where(lane_lo, outs[0], outs[1]).astype(o_ref.dtype)


def _sb_call(proj, gq, gk, batch, seq, bq):
    m = proj.shape[0]
    nq = seq // bq
    return pl.pallas_call(
        functools.partial(_sb_kernel, bq=bq, seq=seq),
        grid=(batch, SB_WIDTH // LANES, nq),
        in_specs=[
            pl.BlockSpec((bq, LANES), lambda b, p, i: (b * nq + i, SB_Q_BLK + p)),
            pl.BlockSpec((seq, LANES), lambda b, p, i: (b, SB_K_BLK + p)),
            pl.BlockSpec((seq, LANES), lambda b, p, i: (b, SB_V_BLK + p)),
            pl.BlockSpec((1, LANES), lambda b, p, i: (0, 0)),
            pl.BlockSpec((1, LANES), lambda b, p, i: (0, 0)),
        ],
        out_specs=pl.BlockSpec((bq, LANES), lambda b, p, i: (b * nq + i, p)),
        out_shape=jax.ShapeDtypeStruct((m, SB_WIDTH), BF16),
        scratch_shapes=[pltpu.VMEM((seq, LANES), BF16), pltpu.VMEM((seq, LANES), BF16)],
        compiler_params=pltpu.CompilerParams(
            dimension_semantics=("arbitrary", "arbitrary", "arbitrary"), vmem_limit_bytes=VMEM_LIMIT),
        name="stickbreak",
    )(proj, proj, proj, gq, gk)


def _aug_lanes(lane, base, first, second):
    out = jnp.zeros(lane.shape, F32)
    for n in range(3):
        out = jnp.where(lane == base + n, first[n], out)
        out = jnp.where(lane == base + 3 + n, second[n], out)
    return out


def _fox_kernel(q_ref, k_ref, v_ref, f_ref, fb_ref, gq_ref, gk_ref, o_ref,
                c_scr, ka_scr, vb_scr, *, bq, seq):
    p_idx = pl.program_id(1)
    i = pl.program_id(2)
    one = jnp.ones((1, 1), F32)

    @pl.when(jnp.logical_and(p_idx == 0, i == 0))
    def _():
        r_i = lax.broadcasted_iota(jnp.int32, (bq, bq), 0)
        c_i = lax.broadcasted_iota(jnp.int32, (bq, bq), 1)
        upto = jnp.where(c_i <= r_i, 1.0, 0.0).astype(BF16)

        def body(n, run):
            rs = pl.ds(pl.multiple_of(n * bq, bq), bq)
            y = f_ref[rs, :] + fb_ref[...]
            lg = jnp.minimum(y, 0.0) - jnp.log(1.0 + jnp.exp(-jnp.abs(y)))
            hi, mid, lo = _split3(lg)
            c = _dot(upto, hi) + _dot(upto, mid) + _dot(upto, lo) + run
            c_scr[rs, :] = c
            return c[bq - 1:bq, :]

        lax.fori_loop(0, seq // bq, body, jnp.zeros((1, LANES), F32))

    @pl.when(i == 0)
    def _():
        lane = lax.broadcasted_iota(jnp.int32, (PRO_ROWS, LANES), 1)
        lane_lo = lane < HEAD_DIM

        def body(n, carry):
            rs = pl.ds(pl.multiple_of(n * PRO_ROWS, PRO_ROWS), PRO_ROWS)
            kn = _pair_rms(k_ref[rs, :], gk_ref[...], lane_lo)
            c = c_scr[rs, :]
            for h in range(2):
                ch = jnp.sum(jnp.where(lane == 2 * p_idx + h, c, 0.0), axis=1, keepdims=True)
                hi, mid, lo = _split3(ch)
                mine = lane_lo if h == 0 else jnp.logical_not(lane_lo)
                aug = _aug_lanes(lane, HEAD_DIM * (1 - h), (one, one, one),
                                 (-hi.astype(F32), -mid.astype(F32), -lo.astype(F32)))
                ka_scr[h, rs, :] = jnp.where(mine, kn, aug).astype(BF16)
            vb_scr[rs, :] = v_ref[rs, :].astype(BF16)
            return carry

        lax.fori_loop(0, seq // PRO_ROWS, body, 0)

    lane = lax.broadcasted_iota(jnp.int32, (bq, LANES), 1)
    lane_lo = lane < HEAD_DIM
    scale = 1.0 / math.sqrt(HEAD_DIM)
    qn = _pair_rms(q_ref[...], gq_ref[...], lane_lo) * scale
    cq = c_scr[pl.ds(pl.multiple_of(i * bq, bq), bq), :]
    r_i = lax.broadcasted_iota(jnp.int32, (bq, bq), 0)
    c_i = lax.broadcasted_iota(jnp.int32, (bq, bq), 1)
    visible = c_i <= r_i

    outs = []
    for h in range(2):
        ch = jnp.sum(jnp.where(lane == 2 * p_idx + h, cq, 0.0), axis=1, keepdims=True)
        hi, mid, lo = _split3(ch)
        mine = lane_lo if h == 0 else jnp.logical_not(lane_lo)
        aug = _aug_lanes(lane, HEAD_DIM * (1 - h),
                         (hi.astype(F32), mid.astype(F32), lo.astype(F32)), (one, one, one))
        qa = jnp.where(mine, qn, aug).astype(BF16)

        def tile(kb, m_run, l_run, acc, mask, h=h, qa=qa):
            rs = pl.ds(pl.multiple_of(kb * bq, bq), bq)
            s = _nt_dot(qa, ka_scr[h, rs, :])
            if mask is not None:
                s = jnp.where(mask, s, NEG_BIG)
            m_new = jnp.maximum(m_run, jnp.max(s, axis=1, keepdims=True))
            alpha = jnp.exp(m_run - m_new)
            p = jnp.exp(s - m_new)
            l_run = alpha * l_run + jnp.sum(p, axis=1, keepdims=True)
            acc = alpha * acc + _dot(p.astype(BF16), vb_scr[rs, :])
            return m_new, l_run, acc

        init = (jnp.full((bq, 1), NEG_BIG, F32), jnp.zeros((bq, 1), F32), jnp.zeros((bq, LANES), F32))
        m_run, l_run, acc = lax.fori_loop(
            0, i, lambda kb, ca, tile=tile: tile(kb, ca[0], ca[1], ca[2], None), init)
        m_run, l_run, acc = tile(i, m_run, l_run, acc, visible)
        outs.append(acc / l_run)
    o_ref[...] = jnp.where(lane_lo, outs[0], outs[1]).astype(o_ref.dtype)


def _fox_call(proj, fbias, gq, gk, batch, seq, bq):
    m = proj.shape[0]
    nq = seq // bq
    return pl.pallas_call(
        functools.partial(_fox_kernel, bq=bq, seq=seq),
        grid=(batch, FOX_WIDTH // LANES, nq),
        in_specs=[
            pl.BlockSpec((bq, LANES), lambda b, p, i: (b * nq + i, FOX_Q_BLK + p)),
            pl.BlockSpec((seq, LANES), lambda b, p, i: (b, FOX_K_BLK + p)),
            pl.BlockSpec((seq, LANES), lambda b, p, i: (b, FOX_V_BLK + p)),
            pl.BlockSpec((seq, LANES), lambda b, p, i: (b, FOX_F_BLK)),
            pl.BlockSpec((1, LANES), lambda b, p, i: (0, 0)),
            pl.BlockSpec((1, LANES), lambda b, p, i: (0, 0)),
            pl.BlockSpec((1, LANES), lambda b, p, i: (0, 0)),
        ],
        out_specs=pl.BlockSpec((bq, LANES), lambda b, p, i: (b * nq + i, p)),
        out_shape=jax.ShapeDtypeStruct((m, FOX_WIDTH), BF16),
        scratch_shapes=[pltpu.VMEM((seq, LANES), F32),
                        pltpu.VMEM((2, seq, LANES), BF16),
                        pltpu.VMEM((seq, LANES), BF16)],
        compiler_params=pltpu.CompilerParams(
            dimension_semantics=("arbitrary", "arbitrary", "arbitrary"), vmem_limit_bytes=VMEM_LIMIT),
        name="forgetting",
    )(proj, proj, proj, proj, fbias, gq, gk)


def _mlp_kernel(x_ref, ohg_ref, osb_ref, ofx_ref, wo1_ref, wo2_ref, wo3_ref, g2_ref,
                w1_ref, w2_ref, o_ref, h2_scr):
    @pl.when(pl.program_id(1) == 0)
    def _():
        x1 = (x_ref[...] + _dot(ohg_ref[...], wo1_ref[...]) + _dot(osb_ref[...], wo2_ref[...])
              + _dot(ofx_ref[...], wo3_ref[...]))
        ms = jnp.mean(x1 * x1, axis=-1, keepdims=True)
        h2_scr[...] = (x1 * lax.rsqrt(ms + EPS) * g2_ref[...]).astype(BF16)
        o_ref[...] = x1

    a = _dot(h2_scr[...], w1_ref[...])
    a = jnp.square(jnp.maximum(a, 0.0)).astype(BF16)
    o_ref[...] += _dot(a, w2_ref[...])


def _mlp_call(x, ohg, osb, ofx, wo1, wo2, wo3, g2, w1, w2, tm, tf):
    m, d = x.shape
    ff = w1.shape[1]
    row = lambda w: pl.BlockSpec((tm, w), lambda i, f: (i, 0))
    whole = lambda a: pl.BlockSpec(a.shape, lambda i, f: (0, 0))
    return pl.pallas_call(
        _mlp_kernel,
        grid=(m // tm, ff // tf),
        in_specs=[row(d), row(HG_WIDTH), row(SB_WIDTH), row(FOX_WIDTH),
                  whole(wo1), whole(wo2), whole(wo3), whole(g2),
                  pl.BlockSpec((d, tf), lambda i, f: (0, f)),
                  pl.BlockSpec((tf, d), lambda i, f: (f, 0))],
        out_specs=row(d),
        out_shape=jax.ShapeDtypeStruct((m, d), F32),
        scratch_shapes=[pltpu.VMEM((tm, d), BF16)],
        compiler_params=pltpu.CompilerParams(
            dimension_semantics=("arbitrary", "arbitrary"), vmem_limit_bytes=VMEM_LIMIT),
        name="outproj_mlp",
    )(x, ohg, osb, ofx, wo1, wo2, wo3, g2, w1, w2)


def _tile_sizes(batch, seq):
    m = batch * seq
    return dict(
        proj_tm=min(512, m),
        hg_tb=min(512, seq),
        bq=min(256, seq),
        mlp_tm=min(1024, m),
        mlp_tf=512,
    )


def kernel(x, lb_logits, norm1_g, w_in, hg_norm_g, sb_q_norm_g, sb_k_norm_g, fox_q_norm_g,
           fox_k_norm_g, fox_f_bias, w_out, norm2_g, w_ff1, w_ff2):
    batch, seq, d = x.shape
    assert d == D_MODEL and seq % HG_ROWS == 0 and x.dtype == F32
    ts = _tile_sizes(batch, seq)
    assert seq % ts["bq"] == 0 and seq % ts["hg_tb"] == 0 and seq % PRO_ROWS == 0
    m = batch * seq
    xf = x.reshape(m, d)
    pair = lambda g: jnp.tile(g.astype(F32), 2)[None, :]

    for l in range(DEPTH):
        w_in_p = jnp.pad(w_in[l], ((0, 0), (0, IN_COLS_PAD - IN_COLS))).astype(BF16)
        proj = _proj_call(xf, norm1_g[l][None, :], w_in_p, ts["proj_tm"])

        o_hg = _hgrn_call(l, lb_logits.astype(F32), proj,
                          jnp.tile(hg_norm_g[l].astype(F32), HG_HEADS)[None, :], batch, seq, ts["hg_tb"])
        o_sb = _sb_call(proj, pair(sb_q_norm_g[l]), pair(sb_k_norm_g[l]), batch, seq, ts["bq"])
        fbias = jnp.pad(fox_f_bias[l].astype(F32), (0, LANES - FOX_HEADS))[None, :]
        o_fx = _fox_call(proj, fbias, pair(fox_q_norm_g[l]), pair(fox_k_norm_g[l]), batch, seq, ts["bq"])

        wo = w_out[l].astype(BF16)
        xf = _mlp_call(xf, o_hg, o_sb, o_fx,
                       wo[:HG_WIDTH], wo[HG_WIDTH:HG_WIDTH + SB_WIDTH], wo[HG_WIDTH + SB_WIDTH:],
                       norm2_g[l][None, :], w_ff1[l].astype(BF16), w_ff2[l].astype(BF16),
                       ts["mlp_tm"], ts["mlp_tf"])
    return xf.reshape(batch, seq, d)
```

```python
import functools
import math

import jax
import jax.numpy as jnp
from jax import lax
from jax.experimental import pallas as pl
from jax.experimental.pallas import tpu as pltpu

F32 = jnp.float32
BF16 = jnp.bfloat16

D_MODEL = 1024
DEPTH = 4
HEAD_DIM = 64
HG_HEADS = 4
HG_KW = 256
HG_WIDTH = 256
SB_WIDTH = 384
FOX_WIDTH = 384
FOX_HEADS = 6
IN_COLS = 4 * 256 + 3 * 384 + 3 * 384 + FOX_HEADS
D_FF = 4 * D_MODEL
EPS = 1e-6
LB_FLOOR = 1e-30
NEG_BIG = -1e30

LANES = 128
IN_COLS_PAD = 27 * LANES
SB_Q_BLK, SB_K_BLK, SB_V_BLK = 8, 11, 14
FOX_Q_BLK, FOX_K_BLK, FOX_V_BLK, FOX_F_BLK = 17, 20, 23, 26

SB_ZERO_LOG = -105.0
SUB = 16
HG_ROWS = 128
VMEM_LIMIT = 56 * 1024 * 1024


def _nt_dot(a, b):
    return lax.dot_general(a, b, (((1,), (1,)), ((), ())), preferred_element_type=F32)


def _dot(a, b):
    return jnp.dot(a, b, preferred_element_type=F32)


def _split2(x):
    hi = x.astype(BF16)
    lo = (x - hi.astype(F32)).astype(BF16)
    return hi, lo


def _split3(x):
    hi = x.astype(BF16)
    r1 = x - hi.astype(F32)
    mid = r1.astype(BF16)
    lo = (r1 - mid.astype(F32)).astype(BF16)
    return hi, mid, lo


def _softplus(z):
    return jnp.maximum(z, 0.0) + jnp.log(1.0 + jnp.exp(-jnp.abs(z)))


def _proj_kernel(x_ref, g_ref, w_ref, o_ref):
    x = x_ref[...]
    ms = jnp.mean(x * x, axis=-1, keepdims=True)
    h = (x * lax.rsqrt(ms + EPS) * g_ref[...]).astype(BF16)
    o_ref[...] = _dot(h, w_ref[...])


def _proj_call(x, g, w, tm):
    m, d = x.shape
    n = w.shape[1]
    return pl.pallas_call(
        _proj_kernel,
        grid=(m // tm,),
        in_specs=[
            pl.BlockSpec((tm, d), lambda i: (i, 0)),
            pl.BlockSpec((1, d), lambda i: (0, 0)),
            pl.BlockSpec((d, n), lambda i: (0, 0)),
        ],
        out_specs=pl.BlockSpec((tm, n), lambda i: (i, 0)),
        out_shape=jax.ShapeDtypeStruct((m, n), F32),
        compiler_params=pltpu.CompilerParams(
            dimension_semantics=("arbitrary",), vmem_limit_bytes=VMEM_LIMIT),
        name="proj",
    )(x, g, w)


def _hgrn_kernel(lbl_ref, q_ref, f_ref, i_ref, g_ref, gn_ref, o_ref, st_scr, oi_scr, *, layer, tb):
    @pl.when(pl.program_id(1) == 0)
    def _():
        st_scr[...] = jnp.zeros_like(st_scr)

    rows = [lbl_ref[j:j + 1, :] for j in range(DEPTH)]
    mx = functools.reduce(jnp.maximum, rows)
    ex = [jnp.exp(r - mx) for r in rows]
    den = functools.reduce(lambda a, b: a + b, ex)
    lb = jnp.zeros_like(mx)
    for j in range(1, layer + 1):
        lb = lb + ex[j] / den
    log_lb = jnp.log(jnp.maximum(lb, LB_FLOOR))
    one_m_lb = 1.0 - lb
    gn = gn_ref[...]

    r_i = lax.broadcasted_iota(jnp.int32, (HG_KW, HG_KW), 0)
    c_i = lax.broadcasted_iota(jnp.int32, (HG_KW, HG_KW), 1)
    same_head = (r_i >> 6) == (c_i >> 6)
    ones_bd = jnp.where(same_head, 1.0, 0.0).astype(BF16)
    t_mod = lax.broadcasted_iota(jnp.int32, (HG_ROWS, HG_KW), 0) & (SUB - 1)
    lane_sub = lax.broadcasted_iota(jnp.int32, (HG_KW, HG_ROWS), 1) >> 4

    def chunk(c, carry):
        rs = pl.ds(pl.multiple_of(c * HG_ROWS, HG_ROWS), HG_ROWS)
        q = q_ref[rs, :]
        fl = f_ref[rs, :]
        v = i_ref[rs, :]
        g = g_ref[rs, :]

        sp = _softplus(fl)
        lf = jnp.maximum(fl, log_lb) + jnp.log(1.0 + jnp.exp(-jnp.abs(fl - log_lb))) - sp
        kk = one_m_lb * jnp.exp(-sp)

        b = lf
        suf = lf
        s = 1
        while s < SUB:
            b = b + jnp.where(t_mod >= s, pltpu.roll(b, s, 0), 0.0)
            suf = suf + jnp.where(t_mod < SUB - s, pltpu.roll(suf, HG_ROWS - s, 0), 0.0)
            s *= 2
        rest = suf - lf
        beta = b + rest

        acc = jnp.zeros((HG_ROWS, HG_WIDTH), F32)
        for d in range(SUB):
            if d == 0:
                p = q * kk
                vd = v
            else:
                kd = pltpu.roll(kk, d, 0)
                bd = pltpu.roll(b, d, 0)
                vd = pltpu.roll(v, d, 0)
                p = jnp.where(t_mod >= d, q * kd * jnp.exp(jnp.minimum(b - bd, 0.0)), 0.0)
            acc = acc + _dot(p.astype(BF16), ones_bd) * vd

        qt = (q * jnp.exp(b)).astype(BF16)
        kt = (kk * jnp.exp(rest)).astype(BF16)
        v_t = v.T
        for j in range(HG_ROWS // SUB):
            st = st_scr[...]
            oi_scr[j * SUB:(j + 1) * SUB, :] = _nt_dot(qt[j * SUB:(j + 1) * SUB, :], st.astype(BF16))
            vm = jnp.where(lane_sub == j, v_t, 0.0).astype(BF16)
            ut = _dot(vm, kt)
            dec = jnp.exp(beta[j * SUB:j * SUB + 1, :])
            st_scr[...] = st * dec + jnp.where(same_head, ut, 0.0)

        o = acc + oi_scr[...]
        hi, lo = _split2(o * o)
        msq = (_dot(hi, ones_bd) + _dot(lo, ones_bd)) * (1.0 / HEAD_DIM)
        y = o * lax.rsqrt(msq + EPS) * gn
        o_ref[rs, :] = (y * (g / (1.0 + jnp.exp(-g)))).astype(o_ref.dtype)
        return carry

    lax.fori_loop(0, tb // HG_ROWS, chunk, 0)


def _hgrn_call(layer, lb_logits, proj, gn, batch, seq, tb):
    m = proj.shape[0]
    nt = seq // tb

    def col(j):
        return pl.BlockSpec((tb, HG_KW), lambda b, t, j=j: (b * nt + t, j))

    return pl.pallas_call(
        functools.partial(_hgrn_kernel, layer=layer, tb=tb),
        grid=(batch, nt),
        in_specs=[pl.BlockSpec((DEPTH, HG_KW), lambda b, t: (0, 0)),
                  col(0), col(1), col(2), col(3),
                  pl.BlockSpec((1, HG_WIDTH), lambda b, t: (0, 0))],
        out_specs=pl.BlockSpec((tb, HG_WIDTH), lambda b, t: (b * nt + t, 0)),
        out_shape=jax.ShapeDtypeStruct((m, HG_WIDTH), BF16),
        scratch_shapes=[pltpu.VMEM((HG_WIDTH, HG_KW), F32),
                        pltpu.VMEM((HG_ROWS, HG_WIDTH), F32)],
        compiler_params=pltpu.CompilerParams(
            dimension_semantics=("arbitrary", "arbitrary"), vmem_limit_bytes=VMEM_LIMIT),
        name="hgrn2",
    )(lb_logits, proj, proj, proj, proj, gn)


def _pair_rms(x, gain, lane_lo):
    x2 = x * x
    s0 = jnp.sum(jnp.where(lane_lo, x2, 0.0), axis=1, keepdims=True)
    s1 = jnp.sum(jnp.where(lane_lo, 0.0, x2), axis=1, keepdims=True)
    ms = jnp.where(lane_lo, s0, s1) * (1.0 / HEAD_DIM)
    return x * lax.rsqrt(ms + EPS) * gain


PRO_ROWS = 512


def _sb_kernel(q_ref, k_ref, v_ref, gq_ref, gk_ref, o_ref, kn_scr, vb_scr, *, bq, seq):
    i = pl.program_id(2)

    @pl.when(i == 0)
    def _():
        lane_lo = lax.broadcasted_iota(jnp.int32, (PRO_ROWS, LANES), 1) < HEAD_DIM

        def body(c, carry):
            rs = pl.ds(pl.multiple_of(c * PRO_ROWS, PRO_ROWS), PRO_ROWS)
            kn_scr[rs, :] = _pair_rms(k_ref[rs, :], gk_ref[...], lane_lo).astype(BF16)
            vb_scr[rs, :] = v_ref[rs, :].astype(BF16)
            return carry

        lax.fori_loop(0, seq // PRO_ROWS, body, 0)

    lane_lo = lax.broadcasted_iota(jnp.int32, (bq, LANES), 1) < HEAD_DIM
    scale = 1.0 / math.sqrt(HEAD_DIM)
    qn = _pair_rms(q_ref[...], gq_ref[...], lane_lo) * scale
    r_i = lax.broadcasted_iota(jnp.int32, (bq, bq), 0)
    c_i = lax.broadcasted_iota(jnp.int32, (bq, bq), 1)
    before = c_i < r_i
    later = jnp.where(r_i > c_i, 1.0, 0.0).astype(BF16)

    def tile(qm, kb, carry, acc, mask):
        rs = pl.ds(pl.multiple_of(kb * bq, bq), bq)
        z = _nt_dot(qm, kn_scr[rs, :])
        sp = _softplus(z)
        l = -sp if mask is None else jnp.where(mask, -sp, 0.0)
        hi, lo = _split2(l)
        tail = _dot(hi, later) + _dot(lo, later) + carry
        a = jnp.exp(z - sp + tail)
        if mask is not None:
            a = jnp.where(mask, a, 0.0)
        acc = acc + _dot(a.astype(BF16), vb_scr[rs, :])
        carry = carry + jnp.sum(l, axis=1, keepdims=True)
        return carry, acc

    qm0 = jnp.where(lane_lo, qn, 0.0).astype(BF16)
    qm1 = jnp.where(lane_lo, 0.0, qn).astype(BF16)
    zc = jnp.zeros((bq, 1), F32)
    za = jnp.zeros((bq, LANES), F32)
    c0, a0 = tile(qm0, i, zc, za, before)
    c1, a1 = tile(qm1, i, zc, za, before)

    def live(st):
        return jnp.logical_and(st[0] >= 0, st[1] > SB_ZERO_LOG)

    def step(st):
        kb, _, c0, a0, c1, a1 = st
        c0, a0 = tile(qm0, kb, c0, a0, None)
        c1, a1 = tile(qm1, kb, c1, a1, None)
        return kb - 1, jnp.max(jnp.maximum(c0, c1)), c0, a0, c1, a1

    st = lax.while_loop(live, step, (i - 1, jnp.max(jnp.maximum(c0, c1)), c0, a0, c1, a1))
    o_ref[...] = jnp.where(lane_lo, st[3], st[5]).astype(o_ref.dtype)


def _sb_call(proj, gq, gk, batch, seq, bq):
    m = proj.shape[0]
    nq = seq // bq
    return pl.pallas_call(
        functools.partial(_sb_kernel, bq=bq, seq=seq),
        grid=(batch, SB_WIDTH // LANES, nq),
        in_specs=[
            pl.BlockSpec((bq, LANES), lambda b, p, i: (b * nq + i, SB_Q_BLK + p)),
            pl.BlockSpec((seq, LANES), lambda b, p, i: (b, SB_K_BLK + p)),
            pl.BlockSpec((seq, LANES), lambda b, p, i: (b, SB_V_BLK + p)),
            pl.BlockSpec((1, LANES), lambda b, p, i: (0, 0)),
            pl.BlockSpec((1, LANES), lambda b, p, i: (0, 0)),
        ],
        out_specs=pl.BlockSpec((bq, LANES), lambda b, p, i: (b * nq + i, p)),
        out_shape=jax.ShapeDtypeStruct((m, SB_WIDTH), BF16),
        scratch_shapes=[pltpu.VMEM((seq, LANES), BF16), pltpu.VMEM((seq, LANES), BF16)],
        compiler_params=pltpu.CompilerParams(
            dimension_semantics=("arbitrary", "arbitrary", "arbitrary"), vmem_limit_bytes=VMEM_LIMIT),
        name="stickbreak",
    )(proj, proj, proj, gq, gk)


def _aug_lanes(lane, base, first, second):
    out = jnp.zeros(lane.shape, F32)
    for n in range(3):
        out = jnp.where(lane == base + n, first[n], out)
        out = jnp.where(lane == base + 3 + n, second[n], out)
    return out


def _fox_kernel(q_ref, k_ref, v_ref, f_ref, fb_ref, gq_ref, gk_ref, o_ref,
                c_scr, ka_scr, vb_scr, *, bq, seq):
    p_idx = pl.program_id(1)
    i = pl.program_id(2)
    one = jnp.ones((1, 1), F32)

    @pl.when(jnp.logical_and(p_idx == 0, i == 0))
    def _():
        r_i = lax.broadcasted_iota(jnp.int32, (bq, bq), 0)
        c_i = lax.broadcasted_iota(jnp.int32, (bq, bq), 1)
        upto = jnp.where(c_i <= r_i, 1.0, 0.0).astype(BF16)

        def body(n, run):
            rs = pl.ds(pl.multiple_of(n * bq, bq), bq)
            y = f_ref[rs, :] + fb_ref[...]
            lg = jnp.minimum(y, 0.0) - jnp.log(1.0 + jnp.exp(-jnp.abs(y)))
            hi, mid, lo = _split3(lg)
            c = _dot(upto, hi) + _dot(upto, mid) + _dot(upto, lo) + run
            c_scr[rs, :] = c
            return c[bq - 1:bq, :]

        lax.fori_loop(0, seq // bq, body, jnp.zeros((1, LANES), F32))

    @pl.when(i == 0)
    def _():
        lane = lax.broadcasted_iota(jnp.int32, (PRO_ROWS, LANES), 1)
        lane_lo = lane < HEAD_DIM

        def body(n, carry):
            rs = pl.ds(pl.multiple_of(n * PRO_ROWS, PRO_ROWS), PRO_ROWS)
            kn = _pair_rms(k_ref[rs, :], gk_ref[...], lane_lo)
            c = c_scr[rs, :]
            for h in range(2):
                ch = jnp.sum(jnp.where(lane == 2 * p_idx + h, c, 0.0), axis=1, keepdims=True)
                hi, mid, lo = _split3(ch)
                mine = lane_lo if h == 0 else jnp.logical_not(lane_lo)
                aug = _aug_lanes(lane, HEAD_DIM * (1 - h), (one, one, one),
                                 (-hi.astype(F32), -mid.astype(F32), -lo.astype(F32)))
                ka_scr[h, rs, :] = jnp.where(mine, kn, aug).astype(BF16)
            vb_scr[rs, :] = v_ref[rs, :].astype(BF16)
            return carry

        lax.fori_loop(0, seq // PRO_ROWS, body, 0)

    lane = lax.broadcasted_iota(jnp.int32, (bq, LANES), 1)
    lane_lo = lane < HEAD_DIM
    scale = 1.0 / math.sqrt(HEAD_DIM)
    qn = _pair_rms(q_ref[...], gq_ref[...], lane_lo) * scale
    cq = c_scr[pl.ds(pl.multiple_of(i * bq, bq), bq), :]
    r_i = lax.broadcasted_iota(jnp.int32, (bq, bq), 0)
    c_i = lax.broadcasted_iota(jnp.int32, (bq, bq), 1)
    visible = c_i <= r_i

    qas = []
    for h in range(2):
        ch = jnp.sum(jnp.where(lane == 2 * p_idx + h, cq, 0.0), axis=1, keepdims=True)
        hi, mid, lo = _split3(ch)
        mine = lane_lo if h == 0 else jnp.logical_not(lane_lo)
        aug = _aug_lanes(lane, HEAD_DIM * (1 - h),
                         (hi.astype(F32), mid.astype(F32), lo.astype(F32)), (one, one, one))
        qas.append(jnp.where(mine, qn, aug).astype(BF16))

    def tile(h, kb, st, mask):
        m_run, l_run, acc = st
        rs = pl.ds(pl.multiple_of(kb * bq, bq), bq)
        s = _nt_dot(qas[h], ka_scr[h, rs, :])
        if mask is not None:
            s = jnp.where(mask, s, NEG_BIG)
        m_new = jnp.maximum(m_run, jnp.max(s, axis=1, keepdims=True))
        alpha = jnp.exp(m_run - m_new)
        p = jnp.exp(s - m_new)
        l_run = alpha * l_run + jnp.sum(p, axis=1, keepdims=True)
        acc = alpha * acc + _dot(p.astype(BF16), vb_scr[rs, :])
        return m_new, l_run, acc

    init = (jnp.full((bq, 1), NEG_BIG, F32), jnp.zeros((bq, 1), F32), jnp.zeros((bq, LANES), F32))
    st0, st1 = lax.fori_loop(
        0, i, lambda kb, st: (tile(0, kb, st[0], None), tile(1, kb, st[1], None)), (init, init))
    st0 = tile(0, i, st0, visible)
    st1 = tile(1, i, st1, visible)
    o_ref[...] = jnp.where(lane_lo, st0[2] / st0[1], st1[2] / st1[1]).astype(o_ref.dtype)


def _fox_call(proj, fbias, gq, gk, batch, seq, bq):
    m = proj.shape[0]
    nq = seq // bq
    return pl.pallas_call(
        functools.partial(_fox_kernel, bq=bq, seq=seq),
        grid=(batch, FOX_WIDTH // LANES, nq),
        in_specs=[
            pl.BlockSpec((bq, LANES), lambda b, p, i: (b * nq + i, FOX_Q_BLK + p)),
            pl.BlockSpec((seq, LANES), lambda b, p, i: (b, FOX_K_BLK + p)),
            pl.BlockSpec((seq, LANES), lambda b, p, i: (b, FOX_V_BLK + p)),
            pl.BlockSpec((seq, LANES), lambda b, p, i: (b, FOX_F_BLK)),
            pl.BlockSpec((1, LANES), lambda b, p, i: (0, 0)),
            pl.BlockSpec((1, LANES), lambda b, p, i: (0, 0)),
            pl.BlockSpec((1, LANES), lambda b, p, i: (0, 0)),
        ],
        out_specs=pl.BlockSpec((bq, LANES), lambda b, p, i: (b * nq + i, p)),
        out_shape=jax.ShapeDtypeStruct((m, FOX_WIDTH), BF16),
        scratch_shapes=[pltpu.VMEM((seq, LANES), F32),
                        pltpu.VMEM((2, seq, LANES), BF16),
                        pltpu.VMEM((seq, LANES), BF16)],
        compiler_params=pltpu.CompilerParams(
            dimension_semantics=("arbitrary", "arbitrary", "arbitrary"), vmem_limit_bytes=VMEM_LIMIT),
        name="forgetting",
    )(proj, proj, proj, proj, fbias, gq, gk)


def _mlp_kernel(x_ref, ohg_ref, osb_ref, ofx_ref, wo1_ref, wo2_ref, wo3_ref, g2_ref,
                w1_ref, w2_ref, o_ref, h2_scr):
    @pl.when(pl.program_id(1) == 0)
    def _():
        x1 = (x_ref[...] + _dot(ohg_ref[...], wo1_ref[...]) + _dot(osb_ref[...], wo2_ref[...])
              + _dot(ofx_ref[...], wo3_ref[...]))
        ms = jnp.mean(x1 * x1, axis=-1, keepdims=True)
        h2_scr[...] = (x1 * lax.rsqrt(ms + EPS) * g2_ref[...]).astype(BF16)
        o_ref[...] = x1

    a = _dot(h2_scr[...], w1_ref[...])
    a = jnp.square(jnp.maximum(a, 0.0)).astype(BF16)
    o_ref[...] += _dot(a, w2_ref[...])


def _mlp_call(x, ohg, osb, ofx, wo1, wo2, wo3, g2, w1, w2, tm, tf):
    m, d = x.shape
    ff = w1.shape[1]
    row = lambda w: pl.BlockSpec((tm, w), lambda i, f: (i, 0))
    whole = lambda a: pl.BlockSpec(a.shape, lambda i, f: (0, 0))
    return pl.pallas_call(
        _mlp_kernel,
        grid=(m // tm, ff // tf),
        in_specs=[row(d), row(HG_WIDTH), row(SB_WIDTH), row(FOX_WIDTH),
                  whole(wo1), whole(wo2), whole(wo3), whole(g2),
                  pl.BlockSpec((d, tf), lambda i, f: (0, f)),
                  pl.BlockSpec((tf, d), lambda i, f: (f, 0))],
        out_specs=row(d),
        out_shape=jax.ShapeDtypeStruct((m, d), F32),
        scratch_shapes=[pltpu.VMEM((tm, d), BF16)],
        compiler_params=pltpu.CompilerParams(
            dimension_semantics=("arbitrary", "arbitrary"), vmem_limit_bytes=VMEM_LIMIT),
        name="outproj_mlp",
    )(x, ohg, osb, ofx, wo1, wo2, wo3, g2, w1, w2)


def _tile_sizes(batch, seq):
    m = batch * seq
    return dict(
        proj_tm=min(512, m),
        hg_tb=min(512, seq),
        sb_bq=min(256, seq),
        fox_bq=min(512, seq),
        mlp_tm=min(1024, m),
        mlp_tf=512,
    )


def kernel(x, lb_logits, norm1_g, w_in, hg_norm_g, sb_q_norm_g, sb_k_norm_g, fox_q_norm_g,
           fox_k_norm_g, fox_f_bias, w_out, norm2_g, w_ff1, w_ff2):
    batch, seq, d = x.shape
    assert d == D_MODEL and seq % HG_ROWS == 0 and x.dtype == F32
    ts = _tile_sizes(batch, seq)
    assert seq % ts["sb_bq"] == 0 and seq % ts["fox_bq"] == 0
    assert seq % ts["hg_tb"] == 0 and seq % PRO_ROWS == 0
    m = batch * seq
    xf = x.reshape(m, d)
    pair = lambda g: jnp.tile(g.astype(F32), 2)[None, :]

    for l in range(DEPTH):
        w_in_p = jnp.pad(w_in[l], ((0, 0), (0, IN_COLS_PAD - IN_COLS))).astype(BF16)
        proj = _proj_call(xf, norm1_g[l][None, :], w_in_p, ts["proj_tm"])

        o_hg = _hgrn_call(l, lb_logits.astype(F32), proj,
                          jnp.tile(hg_norm_g[l].astype(F32), HG_HEADS)[None, :], batch, seq, ts["hg_tb"])
        o_sb = _sb_call(proj, pair(sb_q_norm_g[l]), pair(sb_k_norm_g[l]), batch, seq, ts["sb_bq"])
        fbias = jnp.pad(fox_f_bias[l].astype(F32), (0, LANES - FOX_HEADS))[None, :]
        o_fx = _fox_call(proj, fbias, pair(fox_q_norm_g[l]), pair(fox_k_norm_g[l]), batch, seq, ts["fox_bq"])

        wo = w_out[l].astype(BF16)
        xf = _mlp_call(xf, o_hg, o_sb, o_fx,
                       wo[:HG_WIDTH], wo[HG_WIDTH:HG_WIDTH + SB_WIDTH], wo[HG_WIDTH + SB_WIDTH:],
                       norm2_g[l][None, :], w_ff1[l].astype(BF16), w_ff2[l].astype(BF16),
                       ts["mlp_tm"], ts["mlp_tf"])
    return xf.reshape(batch, seq, d)
```

```python
import functools
import math

import jax
import jax.numpy as jnp
from jax import lax
from jax.experimental import pallas as pl
from jax.experimental.pallas import tpu as pltpu

F32 = jnp.float32
BF16 = jnp.bfloat16

D_MODEL = 1024
DEPTH = 4
HEAD_DIM = 64
HG_HEADS = 4
HG_KW = 256
HG_WIDTH = 256
SB_WIDTH = 384
FOX_WIDTH = 384
FOX_HEADS = 6
IN_COLS = 4 * 256 + 3 * 384 + 3 * 384 + FOX_HEADS
D_FF = 4 * D_MODEL
EPS = 1e-6
LB_FLOOR = 1e-30
NEG_BIG = -1e30

LANES = 128
IN_COLS_PAD = 27 * LANES
SB_Q_BLK, SB_K_BLK, SB_V_BLK = 8, 11, 14
FOX_Q_BLK, FOX_K_BLK, FOX_V_BLK, FOX_F_BLK = 17, 20, 23, 26

SB_ZERO_LOG = -105.0
SUB = 16
HG_ROWS = 128
VMEM_LIMIT = 56 * 1024 * 1024


def _nt_dot(a, b):
    return lax.dot_general(a, b, (((1,), (1,)), ((), ())), preferred_element_type=F32)


def _dot(a, b):
    return jnp.dot(a, b, preferred_element_type=F32)


def _split2(x):
    hi = x.astype(BF16)
    lo = (x - hi.astype(F32)).astype(BF16)
    return hi, lo


def _split3(x):
    hi = x.astype(BF16)
    r1 = x - hi.astype(F32)
    mid = r1.astype(BF16)
    lo = (r1 - mid.astype(F32)).astype(BF16)
    return hi, mid, lo


def _softplus(z):
    return jnp.maximum(z, 0.0) + jnp.log(1.0 + jnp.exp(-jnp.abs(z)))


def _proj_kernel(x_ref, g_ref, w_ref, o_ref):
    x = x_ref[...]
    ms = jnp.mean(x * x, axis=-1, keepdims=True)
    h = (x * lax.rsqrt(ms + EPS) * g_ref[...]).astype(BF16)
    o_ref[...] = _dot(h, w_ref[...])


def _proj_call(x, g, w, tm):
    m, d = x.shape
    n = w.shape[1]
    return pl.pallas_call(
        _proj_kernel,
        grid=(m // tm,),
        in_specs=[
            pl.BlockSpec((tm, d), lambda i: (i, 0)),
            pl.BlockSpec((1, d), lambda i: (0, 0)),
            pl.BlockSpec((d, n), lambda i: (0, 0)),
        ],
        out_specs=pl.BlockSpec((tm, n), lambda i: (i, 0)),
        out_shape=jax.ShapeDtypeStruct((m, n), F32),
        compiler_params=pltpu.CompilerParams(
            dimension_semantics=("arbitrary",), vmem_limit_bytes=VMEM_LIMIT),
        name="proj",
    )(x, g, w)


def _hgrn_kernel(lbl_ref, q_ref, f_ref, i_ref, g_ref, gn_ref, o_ref, st_scr, oi_scr, *, layer, tb):
    @pl.when(pl.program_id(1) == 0)
    def _():
        st_scr[...] = jnp.zeros_like(st_scr)

    rows = [lbl_ref[j:j + 1, :] for j in range(DEPTH)]
    mx = functools.reduce(jnp.maximum, rows)
    ex = [jnp.exp(r - mx) for r in rows]
    den = functools.reduce(lambda a, b: a + b, ex)
    lb = jnp.zeros_like(mx)
    for j in range(1, layer + 1):
        lb = lb + ex[j] / den
    log_lb = jnp.log(jnp.maximum(lb, LB_FLOOR))
    one_m_lb = 1.0 - lb
    gn = gn_ref[...]

    r_i = lax.broadcasted_iota(jnp.int32, (HG_KW, HG_KW), 0)
    c_i = lax.broadcasted_iota(jnp.int32, (HG_KW, HG_KW), 1)
    same_head = (r_i >> 6) == (c_i >> 6)
    ones_bd = jnp.where(same_head, 1.0, 0.0).astype(BF16)
    t_mod = lax.broadcasted_iota(jnp.int32, (HG_ROWS, HG_KW), 0) & (SUB - 1)
    lane_sub = lax.broadcasted_iota(jnp.int32, (HG_KW, HG_ROWS), 1) >> 4

    def chunk(c, carry):
        rs = pl.ds(pl.multiple_of(c * HG_ROWS, HG_ROWS), HG_ROWS)
        q = q_ref[rs, :]
        fl = f_ref[rs, :]
        v = i_ref[rs, :]
        g = g_ref[rs, :]

        sp = _softplus(fl)
        lf = jnp.maximum(fl, log_lb) + jnp.log(1.0 + jnp.exp(-jnp.abs(fl - log_lb))) - sp
        kk = one_m_lb * jnp.exp(-sp)

        b = lf
        suf = lf
        s = 1
        while s < SUB:
            b = b + jnp.where(t_mod >= s, pltpu.roll(b, s, 0), 0.0)
            suf = suf + jnp.where(t_mod < SUB - s, pltpu.roll(suf, HG_ROWS - s, 0), 0.0)
            s *= 2
        rest = suf - lf
        beta = b + rest

        acc = jnp.zeros((HG_ROWS, HG_WIDTH), F32)
        for d in range(SUB):
            if d == 0:
                p = q * kk
                vd = v
            else:
                kd = pltpu.roll(kk, d, 0)
                bd = pltpu.roll(b, d, 0)
                vd = pltpu.roll(v, d, 0)
                p = jnp.where(t_mod >= d, q * kd * jnp.exp(jnp.minimum(b - bd, 0.0)), 0.0)
            acc = acc + _dot(p.astype(BF16), ones_bd) * vd

        qt = (q * jnp.exp(b)).astype(BF16)
        kt = (kk * jnp.exp(rest)).astype(BF16)
        v_t = v.T
        for j in range(HG_ROWS // SUB):
            st = st_scr[...]
            oi_scr[j * SUB:(j + 1) * SUB, :] = _nt_dot(qt[j * SUB:(j + 1) * SUB, :], st.astype(BF16))
            vm = jnp.where(lane_sub == j, v_t, 0.0).astype(BF16)
            ut = _dot(vm, kt)
            dec = jnp.exp(beta[j * SUB:j * SUB + 1, :])
            st_scr[...] = st * dec + jnp.where(same_head, ut, 0.0)

        o = acc + oi_scr[...]
        hi, lo = _split2(o * o)
        msq = (_dot(hi, ones_bd) + _dot(lo, ones_bd)) * (1.0 / HEAD_DIM)
        y = o * lax.rsqrt(msq + EPS) * gn
        o_ref[rs, :] = (y * (g / (1.0 + jnp.exp(-g)))).astype(o_ref.dtype)
        return carry

    lax.fori_loop(0, tb // HG_ROWS, chunk, 0)


def _hgrn_call(layer, lb_logits, proj, gn, batch, seq, tb):
    m = proj.shape[0]
    nt = seq // tb

    def col(j):
        return pl.BlockSpec((tb, HG_KW), lambda b, t, j=j: (b * nt + t, j))

    return pl.pallas_call(
        functools.partial(_hgrn_kernel, layer=layer, tb=tb),
        grid=(batch, nt),
        in_specs=[pl.BlockSpec((DEPTH, HG_KW), lambda b, t: (0, 0)),
                  col(0), col(1), col(2), col(3),
                  pl.BlockSpec((1, HG_WIDTH), lambda b, t: (0, 0))],
        out_specs=pl.BlockSpec((tb, HG_WIDTH), lambda b, t: (b * nt + t, 0)),
        out_shape=jax.ShapeDtypeStruct((m, HG_WIDTH), BF16),
        scratch_shapes=[pltpu.VMEM((HG_WIDTH, HG_KW), F32),
                        pltpu.VMEM((HG_ROWS, HG_WIDTH), F32)],
        compiler_params=pltpu.CompilerParams(
            dimension_semantics=("arbitrary", "arbitrary"), vmem_limit_bytes=VMEM_LIMIT),
        name="hgrn2",
    )(lb_logits, proj, proj, proj, proj, gn)


def _pair_rms(x, gain, lane_lo):
    x2 = x * x
    s0 = jnp.sum(jnp.where(lane_lo, x2, 0.0), axis=1, keepdims=True)
    s1 = jnp.sum(jnp.where(lane_lo, 0.0, x2), axis=1, keepdims=True)
    ms = jnp.where(lane_lo, s0, s1) * (1.0 / HEAD_DIM)
    return x * lax.rsqrt(ms + EPS) * gain


PRO_ROWS = 512


def _sb_kernel(q0_ref, q1_ref, q2_ref, k0_ref, k1_ref, k2_ref, v0_ref, v1_ref, v2_ref,
               gq_ref, gk_ref, o_ref, kn_scr, vb_scr, acc_scr, car_scr, *, bq, seq):
    i = pl.program_id(1)
    q_refs = (q0_ref, q1_ref, q2_ref)
    k_refs = (k0_ref, k1_ref, k2_ref)
    v_refs = (v0_ref, v1_ref, v2_ref)
    n_pairs = len(q_refs)

    @pl.when(i == 0)
    def _():
        lane_lo = lax.broadcasted_iota(jnp.int32, (PRO_ROWS, LANES), 1) < HEAD_DIM

        def body(c, carry):
            rs = pl.ds(pl.multiple_of(c * PRO_ROWS, PRO_ROWS), PRO_ROWS)
            for p in range(n_pairs):
                kn_scr[p, rs, :] = _pair_rms(k_refs[p][rs, :], gk_ref[...], lane_lo).astype(BF16)
                vb_scr[p, rs, :] = v_refs[p][rs, :].astype(BF16)
            return carry

        lax.fori_loop(0, seq // PRO_ROWS, body, 0)

    lane_lo = lax.broadcasted_iota(jnp.int32, (bq, LANES), 1) < HEAD_DIM
    scale = 1.0 / math.sqrt(HEAD_DIM)
    r_i = lax.broadcasted_iota(jnp.int32, (bq, bq), 0)
    c_i = lax.broadcasted_iota(jnp.int32, (bq, bq), 1)
    before = c_i < r_i
    later = jnp.where(r_i > c_i, 1.0, 0.0).astype(BF16)

    qms = []
    for p in range(n_pairs):
        qn = _pair_rms(q_refs[p][...], gq_ref[...], lane_lo) * scale
        qms.append(jnp.where(lane_lo, qn, 0.0).astype(BF16))
        qms.append(jnp.where(lane_lo, 0.0, qn).astype(BF16))

    heads = range(2 * n_pairs)

    def block(kb, mask, first):
        rs = pl.ds(pl.multiple_of(kb * bq, bq), bq)
        zs = [_nt_dot(qms[h], kn_scr[h // 2, rs, :]) for h in heads]
        sps = [_softplus(z) for z in zs]
        ls = [-sp if mask is None else jnp.where(mask, -sp, 0.0) for sp in sps]
        tails = [_dot(l.astype(BF16), later) for l in ls]
        top = None
        for h in heads:
            row = jnp.broadcast_to(jnp.sum(ls[h], axis=1, keepdims=True), (bq, LANES))
            e = zs[h] - sps[h] + tails[h]
            if not first:
                carry = car_scr[h]
                e = e + jnp.concatenate([carry] * (bq // LANES), axis=1)
                row = row + carry
            a = jnp.exp(e)
            if mask is not None:
                a = jnp.where(mask, a, 0.0)
            pv = _dot(a.astype(BF16), vb_scr[h // 2, rs, :])
            acc_scr[h] = pv if first else acc_scr[h] + pv
            car_scr[h] = row
            top = row if top is None else jnp.maximum(top, row)
        return jnp.max(top)

    top = block(i, before, True)

    def live(st):
        return jnp.logical_and(st[0] >= 0, st[1] > SB_ZERO_LOG)

    lax.while_loop(live, lambda st: (st[0] - 1, block(st[0], None, False)), (i - 1, top))
    for p in range(n_pairs):
        o_ref[:, p * LANES:(p + 1) * LANES] = jnp.where(
            lane_lo, acc_scr[2 * p], acc_scr[2 * p + 1]).astype(o_ref.dtype)


def _sb_call(proj, gq, gk, batch, seq, bq):
    m = proj.shape[0]
    nq = seq // bq
    n_pairs = SB_WIDTH // LANES
    q_spec = lambda p: pl.BlockSpec((bq, LANES), lambda b, i, p=p: (b * nq + i, SB_Q_BLK + p))
    k_spec = lambda blk, p: pl.BlockSpec((seq, LANES), lambda b, i, p=p: (b, blk + p))
    gain = pl.BlockSpec((1, LANES), lambda b, i: (0, 0))
    return pl.pallas_call(
        functools.partial(_sb_kernel, bq=bq, seq=seq),
        grid=(batch, nq),
        in_specs=([q_spec(p) for p in range(n_pairs)]
                  + [k_spec(SB_K_BLK, p) for p in range(n_pairs)]
                  + [k_spec(SB_V_BLK, p) for p in range(n_pairs)] + [gain, gain]),
        out_specs=pl.BlockSpec((bq, SB_WIDTH), lambda b, i: (b * nq + i, 0)),
        out_shape=jax.ShapeDtypeStruct((m, SB_WIDTH), BF16),
        scratch_shapes=[pltpu.VMEM((n_pairs, seq, LANES), BF16),
                        pltpu.VMEM((n_pairs, seq, LANES), BF16),
                        pltpu.VMEM((2 * n_pairs, bq, LANES), F32),
                        pltpu.VMEM((2 * n_pairs, bq, LANES), F32)],
        compiler_params=pltpu.CompilerParams(
            dimension_semantics=("arbitrary", "arbitrary"), vmem_limit_bytes=VMEM_LIMIT),
        name="stickbreak",
    )(*([proj] * (3 * n_pairs)), gq, gk)


def _aug_lanes(lane, base, first, second):
    out = jnp.zeros(lane.shape, F32)
    for n in range(3):
        out = jnp.where(lane == base + n, first[n], out)
        out = jnp.where(lane == base + 3 + n, second[n], out)
    return out


def _fox_kernel(q_ref, k_ref, v_ref, f_ref, fb_ref, gq_ref, gk_ref, o_ref,
                c_scr, ka_scr, vt_scr, *, bq, bk, seq):
    p_idx = pl.program_id(1)
    i = pl.program_id(2)
    one = jnp.ones((1, 1), F32)

    @pl.when(jnp.logical_and(p_idx == 0, i == 0))
    def _():
        r_i = lax.broadcasted_iota(jnp.int32, (bk, bk), 0)
        c_i = lax.broadcasted_iota(jnp.int32, (bk, bk), 1)
        upto = jnp.where(c_i <= r_i, 1.0, 0.0).astype(BF16)

        def body(n, run):
            rs = pl.ds(pl.multiple_of(n * bk, bk), bk)
            y = f_ref[rs, :] + fb_ref[...]
            lg = jnp.minimum(y, 0.0) - jnp.log(1.0 + jnp.exp(-jnp.abs(y)))
            hi, mid, lo = _split3(lg)
            c = _dot(upto, hi) + _dot(upto, mid) + _dot(upto, lo) + run
            c_scr[rs, :] = c
            return c[bk - 1:bk, :]

        lax.fori_loop(0, seq // bk, body, jnp.zeros((1, LANES), F32))

    @pl.when(i == 0)
    def _():
        lane = lax.broadcasted_iota(jnp.int32, (bk, LANES), 1)
        lane_lo = lane < HEAD_DIM

        def body(n, carry):
            rs = pl.ds(pl.multiple_of(n * bk, bk), bk)
            kn = _pair_rms(k_ref[rs, :], gk_ref[...], lane_lo)
            c = c_scr[rs, :]
            for h in range(2):
                ch = jnp.sum(jnp.where(lane == 2 * p_idx + h, c, 0.0), axis=1, keepdims=True)
                hi, mid, lo = _split3(ch)
                mine = lane_lo if h == 0 else jnp.logical_not(lane_lo)
                aug = _aug_lanes(lane, HEAD_DIM * (1 - h), (one, one, one),
                                 (-hi.astype(F32), -mid.astype(F32), -lo.astype(F32)))
                ka_scr[h, rs, :] = jnp.where(mine, kn, aug).astype(BF16)
            vt_scr[n] = v_ref[rs, :].T.astype(BF16)
            return carry

        lax.fori_loop(0, seq // bk, body, 0)

    lane = lax.broadcasted_iota(jnp.int32, (bq, LANES), 1)
    lane_lo = lane < HEAD_DIM
    scale = 1.0 / math.sqrt(HEAD_DIM)
    qn = _pair_rms(q_ref[...], gq_ref[...], lane_lo) * scale
    cq = c_scr[pl.ds(pl.multiple_of(i * bq, bq), bq), :]

    qas = []
    for h in range(2):
        ch = jnp.sum(jnp.where(lane == 2 * p_idx + h, cq, 0.0), axis=1, keepdims=True)
        hi, mid, lo = _split3(ch)
        mine = lane_lo if h == 0 else jnp.logical_not(lane_lo)
        aug = _aug_lanes(lane, HEAD_DIM * (1 - h),
                         (hi.astype(F32), mid.astype(F32), lo.astype(F32)), (one, one, one))
        qas.append(jnp.where(mine, qn, aug).astype(BF16))

    key_i = lax.broadcasted_iota(jnp.int32, (bk, bq), 0)
    qry_i = lax.broadcasted_iota(jnp.int32, (bk, bq), 1)

    def scores(h, kb, masked):
        rs = pl.ds(pl.multiple_of(kb * bk, bk), bk)
        s = _nt_dot(ka_scr[h, rs, :], qas[h])
        if masked:
            s = jnp.where(kb * bk + key_i <= i * bq + qry_i, s, NEG_BIG)
        return s

    def absorb(kb, s, st):
        m_run, l_run, acc = st
        m_new = jnp.maximum(m_run, jnp.max(s, axis=0, keepdims=True))
        alpha = jnp.exp(m_run - m_new)
        p = jnp.exp(s - m_new)
        l_run = alpha * l_run + jnp.sum(p, axis=0, keepdims=True)
        acc = alpha * acc + _dot(vt_scr[kb], p.astype(BF16))
        return m_new, l_run, acc

    last = (i * bq + bq - 1) // bk
    init = (jnp.full((1, bq), NEG_BIG, F32), jnp.zeros((1, bq), F32), jnp.zeros((LANES, bq), F32))

    def step(kb, ca):
        kc, s0, s1, st0, st1 = ca
        n0 = scores(0, kb, False)
        n1 = scores(1, kb, False)
        return kb, n0, n1, absorb(kc, s0, st0), absorb(kc, s1, st1)

    kc, s0, s1, st0, st1 = lax.fori_loop(
        0, last, step, (last, scores(0, last, True), scores(1, last, True), init, init))
    st0 = absorb(kc, s0, st0)
    st1 = absorb(kc, s1, st1)
    row_lo = lax.broadcasted_iota(jnp.int32, (LANES, bq), 0) < HEAD_DIM
    o_t = jnp.where(row_lo, st0[2] / st0[1], st1[2] / st1[1])
    o_ref[...] = o_t.T.astype(o_ref.dtype)


def _fox_call(proj, fbias, gq, gk, batch, seq, bq, bk):
    m = proj.shape[0]
    nq = seq // bq
    return pl.pallas_call(
        functools.partial(_fox_kernel, bq=bq, bk=bk, seq=seq),
        grid=(batch, FOX_WIDTH // LANES, nq),
        in_specs=[
            pl.BlockSpec((bq, LANES), lambda b, p, i: (b * nq + i, FOX_Q_BLK + p)),
            pl.BlockSpec((seq, LANES), lambda b, p, i: (b, FOX_K_BLK + p)),
            pl.BlockSpec((seq, LANES), lambda b, p, i: (b, FOX_V_BLK + p)),
            pl.BlockSpec((seq, LANES), lambda b, p, i: (b, FOX_F_BLK)),
            pl.BlockSpec((1, LANES), lambda b, p, i: (0, 0)),
            pl.BlockSpec((1, LANES), lambda b, p, i: (0, 0)),
            pl.BlockSpec((1, LANES), lambda b, p, i: (0, 0)),
        ],
        out_specs=pl.BlockSpec((bq, LANES), lambda b, p, i: (b * nq + i, p)),
        out_shape=jax.ShapeDtypeStruct((m, FOX_WIDTH), BF16),
        scratch_shapes=[pltpu.VMEM((seq, LANES), F32),
                        pltpu.VMEM((2, seq, LANES), BF16),
                        pltpu.VMEM((seq // bk, LANES, bk), BF16)],
        compiler_params=pltpu.CompilerParams(
            dimension_semantics=("arbitrary", "arbitrary", "arbitrary"), vmem_limit_bytes=VMEM_LIMIT),
        name="forgetting",
    )(proj, proj, proj, proj, fbias, gq, gk)


def _mlp_kernel(x_ref, ohg_ref, osb_ref, ofx_ref, wo1_ref, wo2_ref, wo3_ref, g2_ref,
                w1_ref, w2_ref, o_ref, h2_scr):
    @pl.when(pl.program_id(1) == 0)
    def _():
        x1 = (x_ref[...] + _dot(ohg_ref[...], wo1_ref[...]) + _dot(osb_ref[...], wo2_ref[...])
              + _dot(ofx_ref[...], wo3_ref[...]))
        ms = jnp.mean(x1 * x1, axis=-1, keepdims=True)
        h2_scr[...] = (x1 * lax.rsqrt(ms + EPS) * g2_ref[...]).astype(BF16)
        o_ref[...] = x1

    a = _dot(h2_scr[...], w1_ref[...])
    a = jnp.square(jnp.maximum(a, 0.0)).astype(BF16)
    o_ref[...] += _dot(a, w2_ref[...])


def _mlp_call(x, ohg, osb, ofx, wo1, wo2, wo3, g2, w1, w2, tm, tf):
    m, d = x.shape
    ff = w1.shape[1]
    row = lambda w: pl.BlockSpec((tm, w), lambda i, f: (i, 0))
    whole = lambda a: pl.BlockSpec(a.shape, lambda i, f: (0, 0))
    return pl.pallas_call(
        _mlp_kernel,
        grid=(m // tm, ff // tf),
        in_specs=[row(d), row(HG_WIDTH), row(SB_WIDTH), row(FOX_WIDTH),
                  whole(wo1), whole(wo2), whole(wo3), whole(g2),
                  pl.BlockSpec((d, tf), lambda i, f: (0, f)),
                  pl.BlockSpec((tf, d), lambda i, f: (f, 0))],
        out_specs=row(d),
        out_shape=jax.ShapeDtypeStruct((m, d), F32),
        scratch_shapes=[pltpu.VMEM((tm, d), BF16)],
        compiler_params=pltpu.CompilerParams(
            dimension_semantics=("arbitrary", "arbitrary"), vmem_limit_bytes=VMEM_LIMIT),
        name="outproj_mlp",
    )(x, ohg, osb, ofx, wo1, wo2, wo3, g2, w1, w2)


def _tile_sizes(batch, seq):
    m = batch * seq
    return dict(
        proj_tm=min(512, m),
        hg_tb=min(512, seq),
        sb_bq=min(256, seq),
        fox_bq=min(256, seq),
        fox_bk=min(512, seq),
        mlp_tm=min(1024, m),
        mlp_tf=512,
    )


def kernel(x, lb_logits, norm1_g, w_in, hg_norm_g, sb_q_norm_g, sb_k_norm_g, fox_q_norm_g,
           fox_k_norm_g, fox_f_bias, w_out, norm2_g, w_ff1, w_ff2):
    batch, seq, d = x.shape
    assert d == D_MODEL and seq % HG_ROWS == 0 and x.dtype == F32
    ts = _tile_sizes(batch, seq)
    assert seq % ts["sb_bq"] == 0 and seq % ts["fox_bk"] == 0 and ts["fox_bk"] % ts["fox_bq"] == 0
    assert seq % ts["hg_tb"] == 0 and seq % PRO_ROWS == 0
    m = batch * seq
    xf = x.reshape(m, d)
    pair = lambda g: jnp.tile(g.astype(F32), 2)[None, :]

    for l in range(DEPTH):
        w_in_p = jnp.pad(w_in[l], ((0, 0), (0, IN_COLS_PAD - IN_COLS))).astype(BF16)
        proj = _proj_call(xf, norm1_g[l][None, :], w_in_p, ts["proj_tm"])

        o_hg = _hgrn_call(l, lb_logits.astype(F32), proj,
                          jnp.tile(hg_norm_g[l].astype(F32), HG_HEADS)[None, :], batch, seq, ts["hg_tb"])
        o_sb = _sb_call(proj, pair(sb_q_norm_g[l]), pair(sb_k_norm_g[l]), batch, seq, ts["sb_bq"])
        fbias = jnp.pad(fox_f_bias[l].astype(F32), (0, LANES - FOX_HEADS))[None, :]
        o_fx = _fox_call(proj, fbias, pair(fox_q_norm_g[l]), pair(fox_k_norm_g[l]), batch, seq,
                         ts["fox_bq"], ts["fox_bk"])

        wo = w_out[l].astype(BF16)
        xf = _mlp_call(xf, o_hg, o_sb, o_fx,
                       wo[:HG_WIDTH], wo[HG_WIDTH:HG_WIDTH + SB_WIDTH], wo[HG_WIDTH + SB_WIDTH:],
                       norm2_g[l][None, :], w_ff1[l].astype(BF16), w_ff2[l].astype(BF16),
                       ts["mlp_tm"], ts["mlp_tf"])
    return xf.reshape(batch, seq, d)
```

```python
import functools
import math

import jax
import jax.numpy as jnp
from jax import lax
from jax.experimental import pallas as pl
from jax.experimental.pallas import tpu as pltpu

F32 = jnp.float32
BF16 = jnp.bfloat16

D_MODEL = 1024
DEPTH = 4
HEAD_DIM = 64
HG_HEADS = 4
HG_KW = 256
HG_WIDTH = 256
SB_WIDTH = 384
FOX_WIDTH = 384
FOX_HEADS = 6
IN_COLS = 4 * 256 + 3 * 384 + 3 * 384 + FOX_HEADS
D_FF = 4 * D_MODEL
EPS = 1e-6
LB_FLOOR = 1e-30
NEG_BIG = -1e30

LANES = 128
IN_COLS_PAD = 27 * LANES
SB_Q_BLK, SB_K_BLK, SB_V_BLK = 8, 11, 14
FOX_Q_BLK, FOX_K_BLK, FOX_V_BLK, FOX_F_BLK = 17, 20, 23, 26

SB_ZERO_LOG = -105.0
SUB = 16
HG_ROWS = 128
VMEM_LIMIT = 56 * 1024 * 1024


def _nt_dot(a, b):
    return lax.dot_general(a, b, (((1,), (1,)), ((), ())), preferred_element_type=F32)


def _dot(a, b):
    return jnp.dot(a, b, preferred_element_type=F32)


def _split2(x):
    hi = x.astype(BF16)
    lo = (x - hi.astype(F32)).astype(BF16)
    return hi, lo


def _split3(x):
    hi = x.astype(BF16)
    r1 = x - hi.astype(F32)
    mid = r1.astype(BF16)
    lo = (r1 - mid.astype(F32)).astype(BF16)
    return hi, mid, lo


def _softplus(z):
    return jnp.maximum(z, 0.0) + jnp.log(1.0 + jnp.exp(-jnp.abs(z)))


def _proj_kernel(x_ref, g_ref, w_ref, o_ref):
    x = x_ref[...]
    ms = jnp.mean(x * x, axis=-1, keepdims=True)
    h = (x * lax.rsqrt(ms + EPS) * g_ref[...]).astype(BF16)
    o_ref[...] = _dot(h, w_ref[...])


def _proj_call(x, g, w, tm):
    m, d = x.shape
    n = w.shape[1]
    return pl.pallas_call(
        _proj_kernel,
        grid=(m // tm,),
        in_specs=[
            pl.BlockSpec((tm, d), lambda i: (i, 0)),
            pl.BlockSpec((1, d), lambda i: (0, 0)),
            pl.BlockSpec((d, n), lambda i: (0, 0)),
        ],
        out_specs=pl.BlockSpec((tm, n), lambda i: (i, 0)),
        out_shape=jax.ShapeDtypeStruct((m, n), F32),
        compiler_params=pltpu.CompilerParams(
            dimension_semantics=("arbitrary",), vmem_limit_bytes=VMEM_LIMIT),
        name="proj",
    )(x, g, w)


def _hgrn_kernel(lbl_ref, q_ref, f_ref, i_ref, g_ref, gn_ref, o_ref, st_scr, oi_scr, *, layer, tb):
    @pl.when(pl.program_id(1) == 0)
    def _():
        st_scr[...] = jnp.zeros_like(st_scr)

    rows = [lbl_ref[j:j + 1, :] for j in range(DEPTH)]
    mx = functools.reduce(jnp.maximum, rows)
    ex = [jnp.exp(r - mx) for r in rows]
    den = functools.reduce(lambda a, b: a + b, ex)
    lb = jnp.zeros_like(mx)
    for j in range(1, layer + 1):
        lb = lb + ex[j] / den
    log_lb = jnp.log(jnp.maximum(lb, LB_FLOOR))
    one_m_lb = 1.0 - lb
    gn = gn_ref[...]

    r_i = lax.broadcasted_iota(jnp.int32, (HG_KW, HG_KW), 0)
    c_i = lax.broadcasted_iota(jnp.int32, (HG_KW, HG_KW), 1)
    same_head = (r_i >> 6) == (c_i >> 6)
    ones_bd = jnp.where(same_head, 1.0, 0.0).astype(BF16)
    t_mod = lax.broadcasted_iota(jnp.int32, (HG_ROWS, HG_KW), 0) & (SUB - 1)
    lane_sub = lax.broadcasted_iota(jnp.int32, (HG_KW, HG_ROWS), 1) >> 4

    def chunk(c, carry):
        rs = pl.ds(pl.multiple_of(c * HG_ROWS, HG_ROWS), HG_ROWS)
        q = q_ref[rs, :]
        fl = f_ref[rs, :]
        v = i_ref[rs, :]
        g = g_ref[rs, :]

        sp = _softplus(fl)
        lf = jnp.maximum(fl, log_lb) + jnp.log(1.0 + jnp.exp(-jnp.abs(fl - log_lb))) - sp
        kk = one_m_lb * jnp.exp(-sp)

        b = lf
        suf = lf
        s = 1
        while s < SUB:
            b = b + jnp.where(t_mod >= s, pltpu.roll(b, s, 0), 0.0)
            suf = suf + jnp.where(t_mod < SUB - s, pltpu.roll(suf, HG_ROWS - s, 0), 0.0)
            s *= 2
        rest = suf - lf
        beta = b + rest

        acc = jnp.zeros((HG_ROWS, HG_WIDTH), F32)
        for d in range(SUB):
            if d == 0:
                p = q * kk
                vd = v
            else:
                kd = pltpu.roll(kk, d, 0)
                bd = pltpu.roll(b, d, 0)
                vd = pltpu.roll(v, d, 0)
                p = jnp.where(t_mod >= d, q * kd * jnp.exp(jnp.minimum(b - bd, 0.0)), 0.0)
            acc = acc + _dot(p.astype(BF16), ones_bd) * vd

        qt = (q * jnp.exp(b)).astype(BF16)
        kt = (kk * jnp.exp(rest)).astype(BF16)
        v_t = v.T
        for j in range(HG_ROWS // SUB):
            st = st_scr[...]
            oi_scr[j * SUB:(j + 1) * SUB, :] = _nt_dot(qt[j * SUB:(j + 1) * SUB, :], st.astype(BF16))
            vm = jnp.where(lane_sub == j, v_t, 0.0).astype(BF16)
            ut = _dot(vm, kt)
            dec = jnp.exp(beta[j * SUB:j * SUB + 1, :])
            st_scr[...] = st * dec + jnp.where(same_head, ut, 0.0)

        o = acc + oi_scr[...]
        hi, lo = _split2(o * o)
        msq = (_dot(hi, ones_bd) + _dot(lo, ones_bd)) * (1.0 / HEAD_DIM)
        y = o * lax.rsqrt(msq + EPS) * gn
        o_ref[rs, :] = (y * (g / (1.0 + jnp.exp(-g)))).astype(o_ref.dtype)
        return carry

    lax.fori_loop(0, tb // HG_ROWS, chunk, 0)


def _hgrn_call(layer, lb_logits, proj, gn, batch, seq, tb):
    m = proj.shape[0]
    nt = seq // tb

    def col(j):
        return pl.BlockSpec((tb, HG_KW), lambda b, t, j=j: (b * nt + t, j))

    return pl.pallas_call(
        functools.partial(_hgrn_kernel, layer=layer, tb=tb),
        grid=(batch, nt),
        in_specs=[pl.BlockSpec((DEPTH, HG_KW), lambda b, t: (0, 0)),
                  col(0), col(1), col(2), col(3),
                  pl.BlockSpec((1, HG_WIDTH), lambda b, t: (0, 0))],
        out_specs=pl.BlockSpec((tb, HG_WIDTH), lambda b, t: (b * nt + t, 0)),
        out_shape=jax.ShapeDtypeStruct((m, HG_WIDTH), BF16),
        scratch_shapes=[pltpu.VMEM((HG_WIDTH, HG_KW), F32),
                        pltpu.VMEM((HG_ROWS, HG_WIDTH), F32)],
        compiler_params=pltpu.CompilerParams(
            dimension_semantics=("arbitrary", "arbitrary"), vmem_limit_bytes=VMEM_LIMIT),
        name="hgrn2",
    )(lb_logits, proj, proj, proj, proj, gn)


def _pair_rms(x, gain, lane_lo):
    x2 = x * x
    s0 = jnp.sum(jnp.where(lane_lo, x2, 0.0), axis=1, keepdims=True)
    s1 = jnp.sum(jnp.where(lane_lo, 0.0, x2), axis=1, keepdims=True)
    ms = jnp.where(lane_lo, s0, s1) * (1.0 / HEAD_DIM)
    return x * lax.rsqrt(ms + EPS) * gain


PRO_ROWS = 512


def _sb_kernel(q0_ref, q1_ref, q2_ref, k0_ref, k1_ref, k2_ref, v0_ref, v1_ref, v2_ref,
               gq_ref, gk_ref, o_ref, kn_scr, vb_scr, acc_scr, car_scr, *, bq, seq):
    i = pl.program_id(1)
    q_refs = (q0_ref, q1_ref, q2_ref)
    k_refs = (k0_ref, k1_ref, k2_ref)
    v_refs = (v0_ref, v1_ref, v2_ref)
    n_pairs = len(q_refs)

    @pl.when(i == 0)
    def _():
        lane_lo = lax.broadcasted_iota(jnp.int32, (PRO_ROWS, LANES), 1) < HEAD_DIM

        def body(c, carry):
            rs = pl.ds(pl.multiple_of(c * PRO_ROWS, PRO_ROWS), PRO_ROWS)
            for p in range(n_pairs):
                kn_scr[p, rs, :] = _pair_rms(k_refs[p][rs, :], gk_ref[...], lane_lo).astype(BF16)
                vb_scr[p, rs, :] = v_refs[p][rs, :].astype(BF16)
            return carry

        lax.fori_loop(0, seq // PRO_ROWS, body, 0)

    lane_lo = lax.broadcasted_iota(jnp.int32, (bq, LANES), 1) < HEAD_DIM
    scale = 1.0 / math.sqrt(HEAD_DIM)
    r_i = lax.broadcasted_iota(jnp.int32, (bq, bq), 0)
    c_i = lax.broadcasted_iota(jnp.int32, (bq, bq), 1)
    before = c_i < r_i
    later = jnp.where(r_i > c_i, 1.0, 0.0).astype(BF16)

    qms = []
    for p in range(n_pairs):
        qn = _pair_rms(q_refs[p][...], gq_ref[...], lane_lo) * scale
        qms.append(jnp.where(lane_lo, qn, 0.0).astype(BF16))
        qms.append(jnp.where(lane_lo, 0.0, qn).astype(BF16))

    heads = range(2 * n_pairs)

    def block(kb, mask, first):
        rs = pl.ds(pl.multiple_of(kb * bq, bq), bq)
        zs = [_nt_dot(qms[h], kn_scr[h // 2, rs, :]) for h in heads]
        sps = [_softplus(z) for z in zs]
        ls = [-sp if mask is None else jnp.where(mask, -sp, 0.0) for sp in sps]
        tails = [_dot(l.astype(BF16), later) for l in ls]
        top = None
        for h in heads:
            row = jnp.broadcast_to(jnp.sum(ls[h], axis=1, keepdims=True), (bq, LANES))
            e = zs[h] - sps[h] + tails[h]
            if not first:
                carry = car_scr[h]
                e = e + jnp.concatenate([carry] * (bq // LANES), axis=1)
                row = row + carry
            a = jnp.exp(e)
            if mask is not None:
                a = jnp.where(mask, a, 0.0)
            pv = _dot(a.astype(BF16), vb_scr[h // 2, rs, :])
            acc_scr[h] = pv if first else acc_scr[h] + pv
            car_scr[h] = row
            top = row if top is None else jnp.maximum(top, row)
        return jnp.max(top)

    top = block(i, before, True)

    def live(st):
        return jnp.logical_and(st[0] >= 0, st[1] > SB_ZERO_LOG)

    lax.while_loop(live, lambda st: (st[0] - 1, block(st[0], None, False)), (i - 1, top))
    for p in range(n_pairs):
        o_ref[:, p * LANES:(p + 1) * LANES] = jnp.where(
            lane_lo, acc_scr[2 * p], acc_scr[2 * p + 1]).astype(o_ref.dtype)


def _sb_call(proj, gq, gk, batch, seq, bq):
    m = proj.shape[0]
    nq = seq // bq
    n_pairs = SB_WIDTH // LANES
    q_spec = lambda p: pl.BlockSpec((bq, LANES), lambda b, i, p=p: (b * nq + i, SB_Q_BLK + p))
    k_spec = lambda blk, p: pl.BlockSpec((seq, LANES), lambda b, i, p=p: (b, blk + p))
    gain = pl.BlockSpec((1, LANES), lambda b, i: (0, 0))
    return pl.pallas_call(
        functools.partial(_sb_kernel, bq=bq, seq=seq),
        grid=(batch, nq),
        in_specs=([q_spec(p) for p in range(n_pairs)]
                  + [k_spec(SB_K_BLK, p) for p in range(n_pairs)]
                  + [k_spec(SB_V_BLK, p) for p in range(n_pairs)] + [gain, gain]),
        out_specs=pl.BlockSpec((bq, SB_WIDTH), lambda b, i: (b * nq + i, 0)),
        out_shape=jax.ShapeDtypeStruct((m, SB_WIDTH), BF16),
        scratch_shapes=[pltpu.VMEM((n_pairs, seq, LANES), BF16),
                        pltpu.VMEM((n_pairs, seq, LANES), BF16),
                        pltpu.VMEM((2 * n_pairs, bq, LANES), F32),
                        pltpu.VMEM((2 * n_pairs, bq, LANES), F32)],
        compiler_params=pltpu.CompilerParams(
            dimension_semantics=("arbitrary", "arbitrary"), vmem_limit_bytes=VMEM_LIMIT),
        name="stickbreak",
    )(*([proj] * (3 * n_pairs)), gq, gk)


def _aug_lanes(lane, base, first, second):
    out = jnp.zeros(lane.shape, F32)
    for n in range(3):
        out = jnp.where(lane == base + n, first[n], out)
        out = jnp.where(lane == base + 3 + n, second[n], out)
    return out


def _fox_kernel(q_ref, k_ref, v_ref, f_ref, fb_ref, gq_ref, gk_ref, o_ref,
                c_scr, qa_scr, ka_scr, vt_scr, sa_scr, sb_scr, *, bq, bk, seq):
    p_idx = pl.program_id(1)
    one = jnp.ones((1, 1), F32)
    scale = 1.0 / math.sqrt(HEAD_DIM)

    @pl.when(p_idx == 0)
    def _():
        r_i = lax.broadcasted_iota(jnp.int32, (bk, bk), 0)
        c_i = lax.broadcasted_iota(jnp.int32, (bk, bk), 1)
        upto = jnp.where(c_i <= r_i, 1.0, 0.0).astype(BF16)

        def body(n, run):
            rs = pl.ds(pl.multiple_of(n * bk, bk), bk)
            y = f_ref[rs, :] + fb_ref[...]
            lg = jnp.minimum(y, 0.0) - jnp.log(1.0 + jnp.exp(-jnp.abs(y)))
            hi, mid, lo = _split3(lg)
            c = _dot(upto, hi) + _dot(upto, mid) + _dot(upto, lo) + run
            c_scr[rs, :] = c
            return c[bk - 1:bk, :]

        lax.fori_loop(0, seq // bk, body, jnp.zeros((1, LANES), F32))

    lane = lax.broadcasted_iota(jnp.int32, (bk, LANES), 1)
    lane_lo = lane < HEAD_DIM

    def prep(n, carry):
        rs = pl.ds(pl.multiple_of(n * bk, bk), bk)
        kn = _pair_rms(k_ref[rs, :], gk_ref[...], lane_lo)
        qn = _pair_rms(q_ref[rs, :], gq_ref[...], lane_lo) * scale
        c = c_scr[rs, :]
        for h in range(2):
            ch = jnp.sum(jnp.where(lane == 2 * p_idx + h, c, 0.0), axis=1, keepdims=True)
            parts = [x.astype(F32) for x in _split3(ch)]
            mine = lane_lo if h == 0 else jnp.logical_not(lane_lo)
            base = HEAD_DIM * (1 - h)
            k_aug = _aug_lanes(lane, base, (one, one, one), [-x for x in parts])
            q_aug = _aug_lanes(lane, base, parts, (one, one, one))
            ka_scr[h, rs, :] = jnp.where(mine, kn, k_aug).astype(BF16)
            qa_scr[h, rs, :] = jnp.where(mine, qn, q_aug).astype(BF16)
        vt_scr[n] = v_ref[rs, :].T.astype(BF16)
        return carry

    lax.fori_loop(0, seq // bk, prep, 0)

    key_i = lax.broadcasted_iota(jnp.int32, (bk, bq), 0)
    qry_i = lax.broadcasted_iota(jnp.int32, (bk, bq), 1)
    row_lo = lax.broadcasted_iota(jnp.int32, (LANES, bq), 0) < HEAD_DIM

    def q_block(qi, carry):
        qs = pl.ds(pl.multiple_of(qi * bq, bq), bq)
        qas = [qa_scr[h, qs, :] for h in range(2)]

        def scores_into(dst, kb, masked):
            rs = pl.ds(pl.multiple_of(kb * bk, bk), bk)
            for h in range(2):
                s = _nt_dot(ka_scr[h, rs, :], qas[h])
                if masked:
                    s = jnp.where(kb * bk + key_i <= qi * bq + qry_i, s, NEG_BIG)
                dst[h] = s

        def absorb(src, kb, sts):
            out = []
            for h in range(2):
                m_run, l_run, acc = sts[h]
                s = src[h]
                m_new = jnp.maximum(m_run, jnp.max(s, axis=0, keepdims=True))
                alpha = jnp.exp(m_run - m_new)
                p = jnp.exp(s - m_new)
                l_run = alpha * l_run + jnp.sum(p, axis=0, keepdims=True)
                acc = alpha * acc + _dot(vt_scr[kb], p.astype(BF16))
                out.append((m_new, l_run, acc))
            return tuple(out)

        last = (qi * bq + bq - 1) // bk
        init = (jnp.full((1, bq), NEG_BIG, F32), jnp.zeros((1, bq), F32), jnp.zeros((LANES, bq), F32))
        scores_into(sa_scr, last, True)

        def pair(t, ca):
            held, sts = ca
            scores_into(sb_scr, 2 * t, False)
            sts = absorb(sa_scr, held, sts)
            scores_into(sa_scr, 2 * t + 1, False)
            sts = absorb(sb_scr, 2 * t, sts)
            return 2 * t + 1, sts

        held, sts = lax.fori_loop(0, last // 2, pair, (last, (init, init)))

        def odd_tail(sts):
            scores_into(sb_scr, last - 1, False)
            sts = absorb(sa_scr, held, sts)
            return absorb(sb_scr, last - 1, sts)

        sts = lax.cond(last % 2 == 1, odd_tail, lambda sts: absorb(sa_scr, held, sts), sts)
        o_t = jnp.where(row_lo, sts[0][2] / sts[0][1], sts[1][2] / sts[1][1])
        o_ref[qs, :] = o_t.T.astype(o_ref.dtype)
        return carry

    lax.fori_loop(0, seq // bq, q_block, 0)


def _fox_call(proj, fbias, gq, gk, batch, seq, bq, bk):
    m = proj.shape[0]
    col = lambda blk: pl.BlockSpec((seq, LANES), lambda b, p, blk=blk: (b, blk + p))
    vec = pl.BlockSpec((1, LANES), lambda b, p: (0, 0))
    return pl.pallas_call(
        functools.partial(_fox_kernel, bq=bq, bk=bk, seq=seq),
        grid=(batch, FOX_WIDTH // LANES),
        in_specs=[col(FOX_Q_BLK), col(FOX_K_BLK), col(FOX_V_BLK),
                  pl.BlockSpec((seq, LANES), lambda b, p: (b, FOX_F_BLK)), vec, vec, vec],
        out_specs=pl.BlockSpec((seq, LANES), lambda b, p: (b, p)),
        out_shape=jax.ShapeDtypeStruct((m, FOX_WIDTH), BF16),
        scratch_shapes=[pltpu.VMEM((seq, LANES), F32),
                        pltpu.VMEM((2, seq, LANES), BF16),
                        pltpu.VMEM((2, seq, LANES), BF16),
                        pltpu.VMEM((seq // bk, LANES, bk), BF16),
                        pltpu.VMEM((2, bk, bq), F32),
                        pltpu.VMEM((2, bk, bq), F32)],
        compiler_params=pltpu.CompilerParams(
            dimension_semantics=("arbitrary", "arbitrary"), vmem_limit_bytes=VMEM_LIMIT),
        name="forgetting",
    )(proj, proj, proj, proj, fbias, gq, gk)


def _mlp_kernel(x_ref, ohg_ref, osb_ref, ofx_ref, wo1_ref, wo2_ref, wo3_ref, g2_ref,
                w1_ref, w2_ref, o_ref, h2_scr):
    @pl.when(pl.program_id(1) == 0)
    def _():
        x1 = (x_ref[...] + _dot(ohg_ref[...], wo1_ref[...]) + _dot(osb_ref[...], wo2_ref[...])
              + _dot(ofx_ref[...], wo3_ref[...]))
        ms = jnp.mean(x1 * x1, axis=-1, keepdims=True)
        h2_scr[...] = (x1 * lax.rsqrt(ms + EPS) * g2_ref[...]).astype(BF16)
        o_ref[...] = x1

    a = _dot(h2_scr[...], w1_ref[...])
    a = jnp.square(jnp.maximum(a, 0.0)).astype(BF16)
    o_ref[...] += _dot(a, w2_ref[...])


def _mlp_call(x, ohg, osb, ofx, wo1, wo2, wo3, g2, w1, w2, tm, tf):
    m, d = x.shape
    ff = w1.shape[1]
    row = lambda w: pl.BlockSpec((tm, w), lambda i, f: (i, 0))
    whole = lambda a: pl.BlockSpec(a.shape, lambda i, f: (0, 0))
    return pl.pallas_call(
        _mlp_kernel,
        grid=(m // tm, ff // tf),
        in_specs=[row(d), row(HG_WIDTH), row(SB_WIDTH), row(FOX_WIDTH),
                  whole(wo1), whole(wo2), whole(wo3), whole(g2),
                  pl.BlockSpec((d, tf), lambda i, f: (0, f)),
                  pl.BlockSpec((tf, d), lambda i, f: (f, 0))],
        out_specs=row(d),
        out_shape=jax.ShapeDtypeStruct((m, d), F32),
        scratch_shapes=[pltpu.VMEM((tm, d), BF16)],
        compiler_params=pltpu.CompilerParams(
            dimension_semantics=("arbitrary", "arbitrary"), vmem_limit_bytes=VMEM_LIMIT),
        name="outproj_mlp",
    )(x, ohg, osb, ofx, wo1, wo2, wo3, g2, w1, w2)


def _tile_sizes(batch, seq):
    m = batch * seq
    return dict(
        proj_tm=min(512, m),
        hg_tb=min(512, seq),
        sb_bq=min(256, seq),
        fox_bq=min(256, seq),
        fox_bk=min(512, seq),
        mlp_tm=min(1024, m),
        mlp_tf=512,
    )


def kernel(x, lb_logits, norm1_g, w_in, hg_norm_g, sb_q_norm_g, sb_k_norm_g, fox_q_norm_g,
           fox_k_norm_g, fox_f_bias, w_out, norm2_g, w_ff1, w_ff2):
    batch, seq, d = x.shape
    assert d == D_MODEL and seq % HG_ROWS == 0 and x.dtype == F32
    ts = _tile_sizes(batch, seq)
    assert seq % ts["sb_bq"] == 0 and seq % ts["fox_bk"] == 0 and ts["fox_bk"] % ts["fox_bq"] == 0
    assert seq % ts["hg_tb"] == 0 and seq % PRO_ROWS == 0
    m = batch * seq
    xf = x.reshape(m, d)
    pair = lambda g: jnp.tile(g.astype(F32), 2)[None, :]

    for l in range(DEPTH):
        w_in_p = jnp.pad(w_in[l], ((0, 0), (0, IN_COLS_PAD - IN_COLS))).astype(BF16)
        proj = _proj_call(xf, norm1_g[l][None, :], w_in_p, ts["proj_tm"])

        o_hg = _hgrn_call(l, lb_logits.astype(F32), proj,
                          jnp.tile(hg_norm_g[l].astype(F32), HG_HEADS)[None, :], batch, seq, ts["hg_tb"])
        o_sb = _sb_call(proj, pair(sb_q_norm_g[l]), pair(sb_k_norm_g[l]), batch, seq, ts["sb_bq"])
        fbias = jnp.pad(fox_f_bias[l].astype(F32), (0, LANES - FOX_HEADS))[None, :]
        o_fx = _fox_call(proj, fbias, pair(fox_q_norm_g[l]), pair(fox_k_norm_g[l]), batch, seq,
                         ts["fox_bq"], ts["fox_bk"])

        wo = w_out[l].astype(BF16)
        xf = _mlp_call(xf, o_hg, o_sb, o_fx,
                       wo[:HG_WIDTH], wo[HG_WIDTH:HG_WIDTH + SB_WIDTH], wo[HG_WIDTH + SB_WIDTH:],
                       norm2_g[l][None, :], w_ff1[l].astype(BF16), w_ff2[l].astype(BF16),
                       ts["mlp_tm"], ts["mlp_tf"])
    return xf.reshape(batch, seq, d)
```

```python
import functools
import math

import jax
import jax.numpy as jnp
from jax import lax
from jax.experimental import pallas as pl
from jax.experimental.pallas import tpu as pltpu

F32 = jnp.float32
BF16 = jnp.bfloat16

D_MODEL = 1024
DEPTH = 4
HEAD_DIM = 64
HG_HEADS = 4
HG_KW = 256
HG_WIDTH = 256
SB_WIDTH = 384
FOX_WIDTH = 384
FOX_HEADS = 6
IN_COLS = 4 * 256 + 3 * 384 + 3 * 384 + FOX_HEADS
D_FF = 4 * D_MODEL
EPS = 1e-6
LB_FLOOR = 1e-30
NEG_BIG = -1e30

LANES = 128
IN_COLS_PAD = 27 * LANES
SB_Q_BLK, SB_K_BLK, SB_V_BLK = 8, 11, 14
FOX_Q_BLK, FOX_K_BLK, FOX_V_BLK, FOX_F_BLK = 17, 20, 23, 26

SB_ZERO_LOG = -105.0
SUB = 16
HG_ROWS = 128
VMEM_LIMIT = 56 * 1024 * 1024


def _nt_dot(a, b):
    return lax.dot_general(a, b, (((1,), (1,)), ((), ())), preferred_element_type=F32)


def _dot(a, b):
    return jnp.dot(a, b, preferred_element_type=F32)


def _split2(x):
    hi = x.astype(BF16)
    lo = (x - hi.astype(F32)).astype(BF16)
    return hi, lo


def _split3(x):
    hi = x.astype(BF16)
    r1 = x - hi.astype(F32)
    mid = r1.astype(BF16)
    lo = (r1 - mid.astype(F32)).astype(BF16)
    return hi, mid, lo


def _softplus(z):
    return jnp.maximum(z, 0.0) + jnp.log(1.0 + jnp.exp(-jnp.abs(z)))


def _proj_kernel(x_ref, g_ref, w_ref, o_ref):
    x = x_ref[...]
    ms = jnp.mean(x * x, axis=-1, keepdims=True)
    h = (x * lax.rsqrt(ms + EPS) * g_ref[...]).astype(BF16)
    o_ref[...] = _dot(h, w_ref[...])


def _proj_call(x, g, w, tm):
    m, d = x.shape
    n = w.shape[1]
    return pl.pallas_call(
        _proj_kernel,
        grid=(m // tm,),
        in_specs=[
            pl.BlockSpec((tm, d), lambda i: (i, 0)),
            pl.BlockSpec((1, d), lambda i: (0, 0)),
            pl.BlockSpec((d, n), lambda i: (0, 0)),
        ],
        out_specs=pl.BlockSpec((tm, n), lambda i: (i, 0)),
        out_shape=jax.ShapeDtypeStruct((m, n), F32),
        compiler_params=pltpu.CompilerParams(
            dimension_semantics=("arbitrary",), vmem_limit_bytes=VMEM_LIMIT),
        name="proj",
    )(x, g, w)


def _hgrn_kernel(lbl_ref, q_ref, f_ref, i_ref, g_ref, gn_ref, o_ref, st_scr, oi_scr, *, layer, tb):
    @pl.when(pl.program_id(1) == 0)
    def _():
        st_scr[...] = jnp.zeros_like(st_scr)

    rows = [lbl_ref[j:j + 1, :] for j in range(DEPTH)]
    mx = functools.reduce(jnp.maximum, rows)
    ex = [jnp.exp(r - mx) for r in rows]
    den = functools.reduce(lambda a, b: a + b, ex)
    lb = jnp.zeros_like(mx)
    for j in range(1, layer + 1):
        lb = lb + ex[j] / den
    log_lb = jnp.log(jnp.maximum(lb, LB_FLOOR))
    one_m_lb = 1.0 - lb
    gn = gn_ref[...]

    r_i = lax.broadcasted_iota(jnp.int32, (HG_KW, HG_KW), 0)
    c_i = lax.broadcasted_iota(jnp.int32, (HG_KW, HG_KW), 1)
    same_head = (r_i >> 6) == (c_i >> 6)
    ones_bd = jnp.where(same_head, 1.0, 0.0).astype(BF16)
    t_mod = lax.broadcasted_iota(jnp.int32, (HG_ROWS, HG_KW), 0) & (SUB - 1)
    lane_sub = lax.broadcasted_iota(jnp.int32, (HG_KW, HG_ROWS), 1) >> 4

    def chunk(c, carry):
        rs = pl.ds(pl.multiple_of(c * HG_ROWS, HG_ROWS), HG_ROWS)
        q = q_ref[rs, :]
        fl = f_ref[rs, :]
        v = i_ref[rs, :]
        g = g_ref[rs, :]

        sp = _softplus(fl)
        lf = jnp.maximum(fl, log_lb) + jnp.log(1.0 + jnp.exp(-jnp.abs(fl - log_lb))) - sp
        kk = one_m_lb * jnp.exp(-sp)

        b = lf
        suf = lf
        s = 1
        while s < SUB:
            b = b + jnp.where(t_mod >= s, pltpu.roll(b, s, 0), 0.0)
            suf = suf + jnp.where(t_mod < SUB - s, pltpu.roll(suf, HG_ROWS - s, 0), 0.0)
            s *= 2
        rest = suf - lf
        beta = b + rest

        acc = jnp.zeros((HG_ROWS, HG_WIDTH), F32)
        for d in range(SUB):
            if d == 0:
                p = q * kk
                vd = v
            else:
                kd = pltpu.roll(kk, d, 0)
                bd = pltpu.roll(b, d, 0)
                vd = pltpu.roll(v, d, 0)
                p = jnp.where(t_mod >= d, q * kd * jnp.exp(jnp.minimum(b - bd, 0.0)), 0.0)
            acc = acc + _dot(p.astype(BF16), ones_bd) * vd

        qt = (q * jnp.exp(b)).astype(BF16)
        kt = (kk * jnp.exp(rest)).astype(BF16)
        v_t = v.T
        for j in range(HG_ROWS // SUB):
            st = st_scr[...]
            oi_scr[j * SUB:(j + 1) * SUB, :] = _nt_dot(qt[j * SUB:(j + 1) * SUB, :], st.astype(BF16))
            vm = jnp.where(lane_sub == j, v_t, 0.0).astype(BF16)
            ut = _dot(vm, kt)
            dec = jnp.exp(beta[j * SUB:j * SUB + 1, :])
            st_scr[...] = st * dec + jnp.where(same_head, ut, 0.0)

        o = acc + oi_scr[...]
        hi, lo = _split2(o * o)
        msq = (_dot(hi, ones_bd) + _dot(lo, ones_bd)) * (1.0 / HEAD_DIM)
        y = o * lax.rsqrt(msq + EPS) * gn
        o_ref[rs, :] = (y * (g / (1.0 + jnp.exp(-g)))).astype(o_ref.dtype)
        return carry

    lax.fori_loop(0, tb // HG_ROWS, chunk, 0)


def _hgrn_call(layer, lb_logits, proj, gn, batch, seq, tb):
    m = proj.shape[0]
    nt = seq // tb

    def col(j):
        return pl.BlockSpec((tb, HG_KW), lambda b, t, j=j: (b * nt + t, j))

    return pl.pallas_call(
        functools.partial(_hgrn_kernel, layer=layer, tb=tb),
        grid=(batch, nt),
        in_specs=[pl.BlockSpec((DEPTH, HG_KW), lambda b, t: (0, 0)),
                  col(0), col(1), col(2), col(3),
                  pl.BlockSpec((1, HG_WIDTH), lambda b, t: (0, 0))],
        out_specs=pl.BlockSpec((tb, HG_WIDTH), lambda b, t: (b * nt + t, 0)),
        out_shape=jax.ShapeDtypeStruct((m, HG_WIDTH), BF16),
        scratch_shapes=[pltpu.VMEM((HG_WIDTH, HG_KW), F32),
                        pltpu.VMEM((HG_ROWS, HG_WIDTH), F32)],
        compiler_params=pltpu.CompilerParams(
            dimension_semantics=("arbitrary", "arbitrary"), vmem_limit_bytes=VMEM_LIMIT),
        name="hgrn2",
    )(lb_logits, proj, proj, proj, proj, gn)


def _pair_rms(x, gain, lane_lo):
    x2 = x * x
    s0 = jnp.sum(jnp.where(lane_lo, x2, 0.0), axis=1, keepdims=True)
    s1 = jnp.sum(jnp.where(lane_lo, 0.0, x2), axis=1, keepdims=True)
    ms = jnp.where(lane_lo, s0, s1) * (1.0 / HEAD_DIM)
    return x * lax.rsqrt(ms + EPS) * gain


PRO_ROWS = 512


def _sb_kernel(q0_ref, q1_ref, q2_ref, k0_ref, k1_ref, k2_ref, v0_ref, v1_ref, v2_ref,
               gq_ref, gk_ref, o_ref, kn_scr, vb_scr, acc_scr, car_scr, *, bq, seq):
    i = pl.program_id(1)
    q_refs = (q0_ref, q1_ref, q2_ref)
    k_refs = (k0_ref, k1_ref, k2_ref)
    v_refs = (v0_ref, v1_ref, v2_ref)
    n_pairs = len(q_refs)

    @pl.when(i == 0)
    def _():
        lane_lo = lax.broadcasted_iota(jnp.int32, (PRO_ROWS, LANES), 1) < HEAD_DIM

        def body(c, carry):
            rs = pl.ds(pl.multiple_of(c * PRO_ROWS, PRO_ROWS), PRO_ROWS)
            for p in range(n_pairs):
                kn_scr[p, rs, :] = _pair_rms(k_refs[p][rs, :], gk_ref[...], lane_lo).astype(BF16)
                vb_scr[p, rs, :] = v_refs[p][rs, :].astype(BF16)
            return carry

        lax.fori_loop(0, seq // PRO_ROWS, body, 0)

    lane_lo = lax.broadcasted_iota(jnp.int32, (bq, LANES), 1) < HEAD_DIM
    scale = 1.0 / math.sqrt(HEAD_DIM)
    r_i = lax.broadcasted_iota(jnp.int32, (bq, bq), 0)
    c_i = lax.broadcasted_iota(jnp.int32, (bq, bq), 1)
    before = c_i < r_i
    later = jnp.where(r_i > c_i, 1.0, 0.0).astype(BF16)

    qms = []
    for p in range(n_pairs):
        qn = _pair_rms(q_refs[p][...], gq_ref[...], lane_lo) * scale
        qms.append(jnp.where(lane_lo, qn, 0.0).astype(BF16))
        qms.append(jnp.where(lane_lo, 0.0, qn).astype(BF16))

    heads = range(2 * n_pairs)

    def block(kb, mask, first):
        rs = pl.ds(pl.multiple_of(kb * bq, bq), bq)
        zs = [_nt_dot(qms[h], kn_scr[h // 2, rs, :]) for h in heads]
        sps = [_softplus(z) for z in zs]
        ls = [-sp if mask is None else jnp.where(mask, -sp, 0.0) for sp in sps]
        tails = [_dot(l.astype(BF16), later) for l in ls]
        top = None
        for h in heads:
            row = jnp.broadcast_to(jnp.sum(ls[h], axis=1, keepdims=True), (bq, LANES))
            e = zs[h] - sps[h] + tails[h]
            if not first:
                carry = car_scr[h]
                e = e + jnp.concatenate([carry] * (bq // LANES), axis=1)
                row = row + carry
            a = jnp.exp(e)
            if mask is not None:
                a = jnp.where(mask, a, 0.0)
            pv = _dot(a.astype(BF16), vb_scr[h // 2, rs, :])
            acc_scr[h] = pv if first else acc_scr[h] + pv
            car_scr[h] = row
            top = row if top is None else jnp.maximum(top, row)
        return jnp.max(top)

    top = block(i, before, True)

    def live(st):
        return jnp.logical_and(st[0] >= 0, st[1] > SB_ZERO_LOG)

    lax.while_loop(live, lambda st: (st[0] - 1, block(st[0], None, False)), (i - 1, top))
    for p in range(n_pairs):
        o_ref[:, p * LANES:(p + 1) * LANES] = jnp.where(
            lane_lo, acc_scr[2 * p], acc_scr[2 * p + 1]).astype(o_ref.dtype)


def _sb_call(proj, gq, gk, batch, seq, bq):
    m = proj.shape[0]
    nq = seq // bq
    n_pairs = SB_WIDTH // LANES
    q_spec = lambda p: pl.BlockSpec((bq, LANES), lambda b, i, p=p: (b * nq + i, SB_Q_BLK + p))
    k_spec = lambda blk, p: pl.BlockSpec((seq, LANES), lambda b, i, p=p: (b, blk + p))
    gain = pl.BlockSpec((1, LANES), lambda b, i: (0, 0))
    return pl.pallas_call(
        functools.partial(_sb_kernel, bq=bq, seq=seq),
        grid=(batch, nq),
        in_specs=([q_spec(p) for p in range(n_pairs)]
                  + [k_spec(SB_K_BLK, p) for p in range(n_pairs)]
                  + [k_spec(SB_V_BLK, p) for p in range(n_pairs)] + [gain, gain]),
        out_specs=pl.BlockSpec((bq, SB_WIDTH), lambda b, i: (b * nq + i, 0)),
        out_shape=jax.ShapeDtypeStruct((m, SB_WIDTH), BF16),
        scratch_shapes=[pltpu.VMEM((n_pairs, seq, LANES), BF16),
                        pltpu.VMEM((n_pairs, seq, LANES), BF16),
                        pltpu.VMEM((2 * n_pairs, bq, LANES), F32),
                        pltpu.VMEM((2 * n_pairs, bq, LANES), F32)],
        compiler_params=pltpu.CompilerParams(
            dimension_semantics=("arbitrary", "arbitrary"), vmem_limit_bytes=VMEM_LIMIT),
        name="stickbreak",
    )(*([proj] * (3 * n_pairs)), gq, gk)


def _aug_lanes(lane, base, first, second):
    out = jnp.zeros(lane.shape, F32)
    for n in range(3):
        out = jnp.where(lane == base + n, first[n], out)
        out = jnp.where(lane == base + 3 + n, second[n], out)
    return out


def _fox_kernel(q_ref, k_ref, v_ref, f_ref, fb_ref, gq_ref, gk_ref, o_ref,
                c_scr, qa_scr, ka_scr, vt_scr, sa_scr, sb_scr, *, bq, bk, seq):
    p_idx = pl.program_id(1)
    one = jnp.ones((1, 1), F32)
    scale = 1.0 / math.sqrt(HEAD_DIM)

    @pl.when(p_idx == 0)
    def _():
        r_i = lax.broadcasted_iota(jnp.int32, (bk, bk), 0)
        c_i = lax.broadcasted_iota(jnp.int32, (bk, bk), 1)
        upto = jnp.where(c_i <= r_i, 1.0, 0.0).astype(BF16)

        def body(n, run):
            rs = pl.ds(pl.multiple_of(n * bk, bk), bk)
            y = f_ref[rs, :] + fb_ref[...]
            lg = jnp.minimum(y, 0.0) - jnp.log(1.0 + jnp.exp(-jnp.abs(y)))
            hi, mid, lo = _split3(lg)
            c = _dot(upto, hi) + _dot(upto, mid) + _dot(upto, lo) + run
            c_scr[rs, :] = c
            return c[bk - 1:bk, :]

        lax.fori_loop(0, seq // bk, body, jnp.zeros((1, LANES), F32))

    lane = lax.broadcasted_iota(jnp.int32, (bk, LANES), 1)
    lane_lo = lane < HEAD_DIM

    def prep(n, carry):
        rs = pl.ds(pl.multiple_of(n * bk, bk), bk)
        kn = _pair_rms(k_ref[rs, :], gk_ref[...], lane_lo)
        qn = _pair_rms(q_ref[rs, :], gq_ref[...], lane_lo) * scale
        c = c_scr[rs, :]
        for h in range(2):
            ch = jnp.sum(jnp.where(lane == 2 * p_idx + h, c, 0.0), axis=1, keepdims=True)
            parts = [x.astype(F32) for x in _split3(ch)]
            mine = lane_lo if h == 0 else jnp.logical_not(lane_lo)
            base = HEAD_DIM * (1 - h)
            k_aug = _aug_lanes(lane, base, (one, one, one), [-x for x in parts])
            q_aug = _aug_lanes(lane, base, parts, (one, one, one))
            ka_scr[h, rs, :] = jnp.where(mine, kn, k_aug).astype(BF16)
            qa_scr[h, rs, :] = jnp.where(mine, qn, q_aug).astype(BF16)
        vt_scr[n] = v_ref[rs, :].T.astype(BF16)
        return carry

    lax.fori_loop(0, seq // bk, prep, 0)

    key_i = lax.broadcasted_iota(jnp.int32, (bk, bq), 0)
    qry_i = lax.broadcasted_iota(jnp.int32, (bk, bq), 1)
    row_lo = lax.broadcasted_iota(jnp.int32, (LANES, bq), 0) < HEAD_DIM

    def q_block(qi, carry):
        qs = pl.ds(pl.multiple_of(qi * bq, bq), bq)
        qas = [qa_scr[h, qs, :] for h in range(2)]

        def scores_into(dst, kb, masked):
            rs = pl.ds(pl.multiple_of(kb * bk, bk), bk)
            for h in range(2):
                s = _nt_dot(ka_scr[h, rs, :], qas[h])
                if masked:
                    s = jnp.where(kb * bk + key_i <= qi * bq + qry_i, s, NEG_BIG)
                dst[h] = s

        def absorb(src, kb, sts):
            out = []
            for h in range(2):
                m_run, l_run, acc = sts[h]
                s = src[h]
                m_new = jnp.maximum(m_run, jnp.max(s, axis=0, keepdims=True))
                alpha = jnp.exp(m_run - m_new)
                p = jnp.exp(s - m_new)
                l_run = alpha * l_run + jnp.sum(p, axis=0, keepdims=True)
                acc = alpha * acc + _dot(vt_scr[kb], p.astype(BF16))
                out.append((m_new, l_run, acc))
            return tuple(out)

        last = (qi * bq + bq - 1) // bk
        init = (jnp.full((1, bq), NEG_BIG, F32), jnp.zeros((1, bq), F32), jnp.zeros((LANES, bq), F32))
        scores_into(sa_scr, last, True)

        def pair(t, ca):
            held, sts = ca
            scores_into(sb_scr, 2 * t, False)
            sts = absorb(sa_scr, held, sts)
            scores_into(sa_scr, 2 * t + 1, False)
            sts = absorb(sb_scr, 2 * t, sts)
            return 2 * t + 1, sts

        held, sts = lax.fori_loop(0, last // 2, pair, (last, (init, init)))

        def odd_tail(sts):
            scores_into(sb_scr, last - 1, False)
            sts = absorb(sa_scr, held, sts)
            return absorb(sb_scr, last - 1, sts)

        sts = lax.cond(last % 2 == 1, odd_tail, lambda sts: absorb(sa_scr, held, sts), sts)
        o_t = jnp.where(row_lo, sts[0][2] / sts[0][1], sts[1][2] / sts[1][1])
        o_ref[qs, :] = o_t.T.astype(o_ref.dtype)
        return carry

    lax.fori_loop(0, seq // bq, q_block, 0)


def _fox_call(proj, fbias, gq, gk, batch, seq, bq, bk):
    m = proj.shape[0]
    col = lambda blk: pl.BlockSpec((seq, LANES), lambda b, p, blk=blk: (b, blk + p))
    vec = pl.BlockSpec((1, LANES), lambda b, p: (0, 0))
    return pl.pallas_call(
        functools.partial(_fox_kernel, bq=bq, bk=bk, seq=seq),
        grid=(batch, FOX_WIDTH // LANES),
        in_specs=[col(FOX_Q_BLK), col(FOX_K_BLK), col(FOX_V_BLK),
                  pl.BlockSpec((seq, LANES), lambda b, p: (b, FOX_F_BLK)), vec, vec, vec],
        out_specs=pl.BlockSpec((seq, LANES), lambda b, p: (b, p)),
        out_shape=jax.ShapeDtypeStruct((m, FOX_WIDTH), BF16),
        scratch_shapes=[pltpu.VMEM((seq, LANES), F32),
                        pltpu.VMEM((2, seq, LANES), BF16),
                        pltpu.VMEM((2, seq, LANES), BF16),
                        pltpu.VMEM((seq // bk, LANES, bk), BF16),
                        pltpu.VMEM((2, bk, bq), F32),
                        pltpu.VMEM((2, bk, bq), F32)],
        compiler_params=pltpu.CompilerParams(
            dimension_semantics=("arbitrary", "arbitrary"), vmem_limit_bytes=VMEM_LIMIT),
        name="forgetting",
    )(proj, proj, proj, proj, fbias, gq, gk)


def _mlp_kernel(x_ref, ohg_ref, osb_ref, ofx_ref, wo1_ref, wo2_ref, wo3_ref, g2_ref,
                w1_ref, w2_ref, o_ref, *, tf):
    x1 = (x_ref[...] + _dot(ohg_ref[...], wo1_ref[...]) + _dot(osb_ref[...], wo2_ref[...])
          + _dot(ofx_ref[...], wo3_ref[...]))
    ms = jnp.mean(x1 * x1, axis=-1, keepdims=True)
    h2 = (x1 * lax.rsqrt(ms + EPS) * g2_ref[...]).astype(BF16)
    o_ref[...] = x1
    acc = None
    for f in range(w1_ref.shape[1] // tf):
        a = _dot(h2, w1_ref[:, f * tf:(f + 1) * tf])
        a = jnp.square(jnp.maximum(a, 0.0)).astype(BF16)
        y = _dot(a, w2_ref[f * tf:(f + 1) * tf, :])
        acc = y if acc is None else acc + y
    o_ref[...] = o_ref[...] + acc


def _mlp_call(x, ohg, osb, ofx, wo1, wo2, wo3, g2, w1, w2, tm, tf):
    m, d = x.shape
    row = lambda w: pl.BlockSpec((tm, w), lambda i: (i, 0))
    whole = lambda a: pl.BlockSpec(a.shape, lambda i: (0, 0))
    return pl.pallas_call(
        functools.partial(_mlp_kernel, tf=tf),
        grid=(m // tm,),
        in_specs=[row(d), row(HG_WIDTH), row(SB_WIDTH), row(FOX_WIDTH),
                  whole(wo1), whole(wo2), whole(wo3), whole(g2), whole(w1), whole(w2)],
        out_specs=row(d),
        out_shape=jax.ShapeDtypeStruct((m, d), F32),
        compiler_params=pltpu.CompilerParams(
            dimension_semantics=("arbitrary",), vmem_limit_bytes=VMEM_LIMIT),
        name="outproj_mlp",
    )(x, ohg, osb, ofx, wo1, wo2, wo3, g2, w1, w2)


def _tile_sizes(batch, seq):
    m = batch * seq
    return dict(
        proj_tm=min(512, m),
        hg_tb=min(512, seq),
        sb_bq=min(256, seq),
        fox_bq=min(256, seq),
        fox_bk=min(512, seq),
        mlp_tm=min(512, m),
        mlp_tf=1024,
    )


def kernel(x, lb_logits, norm1_g, w_in, hg_norm_g, sb_q_norm_g, sb_k_norm_g, fox_q_norm_g,
           fox_k_norm_g, fox_f_bias, w_out, norm2_g, w_ff1, w_ff2):
    batch, seq, d = x.shape
    assert d == D_MODEL and seq % HG_ROWS == 0 and x.dtype == F32
    ts = _tile_sizes(batch, seq)
    assert seq % ts["sb_bq"] == 0 and seq % ts["fox_bk"] == 0 and ts["fox_bk"] % ts["fox_bq"] == 0
    assert seq % ts["hg_tb"] == 0 and seq % PRO_ROWS == 0
    m = batch * seq
    xf = x.reshape(m, d)
    pair = lambda g: jnp.tile(g.astype(F32), 2)[None, :]

    for l in range(DEPTH):
        w_in_p = jnp.pad(w_in[l], ((0, 0), (0, IN_COLS_PAD - IN_COLS))).astype(BF16)
        proj = _proj_call(xf, norm1_g[l][None, :], w_in_p, ts["proj_tm"])

        o_hg = _hgrn_call(l, lb_logits.astype(F32), proj,
                          jnp.tile(hg_norm_g[l].astype(F32), HG_HEADS)[None, :], batch, seq, ts["hg_tb"])
        o_sb = _sb_call(proj, pair(sb_q_norm_g[l]), pair(sb_k_norm_g[l]), batch, seq, ts["sb_bq"])
        fbias = jnp.pad(fox_f_bias[l].astype(F32), (0, LANES - FOX_HEADS))[None, :]
        o_fx = _fox_call(proj, fbias, pair(fox_q_norm_g[l]), pair(fox_k_norm_g[l]), batch, seq,
                         ts["fox_bq"], ts["fox_bk"])

        wo = w_out[l].astype(BF16)
        xf = _mlp_call(xf, o_hg, o_sb, o_fx,
                       wo[:HG_WIDTH], wo[HG_WIDTH:HG_WIDTH + SB_WIDTH], wo[HG_WIDTH + SB_WIDTH:],
                       norm2_g[l][None, :], w_ff1[l].astype(BF16), w_ff2[l].astype(BF16),
                       ts["mlp_tm"], ts["mlp_tf"])
    return xf.reshape(batch, seq, d)
```

```python
import functools
import math

import jax
import jax.numpy as jnp
from jax import lax
from jax.experimental import pallas as pl
from jax.experimental.pallas import tpu as pltpu

F32 = jnp.float32
BF16 = jnp.bfloat16

D_MODEL = 1024
DEPTH = 4
HEAD_DIM = 64
HG_HEADS = 4
HG_KW = 256
HG_WIDTH = 256
SB_WIDTH = 384
FOX_WIDTH = 384
FOX_HEADS = 6
IN_COLS = 4 * 256 + 3 * 384 + 3 * 384 + FOX_HEADS
D_FF = 4 * D_MODEL
EPS = 1e-6
LB_FLOOR = 1e-30
NEG_BIG = -1e30
LOG2E = math.log2(math.e)

LANES = 128
IN_COLS_PAD = 27 * LANES
SB_Q_BLK, SB_K_BLK, SB_V_BLK = 8, 11, 14
FOX_Q_BLK, FOX_K_BLK, FOX_V_BLK, FOX_F_BLK = 17, 20, 23, 26

SB_ZERO_LOG = -105.0
SUB = 16
HG_ROWS = 128
VMEM_LIMIT = 56 * 1024 * 1024


def _nt_dot(a, b):
    return lax.dot_general(a, b, (((1,), (1,)), ((), ())), preferred_element_type=F32)


def _dot(a, b):
    return jnp.dot(a, b, preferred_element_type=F32)


def _split2(x):
    hi = x.astype(BF16)
    lo = (x - hi.astype(F32)).astype(BF16)
    return hi, lo


def _split3(x):
    hi = x.astype(BF16)
    r1 = x - hi.astype(F32)
    mid = r1.astype(BF16)
    lo = (r1 - mid.astype(F32)).astype(BF16)
    return hi, mid, lo


def _softplus(z):
    return jnp.maximum(z, 0.0) + jnp.log(1.0 + jnp.exp(-jnp.abs(z)))


def _proj_kernel(x_ref, g_ref, w_ref, o_ref):
    x = x_ref[...]
    ms = jnp.mean(x * x, axis=-1, keepdims=True)
    h = (x * lax.rsqrt(ms + EPS) * g_ref[...]).astype(BF16)
    o_ref[...] = _dot(h, w_ref[...])


def _proj_call(x, g, w, tm):
    m, d = x.shape
    n = w.shape[1]
    return pl.pallas_call(
        _proj_kernel,
        grid=(m // tm,),
        in_specs=[
            pl.BlockSpec((tm, d), lambda i: (i, 0)),
            pl.BlockSpec((1, d), lambda i: (0, 0)),
            pl.BlockSpec((d, n), lambda i: (0, 0)),
        ],
        out_specs=pl.BlockSpec((tm, n), lambda i: (i, 0)),
        out_shape=jax.ShapeDtypeStruct((m, n), F32),
        compiler_params=pltpu.CompilerParams(
            dimension_semantics=("arbitrary",), vmem_limit_bytes=VMEM_LIMIT),
        name="proj",
    )(x, g, w)


def _hgrn_kernel(lbl_ref, q_ref, f_ref, i_ref, g_ref, gn_ref, o_ref, st_scr, oi_scr, *, layer, tb):
    @pl.when(pl.program_id(1) == 0)
    def _():
        st_scr[...] = jnp.zeros_like(st_scr)

    rows = [lbl_ref[j:j + 1, :] for j in range(DEPTH)]
    mx = functools.reduce(jnp.maximum, rows)
    ex = [jnp.exp(r - mx) for r in rows]
    den = functools.reduce(lambda a, b: a + b, ex)
    lb = jnp.zeros_like(mx)
    for j in range(1, layer + 1):
        lb = lb + ex[j] / den
    log_lb = jnp.log(jnp.maximum(lb, LB_FLOOR))
    one_m_lb = 1.0 - lb
    gn = gn_ref[...]

    r_i = lax.broadcasted_iota(jnp.int32, (HG_KW, HG_KW), 0)
    c_i = lax.broadcasted_iota(jnp.int32, (HG_KW, HG_KW), 1)
    same_head = (r_i >> 6) == (c_i >> 6)
    ones_bd = jnp.where(same_head, 1.0, 0.0).astype(BF16)
    t_mod = lax.broadcasted_iota(jnp.int32, (HG_ROWS, HG_KW), 0) & (SUB - 1)
    lane_sub = lax.broadcasted_iota(jnp.int32, (HG_KW, HG_ROWS), 1) >> 4

    def chunk(c, carry):
        rs = pl.ds(pl.multiple_of(c * HG_ROWS, HG_ROWS), HG_ROWS)
        q = q_ref[rs, :]
        fl = f_ref[rs, :]
        v = i_ref[rs, :]
        g = g_ref[rs, :]

        sp = _softplus(fl)
        lf = jnp.maximum(fl, log_lb) + jnp.log(1.0 + jnp.exp(-jnp.abs(fl - log_lb))) - sp
        kk = one_m_lb * jnp.exp(-sp)

        b = lf
        suf = lf
        s = 1
        while s < SUB:
            b = b + jnp.where(t_mod >= s, pltpu.roll(b, s, 0), 0.0)
            suf = suf + jnp.where(t_mod < SUB - s, pltpu.roll(suf, HG_ROWS - s, 0), 0.0)
            s *= 2
        rest = suf - lf
        beta = b + rest

        acc = jnp.zeros((HG_ROWS, HG_WIDTH), F32)
        for d in range(SUB):
            if d == 0:
                p = q * kk
                vd = v
            else:
                kd = pltpu.roll(kk, d, 0)
                bd = pltpu.roll(b, d, 0)
                vd = pltpu.roll(v, d, 0)
                p = jnp.where(t_mod >= d, q * kd * jnp.exp(jnp.minimum(b - bd, 0.0)), 0.0)
            acc = acc + _dot(p.astype(BF16), ones_bd) * vd

        qt = (q * jnp.exp(b)).astype(BF16)
        kt = (kk * jnp.exp(rest)).astype(BF16)
        v_t = v.T
        for j in range(HG_ROWS // SUB):
            st = st_scr[...]
            oi_scr[j * SUB:(j + 1) * SUB, :] = _nt_dot(qt[j * SUB:(j + 1) * SUB, :], st.astype(BF16))
            vm = jnp.where(lane_sub == j, v_t, 0.0).astype(BF16)
            ut = _dot(vm, kt)
            dec = jnp.exp(beta[j * SUB:j * SUB + 1, :])
            st_scr[...] = st * dec + jnp.where(same_head, ut, 0.0)

        o = acc + oi_scr[...]
        hi, lo = _split2(o * o)
        msq = (_dot(hi, ones_bd) + _dot(lo, ones_bd)) * (1.0 / HEAD_DIM)
        y = o * lax.rsqrt(msq + EPS) * gn
        o_ref[rs, :] = (y * (g / (1.0 + jnp.exp(-g)))).astype(o_ref.dtype)
        return carry

    lax.fori_loop(0, tb // HG_ROWS, chunk, 0)


def _hgrn_call(layer, lb_logits, proj, gn, batch, seq, tb):
    m = proj.shape[0]
    nt = seq // tb

    def col(j):
        return pl.BlockSpec((tb, HG_KW), lambda b, t, j=j: (b * nt + t, j))

    return pl.pallas_call(
        functools.partial(_hgrn_kernel, layer=layer, tb=tb),
        grid=(batch, nt),
        in_specs=[pl.BlockSpec((DEPTH, HG_KW), lambda b, t: (0, 0)),
                  col(0), col(1), col(2), col(3),
                  pl.BlockSpec((1, HG_WIDTH), lambda b, t: (0, 0))],
        out_specs=pl.BlockSpec((tb, HG_WIDTH), lambda b, t: (b * nt + t, 0)),
        out_shape=jax.ShapeDtypeStruct((m, HG_WIDTH), BF16),
        scratch_shapes=[pltpu.VMEM((HG_WIDTH, HG_KW), F32),
                        pltpu.VMEM((HG_ROWS, HG_WIDTH), F32)],
        compiler_params=pltpu.CompilerParams(
            dimension_semantics=("arbitrary", "arbitrary"), vmem_limit_bytes=VMEM_LIMIT),
        name="hgrn2",
    )(lb_logits, proj, proj, proj, proj, gn)


def _pair_rms(x, gain, lane_lo):
    x2 = x * x
    s0 = jnp.sum(jnp.where(lane_lo, x2, 0.0), axis=1, keepdims=True)
    s1 = jnp.sum(jnp.where(lane_lo, 0.0, x2), axis=1, keepdims=True)
    ms = jnp.where(lane_lo, s0, s1) * (1.0 / HEAD_DIM)
    return x * lax.rsqrt(ms + EPS) * gain


PRO_ROWS = 512


def _sb_kernel(q0_ref, q1_ref, q2_ref, k0_ref, k1_ref, k2_ref, v0_ref, v1_ref, v2_ref,
               gq_ref, gk_ref, o_ref, kn_scr, vb_scr, acc_scr, car_scr, *, bq, seq):
    i = pl.program_id(1)
    q_refs = (q0_ref, q1_ref, q2_ref)
    k_refs = (k0_ref, k1_ref, k2_ref)
    v_refs = (v0_ref, v1_ref, v2_ref)
    n_pairs = len(q_refs)

    @pl.when(i == 0)
    def _():
        lane_lo = lax.broadcasted_iota(jnp.int32, (PRO_ROWS, LANES), 1) < HEAD_DIM

        def body(c, carry):
            rs = pl.ds(pl.multiple_of(c * PRO_ROWS, PRO_ROWS), PRO_ROWS)
            for p in range(n_pairs):
                kn_scr[p, rs, :] = _pair_rms(k_refs[p][rs, :], gk_ref[...], lane_lo).astype(BF16)
                vb_scr[p, rs, :] = v_refs[p][rs, :].astype(BF16)
            return carry

        lax.fori_loop(0, seq // PRO_ROWS, body, 0)

    lane_lo = lax.broadcasted_iota(jnp.int32, (bq, LANES), 1) < HEAD_DIM
    scale = 1.0 / math.sqrt(HEAD_DIM)
    r_i = lax.broadcasted_iota(jnp.int32, (bq, bq), 0)
    c_i = lax.broadcasted_iota(jnp.int32, (bq, bq), 1)
    before = c_i < r_i
    later = jnp.where(r_i > c_i, 1.0, 0.0).astype(BF16)

    qms = []
    for p in range(n_pairs):
        qn = _pair_rms(q_refs[p][...], gq_ref[...], lane_lo) * scale
        qms.append(jnp.where(lane_lo, qn, 0.0).astype(BF16))
        qms.append(jnp.where(lane_lo, 0.0, qn).astype(BF16))

    heads = range(2 * n_pairs)

    def block(kb, mask, first):
        rs = pl.ds(pl.multiple_of(kb * bq, bq), bq)
        zs = [_nt_dot(qms[h], kn_scr[h // 2, rs, :]) for h in heads]
        sps = [_softplus(z) for z in zs]
        ls = [-sp if mask is None else jnp.where(mask, -sp, 0.0) for sp in sps]
        tails = [_dot(l.astype(BF16), later) for l in ls]
        top = None
        for h in heads:
            row = jnp.broadcast_to(jnp.sum(ls[h], axis=1, keepdims=True), (bq, LANES))
            e = zs[h] - sps[h] + tails[h]
            if not first:
                carry = car_scr[h]
                e = e + jnp.concatenate([carry] * (bq // LANES), axis=1)
                row = row + carry
            a = jnp.exp(e)
            if mask is not None:
                a = jnp.where(mask, a, 0.0)
            pv = _dot(a.astype(BF16), vb_scr[h // 2, rs, :])
            acc_scr[h] = pv if first else acc_scr[h] + pv
            car_scr[h] = row
            top = row if top is None else jnp.maximum(top, row)
        return jnp.max(top)

    top = block(i, before, True)

    def live(st):
        return jnp.logical_and(st[0] >= 0, st[1] > SB_ZERO_LOG)

    lax.while_loop(live, lambda st: (st[0] - 1, block(st[0], None, False)), (i - 1, top))
    for p in range(n_pairs):
        o_ref[:, p * LANES:(p + 1) * LANES] = jnp.where(
            lane_lo, acc_scr[2 * p], acc_scr[2 * p + 1]).astype(o_ref.dtype)


def _sb_call(proj, gq, gk, batch, seq, bq):
    m = proj.shape[0]
    nq = seq // bq
    n_pairs = SB_WIDTH // LANES
    q_spec = lambda p: pl.BlockSpec((bq, LANES), lambda b, i, p=p: (b * nq + i, SB_Q_BLK + p))
    k_spec = lambda blk, p: pl.BlockSpec((seq, LANES), lambda b, i, p=p: (b, blk + p))
    gain = pl.BlockSpec((1, LANES), lambda b, i: (0, 0))
    return pl.pallas_call(
        functools.partial(_sb_kernel, bq=bq, seq=seq),
        grid=(batch, nq),
        in_specs=([q_spec(p) for p in range(n_pairs)]
                  + [k_spec(SB_K_BLK, p) for p in range(n_pairs)]
                  + [k_spec(SB_V_BLK, p) for p in range(n_pairs)] + [gain, gain]),
        out_specs=pl.BlockSpec((bq, SB_WIDTH), lambda b, i: (b * nq + i, 0)),
        out_shape=jax.ShapeDtypeStruct((m, SB_WIDTH), BF16),
        scratch_shapes=[pltpu.VMEM((n_pairs, seq, LANES), BF16),
                        pltpu.VMEM((n_pairs, seq, LANES), BF16),
                        pltpu.VMEM((2 * n_pairs, bq, LANES), F32),
                        pltpu.VMEM((2 * n_pairs, bq, LANES), F32)],
        compiler_params=pltpu.CompilerParams(
            dimension_semantics=("arbitrary", "arbitrary"), vmem_limit_bytes=VMEM_LIMIT),
        name="stickbreak",
    )(*([proj] * (3 * n_pairs)), gq, gk)


def _aug_lanes(lane, base, first, second):
    out = jnp.zeros(lane.shape, F32)
    for n in range(3):
        out = jnp.where(lane == base + n, first[n], out)
        out = jnp.where(lane == base + 3 + n, second[n], out)
    return out


def _fox_kernel(q_ref, k_ref, v_ref, f_ref, fb_ref, gq_ref, gk_ref, o_ref,
                c_scr, qa_scr, ka_scr, vt_scr, sa_scr, sb_scr, *, bq, bk, seq):
    p_idx = pl.program_id(1)
    one = jnp.ones((1, 1), F32)
    scale = 1.0 / math.sqrt(HEAD_DIM)

    @pl.when(p_idx == 0)
    def _():
        r_i = lax.broadcasted_iota(jnp.int32, (bk, bk), 0)
        c_i = lax.broadcasted_iota(jnp.int32, (bk, bk), 1)
        upto = jnp.where(c_i <= r_i, 1.0, 0.0).astype(BF16)

        def body(n, run):
            rs = pl.ds(pl.multiple_of(n * bk, bk), bk)
            y = f_ref[rs, :] + fb_ref[...]
            lg = jnp.minimum(y, 0.0) - jnp.log(1.0 + jnp.exp(-jnp.abs(y)))
            hi, mid, lo = _split3(lg)
            c = _dot(upto, hi) + _dot(upto, mid) + _dot(upto, lo) + run
            c_scr[rs, :] = c
            return c[bk - 1:bk, :]

        lax.fori_loop(0, seq // bk, body, jnp.zeros((1, LANES), F32))

    lane = lax.broadcasted_iota(jnp.int32, (bk, LANES), 1)
    lane_lo = lane < HEAD_DIM
    row_lo_k = lax.broadcasted_iota(jnp.int32, (LANES, bk), 0) < HEAD_DIM

    def prep(n, carry):
        rs = pl.ds(pl.multiple_of(n * bk, bk), bk)
        kn = _pair_rms(k_ref[rs, :], gk_ref[...], lane_lo)
        qn = _pair_rms(q_ref[rs, :], gq_ref[...], lane_lo) * (scale * LOG2E)
        c = c_scr[rs, :] * LOG2E
        v_t = v_ref[rs, :].T
        for h in range(2):
            ch = jnp.sum(jnp.where(lane == 2 * p_idx + h, c, 0.0), axis=1, keepdims=True)
            parts = [x.astype(F32) for x in _split3(ch)]
            mine = lane_lo if h == 0 else jnp.logical_not(lane_lo)
            base = HEAD_DIM * (1 - h)
            k_aug = _aug_lanes(lane, base, (one, one, one), [-x for x in parts])
            q_aug = _aug_lanes(lane, base, parts, (one, one, one))
            ka_scr[h, rs, :] = jnp.where(mine, kn, k_aug).astype(BF16)
            qa_scr[h, rs, :] = jnp.where(mine, qn, q_aug).astype(BF16)
            vt_scr[h, n] = jnp.where(row_lo_k if h == 0 else jnp.logical_not(row_lo_k), v_t, 1.0).astype(BF16)
        return carry

    lax.fori_loop(0, seq // bk, prep, 0)

    key_minus_qry = (lax.broadcasted_iota(jnp.int32, (bk, bq), 0)
                     - lax.broadcasted_iota(jnp.int32, (bk, bq), 1))
    row_lo = lax.broadcasted_iota(jnp.int32, (LANES, bq), 0) < HEAD_DIM

    def q_block(qi, carry):
        qs = pl.ds(pl.multiple_of(qi * bq, bq), bq)
        qas = [qa_scr[h, qs, :] for h in range(2)]

        def scores_into(dst, kb, masked):
            rs = pl.ds(pl.multiple_of(kb * bk, bk), bk)
            for h in range(2):
                s = _nt_dot(ka_scr[h, rs, :], qas[h])
                if masked:
                    s = jnp.where(key_minus_qry <= qi * bq - kb * bk, s, NEG_BIG)
                dst[h] = s

        def absorb(src, kb, sts):
            out = []
            for h in range(2):
                m_run, acc = sts[h]
                s = src[h]
                m_new = jnp.maximum(m_run, jnp.max(s, axis=0, keepdims=True))
                p = jnp.exp2(s - m_new)
                acc = jnp.exp2(m_run - m_new) * acc + _dot(vt_scr[h, kb], p.astype(BF16))
                out.append((m_new, acc))
            return tuple(out)

        last = (qi * bq + bq - 1) // bk
        init = (jnp.full((1, bq), NEG_BIG, F32), jnp.zeros((LANES, bq), F32))
        scores_into(sa_scr, last, True)

        def pair(t, ca):
            held, sts = ca
            scores_into(sb_scr, 2 * t, False)
            sts = absorb(sa_scr, held, sts)
            scores_into(sa_scr, 2 * t + 1, False)
            sts = absorb(sb_scr, 2 * t, sts)
            return 2 * t + 1, sts

        held, sts = lax.fori_loop(0, last // 2, pair, (last, (init, init)))

        def odd_tail(sts):
            scores_into(sb_scr, last - 1, False)
            sts = absorb(sa_scr, held, sts)
            return absorb(sb_scr, last - 1, sts)

        sts = lax.cond(last % 2 == 1, odd_tail, lambda sts: absorb(sa_scr, held, sts), sts)
        acc0, acc1 = sts[0][1], sts[1][1]
        o_t = jnp.where(row_lo, acc0 / acc0[HEAD_DIM:HEAD_DIM + 1, :], acc1 / acc1[0:1, :])
        o_ref[qs, :] = o_t.T.astype(o_ref.dtype)
        return carry

    lax.fori_loop(0, seq // bq, q_block, 0)


def _fox_call(proj, fbias, gq, gk, batch, seq, bq, bk):
    m = proj.shape[0]
    col = lambda blk: pl.BlockSpec((seq, LANES), lambda b, p, blk=blk: (b, blk + p))
    vec = pl.BlockSpec((1, LANES), lambda b, p: (0, 0))
    return pl.pallas_call(
        functools.partial(_fox_kernel, bq=bq, bk=bk, seq=seq),
        grid=(batch, FOX_WIDTH // LANES),
        in_specs=[col(FOX_Q_BLK), col(FOX_K_BLK), col(FOX_V_BLK),
                  pl.BlockSpec((seq, LANES), lambda b, p: (b, FOX_F_BLK)), vec, vec, vec],
        out_specs=pl.BlockSpec((seq, LANES), lambda b, p: (b, p)),
        out_shape=jax.ShapeDtypeStruct((m, FOX_WIDTH), BF16),
        scratch_shapes=[pltpu.VMEM((seq, LANES), F32),
                        pltpu.VMEM((2, seq, LANES), BF16),
                        pltpu.VMEM((2, seq, LANES), BF16),
                        pltpu.VMEM((2, seq // bk, LANES, bk), BF16),
                        pltpu.VMEM((2, bk, bq), F32),
                        pltpu.VMEM((2, bk, bq), F32)],
        compiler_params=pltpu.CompilerParams(
            dimension_semantics=("arbitrary", "arbitrary"), vmem_limit_bytes=VMEM_LIMIT),
        name="forgetting",
    )(proj, proj, proj, proj, fbias, gq, gk)


def _mlp_kernel(x_ref, ohg_ref, osb_ref, ofx_ref, wo1_ref, wo2_ref, wo3_ref, g2_ref,
                w1_ref, w2_ref, o_ref, *, tf):
    x1 = (x_ref[...] + _dot(ohg_ref[...], wo1_ref[...]) + _dot(osb_ref[...], wo2_ref[...])
          + _dot(ofx_ref[...], wo3_ref[...]))
    ms = jnp.mean(x1 * x1, axis=-1, keepdims=True)
    h2 = (x1 * lax.rsqrt(ms + EPS) * g2_ref[...]).astype(BF16)
    o_ref[...] = x1
    acc = None
    for f in range(w1_ref.shape[1] // tf):
        a = _dot(h2, w1_ref[:, f * tf:(f + 1) * tf])
        a = jnp.square(jnp.maximum(a, 0.0)).astype(BF16)
        y = _dot(a, w2_ref[f * tf:(f + 1) * tf, :])
        acc = y if acc is None else acc + y
    o_ref[...] = o_ref[...] + acc


def _mlp_call(x, ohg, osb, ofx, wo1, wo2, wo3, g2, w1, w2, tm, tf):
    m, d = x.shape
    row = lambda w: pl.BlockSpec((tm, w), lambda i: (i, 0))
    whole = lambda a: pl.BlockSpec(a.shape, lambda i: (0, 0))
    return pl.pallas_call(
        functools.partial(_mlp_kernel, tf=tf),
        grid=(m // tm,),
        in_specs=[row(d), row(HG_WIDTH), row(SB_WIDTH), row(FOX_WIDTH),
                  whole(wo1), whole(wo2), whole(wo3), whole(g2), whole(w1), whole(w2)],
        out_specs=row(d),
        out_shape=jax.ShapeDtypeStruct((m, d), F32),
        compiler_params=pltpu.CompilerParams(
            dimension_semantics=("arbitrary",), vmem_limit_bytes=VMEM_LIMIT),
        name="outproj_mlp",
    )(x, ohg, osb, ofx, wo1, wo2, wo3, g2, w1, w2)


def _tile_sizes(batch, seq):
    m = batch * seq
    return dict(
        proj_tm=min(512, m),
        hg_tb=min(512, seq),
        sb_bq=min(256, seq),
        fox_bq=min(256, seq),
        fox_bk=min(512, seq),
        mlp_tm=min(512, m),
        mlp_tf=1024,
    )


def kernel(x, lb_logits, norm1_g, w_in, hg_norm_g, sb_q_norm_g, sb_k_norm_g, fox_q_norm_g,
           fox_k_norm_g, fox_f_bias, w_out, norm2_g, w_ff1, w_ff2):
    batch, seq, d = x.shape
    assert d == D_MODEL and seq % HG_ROWS == 0 and x.dtype == F32
    ts = _tile_sizes(batch, seq)
    assert seq % ts["sb_bq"] == 0 and seq % ts["fox_bk"] == 0 and ts["fox_bk"] % ts["fox_bq"] == 0
    assert seq % ts["hg_tb"] == 0 and seq % PRO_ROWS == 0
    m = batch * seq
    xf = x.reshape(m, d)
    pair = lambda g: jnp.tile(g.astype(F32), 2)[None, :]

    for l in range(DEPTH):
        w_in_p = jnp.pad(w_in[l], ((0, 0), (0, IN_COLS_PAD - IN_COLS))).astype(BF16)
        proj = _proj_call(xf, norm1_g[l][None, :], w_in_p, ts["proj_tm"])

        o_hg = _hgrn_call(l, lb_logits.astype(F32), proj,
                          jnp.tile(hg_norm_g[l].astype(F32), HG_HEADS)[None, :], batch, seq, ts["hg_tb"])
        o_sb = _sb_call(proj, pair(sb_q_norm_g[l]), pair(sb_k_norm_g[l]), batch, seq, ts["sb_bq"])
        fbias = jnp.pad(fox_f_bias[l].astype(F32), (0, LANES - FOX_HEADS))[None, :]
        o_fx = _fox_call(proj, fbias, pair(fox_q_norm_g[l]), pair(fox_k_norm_g[l]), batch, seq,
                         ts["fox_bq"], ts["fox_bk"])

        wo = w_out[l].astype(BF16)
        xf = _mlp_call(xf, o_hg, o_sb, o_fx,
                       wo[:HG_WIDTH], wo[HG_WIDTH:HG_WIDTH + SB_WIDTH], wo[HG_WIDTH + SB_WIDTH:],
                       norm2_g[l][None, :], w_ff1[l].astype(BF16), w_ff2[l].astype(BF16),
                       ts["mlp_tm"], ts["mlp_tf"])
    return xf.reshape(batch, seq, d)
```

```python
import functools
import math

import jax
import jax.numpy as jnp
from jax import lax
from jax.experimental import pallas as pl
from jax.experimental.pallas import tpu as pltpu

F32 = jnp.float32
BF16 = jnp.bfloat16

D_MODEL = 1024
DEPTH = 4
HEAD_DIM = 64
HG_HEADS = 4
HG_KW = 256
HG_WIDTH = 256
SB_WIDTH = 384
FOX_WIDTH = 384
FOX_HEADS = 6
IN_COLS = 4 * 256 + 3 * 384 + 3 * 384 + FOX_HEADS
D_FF = 4 * D_MODEL
EPS = 1e-6
LB_FLOOR = 1e-30
NEG_BIG = -1e30
LOG2E = math.log2(math.e)

LANES = 128
IN_COLS_PAD = 27 * LANES
SB_Q_BLK, SB_K_BLK, SB_V_BLK = 8, 11, 14
FOX_Q_BLK, FOX_K_BLK, FOX_V_BLK, FOX_F_BLK = 17, 20, 23, 26

SB_ZERO_LOG = -105.0
SUB = 16
HG_ROWS = 128
VMEM_LIMIT = 56 * 1024 * 1024


def _nt_dot(a, b):
    return lax.dot_general(a, b, (((1,), (1,)), ((), ())), preferred_element_type=F32)


def _dot(a, b):
    return jnp.dot(a, b, preferred_element_type=F32)


def _split2(x):
    hi = x.astype(BF16)
    lo = (x - hi.astype(F32)).astype(BF16)
    return hi, lo


def _split3(x):
    hi = x.astype(BF16)
    r1 = x - hi.astype(F32)
    mid = r1.astype(BF16)
    lo = (r1 - mid.astype(F32)).astype(BF16)
    return hi, mid, lo


def _softplus(z):
    return jnp.maximum(z, 0.0) + jnp.log(1.0 + jnp.exp(-jnp.abs(z)))


def _proj_kernel(x_ref, g_ref, w_ref, o_ref):
    x = x_ref[...]
    ms = jnp.mean(x * x, axis=-1, keepdims=True)
    h = (x * lax.rsqrt(ms + EPS) * g_ref[...]).astype(BF16)
    o_ref[...] = _dot(h, w_ref[...])


def _proj_call(x, g, w, tm):
    m, d = x.shape
    n = w.shape[1]
    return pl.pallas_call(
        _proj_kernel,
        grid=(m // tm,),
        in_specs=[
            pl.BlockSpec((tm, d), lambda i: (i, 0)),
            pl.BlockSpec((1, d), lambda i: (0, 0)),
            pl.BlockSpec((d, n), lambda i: (0, 0)),
        ],
        out_specs=pl.BlockSpec((tm, n), lambda i: (i, 0)),
        out_shape=jax.ShapeDtypeStruct((m, n), F32),
        compiler_params=pltpu.CompilerParams(
            dimension_semantics=("arbitrary",), vmem_limit_bytes=VMEM_LIMIT),
        name="proj",
    )(x, g, w)


def _hgrn_kernel(lbl_ref, q_ref, f_ref, i_ref, g_ref, gn_ref, o_ref, st_scr, oi_scr, *, layer, tb):
    @pl.when(pl.program_id(1) == 0)
    def _():
        st_scr[...] = jnp.zeros_like(st_scr)

    rows = [lbl_ref[j:j + 1, :] for j in range(DEPTH)]
    mx = functools.reduce(jnp.maximum, rows)
    ex = [jnp.exp(r - mx) for r in rows]
    den = functools.reduce(lambda a, b: a + b, ex)
    lb = jnp.zeros_like(mx)
    for j in range(1, layer + 1):
        lb = lb + ex[j] / den
    log_lb = jnp.log(jnp.maximum(lb, LB_FLOOR))
    one_m_lb = 1.0 - lb
    log_one_m_lb = jnp.log(one_m_lb)
    gn = gn_ref[...]

    r_i = lax.broadcasted_iota(jnp.int32, (HG_KW, HG_KW), 0)
    c_i = lax.broadcasted_iota(jnp.int32, (HG_KW, HG_KW), 1)
    same_head = (r_i >> 6) == (c_i >> 6)
    ones_bd = jnp.where(same_head, 1.0, 0.0).astype(BF16)
    t_mod = lax.broadcasted_iota(jnp.int32, (HG_ROWS, HG_KW), 0) & (SUB - 1)
    lane_sub = lax.broadcasted_iota(jnp.int32, (HG_KW, HG_ROWS), 1) >> 4

    def chunk(c, carry):
        rs = pl.ds(pl.multiple_of(c * HG_ROWS, HG_ROWS), HG_ROWS)
        q = q_ref[rs, :]
        fl = f_ref[rs, :]
        v = i_ref[rs, :]
        g = g_ref[rs, :]

        sp = _softplus(fl)
        lf = jnp.maximum(fl, log_lb) + jnp.log(1.0 + jnp.exp(-jnp.abs(fl - log_lb))) - sp
        kk = one_m_lb * jnp.exp(-sp)

        b = lf
        suf = lf
        s = 1
        while s < SUB:
            b = b + jnp.where(t_mod >= s, pltpu.roll(b, s, 0), 0.0)
            suf = suf + jnp.where(t_mod < SUB - s, pltpu.roll(suf, HG_ROWS - s, 0), 0.0)
            s *= 2
        rest = suf - lf
        beta = b + rest

        b2 = b * LOG2E
        w2 = (log_one_m_lb - sp) * LOG2E - b2
        acc = _dot((q * kk).astype(BF16), ones_bd) * v
        for d in range(1, SUB):
            e = jnp.where(t_mod >= d, b2 + pltpu.roll(w2, d, 0), NEG_BIG)
            p = q * jnp.exp2(e)
            acc = acc + _dot(p.astype(BF16), ones_bd) * pltpu.roll(v, d, 0)

        qt = (q * jnp.exp2(b2)).astype(BF16)
        kt = (kk * jnp.exp(rest)).astype(BF16)
        v_t = v.T
        for j in range(HG_ROWS // SUB):
            st = st_scr[...]
            oi_scr[j * SUB:(j + 1) * SUB, :] = _nt_dot(qt[j * SUB:(j + 1) * SUB, :], st.astype(BF16))
            vm = jnp.where(lane_sub == j, v_t, 0.0).astype(BF16)
            ut = _dot(vm, kt)
            dec = jnp.exp(beta[j * SUB:j * SUB + 1, :])
            st_scr[...] = st * dec + jnp.where(same_head, ut, 0.0)

        o = acc + oi_scr[...]
        hi, lo = _split2(o * o)
        msq = (_dot(hi, ones_bd) + _dot(lo, ones_bd)) * (1.0 / HEAD_DIM)
        y = o * lax.rsqrt(msq + EPS) * gn
        o_ref[rs, :] = (y * (g / (1.0 + jnp.exp(-g)))).astype(o_ref.dtype)
        return carry

    lax.fori_loop(0, tb // HG_ROWS, chunk, 0)


def _hgrn_call(layer, lb_logits, proj, gn, batch, seq, tb):
    m = proj.shape[0]
    nt = seq // tb

    def col(j):
        return pl.BlockSpec((tb, HG_KW), lambda b, t, j=j: (b * nt + t, j))

    return pl.pallas_call(
        functools.partial(_hgrn_kernel, layer=layer, tb=tb),
        grid=(batch, nt),
        in_specs=[pl.BlockSpec((DEPTH, HG_KW), lambda b, t: (0, 0)),
                  col(0), col(1), col(2), col(3),
                  pl.BlockSpec((1, HG_WIDTH), lambda b, t: (0, 0))],
        out_specs=pl.BlockSpec((tb, HG_WIDTH), lambda b, t: (b * nt + t, 0)),
        out_shape=jax.ShapeDtypeStruct((m, HG_WIDTH), BF16),
        scratch_shapes=[pltpu.VMEM((HG_WIDTH, HG_KW), F32),
                        pltpu.VMEM((HG_ROWS, HG_WIDTH), F32)],
        compiler_params=pltpu.CompilerParams(
            dimension_semantics=("arbitrary", "arbitrary"), vmem_limit_bytes=VMEM_LIMIT),
        name="hgrn2",
    )(lb_logits, proj, proj, proj, proj, gn)


def _pair_rms(x, gain, lane_lo):
    x2 = x * x
    s0 = jnp.sum(jnp.where(lane_lo, x2, 0.0), axis=1, keepdims=True)
    s1 = jnp.sum(jnp.where(lane_lo, 0.0, x2), axis=1, keepdims=True)
    ms = jnp.where(lane_lo, s0, s1) * (1.0 / HEAD_DIM)
    return x * lax.rsqrt(ms + EPS) * gain


PRO_ROWS = 512


def _sb_kernel(q0_ref, q1_ref, q2_ref, k0_ref, k1_ref, k2_ref, v0_ref, v1_ref, v2_ref,
               gq_ref, gk_ref, o_ref, kn_scr, vb_scr, acc_scr, car_scr, *, bq, seq):
    i = pl.program_id(1)
    q_refs = (q0_ref, q1_ref, q2_ref)
    k_refs = (k0_ref, k1_ref, k2_ref)
    v_refs = (v0_ref, v1_ref, v2_ref)
    n_pairs = len(q_refs)

    @pl.when(i == 0)
    def _():
        lane_lo = lax.broadcasted_iota(jnp.int32, (PRO_ROWS, LANES), 1) < HEAD_DIM

        def body(c, carry):
            rs = pl.ds(pl.multiple_of(c * PRO_ROWS, PRO_ROWS), PRO_ROWS)
            for p in range(n_pairs):
                kn_scr[p, rs, :] = _pair_rms(k_refs[p][rs, :], gk_ref[...], lane_lo).astype(BF16)
                vb_scr[p, rs, :] = v_refs[p][rs, :].astype(BF16)
            return carry

        lax.fori_loop(0, seq // PRO_ROWS, body, 0)

    lane_lo = lax.broadcasted_iota(jnp.int32, (bq, LANES), 1) < HEAD_DIM
    scale = 1.0 / math.sqrt(HEAD_DIM)
    r_i = lax.broadcasted_iota(jnp.int32, (bq, bq), 0)
    c_i = lax.broadcasted_iota(jnp.int32, (bq, bq), 1)
    before = c_i < r_i
    later = jnp.where(r_i > c_i, 1.0, 0.0).astype(BF16)

    qms = []
    for p in range(n_pairs):
        qn = _pair_rms(q_refs[p][...], gq_ref[...], lane_lo) * scale
        qms.append(jnp.where(lane_lo, qn, 0.0).astype(BF16))
        qms.append(jnp.where(lane_lo, 0.0, qn).astype(BF16))

    heads = range(2 * n_pairs)

    def block(kb, mask, first):
        rs = pl.ds(pl.multiple_of(kb * bq, bq), bq)
        zs = [_nt_dot(qms[h], kn_scr[h // 2, rs, :]) for h in heads]
        sps = [_softplus(z) for z in zs]
        ls = [-sp if mask is None else jnp.where(mask, -sp, 0.0) for sp in sps]
        tails = [_dot(l.astype(BF16), later) for l in ls]
        top = None
        for h in heads:
            row = jnp.broadcast_to(jnp.sum(ls[h], axis=1, keepdims=True), (bq, LANES))
            e = zs[h] - sps[h] + tails[h]
            if not first:
                carry = car_scr[h]
                e = e + jnp.concatenate([carry] * (bq // LANES), axis=1)
                row = row + carry
            a = jnp.exp(e)
            if mask is not None:
                a = jnp.where(mask, a, 0.0)
            pv = _dot(a.astype(BF16), vb_scr[h // 2, rs, :])
            acc_scr[h] = pv if first else acc_scr[h] + pv
            car_scr[h] = row
            top = row if top is None else jnp.maximum(top, row)
        return jnp.max(top)

    top = block(i, before, True)

    def live(st):
        return jnp.logical_and(st[0] >= 0, st[1] > SB_ZERO_LOG)

    lax.while_loop(live, lambda st: (st[0] - 1, block(st[0], None, False)), (i - 1, top))
    for p in range(n_pairs):
        o_ref[:, p * LANES:(p + 1) * LANES] = jnp.where(
            lane_lo, acc_scr[2 * p], acc_scr[2 * p + 1]).astype(o_ref.dtype)


def _sb_call(proj, gq, gk, batch, seq, bq):
    m = proj.shape[0]
    nq = seq // bq
    n_pairs = SB_WIDTH // LANES
    q_spec = lambda p: pl.BlockSpec((bq, LANES), lambda b, i, p=p: (b * nq + i, SB_Q_BLK + p))
    k_spec = lambda blk, p: pl.BlockSpec((seq, LANES), lambda b, i, p=p: (b, blk + p))
    gain = pl.BlockSpec((1, LANES), lambda b, i: (0, 0))
    return pl.pallas_call(
        functools.partial(_sb_kernel, bq=bq, seq=seq),
        grid=(batch, nq),
        in_specs=([q_spec(p) for p in range(n_pairs)]
                  + [k_spec(SB_K_BLK, p) for p in range(n_pairs)]
                  + [k_spec(SB_V_BLK, p) for p in range(n_pairs)] + [gain, gain]),
        out_specs=pl.BlockSpec((bq, SB_WIDTH), lambda b, i: (b * nq + i, 0)),
        out_shape=jax.ShapeDtypeStruct((m, SB_WIDTH), BF16),
        scratch_shapes=[pltpu.VMEM((n_pairs, seq, LANES), BF16),
                        pltpu.VMEM((n_pairs, seq, LANES), BF16),
                        pltpu.VMEM((2 * n_pairs, bq, LANES), F32),
                        pltpu.VMEM((2 * n_pairs, bq, LANES), F32)],
        compiler_params=pltpu.CompilerParams(
            dimension_semantics=("arbitrary", "arbitrary"), vmem_limit_bytes=VMEM_LIMIT),
        name="stickbreak",
    )(*([proj] * (3 * n_pairs)), gq, gk)


def _aug_lanes(lane, base, first, second):
    out = jnp.zeros(lane.shape, F32)
    for n in range(3):
        out = jnp.where(lane == base + n, first[n], out)
        out = jnp.where(lane == base + 3 + n, second[n], out)
    return out


def _fox_kernel(q_ref, k_ref, v_ref, f_ref, fb_ref, gq_ref, gk_ref, o_ref,
                c_scr, qa_scr, ka_scr, vt_scr, sa_scr, sb_scr, *, bq, bk, seq):
    p_idx = pl.program_id(1)
    one = jnp.ones((1, 1), F32)
    scale = 1.0 / math.sqrt(HEAD_DIM)

    @pl.when(p_idx == 0)
    def _():
        r_i = lax.broadcasted_iota(jnp.int32, (bk, bk), 0)
        c_i = lax.broadcasted_iota(jnp.int32, (bk, bk), 1)
        upto = jnp.where(c_i <= r_i, 1.0, 0.0).astype(BF16)

        def body(n, run):
            rs = pl.ds(pl.multiple_of(n * bk, bk), bk)
            y = f_ref[rs, :] + fb_ref[...]
            lg = jnp.minimum(y, 0.0) - jnp.log(1.0 + jnp.exp(-jnp.abs(y)))
            hi, mid, lo = _split3(lg)
            c = _dot(upto, hi) + _dot(upto, mid) + _dot(upto, lo) + run
            c_scr[rs, :] = c
            return c[bk - 1:bk, :]

        lax.fori_loop(0, seq // bk, body, jnp.zeros((1, LANES), F32))

    lane = lax.broadcasted_iota(jnp.int32, (bk, LANES), 1)
    lane_lo = lane < HEAD_DIM
    row_lo_k = lax.broadcasted_iota(jnp.int32, (LANES, bk), 0) < HEAD_DIM

    def prep(n, carry):
        rs = pl.ds(pl.multiple_of(n * bk, bk), bk)
        kn = _pair_rms(k_ref[rs, :], gk_ref[...], lane_lo)
        qn = _pair_rms(q_ref[rs, :], gq_ref[...], lane_lo) * (scale * LOG2E)
        c = c_scr[rs, :] * LOG2E
        v_t = v_ref[rs, :].T
        for h in range(2):
            ch = jnp.sum(jnp.where(lane == 2 * p_idx + h, c, 0.0), axis=1, keepdims=True)
            parts = [x.astype(F32) for x in _split3(ch)]
            mine = lane_lo if h == 0 else jnp.logical_not(lane_lo)
            base = HEAD_DIM * (1 - h)
            k_aug = _aug_lanes(lane, base, (one, one, one), [-x for x in parts])
            q_aug = _aug_lanes(lane, base, parts, (one, one, one))
            ka_scr[h, rs, :] = jnp.where(mine, kn, k_aug).astype(BF16)
            qa_scr[h, rs, :] = jnp.where(mine, qn, q_aug).astype(BF16)
            vt_scr[h, n] = jnp.where(row_lo_k if h == 0 else jnp.logical_not(row_lo_k), v_t, 1.0).astype(BF16)
        return carry

    lax.fori_loop(0, seq // bk, prep, 0)

    key_minus_qry = (lax.broadcasted_iota(jnp.int32, (bk, bq), 0)
                     - lax.broadcasted_iota(jnp.int32, (bk, bq), 1))
    row_lo = lax.broadcasted_iota(jnp.int32, (LANES, bq), 0) < HEAD_DIM

    nq = seq // bq
    n_items = sum((qi * bq + bq - 1) // bk + 1 for qi in range(nq))
    assert n_items % 2 == 0
    no_mask = jnp.int32(bk)

    def unpack(item):
        qi = jnp.minimum(item[0], nq - 1)
        last = (qi * bq + bq - 1) // bk
        first = item[1] == 0
        kb = jnp.where(first, last, item[1] - 1)
        return qi, last, first, kb

    def advance(item):
        qi, last, _, _ = unpack(item)
        wrap = item[1] >= last
        return jnp.where(wrap, item[0] + 1, item[0]), jnp.where(wrap, 0, item[1] + 1)

    def scores_into(dst, item):
        qi, _, first, kb = unpack(item)
        qs = pl.ds(pl.multiple_of(qi * bq, bq), bq)
        rs = pl.ds(pl.multiple_of(kb * bk, bk), bk)
        limit = jnp.where(first, qi * bq - kb * bk, no_mask)
        for h in range(2):
            s = _nt_dot(ka_scr[h, rs, :], qa_scr[h, qs, :])
            dst[h] = jnp.where(key_minus_qry <= limit, s, NEG_BIG)

    def write_out(qi, sts):
        acc0, acc1 = sts[0][1], sts[1][1]
        o_t = jnp.where(row_lo, acc0 / acc0[HEAD_DIM:HEAD_DIM + 1, :], acc1 / acc1[0:1, :])
        o_ref[pl.ds(pl.multiple_of(qi * bq, bq), bq), :] = o_t.T.astype(o_ref.dtype)

    def absorb(src, item, prev_qi, sts):
        qi, _, first, kb = unpack(item)
        write_out(prev_qi, sts)
        out = []
        for h in range(2):
            m_run, acc = sts[h]
            m_run = jnp.where(first, NEG_BIG, m_run)
            s = src[h]
            m_new = jnp.maximum(m_run, jnp.max(s, axis=0, keepdims=True))
            p = jnp.exp2(s - m_new)
            acc = jnp.exp2(m_run - m_new) * acc + _dot(vt_scr[h, kb], p.astype(BF16))
            out.append((m_new, acc))
        return qi, tuple(out)

    def pair(u, ca):
        held, prev_qi, sts = ca
        nxt = advance(held)
        scores_into(sb_scr, nxt)
        prev_qi, sts = absorb(sa_scr, held, prev_qi, sts)
        held = advance(nxt)
        scores_into(sa_scr, held)
        prev_qi, sts = absorb(sb_scr, nxt, prev_qi, sts)
        return held, prev_qi, sts

    start = (jnp.int32(0), jnp.int32(0))
    init = (jnp.full((1, bq), NEG_BIG, F32), jnp.ones((LANES, bq), F32))
    scores_into(sa_scr, start)
    _, last_qi, sts = lax.fori_loop(0, n_items // 2, pair, (start, jnp.int32(0), (init, init)))
    write_out(last_qi, sts)


def _fox_call(proj, fbias, gq, gk, batch, seq, bq, bk):
    m = proj.shape[0]
    col = lambda blk: pl.BlockSpec((seq, LANES), lambda b, p, blk=blk: (b, blk + p))
    vec = pl.BlockSpec((1, LANES), lambda b, p: (0, 0))
    return pl.pallas_call(
        functools.partial(_fox_kernel, bq=bq, bk=bk, seq=seq),
        grid=(batch, FOX_WIDTH // LANES),
        in_specs=[col(FOX_Q_BLK), col(FOX_K_BLK), col(FOX_V_BLK),
                  pl.BlockSpec((seq, LANES), lambda b, p: (b, FOX_F_BLK)), vec, vec, vec],
        out_specs=pl.BlockSpec((seq, LANES), lambda b, p: (b, p)),
        out_shape=jax.ShapeDtypeStruct((m, FOX_WIDTH), BF16),
        scratch_shapes=[pltpu.VMEM((seq, LANES), F32),
                        pltpu.VMEM((2, seq, LANES), BF16),
                        pltpu.VMEM((2, seq, LANES), BF16),
                        pltpu.VMEM((2, seq // bk, LANES, bk), BF16),
                        pltpu.VMEM((2, bk, bq), F32),
                        pltpu.VMEM((2, bk, bq), F32)],
        compiler_params=pltpu.CompilerParams(
            dimension_semantics=("arbitrary", "arbitrary"), vmem_limit_bytes=VMEM_LIMIT),
        name="forgetting",
    )(proj, proj, proj, proj, fbias, gq, gk)


def _mlp_kernel(x_ref, ohg_ref, osb_ref, ofx_ref, wo1_ref, wo2_ref, wo3_ref, g2_ref,
                w1_ref, w2_ref, o_ref, *, tf):
    x1 = (x_ref[...] + _dot(ohg_ref[...], wo1_ref[...]) + _dot(osb_ref[...], wo2_ref[...])
          + _dot(ofx_ref[...], wo3_ref[...]))
    ms = jnp.mean(x1 * x1, axis=-1, keepdims=True)
    h2 = (x1 * lax.rsqrt(ms + EPS) * g2_ref[...]).astype(BF16)
    o_ref[...] = x1
    acc = None
    for f in range(w1_ref.shape[1] // tf):
        a = _dot(h2, w1_ref[:, f * tf:(f + 1) * tf])
        a = jnp.square(jnp.maximum(a, 0.0)).astype(BF16)
        y = _dot(a, w2_ref[f * tf:(f + 1) * tf, :])
        acc = y if acc is None else acc + y
    o_ref[...] = o_ref[...] + acc


def _mlp_call(x, ohg, osb, ofx, wo1, wo2, wo3, g2, w1, w2, tm, tf):
    m, d = x.shape
    row = lambda w: pl.BlockSpec((tm, w), lambda i: (i, 0))
    whole = lambda a: pl.BlockSpec(a.shape, lambda i: (0, 0))
    return pl.pallas_call(
        functools.partial(_mlp_kernel, tf=tf),
        grid=(m // tm,),
        in_specs=[row(d), row(HG_WIDTH), row(SB_WIDTH), row(FOX_WIDTH),
                  whole(wo1), whole(wo2), whole(wo3), whole(g2), whole(w1), whole(w2)],
        out_specs=row(d),
        out_shape=jax.ShapeDtypeStruct((m, d), F32),
        compiler_params=pltpu.CompilerParams(
            dimension_semantics=("arbitrary",), vmem_limit_bytes=VMEM_LIMIT),
        name="outproj_mlp",
    )(x, ohg, osb, ofx, wo1, wo2, wo3, g2, w1, w2)


def _tile_sizes(batch, seq):
    m = batch * seq
    return dict(
        proj_tm=min(512, m),
        hg_tb=min(512, seq),
        sb_bq=min(256, seq),
        fox_bq=min(256, seq),
        fox_bk=min(512, seq),
        mlp_tm=min(512, m),
        mlp_tf=1024,
    )


def kernel(x, lb_logits, norm1_g, w_in, hg_norm_g, sb_q_norm_g, sb_k_norm_g, fox_q_norm_g,
           fox_k_norm_g, fox_f_bias, w_out, norm2_g, w_ff1, w_ff2):
    batch, seq, d = x.shape
    assert d == D_MODEL and seq % HG_ROWS == 0 and x.dtype == F32
    ts = _tile_sizes(batch, seq)
    assert seq % ts["sb_bq"] == 0 and seq % ts["fox_bk"] == 0 and ts["fox_bk"] % ts["fox_bq"] == 0
    assert seq % ts["hg_tb"] == 0 and seq % PRO_ROWS == 0
    m = batch * seq
    xf = x.reshape(m, d)
    pair = lambda g: jnp.tile(g.astype(F32), 2)[None, :]

    for l in range(DEPTH):
        w_in_p = jnp.pad(w_in[l], ((0, 0), (0, IN_COLS_PAD - IN_COLS))).astype(BF16)
        proj = _proj_call(xf, norm1_g[l][None, :], w_in_p, ts["proj_tm"])

        o_hg = _hgrn_call(l, lb_logits.astype(F32), proj,
                          jnp.tile(hg_norm_g[l].astype(F32), HG_HEADS)[None, :], batch, seq, ts["hg_tb"])
        o_sb = _sb_call(proj, pair(sb_q_norm_g[l]), pair(sb_k_norm_g[l]), batch, seq, ts["sb_bq"])
        fbias = jnp.pad(fox_f_bias[l].astype(F32), (0, LANES - FOX_HEADS))[None, :]
        o_fx = _fox_call(proj, fbias, pair(fox_q_norm_g[l]), pair(fox_k_norm_g[l]), batch, seq,
                         ts["fox_bq"], ts["fox_bk"])

        wo = w_out[l].astype(BF16)
        xf = _mlp_call(xf, o_hg, o_sb, o_fx,
                       wo[:HG_WIDTH], wo[HG_WIDTH:HG_WIDTH + SB_WIDTH], wo[HG_WIDTH + SB_WIDTH:],
                       norm2_g[l][None, :], w_ff1[l].astype(BF16), w_ff2[l].astype(BF16),
                       ts["mlp_tm"], ts["mlp_tf"])
    return xf.reshape(batch, seq, d)
```

```python
import functools
import math

import jax
import jax.numpy as jnp
from jax import lax
from jax.experimental import pallas as pl
from jax.experimental.pallas import tpu as pltpu

F32 = jnp.float32
BF16 = jnp.bfloat16

D_MODEL = 1024
DEPTH = 4
HEAD_DIM = 64
HG_HEADS = 4
HG_KW = 256
HG_WIDTH = 256
SB_WIDTH = 384
FOX_WIDTH = 384
FOX_HEADS = 6
IN_COLS = 4 * 256 + 3 * 384 + 3 * 384 + FOX_HEADS
D_FF = 4 * D_MODEL
EPS = 1e-6
LB_FLOOR = 1e-30
NEG_BIG = -1e30
LOG2E = math.log2(math.e)

LANES = 128
IN_COLS_PAD = 27 * LANES
SB_Q_BLK, SB_K_BLK, SB_V_BLK = 8, 11, 14
FOX_Q_BLK, FOX_K_BLK, FOX_V_BLK, FOX_F_BLK = 17, 20, 23, 26

SB_ZERO_LOG = -105.0
SUB = 16
HG_ROWS = 128
VMEM_LIMIT = 56 * 1024 * 1024


def _nt_dot(a, b):
    return lax.dot_general(a, b, (((1,), (1,)), ((), ())), preferred_element_type=F32)


def _dot(a, b):
    return jnp.dot(a, b, preferred_element_type=F32)


def _split2(x):
    hi = x.astype(BF16)
    lo = (x - hi.astype(F32)).astype(BF16)
    return hi, lo


def _split3(x):
    hi = x.astype(BF16)
    r1 = x - hi.astype(F32)
    mid = r1.astype(BF16)
    lo = (r1 - mid.astype(F32)).astype(BF16)
    return hi, mid, lo


def _softplus(z):
    return jnp.maximum(z, 0.0) + jnp.log(1.0 + jnp.exp(-jnp.abs(z)))


def _softplus2(z2):
    return jnp.maximum(z2, 0.0) + jnp.log2(1.0 + jnp.exp2(-jnp.abs(z2)))


def _proj_kernel(x_ref, g_ref, w_ref, o_ref):
    x = x_ref[...]
    ms = jnp.mean(x * x, axis=-1, keepdims=True)
    h = (x * lax.rsqrt(ms + EPS) * g_ref[...]).astype(BF16)
    o_ref[...] = _dot(h, w_ref[...])


def _proj_call(x, g, w, tm):
    m, d = x.shape
    n = w.shape[1]
    return pl.pallas_call(
        _proj_kernel,
        grid=(m // tm,),
        in_specs=[
            pl.BlockSpec((tm, d), lambda i: (i, 0)),
            pl.BlockSpec((1, d), lambda i: (0, 0)),
            pl.BlockSpec((d, n), lambda i: (0, 0)),
        ],
        out_specs=pl.BlockSpec((tm, n), lambda i: (i, 0)),
        out_shape=jax.ShapeDtypeStruct((m, n), F32),
        compiler_params=pltpu.CompilerParams(
            dimension_semantics=("arbitrary",), vmem_limit_bytes=VMEM_LIMIT),
        name="proj",
    )(x, g, w)


def _hgrn_kernel(lbl_ref, q_ref, f_ref, i_ref, g_ref, gn_ref, o_ref, st_scr, oi_scr, *, layer, tb):
    @pl.when(pl.program_id(1) == 0)
    def _():
        st_scr[...] = jnp.zeros_like(st_scr)

    rows = [lbl_ref[j:j + 1, :] for j in range(DEPTH)]
    mx = functools.reduce(jnp.maximum, rows)
    ex = [jnp.exp(r - mx) for r in rows]
    den = functools.reduce(lambda a, b: a + b, ex)
    lb = jnp.zeros_like(mx)
    for j in range(1, layer + 1):
        lb = lb + ex[j] / den
    log_lb = jnp.log(jnp.maximum(lb, LB_FLOOR))
    one_m_lb = 1.0 - lb
    log_one_m_lb = jnp.log(one_m_lb)
    gn = gn_ref[...]

    r_i = lax.broadcasted_iota(jnp.int32, (HG_KW, HG_KW), 0)
    c_i = lax.broadcasted_iota(jnp.int32, (HG_KW, HG_KW), 1)
    same_head = (r_i >> 6) == (c_i >> 6)
    ones_bd = jnp.where(same_head, 1.0, 0.0).astype(BF16)
    t_mod = lax.broadcasted_iota(jnp.int32, (HG_ROWS, HG_KW), 0) & (SUB - 1)
    lane_sub = lax.broadcasted_iota(jnp.int32, (HG_KW, HG_ROWS), 1) >> 4

    def chunk(c, carry):
        rs = pl.ds(pl.multiple_of(c * HG_ROWS, HG_ROWS), HG_ROWS)
        q = q_ref[rs, :]
        fl = f_ref[rs, :]
        v = i_ref[rs, :]
        g = g_ref[rs, :]

        sp = _softplus(fl)
        lf = jnp.maximum(fl, log_lb) + jnp.log(1.0 + jnp.exp(-jnp.abs(fl - log_lb))) - sp
        kk = one_m_lb * jnp.exp(-sp)

        b = lf
        suf = lf
        s = 1
        while s < SUB:
            b = b + jnp.where(t_mod >= s, pltpu.roll(b, s, 0), 0.0)
            suf = suf + jnp.where(t_mod < SUB - s, pltpu.roll(suf, HG_ROWS - s, 0), 0.0)
            s *= 2
        rest = suf - lf
        beta = b + rest

        b2 = b * LOG2E
        w2 = (log_one_m_lb - sp) * LOG2E - b2
        acc = _dot((q * kk).astype(BF16), ones_bd) * v
        for d in range(1, SUB):
            e = jnp.where(t_mod >= d, b2 + pltpu.roll(w2, d, 0), NEG_BIG)
            p = q * jnp.exp2(e)
            acc = acc + _dot(p.astype(BF16), ones_bd) * pltpu.roll(v, d, 0)

        qt = (q * jnp.exp2(b2)).astype(BF16)
        kt = (kk * jnp.exp(rest)).astype(BF16)
        v_t = v.T
        for j in range(HG_ROWS // SUB):
            st = st_scr[...]
            oi_scr[j * SUB:(j + 1) * SUB, :] = _nt_dot(qt[j * SUB:(j + 1) * SUB, :], st.astype(BF16))
            vm = jnp.where(lane_sub == j, v_t, 0.0).astype(BF16)
            ut = _dot(vm, kt)
            dec = jnp.exp(beta[j * SUB:j * SUB + 1, :])
            st_scr[...] = st * dec + jnp.where(same_head, ut, 0.0)

        o = acc + oi_scr[...]
        hi, lo = _split2(o * o)
        msq = (_dot(hi, ones_bd) + _dot(lo, ones_bd)) * (1.0 / HEAD_DIM)
        y = o * lax.rsqrt(msq + EPS) * gn
        o_ref[rs, :] = (y * (g / (1.0 + jnp.exp(-g)))).astype(o_ref.dtype)
        return carry

    lax.fori_loop(0, tb // HG_ROWS, chunk, 0)


def _hgrn_call(layer, lb_logits, proj, gn, batch, seq, tb):
    m = proj.shape[0]
    nt = seq // tb

    def col(j):
        return pl.BlockSpec((tb, HG_KW), lambda b, t, j=j: (b * nt + t, j))

    return pl.pallas_call(
        functools.partial(_hgrn_kernel, layer=layer, tb=tb),
        grid=(batch, nt),
        in_specs=[pl.BlockSpec((DEPTH, HG_KW), lambda b, t: (0, 0)),
                  col(0), col(1), col(2), col(3),
                  pl.BlockSpec((1, HG_WIDTH), lambda b, t: (0, 0))],
        out_specs=pl.BlockSpec((tb, HG_WIDTH), lambda b, t: (b * nt + t, 0)),
        out_shape=jax.ShapeDtypeStruct((m, HG_WIDTH), BF16),
        scratch_shapes=[pltpu.VMEM((HG_WIDTH, HG_KW), F32),
                        pltpu.VMEM((HG_ROWS, HG_WIDTH), F32)],
        compiler_params=pltpu.CompilerParams(
            dimension_semantics=("arbitrary", "arbitrary"), vmem_limit_bytes=VMEM_LIMIT),
        name="hgrn2",
    )(lb_logits, proj, proj, proj, proj, gn)


def _pair_rms(x, gain, lane_lo):
    x2 = x * x
    s0 = jnp.sum(jnp.where(lane_lo, x2, 0.0), axis=1, keepdims=True)
    s1 = jnp.sum(jnp.where(lane_lo, 0.0, x2), axis=1, keepdims=True)
    ms = jnp.where(lane_lo, s0, s1) * (1.0 / HEAD_DIM)
    return x * lax.rsqrt(ms + EPS) * gain


PRO_ROWS = 512


def _sb_kernel(q0_ref, q1_ref, q2_ref, k0_ref, k1_ref, k2_ref, v0_ref, v1_ref, v2_ref,
               gq_ref, gk_ref, o_ref, kn_scr, vb_scr, acc_scr, car_scr, *, bq, seq):
    i = pl.program_id(1)
    q_refs = (q0_ref, q1_ref, q2_ref)
    k_refs = (k0_ref, k1_ref, k2_ref)
    v_refs = (v0_ref, v1_ref, v2_ref)
    n_pairs = len(q_refs)

    @pl.when(i == 0)
    def _():
        lane_lo = lax.broadcasted_iota(jnp.int32, (PRO_ROWS, LANES), 1) < HEAD_DIM

        def body(c, carry):
            rs = pl.ds(pl.multiple_of(c * PRO_ROWS, PRO_ROWS), PRO_ROWS)
            for p in range(n_pairs):
                kn_scr[p, rs, :] = _pair_rms(k_refs[p][rs, :], gk_ref[...], lane_lo).astype(BF16)
                vb_scr[p, rs, :] = v_refs[p][rs, :].astype(BF16)
            return carry

        lax.fori_loop(0, seq // PRO_ROWS, body, 0)

    lane_lo = lax.broadcasted_iota(jnp.int32, (bq, LANES), 1) < HEAD_DIM
    scale = 1.0 / math.sqrt(HEAD_DIM)
    r_i = lax.broadcasted_iota(jnp.int32, (bq, bq), 0)
    c_i = lax.broadcasted_iota(jnp.int32, (bq, bq), 1)
    before = c_i < r_i
    later = jnp.where(r_i > c_i, 1.0, 0.0).astype(BF16)

    qms = []
    for p in range(n_pairs):
        qn = _pair_rms(q_refs[p][...], gq_ref[...], lane_lo) * (scale * LOG2E)
        qms.append(jnp.where(lane_lo, qn, 0.0).astype(BF16))
        qms.append(jnp.where(lane_lo, 0.0, qn).astype(BF16))

    heads = range(2 * n_pairs)

    def block(kb, mask, first):
        rs = pl.ds(pl.multiple_of(kb * bq, bq), bq)
        zs = [_nt_dot(qms[h], kn_scr[h // 2, rs, :]) for h in heads]
        sps = [_softplus2(z) for z in zs]
        us = sps if mask is None else [jnp.where(mask, sp, 0.0) for sp in sps]
        tails = [_dot(u.astype(BF16), later) for u in us]
        low = None
        for h in heads:
            row = jnp.broadcast_to(jnp.sum(us[h], axis=1, keepdims=True), (bq, LANES))
            e = zs[h] - sps[h] - tails[h]
            if not first:
                carry = car_scr[h]
                e = e - jnp.concatenate([carry] * (bq // LANES), axis=1)
                row = row + carry
            a = jnp.exp2(e)
            if mask is not None:
                a = jnp.where(mask, a, 0.0)
            pv = _dot(a.astype(BF16), vb_scr[h // 2, rs, :])
            acc_scr[h] = pv if first else acc_scr[h] + pv
            car_scr[h] = row
            low = row if low is None else jnp.minimum(low, row)
        return jnp.min(low)

    low = block(i, before, True)

    def live(st):
        return jnp.logical_and(st[0] >= 0, st[1] < -SB_ZERO_LOG * LOG2E)

    lax.while_loop(live, lambda st: (st[0] - 1, block(st[0], None, False)), (i - 1, low))
    for p in range(n_pairs):
        o_ref[:, p * LANES:(p + 1) * LANES] = jnp.where(
            lane_lo, acc_scr[2 * p], acc_scr[2 * p + 1]).astype(o_ref.dtype)


def _sb_call(proj, gq, gk, batch, seq, bq):
    m = proj.shape[0]
    nq = seq // bq
    n_pairs = SB_WIDTH // LANES
    q_spec = lambda p: pl.BlockSpec((bq, LANES), lambda b, i, p=p: (b * nq + i, SB_Q_BLK + p))
    k_spec = lambda blk, p: pl.BlockSpec((seq, LANES), lambda b, i, p=p: (b, blk + p))
    gain = pl.BlockSpec((1, LANES), lambda b, i: (0, 0))
    return pl.pallas_call(
        functools.partial(_sb_kernel, bq=bq, seq=seq),
        grid=(batch, nq),
        in_specs=([q_spec(p) for p in range(n_pairs)]
                  + [k_spec(SB_K_BLK, p) for p in range(n_pairs)]
                  + [k_spec(SB_V_BLK, p) for p in range(n_pairs)] + [gain, gain]),
        out_specs=pl.BlockSpec((bq, SB_WIDTH), lambda b, i: (b * nq + i, 0)),
        out_shape=jax.ShapeDtypeStruct((m, SB_WIDTH), BF16),
        scratch_shapes=[pltpu.VMEM((n_pairs, seq, LANES), BF16),
                        pltpu.VMEM((n_pairs, seq, LANES), BF16),
                        pltpu.VMEM((2 * n_pairs, bq, LANES), F32),
                        pltpu.VMEM((2 * n_pairs, bq, LANES), F32)],
        compiler_params=pltpu.CompilerParams(
            dimension_semantics=("arbitrary", "arbitrary"), vmem_limit_bytes=VMEM_LIMIT),
        name="stickbreak",
    )(*([proj] * (3 * n_pairs)), gq, gk)


def _aug_lanes(lane, base, first, second):
    out = jnp.zeros(lane.shape, F32)
    for n in range(3):
        out = jnp.where(lane == base + n, first[n], out)
        out = jnp.where(lane == base + 3 + n, second[n], out)
    return out


def _fox_kernel(q_ref, k_ref, v_ref, f_ref, fb_ref, gq_ref, gk_ref, o_ref,
                c_scr, qa_scr, ka_scr, vt_scr, sa_scr, sb_scr, *, bq, bk, seq):
    p_idx = pl.program_id(1)
    one = jnp.ones((1, 1), F32)
    scale = 1.0 / math.sqrt(HEAD_DIM)

    @pl.when(p_idx == 0)
    def _():
        r_i = lax.broadcasted_iota(jnp.int32, (bk, bk), 0)
        c_i = lax.broadcasted_iota(jnp.int32, (bk, bk), 1)
        upto = jnp.where(c_i <= r_i, 1.0, 0.0).astype(BF16)

        def body(n, run):
            rs = pl.ds(pl.multiple_of(n * bk, bk), bk)
            y = f_ref[rs, :] + fb_ref[...]
            lg = jnp.minimum(y, 0.0) - jnp.log(1.0 + jnp.exp(-jnp.abs(y)))
            hi, mid, lo = _split3(lg)
            c = _dot(upto, hi) + _dot(upto, mid) + _dot(upto, lo) + run
            c_scr[rs, :] = c
            return c[bk - 1:bk, :]

        lax.fori_loop(0, seq // bk, body, jnp.zeros((1, LANES), F32))

    lane = lax.broadcasted_iota(jnp.int32, (bk, LANES), 1)
    lane_lo = lane < HEAD_DIM
    row_lo_k = lax.broadcasted_iota(jnp.int32, (LANES, bk), 0) < HEAD_DIM

    def prep(n, carry):
        rs = pl.ds(pl.multiple_of(n * bk, bk), bk)
        kn = _pair_rms(k_ref[rs, :], gk_ref[...], lane_lo)
        qn = _pair_rms(q_ref[rs, :], gq_ref[...], lane_lo) * (scale * LOG2E)
        c = c_scr[rs, :] * LOG2E
        v_t = v_ref[rs, :].T
        for h in range(2):
            ch = jnp.sum(jnp.where(lane == 2 * p_idx + h, c, 0.0), axis=1, keepdims=True)
            parts = [x.astype(F32) for x in _split3(ch)]
            mine = lane_lo if h == 0 else jnp.logical_not(lane_lo)
            base = HEAD_DIM * (1 - h)
            k_aug = _aug_lanes(lane, base, (one, one, one), [-x for x in parts])
            q_aug = _aug_lanes(lane, base, parts, (one, one, one))
            ka_scr[h, rs, :] = jnp.where(mine, kn, k_aug).astype(BF16)
            qa_scr[h, rs, :] = jnp.where(mine, qn, q_aug).astype(BF16)
            vt_scr[h, n] = jnp.where(row_lo_k if h == 0 else jnp.logical_not(row_lo_k), v_t, 1.0).astype(BF16)
        return carry

    lax.fori_loop(0, seq // bk, prep, 0)

    key_minus_qry = (lax.broadcasted_iota(jnp.int32, (bk, bq), 0)
                     - lax.broadcasted_iota(jnp.int32, (bk, bq), 1))
    row_lo = lax.broadcasted_iota(jnp.int32, (LANES, bq), 0) < HEAD_DIM

    nq = seq // bq
    n_items = sum((qi * bq + bq - 1) // bk + 1 for qi in range(nq))
    assert n_items % 2 == 0
    no_mask = jnp.int32(bk)

    def unpack(item):
        qi = jnp.minimum(item[0], nq - 1)
        last = (qi * bq + bq - 1) // bk
        first = item[1] == 0
        kb = jnp.where(first, last, item[1] - 1)
        return qi, last, first, kb

    def advance(item):
        qi, last, _, _ = unpack(item)
        wrap = item[1] >= last
        return jnp.where(wrap, item[0] + 1, item[0]), jnp.where(wrap, 0, item[1] + 1)

    def scores_into(dst, item):
        qi, _, first, kb = unpack(item)
        qs = pl.ds(pl.multiple_of(qi * bq, bq), bq)
        rs = pl.ds(pl.multiple_of(kb * bk, bk), bk)
        limit = jnp.where(first, qi * bq - kb * bk, no_mask)
        for h in range(2):
            s = _nt_dot(ka_scr[h, rs, :], qa_scr[h, qs, :])
            dst[h] = jnp.where(key_minus_qry <= limit, s, NEG_BIG)

    def write_out(qi, sts):
        acc0, acc1 = sts[0][1], sts[1][1]
        o_t = jnp.where(row_lo, acc0 / acc0[HEAD_DIM:HEAD_DIM + 1, :], acc1 / acc1[0:1, :])
        o_ref[pl.ds(pl.multiple_of(qi * bq, bq), bq), :] = o_t.T.astype(o_ref.dtype)

    def absorb(src, item, prev_qi, sts):
        qi, _, first, kb = unpack(item)
        write_out(prev_qi, sts)
        out = []
        for h in range(2):
            m_run, acc = sts[h]
            m_run = jnp.where(first, NEG_BIG, m_run)
            s = src[h]
            m_new = jnp.maximum(m_run, jnp.max(s, axis=0, keepdims=True))
            p = jnp.exp2(s - m_new)
            acc = jnp.exp2(m_run - m_new) * acc + _dot(vt_scr[h, kb], p.astype(BF16))
            out.append((m_new, acc))
        return qi, tuple(out)

    unroll = 4 if n_items % 4 == 0 else 2
    bufs = (sa_scr, sb_scr)

    def group(u, ca):
        held, prev_qi, sts = ca
        for k in range(unroll):
            nxt = advance(held)
            scores_into(bufs[(k + 1) % 2], nxt)
            prev_qi, sts = absorb(bufs[k % 2], held, prev_qi, sts)
            held = nxt
        return held, prev_qi, sts

    start = (jnp.int32(0), jnp.int32(0))
    init = (jnp.full((1, bq), NEG_BIG, F32), jnp.ones((LANES, bq), F32))
    scores_into(sa_scr, start)
    _, last_qi, sts = lax.fori_loop(0, n_items // unroll, group, (start, jnp.int32(0), (init, init)))
    write_out(last_qi, sts)


def _fox_call(proj, fbias, gq, gk, batch, seq, bq, bk):
    m = proj.shape[0]
    col = lambda blk: pl.BlockSpec((seq, LANES), lambda b, p, blk=blk: (b, blk + p))
    vec = pl.BlockSpec((1, LANES), lambda b, p: (0, 0))
    return pl.pallas_call(
        functools.partial(_fox_kernel, bq=bq, bk=bk, seq=seq),
        grid=(batch, FOX_WIDTH // LANES),
        in_specs=[col(FOX_Q_BLK), col(FOX_K_BLK), col(FOX_V_BLK),
                  pl.BlockSpec((seq, LANES), lambda b, p: (b, FOX_F_BLK)), vec, vec, vec],
        out_specs=pl.BlockSpec((seq, LANES), lambda b, p: (b, p)),
        out_shape=jax.ShapeDtypeStruct((m, FOX_WIDTH), BF16),
        scratch_shapes=[pltpu.VMEM((seq, LANES), F32),
                        pltpu.VMEM((2, seq, LANES), BF16),
                        pltpu.VMEM((2, seq, LANES), BF16),
                        pltpu.VMEM((2, seq // bk, LANES, bk), BF16),
                        pltpu.VMEM((2, bk, bq), F32),
                        pltpu.VMEM((2, bk, bq), F32)],
        compiler_params=pltpu.CompilerParams(
            dimension_semantics=("arbitrary", "arbitrary"), vmem_limit_bytes=VMEM_LIMIT),
        name="forgetting",
    )(proj, proj, proj, proj, fbias, gq, gk)


def _mlp_kernel(x_ref, ohg_ref, osb_ref, ofx_ref, wo1_ref, wo2_ref, wo3_ref, g2_ref,
                w1_ref, w2_ref, o_ref, *, tf):
    x1 = (x_ref[...] + _dot(ohg_ref[...], wo1_ref[...]) + _dot(osb_ref[...], wo2_ref[...])
          + _dot(ofx_ref[...], wo3_ref[...]))
    ms = jnp.mean(x1 * x1, axis=-1, keepdims=True)
    h2 = (x1 * lax.rsqrt(ms + EPS) * g2_ref[...]).astype(BF16)
    o_ref[...] = x1
    acc = None
    for f in range(w1_ref.shape[1] // tf):
        a = _dot(h2, w1_ref[:, f * tf:(f + 1) * tf])
        a = jnp.square(jnp.maximum(a, 0.0)).astype(BF16)
        y = _dot(a, w2_ref[f * tf:(f + 1) * tf, :])
        acc = y if acc is None else acc + y
    o_ref[...] = o_ref[...] + acc


def _mlp_call(x, ohg, osb, ofx, wo1, wo2, wo3, g2, w1, w2, tm, tf):
    m, d = x.shape
    row = lambda w: pl.BlockSpec((tm, w), lambda i: (i, 0))
    whole = lambda a: pl.BlockSpec(a.shape, lambda i: (0, 0))
    return pl.pallas_call(
        functools.partial(_mlp_kernel, tf=tf),
        grid=(m // tm,),
        in_specs=[row(d), row(HG_WIDTH), row(SB_WIDTH), row(FOX_WIDTH),
                  whole(wo1), whole(wo2), whole(wo3), whole(g2), whole(w1), whole(w2)],
        out_specs=row(d),
        out_shape=jax.ShapeDtypeStruct((m, d), F32),
        compiler_params=pltpu.CompilerParams(
            dimension_semantics=("arbitrary",), vmem_limit_bytes=VMEM_LIMIT),
        name="outproj_mlp",
    )(x, ohg, osb, ofx, wo1, wo2, wo3, g2, w1, w2)


def _tile_sizes(batch, seq):
    m = batch * seq
    return dict(
        proj_tm=min(512, m),
        hg_tb=min(512, seq),
        sb_bq=min(256, seq),
        fox_bq=min(256, seq),
        fox_bk=min(512, seq),
        mlp_tm=min(512, m),
        mlp_tf=1024,
    )


def kernel(x, lb_logits, norm1_g, w_in, hg_norm_g, sb_q_norm_g, sb_k_norm_g, fox_q_norm_g,
           fox_k_norm_g, fox_f_bias, w_out, norm2_g, w_ff1, w_ff2):
    batch, seq, d = x.shape
    assert d == D_MODEL and seq % HG_ROWS == 0 and x.dtype == F32
    ts = _tile_sizes(batch, seq)
    assert seq % ts["sb_bq"] == 0 and seq % ts["fox_bk"] == 0 and ts["fox_bk"] % ts["fox_bq"] == 0
    assert seq % ts["hg_tb"] == 0 and seq % PRO_ROWS == 0
    m = batch * seq
    xf = x.reshape(m, d)
    pair = lambda g: jnp.tile(g.astype(F32), 2)[None, :]

    for l in range(DEPTH):
        w_in_p = jnp.pad(w_in[l], ((0, 0), (0, IN_COLS_PAD - IN_COLS))).astype(BF16)
        proj = _proj_call(xf, norm1_g[l][None, :], w_in_p, ts["proj_tm"])

        o_hg = _hgrn_call(l, lb_logits.astype(F32), proj,
                          jnp.tile(hg_norm_g[l].astype(F32), HG_HEADS)[None, :], batch, seq, ts["hg_tb"])
        o_sb = _sb_call(proj, pair(sb_q_norm_g[l]), pair(sb_k_norm_g[l]), batch, seq, ts["sb_bq"])
        fbias = jnp.pad(fox_f_bias[l].astype(F32), (0, LANES - FOX_HEADS))[None, :]
        o_fx = _fox_call(proj, fbias, pair(fox_q_norm_g[l]), pair(fox_k_norm_g[l]), batch, seq,
                         ts["fox_bq"], ts["fox_bk"])

        wo = w_out[l].astype(BF16)
        xf = _mlp_call(xf, o_hg, o_sb, o_fx,
                       wo[:HG_WIDTH], wo[HG_WIDTH:HG_WIDTH + SB_WIDTH], wo[HG_WIDTH + SB_WIDTH:],
                       norm2_g[l][None, :], w_ff1[l].astype(BF16), w_ff2[l].astype(BF16),
                       ts["mlp_tm"], ts["mlp_tf"])
    return xf.reshape(batch, seq, d)
```

```python
import functools
import math

import jax
import jax.numpy as jnp
from jax import lax
from jax.experimental import pallas as pl
from jax.experimental.pallas import tpu as pltpu

F32 = jnp.float32
BF16 = jnp.bfloat16

D_MODEL = 1024
DEPTH = 4
HEAD_DIM = 64
HG_HEADS = 4
HG_KW = 256
HG_WIDTH = 256
SB_WIDTH = 384
FOX_WIDTH = 384
FOX_HEADS = 6
IN_COLS = 4 * 256 + 3 * 384 + 3 * 384 + FOX_HEADS
D_FF = 4 * D_MODEL
EPS = 1e-6
LB_FLOOR = 1e-30
NEG_BIG = -1e30
LOG2E = math.log2(math.e)

LANES = 128
IN_COLS_PAD = 27 * LANES
SB_Q_BLK, SB_K_BLK, SB_V_BLK = 8, 11, 14
FOX_Q_BLK, FOX_K_BLK, FOX_V_BLK, FOX_F_BLK = 17, 20, 23, 26

SB_ZERO_LOG = -105.0
SUB = 32
HG_ROWS = 128
VMEM_LIMIT = 56 * 1024 * 1024


def _nt_dot(a, b):
    return lax.dot_general(a, b, (((1,), (1,)), ((), ())), preferred_element_type=F32)


def _dot(a, b):
    return jnp.dot(a, b, preferred_element_type=F32)


def _split2(x):
    hi = x.astype(BF16)
    lo = (x - hi.astype(F32)).astype(BF16)
    return hi, lo


def _split3(x):
    hi = x.astype(BF16)
    r1 = x - hi.astype(F32)
    mid = r1.astype(BF16)
    lo = (r1 - mid.astype(F32)).astype(BF16)
    return hi, mid, lo


def _softplus(z):
    return jnp.maximum(z, 0.0) + jnp.log(1.0 + jnp.exp(-jnp.abs(z)))


def _softplus2(z2):
    return jnp.maximum(z2, 0.0) + jnp.log2(1.0 + jnp.exp2(-jnp.abs(z2)))


def _proj_kernel(x_ref, g_ref, w_ref, o_ref):
    x = x_ref[...]
    ms = jnp.mean(x * x, axis=-1, keepdims=True)
    h = (x * lax.rsqrt(ms + EPS) * g_ref[...]).astype(BF16)
    o_ref[...] = _dot(h, w_ref[...])


def _layer(l, *shape):
    return pl.BlockSpec((None,) + shape, lambda *_: (l,) + (0,) * len(shape))


def _proj_call(x, g, w, l, tm):
    m, d = x.shape
    n = w.shape[2]
    return pl.pallas_call(
        _proj_kernel,
        grid=(m // tm,),
        in_specs=[
            pl.BlockSpec((tm, d), lambda i: (i, 0)),
            _layer(l, 1, d),
            _layer(l, d, n),
        ],
        out_specs=pl.BlockSpec((tm, n), lambda i: (i, 0)),
        out_shape=jax.ShapeDtypeStruct((m, n), F32),
        compiler_params=pltpu.CompilerParams(
            dimension_semantics=("arbitrary",), vmem_limit_bytes=VMEM_LIMIT),
        name="proj",
    )(x, g, w)


def _hgrn_kernel(lbl_ref, q_ref, f_ref, i_ref, g_ref, gn_ref, o_ref, st_scr, oi_scr, *, layer, tb):
    @pl.when(pl.program_id(1) == 0)
    def _():
        st_scr[...] = jnp.zeros_like(st_scr)

    rows = [lbl_ref[j:j + 1, :] for j in range(DEPTH)]
    mx = functools.reduce(jnp.maximum, rows)
    ex = [jnp.exp(r - mx) for r in rows]
    den = functools.reduce(lambda a, b: a + b, ex)
    lb = jnp.zeros_like(mx)
    for j in range(1, layer + 1):
        lb = lb + ex[j] / den
    log_lb = jnp.log(jnp.maximum(lb, LB_FLOOR))
    one_m_lb = 1.0 - lb
    log_one_m_lb = jnp.log(one_m_lb)
    gn = gn_ref[...]

    r_i = lax.broadcasted_iota(jnp.int32, (HG_KW, HG_KW), 0)
    c_i = lax.broadcasted_iota(jnp.int32, (HG_KW, HG_KW), 1)
    same_head = (r_i >> 6) == (c_i >> 6)
    ones_bd = jnp.where(same_head, 1.0, 0.0).astype(BF16)
    t_mod = lax.broadcasted_iota(jnp.int32, (HG_ROWS, HG_KW), 0) & (SUB - 1)
    lane_sub = lax.broadcasted_iota(jnp.int32, (HG_KW, HG_ROWS), 1) // SUB

    def chunk(c, carry):
        rs = pl.ds(pl.multiple_of(c * HG_ROWS, HG_ROWS), HG_ROWS)
        q = q_ref[rs, :]
        fl = f_ref[rs, :]
        v = i_ref[rs, :]
        g = g_ref[rs, :]

        sp = _softplus(fl)
        lf = jnp.maximum(fl, log_lb) + jnp.log(1.0 + jnp.exp(-jnp.abs(fl - log_lb))) - sp
        kk = one_m_lb * jnp.exp(-sp)

        b = lf
        suf = lf
        s = 1
        while s < SUB:
            b = b + jnp.where(t_mod >= s, pltpu.roll(b, s, 0), 0.0)
            suf = suf + jnp.where(t_mod < SUB - s, pltpu.roll(suf, HG_ROWS - s, 0), 0.0)
            s *= 2
        rest = suf - lf
        beta = b + rest

        b2 = b * LOG2E
        w2 = (log_one_m_lb - sp) * LOG2E - b2
        acc = _dot((q * kk).astype(BF16), ones_bd) * v
        for d in range(1, SUB):
            e = jnp.where(t_mod >= d, b2 + pltpu.roll(w2, d, 0), NEG_BIG)
            p = q * jnp.exp2(e)
            acc = acc + _dot(p.astype(BF16), ones_bd) * pltpu.roll(v, d, 0)

        qt = (q * jnp.exp2(b2)).astype(BF16)
        kt = (kk * jnp.exp(rest)).astype(BF16)
        v_t = v.T
        for j in range(HG_ROWS // SUB):
            st = st_scr[...]
            oi_scr[j * SUB:(j + 1) * SUB, :] = _nt_dot(qt[j * SUB:(j + 1) * SUB, :], st.astype(BF16))
            vm = jnp.where(lane_sub == j, v_t, 0.0).astype(BF16)
            ut = _dot(vm, kt)
            dec = jnp.exp(beta[j * SUB:j * SUB + 1, :])
            st_scr[...] = st * dec + jnp.where(same_head, ut, 0.0)

        o = acc + oi_scr[...]
        hi, lo = _split2(o * o)
        msq = (_dot(hi, ones_bd) + _dot(lo, ones_bd)) * (1.0 / HEAD_DIM)
        y = o * lax.rsqrt(msq + EPS) * gn
        o_ref[rs, :] = (y * (g / (1.0 + jnp.exp(-g)))).astype(o_ref.dtype)
        return carry

    lax.fori_loop(0, tb // HG_ROWS, chunk, 0)


def _hgrn_call(layer, lb_logits, proj, gn, batch, seq, tb):
    m = proj.shape[0]
    nt = seq // tb

    def col(j):
        return pl.BlockSpec((tb, HG_KW), lambda b, t, j=j: (b * nt + t, j))

    return pl.pallas_call(
        functools.partial(_hgrn_kernel, layer=layer, tb=tb),
        grid=(batch, nt),
        in_specs=[pl.BlockSpec((DEPTH, HG_KW), lambda b, t: (0, 0)),
                  col(0), col(1), col(2), col(3), _layer(layer, 1, HG_WIDTH)],
        out_specs=pl.BlockSpec((tb, HG_WIDTH), lambda b, t: (b * nt + t, 0)),
        out_shape=jax.ShapeDtypeStruct((m, HG_WIDTH), BF16),
        scratch_shapes=[pltpu.VMEM((HG_WIDTH, HG_KW), F32),
                        pltpu.VMEM((HG_ROWS, HG_WIDTH), F32)],
        compiler_params=pltpu.CompilerParams(
            dimension_semantics=("arbitrary", "arbitrary"), vmem_limit_bytes=VMEM_LIMIT),
        name="hgrn2",
    )(lb_logits, proj, proj, proj, proj, gn)


def _pair_rms(x, gain, lane_lo):
    x2 = x * x
    s0 = jnp.sum(jnp.where(lane_lo, x2, 0.0), axis=1, keepdims=True)
    s1 = jnp.sum(jnp.where(lane_lo, 0.0, x2), axis=1, keepdims=True)
    ms = jnp.where(lane_lo, s0, s1) * (1.0 / HEAD_DIM)
    return x * lax.rsqrt(ms + EPS) * gain


PRO_ROWS = 512


def _sb_kernel(q0_ref, q1_ref, q2_ref, k0_ref, k1_ref, k2_ref, v0_ref, v1_ref, v2_ref,
               gq_ref, gk_ref, o_ref, kn_scr, vb_scr, acc_scr, car_scr, *, bq, seq):
    i = pl.program_id(1)
    q_refs = (q0_ref, q1_ref, q2_ref)
    k_refs = (k0_ref, k1_ref, k2_ref)
    v_refs = (v0_ref, v1_ref, v2_ref)
    n_pairs = len(q_refs)

    @pl.when(i == 0)
    def _():
        lane_lo = lax.broadcasted_iota(jnp.int32, (PRO_ROWS, LANES), 1) < HEAD_DIM

        def body(c, carry):
            rs = pl.ds(pl.multiple_of(c * PRO_ROWS, PRO_ROWS), PRO_ROWS)
            for p in range(n_pairs):
                kn_scr[p, rs, :] = _pair_rms(k_refs[p][rs, :], gk_ref[...], lane_lo).astype(BF16)
                vb_scr[p, rs, :] = v_refs[p][rs, :].astype(BF16)
            return carry

        lax.fori_loop(0, seq // PRO_ROWS, body, 0)

    lane_lo = lax.broadcasted_iota(jnp.int32, (bq, LANES), 1) < HEAD_DIM
    scale = 1.0 / math.sqrt(HEAD_DIM)
    r_i = lax.broadcasted_iota(jnp.int32, (bq, bq), 0)
    c_i = lax.broadcasted_iota(jnp.int32, (bq, bq), 1)
    before = c_i < r_i
    later = jnp.where(r_i > c_i, 1.0, 0.0).astype(BF16)

    qms = []
    for p in range(n_pairs):
        qn = _pair_rms(q_refs[p][...], gq_ref[...], lane_lo) * (scale * LOG2E)
        qms.append(jnp.where(lane_lo, qn, 0.0).astype(BF16))
        qms.append(jnp.where(lane_lo, 0.0, qn).astype(BF16))

    heads = range(2 * n_pairs)

    def block(kb, mask, first):
        rs = pl.ds(pl.multiple_of(kb * bq, bq), bq)
        zs = [_nt_dot(qms[h], kn_scr[h // 2, rs, :]) for h in heads]
        sps = [_softplus2(z) for z in zs]
        us = sps if mask is None else [jnp.where(mask, sp, 0.0) for sp in sps]
        tails = [_dot(u.astype(BF16), later) for u in us]
        low = None
        for h in heads:
            row = jnp.broadcast_to(jnp.sum(us[h], axis=1, keepdims=True), (bq, LANES))
            e = zs[h] - sps[h] - tails[h]
            if not first:
                carry = car_scr[h]
                e = e - jnp.concatenate([carry] * (bq // LANES), axis=1)
                row = row + carry
            a = jnp.exp2(e)
            if mask is not None:
                a = jnp.where(mask, a, 0.0)
            pv = _dot(a.astype(BF16), vb_scr[h // 2, rs, :])
            acc_scr[h] = pv if first else acc_scr[h] + pv
            car_scr[h] = row
            low = row if low is None else jnp.minimum(low, row)
        return jnp.min(low)

    low = block(i, before, True)

    def live(st):
        return jnp.logical_and(st[0] >= 0, st[1] < -SB_ZERO_LOG * LOG2E)

    lax.while_loop(live, lambda st: (st[0] - 1, block(st[0], None, False)), (i - 1, low))
    for p in range(n_pairs):
        o_ref[:, p * LANES:(p + 1) * LANES] = jnp.where(
            lane_lo, acc_scr[2 * p], acc_scr[2 * p + 1]).astype(o_ref.dtype)


def _sb_call(proj, gq, gk, l, batch, seq, bq):
    m = proj.shape[0]
    nq = seq // bq
    n_pairs = SB_WIDTH // LANES
    q_spec = lambda p: pl.BlockSpec((bq, LANES), lambda b, i, p=p: (b * nq + i, SB_Q_BLK + p))
    k_spec = lambda blk, p: pl.BlockSpec((seq, LANES), lambda b, i, p=p: (b, blk + p))
    gain = _layer(l, 1, LANES)
    return pl.pallas_call(
        functools.partial(_sb_kernel, bq=bq, seq=seq),
        grid=(batch, nq),
        in_specs=([q_spec(p) for p in range(n_pairs)]
                  + [k_spec(SB_K_BLK, p) for p in range(n_pairs)]
                  + [k_spec(SB_V_BLK, p) for p in range(n_pairs)] + [gain, gain]),
        out_specs=pl.BlockSpec((bq, SB_WIDTH), lambda b, i: (b * nq + i, 0)),
        out_shape=jax.ShapeDtypeStruct((m, SB_WIDTH), BF16),
        scratch_shapes=[pltpu.VMEM((n_pairs, seq, LANES), BF16),
                        pltpu.VMEM((n_pairs, seq, LANES), BF16),
                        pltpu.VMEM((2 * n_pairs, bq, LANES), F32),
                        pltpu.VMEM((2 * n_pairs, bq, LANES), F32)],
        compiler_params=pltpu.CompilerParams(
            dimension_semantics=("arbitrary", "arbitrary"), vmem_limit_bytes=VMEM_LIMIT),
        name="stickbreak",
    )(*([proj] * (3 * n_pairs)), gq, gk)


def _aug_lanes(lane, base, first, second):
    out = jnp.zeros(lane.shape, F32)
    for n in range(3):
        out = jnp.where(lane == base + n, first[n], out)
        out = jnp.where(lane == base + 3 + n, second[n], out)
    return out


def _fox_kernel(q_ref, k_ref, v_ref, f_ref, fb_ref, gq_ref, gk_ref, o_ref,
                c_scr, qa_scr, ka_scr, vt_scr, sa_scr, sb_scr, *, bq, bk, seq):
    p_idx = pl.program_id(1)
    one = jnp.ones((1, 1), F32)
    scale = 1.0 / math.sqrt(HEAD_DIM)

    @pl.when(p_idx == 0)
    def _():
        r_i = lax.broadcasted_iota(jnp.int32, (bk, bk), 0)
        c_i = lax.broadcasted_iota(jnp.int32, (bk, bk), 1)
        upto = jnp.where(c_i <= r_i, 1.0, 0.0).astype(BF16)

        def body(n, run):
            rs = pl.ds(pl.multiple_of(n * bk, bk), bk)
            y = f_ref[rs, :] + fb_ref[...]
            lg = jnp.minimum(y, 0.0) - jnp.log(1.0 + jnp.exp(-jnp.abs(y)))
            hi, mid, lo = _split3(lg)
            c = _dot(upto, hi) + _dot(upto, mid) + _dot(upto, lo) + run
            c_scr[rs, :] = c
            return c[bk - 1:bk, :]

        lax.fori_loop(0, seq // bk, body, jnp.zeros((1, LANES), F32))

    lane = lax.broadcasted_iota(jnp.int32, (bk, LANES), 1)
    lane_lo = lane < HEAD_DIM
    row_lo_k = lax.broadcasted_iota(jnp.int32, (LANES, bk), 0) < HEAD_DIM

    def prep(n, carry):
        rs = pl.ds(pl.multiple_of(n * bk, bk), bk)
        kn = _pair_rms(k_ref[rs, :], gk_ref[...], lane_lo)
        qn = _pair_rms(q_ref[rs, :], gq_ref[...], lane_lo) * (scale * LOG2E)
        c = c_scr[rs, :] * LOG2E
        v_t = v_ref[rs, :].T
        for h in range(2):
            ch = jnp.sum(jnp.where(lane == 2 * p_idx + h, c, 0.0), axis=1, keepdims=True)
            parts = [x.astype(F32) for x in _split3(ch)]
            mine = lane_lo if h == 0 else jnp.logical_not(lane_lo)
            base = HEAD_DIM * (1 - h)
            k_aug = _aug_lanes(lane, base, (one, one, one), [-x for x in parts])
            q_aug = _aug_lanes(lane, base, parts, (one, one, one))
            ka_scr[h, rs, :] = jnp.where(mine, kn, k_aug).astype(BF16)
            qa_scr[h, rs, :] = jnp.where(mine, qn, q_aug).astype(BF16)
            vt_scr[h, n] = jnp.where(row_lo_k if h == 0 else jnp.logical_not(row_lo_k), v_t, 1.0).astype(BF16)
        return carry

    lax.fori_loop(0, seq // bk, prep, 0)

    key_minus_qry = (lax.broadcasted_iota(jnp.int32, (bk, bq), 0)
                     - lax.broadcasted_iota(jnp.int32, (bk, bq), 1))
    row_lo = lax.broadcasted_iota(jnp.int32, (LANES, bq), 0) < HEAD_DIM

    nq = seq // bq
    n_items = sum((qi * bq + bq - 1) // bk + 1 for qi in range(nq))
    assert n_items % 2 == 0
    no_mask = jnp.int32(bk)

    def unpack(item):
        qi = jnp.minimum(item[0], nq - 1)
        last = (qi * bq + bq - 1) // bk
        first = item[1] == 0
        kb = jnp.where(first, last, item[1] - 1)
        return qi, last, first, kb

    def advance(item):
        qi, last, _, _ = unpack(item)
        wrap = item[1] >= last
        return jnp.where(wrap, item[0] + 1, item[0]), jnp.where(wrap, 0, item[1] + 1)

    def scores_into(dst, item):
        qi, _, first, kb = unpack(item)
        qs = pl.ds(pl.multiple_of(qi * bq, bq), bq)
        rs = pl.ds(pl.multiple_of(kb * bk, bk), bk)
        limit = jnp.where(first, qi * bq - kb * bk, no_mask)
        for h in range(2):
            s = _nt_dot(ka_scr[h, rs, :], qa_scr[h, qs, :])
            dst[h] = jnp.where(key_minus_qry <= limit, s, NEG_BIG)

    def write_out(qi, sts):
        acc0, acc1 = sts[0][1], sts[1][1]
        o_t = jnp.where(row_lo, acc0 / acc0[HEAD_DIM:HEAD_DIM + 1, :], acc1 / acc1[0:1, :])
        o_ref[pl.ds(pl.multiple_of(qi * bq, bq), bq), :] = o_t.T.astype(o_ref.dtype)

    def absorb(src, item, prev_qi, sts):
        qi, _, first, kb = unpack(item)
        write_out(prev_qi, sts)
        out = []
        for h in range(2):
            m_run, acc = sts[h]
            m_run = jnp.where(first, NEG_BIG, m_run)
            s = src[h]
            m_new = jnp.maximum(m_run, jnp.max(s, axis=0, keepdims=True))
            p = jnp.exp2(s - m_new)
            acc = jnp.exp2(m_run - m_new) * acc + _dot(vt_scr[h, kb], p.astype(BF16))
            out.append((m_new, acc))
        return qi, tuple(out)

    unroll = 4 if n_items % 4 == 0 else 2
    bufs = (sa_scr, sb_scr)

    def group(u, ca):
        held, prev_qi, sts = ca
        for k in range(unroll):
            nxt = advance(held)
            scores_into(bufs[(k + 1) % 2], nxt)
            prev_qi, sts = absorb(bufs[k % 2], held, prev_qi, sts)
            held = nxt
        return held, prev_qi, sts

    start = (jnp.int32(0), jnp.int32(0))
    init = (jnp.full((1, bq), NEG_BIG, F32), jnp.ones((LANES, bq), F32))
    scores_into(sa_scr, start)
    _, last_qi, sts = lax.fori_loop(0, n_items // unroll, group, (start, jnp.int32(0), (init, init)))
    write_out(last_qi, sts)


def _fox_call(proj, fbias, gq, gk, l, batch, seq, bq, bk):
    m = proj.shape[0]
    col = lambda blk: pl.BlockSpec((seq, LANES), lambda b, p, blk=blk: (b, blk + p))
    vec = _layer(l, 1, LANES)
    return pl.pallas_call(
        functools.partial(_fox_kernel, bq=bq, bk=bk, seq=seq),
        grid=(batch, FOX_WIDTH // LANES),
        in_specs=[col(FOX_Q_BLK), col(FOX_K_BLK), col(FOX_V_BLK),
                  pl.BlockSpec((seq, LANES), lambda b, p: (b, FOX_F_BLK)), vec, vec, vec],
        out_specs=pl.BlockSpec((seq, LANES), lambda b, p: (b, p)),
        out_shape=jax.ShapeDtypeStruct((m, FOX_WIDTH), BF16),
        scratch_shapes=[pltpu.VMEM((seq, LANES), F32),
                        pltpu.VMEM((2, seq, LANES), BF16),
                        pltpu.VMEM((2, seq, LANES), BF16),
                        pltpu.VMEM((2, seq // bk, LANES, bk), BF16),
                        pltpu.VMEM((2, bk, bq), F32),
                        pltpu.VMEM((2, bk, bq), F32)],
        compiler_params=pltpu.CompilerParams(
            dimension_semantics=("arbitrary", "arbitrary"), vmem_limit_bytes=VMEM_LIMIT),
        name="forgetting",
    )(proj, proj, proj, proj, fbias, gq, gk)


def _mlp_kernel(x_ref, ohg_ref, osb_ref, ofx_ref, wo1_ref, wo2_ref, wo3_ref, g2_ref,
                w1_ref, w2_ref, o_ref, *, tf):
    x1 = (x_ref[...] + _dot(ohg_ref[...], wo1_ref[...]) + _dot(osb_ref[...], wo2_ref[...])
          + _dot(ofx_ref[...], wo3_ref[...]))
    ms = jnp.mean(x1 * x1, axis=-1, keepdims=True)
    h2 = (x1 * lax.rsqrt(ms + EPS) * g2_ref[...]).astype(BF16)
    o_ref[...] = x1
    acc = None
    for f in range(w1_ref.shape[1] // tf):
        a = _dot(h2, w1_ref[:, f * tf:(f + 1) * tf])
        a = jnp.square(jnp.maximum(a, 0.0)).astype(BF16)
        y = _dot(a, w2_ref[f * tf:(f + 1) * tf, :])
        acc = y if acc is None else acc + y
    o_ref[...] = o_ref[...] + acc


def _mlp_call(x, ohg, osb, ofx, wo1, wo2, wo3, g2, w1, w2, l, tm, tf):
    m, d = x.shape
    row = lambda w: pl.BlockSpec((tm, w), lambda i: (i, 0))
    whole = lambda a: _layer(l, *a.shape[1:])
    return pl.pallas_call(
        functools.partial(_mlp_kernel, tf=tf),
        grid=(m // tm,),
        in_specs=[row(d), row(HG_WIDTH), row(SB_WIDTH), row(FOX_WIDTH),
                  whole(wo1), whole(wo2), whole(wo3), whole(g2), whole(w1), whole(w2)],
        out_specs=row(d),
        out_shape=jax.ShapeDtypeStruct((m, d), F32),
        compiler_params=pltpu.CompilerParams(
            dimension_semantics=("arbitrary",), vmem_limit_bytes=VMEM_LIMIT),
        name="outproj_mlp",
    )(x, ohg, osb, ofx, wo1, wo2, wo3, g2, w1, w2)


def _tile_sizes(batch, seq):
    m = batch * seq
    return dict(
        proj_tm=min(512, m),
        hg_tb=min(512, seq),
        sb_bq=min(256, seq),
        fox_bq=min(256, seq),
        fox_bk=min(512, seq),
        mlp_tm=min(512, m),
        mlp_tf=1024,
    )


def kernel(x, lb_logits, norm1_g, w_in, hg_norm_g, sb_q_norm_g, sb_k_norm_g, fox_q_norm_g,
           fox_k_norm_g, fox_f_bias, w_out, norm2_g, w_ff1, w_ff2):
    batch, seq, d = x.shape
    assert d == D_MODEL and seq % HG_ROWS == 0 and x.dtype == F32
    ts = _tile_sizes(batch, seq)
    assert seq % ts["sb_bq"] == 0 and seq % ts["fox_bk"] == 0 and ts["fox_bk"] % ts["fox_bq"] == 0
    assert seq % ts["hg_tb"] == 0 and seq % PRO_ROWS == 0
    m = batch * seq
    xf = x.reshape(m, d)
    row = lambda a: a.astype(F32)[:, None, :]
    pair = lambda g: row(jnp.tile(g, (1, 2)))
    w_in_p = jnp.pad(w_in, ((0, 0), (0, 0), (0, IN_COLS_PAD - IN_COLS))).astype(BF16)
    wo = w_out.astype(BF16)
    wo_hg, wo_sb, wo_fx = wo[:, :HG_WIDTH], wo[:, HG_WIDTH:HG_WIDTH + SB_WIDTH], wo[:, HG_WIDTH + SB_WIDTH:]
    w1, w2 = w_ff1.astype(BF16), w_ff2.astype(BF16)
    g1, g2, g_hg = row(norm1_g), row(norm2_g), row(jnp.tile(hg_norm_g, (1, HG_HEADS)))
    g_sq, g_sk, g_fq, g_fk = pair(sb_q_norm_g), pair(sb_k_norm_g), pair(fox_q_norm_g), pair(fox_k_norm_g)
    fbias = row(jnp.pad(fox_f_bias, ((0, 0), (0, LANES - FOX_HEADS))))
    lbl = lb_logits.astype(F32)

    for l in range(DEPTH):
        proj = _proj_call(xf, g1, w_in_p, l, ts["proj_tm"])
        o_hg = _hgrn_call(l, lbl, proj, g_hg, batch, seq, ts["hg_tb"])
        o_sb = _sb_call(proj, g_sq, g_sk, l, batch, seq, ts["sb_bq"])
        o_fx = _fox_call(proj, fbias, g_fq, g_fk, l, batch, seq, ts["fox_bq"], ts["fox_bk"])
        xf = _mlp_call(xf, o_hg, o_sb, o_fx, wo_hg, wo_sb, wo_fx, g2, w1, w2, l,
                       ts["mlp_tm"], ts["mlp_tf"])
    return xf.reshape(batch, seq, d)
```

```python
import functools
import math

import jax
import jax.numpy as jnp
from jax import lax
from jax.experimental import pallas as pl
from jax.experimental.pallas import tpu as pltpu

F32 = jnp.float32
BF16 = jnp.bfloat16

D_MODEL = 1024
DEPTH = 4
HEAD_DIM = 64
HG_HEADS = 4
HG_KW = 256
HG_WIDTH = 256
SB_WIDTH = 384
FOX_WIDTH = 384
FOX_HEADS = 6
IN_COLS = 4 * 256 + 3 * 384 + 3 * 384 + FOX_HEADS
D_FF = 4 * D_MODEL
EPS = 1e-6
LB_FLOOR = 1e-30
NEG_BIG = -1e30
LOG2E = math.log2(math.e)

LANES = 128
IN_COLS_PAD = 27 * LANES
SB_Q_BLK, SB_K_BLK, SB_V_BLK = 8, 11, 14
FOX_Q_BLK, FOX_K_BLK, FOX_V_BLK, FOX_F_BLK = 17, 20, 23, 26

SB_ZERO_LOG = -105.0
SUB = 16
HG_ROWS = 256
VMEM_LIMIT = 56 * 1024 * 1024


def _nt_dot(a, b):
    return lax.dot_general(a, b, (((1,), (1,)), ((), ())), preferred_element_type=F32)


def _dot(a, b):
    return jnp.dot(a, b, preferred_element_type=F32)


def _split2(x):
    hi = x.astype(BF16)
    lo = (x - hi.astype(F32)).astype(BF16)
    return hi, lo


def _split3(x):
    hi = x.astype(BF16)
    r1 = x - hi.astype(F32)
    mid = r1.astype(BF16)
    lo = (r1 - mid.astype(F32)).astype(BF16)
    return hi, mid, lo


def _softplus(z):
    return jnp.maximum(z, 0.0) + jnp.log(1.0 + jnp.exp(-jnp.abs(z)))


def _softplus2(z2):
    return jnp.maximum(z2, 0.0) + jnp.log2(1.0 + jnp.exp2(-jnp.abs(z2)))


def _proj_kernel(x_ref, g_ref, w_ref, o_ref):
    x = x_ref[...]
    ms = jnp.mean(x * x, axis=-1, keepdims=True)
    h = (x * lax.rsqrt(ms + EPS) * g_ref[...]).astype(BF16)
    o_ref[...] = _dot(h, w_ref[...])


def _layer(l, *shape):
    return pl.BlockSpec((None,) + shape, lambda *_: (l,) + (0,) * len(shape))


def _proj_call(x, g, w, l, tm):
    m, d = x.shape
    n = w.shape[2]
    return pl.pallas_call(
        _proj_kernel,
        grid=(m // tm,),
        in_specs=[
            pl.BlockSpec((tm, d), lambda i: (i, 0)),
            _layer(l, 1, d),
            _layer(l, d, n),
        ],
        out_specs=pl.BlockSpec((tm, n), lambda i: (i, 0)),
        out_shape=jax.ShapeDtypeStruct((m, n), F32),
        compiler_params=pltpu.CompilerParams(
            dimension_semantics=("arbitrary",), vmem_limit_bytes=VMEM_LIMIT),
        name="proj",
    )(x, g, w)


def _hgrn_kernel(lbl_ref, q_ref, f_ref, i_ref, g_ref, gn_ref, o_ref, st_scr, oi_scr, *, layer, tb):
    @pl.when(pl.program_id(1) == 0)
    def _():
        st_scr[...] = jnp.zeros_like(st_scr)

    rows = [lbl_ref[j:j + 1, :] for j in range(DEPTH)]
    mx = functools.reduce(jnp.maximum, rows)
    ex = [jnp.exp(r - mx) for r in rows]
    den = functools.reduce(lambda a, b: a + b, ex)
    lb = jnp.zeros_like(mx)
    for j in range(1, layer + 1):
        lb = lb + ex[j] / den
    log_lb = jnp.log(jnp.maximum(lb, LB_FLOOR))
    one_m_lb = 1.0 - lb
    log_one_m_lb = jnp.log(one_m_lb)
    gn = gn_ref[...]

    r_i = lax.broadcasted_iota(jnp.int32, (HG_KW, HG_KW), 0)
    c_i = lax.broadcasted_iota(jnp.int32, (HG_KW, HG_KW), 1)
    same_head = (r_i >> 6) == (c_i >> 6)
    ones_bd = jnp.where(same_head, 1.0, 0.0).astype(BF16)
    t_mod = lax.broadcasted_iota(jnp.int32, (HG_ROWS, HG_KW), 0) & (SUB - 1)
    lane_sub = lax.broadcasted_iota(jnp.int32, (HG_KW, HG_ROWS), 1) // SUB
    t_r = lax.broadcasted_iota(jnp.int32, (HG_ROWS, HG_ROWS), 0)
    t_c = lax.broadcasted_iota(jnp.int32, (HG_ROWS, HG_ROWS), 1)
    in_sub = (t_r // SUB) == (t_c // SUB)
    same_sub = jnp.where(in_sub, 1.0, 0.0).astype(BF16)
    upto_sub = jnp.where(jnp.logical_and(in_sub, t_c <= t_r), 1.0, 0.0).astype(BF16)

    def chunk(c, carry):
        rs = pl.ds(pl.multiple_of(c * HG_ROWS, HG_ROWS), HG_ROWS)
        q = q_ref[rs, :]
        fl = f_ref[rs, :]
        v = i_ref[rs, :]
        g = g_ref[rs, :]

        sp = _softplus(fl)
        lf = jnp.maximum(fl, log_lb) + jnp.log(1.0 + jnp.exp(-jnp.abs(fl - log_lb))) - sp
        kk = one_m_lb * jnp.exp(-sp)

        parts = _split3(lf)
        b = functools.reduce(lambda x, y: x + y, [_dot(upto_sub, part) for part in parts])
        beta = functools.reduce(lambda x, y: x + y, [_dot(same_sub, part) for part in parts])
        rest = beta - b

        b2 = b * LOG2E
        w2 = (log_one_m_lb - sp) * LOG2E - b2
        acc = _dot((q * kk).astype(BF16), ones_bd) * v
        for d in range(1, SUB):
            e = jnp.where(t_mod >= d, b2 + pltpu.roll(w2, d, 0), NEG_BIG)
            p = q * jnp.exp2(e)
            acc = acc + _dot(p.astype(BF16), ones_bd) * pltpu.roll(v, d, 0)

        qt = (q * jnp.exp2(b2)).astype(BF16)
        kt = (kk * jnp.exp(rest)).astype(BF16)
        v_t = v.T
        for j in range(HG_ROWS // SUB):
            st = st_scr[...]
            oi_scr[j * SUB:(j + 1) * SUB, :] = _nt_dot(qt[j * SUB:(j + 1) * SUB, :], st.astype(BF16))
            vm = jnp.where(lane_sub == j, v_t, 0.0).astype(BF16)
            ut = _dot(vm, kt)
            dec = jnp.exp(beta[j * SUB:j * SUB + 1, :])
            st_scr[...] = st * dec + jnp.where(same_head, ut, 0.0)

        o = acc + oi_scr[...]
        hi, lo = _split2(o * o)
        msq = (_dot(hi, ones_bd) + _dot(lo, ones_bd)) * (1.0 / HEAD_DIM)
        y = o * lax.rsqrt(msq + EPS) * gn
        o_ref[rs, :] = (y * (g / (1.0 + jnp.exp(-g)))).astype(o_ref.dtype)
        return carry

    lax.fori_loop(0, tb // HG_ROWS, chunk, 0)


def _hgrn_call(layer, lb_logits, proj, gn, batch, seq, tb):
    m = proj.shape[0]
    nt = seq // tb

    def col(j):
        return pl.BlockSpec((tb, HG_KW), lambda b, t, j=j: (b * nt + t, j))

    return pl.pallas_call(
        functools.partial(_hgrn_kernel, layer=layer, tb=tb),
        grid=(batch, nt),
        in_specs=[pl.BlockSpec((DEPTH, HG_KW), lambda b, t: (0, 0)),
                  col(0), col(1), col(2), col(3), _layer(layer, 1, HG_WIDTH)],
        out_specs=pl.BlockSpec((tb, HG_WIDTH), lambda b, t: (b * nt + t, 0)),
        out_shape=jax.ShapeDtypeStruct((m, HG_WIDTH), BF16),
        scratch_shapes=[pltpu.VMEM((HG_WIDTH, HG_KW), F32),
                        pltpu.VMEM((HG_ROWS, HG_WIDTH), F32)],
        compiler_params=pltpu.CompilerParams(
            dimension_semantics=("arbitrary", "arbitrary"), vmem_limit_bytes=VMEM_LIMIT),
        name="hgrn2",
    )(lb_logits, proj, proj, proj, proj, gn)


def _pair_rms(x, gain, lane_lo):
    x2 = x * x
    s0 = jnp.sum(jnp.where(lane_lo, x2, 0.0), axis=1, keepdims=True)
    s1 = jnp.sum(jnp.where(lane_lo, 0.0, x2), axis=1, keepdims=True)
    ms = jnp.where(lane_lo, s0, s1) * (1.0 / HEAD_DIM)
    return x * lax.rsqrt(ms + EPS) * gain


PRO_ROWS = 512


def _sb_kernel(q0_ref, q1_ref, q2_ref, k0_ref, k1_ref, k2_ref, v0_ref, v1_ref, v2_ref,
               gq_ref, gk_ref, o_ref, kn_scr, vb_scr, acc_scr, car_scr, *, bq, seq):
    i = pl.program_id(1)
    q_refs = (q0_ref, q1_ref, q2_ref)
    k_refs = (k0_ref, k1_ref, k2_ref)
    v_refs = (v0_ref, v1_ref, v2_ref)
    n_pairs = len(q_refs)

    @pl.when(i == 0)
    def _():
        lane_lo = lax.broadcasted_iota(jnp.int32, (PRO_ROWS, LANES), 1) < HEAD_DIM

        def body(c, carry):
            rs = pl.ds(pl.multiple_of(c * PRO_ROWS, PRO_ROWS), PRO_ROWS)
            for p in range(n_pairs):
                kn_scr[p, rs, :] = _pair_rms(k_refs[p][rs, :], gk_ref[...], lane_lo).astype(BF16)
                vb_scr[p, rs, :] = v_refs[p][rs, :].astype(BF16)
            return carry

        lax.fori_loop(0, seq // PRO_ROWS, body, 0)

    lane_lo = lax.broadcasted_iota(jnp.int32, (bq, LANES), 1) < HEAD_DIM
    scale = 1.0 / math.sqrt(HEAD_DIM)
    r_i = lax.broadcasted_iota(jnp.int32, (bq, bq), 0)
    c_i = lax.broadcasted_iota(jnp.int32, (bq, bq), 1)
    before = c_i < r_i
    later = jnp.where(r_i > c_i, 1.0, 0.0).astype(BF16)

    qms = []
    for p in range(n_pairs):
        qn = _pair_rms(q_refs[p][...], gq_ref[...], lane_lo) * (scale * LOG2E)
        qms.append(jnp.where(lane_lo, qn, 0.0).astype(BF16))
        qms.append(jnp.where(lane_lo, 0.0, qn).astype(BF16))

    heads = range(2 * n_pairs)

    def block(kb, mask, first):
        rs = pl.ds(pl.multiple_of(kb * bq, bq), bq)
        zs = [_nt_dot(qms[h], kn_scr[h // 2, rs, :]) for h in heads]
        sps = [_softplus2(z) for z in zs]
        us = sps if mask is None else [jnp.where(mask, sp, 0.0) for sp in sps]
        tails = [_dot(u.astype(BF16), later) for u in us]
        low = None
        for h in heads:
            row = jnp.broadcast_to(jnp.sum(us[h], axis=1, keepdims=True), (bq, LANES))
            e = zs[h] - sps[h] - tails[h]
            if not first:
                carry = car_scr[h]
                e = e - jnp.concatenate([carry] * (bq // LANES), axis=1)
                row = row + carry
            a = jnp.exp2(e)
            if mask is not None:
                a = jnp.where(mask, a, 0.0)
            pv = _dot(a.astype(BF16), vb_scr[h // 2, rs, :])
            acc_scr[h] = pv if first else acc_scr[h] + pv
            car_scr[h] = row
            low = row if low is None else jnp.minimum(low, row)
        return jnp.min(low)

    low = block(i, before, True)

    def live(st):
        return jnp.logical_and(st[0] >= 0, st[1] < -SB_ZERO_LOG * LOG2E)

    lax.while_loop(live, lambda st: (st[0] - 1, block(st[0], None, False)), (i - 1, low))
    for p in range(n_pairs):
        o_ref[:, p * LANES:(p + 1) * LANES] = jnp.where(
            lane_lo, acc_scr[2 * p], acc_scr[2 * p + 1]).astype(o_ref.dtype)


def _sb_call(proj, gq, gk, l, batch, seq, bq):
    m = proj.shape[0]
    nq = seq // bq
    n_pairs = SB_WIDTH // LANES
    q_spec = lambda p: pl.BlockSpec((bq, LANES), lambda b, i, p=p: (b * nq + i, SB_Q_BLK + p))
    k_spec = lambda blk, p: pl.BlockSpec((seq, LANES), lambda b, i, p=p: (b, blk + p))
    gain = _layer(l, 1, LANES)
    return pl.pallas_call(
        functools.partial(_sb_kernel, bq=bq, seq=seq),
        grid=(batch, nq),
        in_specs=([q_spec(p) for p in range(n_pairs)]
                  + [k_spec(SB_K_BLK, p) for p in range(n_pairs)]
                  + [k_spec(SB_V_BLK, p) for p in range(n_pairs)] + [gain, gain]),
        out_specs=pl.BlockSpec((bq, SB_WIDTH), lambda b, i: (b * nq + i, 0)),
        out_shape=jax.ShapeDtypeStruct((m, SB_WIDTH), BF16),
        scratch_shapes=[pltpu.VMEM((n_pairs, seq, LANES), BF16),
                        pltpu.VMEM((n_pairs, seq, LANES), BF16),
                        pltpu.VMEM((2 * n_pairs, bq, LANES), F32),
                        pltpu.VMEM((2 * n_pairs, bq, LANES), F32)],
        compiler_params=pltpu.CompilerParams(
            dimension_semantics=("arbitrary", "arbitrary"), vmem_limit_bytes=VMEM_LIMIT),
        name="stickbreak",
    )(*([proj] * (3 * n_pairs)), gq, gk)


def _aug_lanes(lane, base, first, second):
    out = jnp.zeros(lane.shape, F32)
    for n in range(3):
        out = jnp.where(lane == base + n, first[n], out)
        out = jnp.where(lane == base + 3 + n, second[n], out)
    return out


def _fox_kernel(q_ref, k_ref, v_ref, f_ref, fb_ref, gq_ref, gk_ref, o_ref,
                c_scr, qa_scr, ka_scr, vt_scr, sa_scr, sb_scr, *, bq, bk, seq):
    p_idx = pl.program_id(1)
    one = jnp.ones((1, 1), F32)
    scale = 1.0 / math.sqrt(HEAD_DIM)

    @pl.when(p_idx == 0)
    def _():
        r_i = lax.broadcasted_iota(jnp.int32, (bk, bk), 0)
        c_i = lax.broadcasted_iota(jnp.int32, (bk, bk), 1)
        upto = jnp.where(c_i <= r_i, 1.0, 0.0).astype(BF16)

        def body(n, run):
            rs = pl.ds(pl.multiple_of(n * bk, bk), bk)
            y = f_ref[rs, :] + fb_ref[...]
            lg = jnp.minimum(y, 0.0) - jnp.log(1.0 + jnp.exp(-jnp.abs(y)))
            hi, mid, lo = _split3(lg)
            c = _dot(upto, hi) + _dot(upto, mid) + _dot(upto, lo) + run
            c_scr[rs, :] = c
            return c[bk - 1:bk, :]

        lax.fori_loop(0, seq // bk, body, jnp.zeros((1, LANES), F32))

    lane = lax.broadcasted_iota(jnp.int32, (bk, LANES), 1)
    lane_lo = lane < HEAD_DIM
    row_lo_k = lax.broadcasted_iota(jnp.int32, (LANES, bk), 0) < HEAD_DIM

    def prep(n, carry):
        rs = pl.ds(pl.multiple_of(n * bk, bk), bk)
        kn = _pair_rms(k_ref[rs, :], gk_ref[...], lane_lo)
        qn = _pair_rms(q_ref[rs, :], gq_ref[...], lane_lo) * (scale * LOG2E)
        c = c_scr[rs, :] * LOG2E
        v_t = v_ref[rs, :].T
        for h in range(2):
            ch = jnp.sum(jnp.where(lane == 2 * p_idx + h, c, 0.0), axis=1, keepdims=True)
            parts = [x.astype(F32) for x in _split3(ch)]
            mine = lane_lo if h == 0 else jnp.logical_not(lane_lo)
            base = HEAD_DIM * (1 - h)
            k_aug = _aug_lanes(lane, base, (one, one, one), [-x for x in parts])
            q_aug = _aug_lanes(lane, base, parts, (one, one, one))
            ka_scr[h, rs, :] = jnp.where(mine, kn, k_aug).astype(BF16)
            qa_scr[h, rs, :] = jnp.where(mine, qn, q_aug).astype(BF16)
            vt_scr[h, n] = jnp.where(row_lo_k if h == 0 else jnp.logical_not(row_lo_k), v_t, 1.0).astype(BF16)
        return carry

    lax.fori_loop(0, seq // bk, prep, 0)

    key_minus_qry = (lax.broadcasted_iota(jnp.int32, (bk, bq), 0)
                     - lax.broadcasted_iota(jnp.int32, (bk, bq), 1))
    row_lo = lax.broadcasted_iota(jnp.int32, (LANES, bq), 0) < HEAD_DIM

    nq = seq // bq
    n_items = sum((qi * bq + bq - 1) // bk + 1 for qi in range(nq))
    assert n_items % 2 == 0
    no_mask = jnp.int32(bk)

    def unpack(item):
        qi = jnp.minimum(item[0], nq - 1)
        last = (qi * bq + bq - 1) // bk
        first = item[1] == 0
        kb = jnp.where(first, last, item[1] - 1)
        return qi, last, first, kb

    def advance(item):
        qi, last, _, _ = unpack(item)
        wrap = item[1] >= last
        return jnp.where(wrap, item[0] + 1, item[0]), jnp.where(wrap, 0, item[1] + 1)

    def scores_into(dst, item):
        qi, _, first, kb = unpack(item)
        qs = pl.ds(pl.multiple_of(qi * bq, bq), bq)
        rs = pl.ds(pl.multiple_of(kb * bk, bk), bk)
        limit = jnp.where(first, qi * bq - kb * bk, no_mask)
        for h in range(2):
            s = _nt_dot(ka_scr[h, rs, :], qa_scr[h, qs, :])
            dst[h] = jnp.where(key_minus_qry <= limit, s, NEG_BIG)

    def write_out(qi, sts):
        acc0, acc1 = sts[0][1], sts[1][1]
        o_t = jnp.where(row_lo, acc0 / acc0[HEAD_DIM:HEAD_DIM + 1, :], acc1 / acc1[0:1, :])
        o_ref[pl.ds(pl.multiple_of(qi * bq, bq), bq), :] = o_t.T.astype(o_ref.dtype)

    def absorb(src, item, prev_qi, sts):
        qi, _, first, kb = unpack(item)
        write_out(prev_qi, sts)
        out = []
        for h in range(2):
            m_run, acc = sts[h]
            m_run = jnp.where(first, NEG_BIG, m_run)
            s = src[h]
            m_new = jnp.maximum(m_run, jnp.max(s, axis=0, keepdims=True))
            p = jnp.exp2(s - m_new)
            acc = jnp.exp2(m_run - m_new) * acc + _dot(vt_scr[h, kb], p.astype(BF16))
            out.append((m_new, acc))
        return qi, tuple(out)

    unroll = 4 if n_items % 4 == 0 else 2
    bufs = (sa_scr, sb_scr)

    def group(u, ca):
        held, prev_qi, sts = ca
        for k in range(unroll):
            nxt = advance(held)
            scores_into(bufs[(k + 1) % 2], nxt)
            prev_qi, sts = absorb(bufs[k % 2], held, prev_qi, sts)
            held = nxt
        return held, prev_qi, sts

    start = (jnp.int32(0), jnp.int32(0))
    init = (jnp.full((1, bq), NEG_BIG, F32), jnp.ones((LANES, bq), F32))
    scores_into(sa_scr, start)
    _, last_qi, sts = lax.fori_loop(0, n_items // unroll, group, (start, jnp.int32(0), (init, init)))
    write_out(last_qi, sts)


def _fox_call(proj, fbias, gq, gk, l, batch, seq, bq, bk):
    m = proj.shape[0]
    col = lambda blk: pl.BlockSpec((seq, LANES), lambda b, p, blk=blk: (b, blk + p))
    vec = _layer(l, 1, LANES)
    return pl.pallas_call(
        functools.partial(_fox_kernel, bq=bq, bk=bk, seq=seq),
        grid=(batch, FOX_WIDTH // LANES),
        in_specs=[col(FOX_Q_BLK), col(FOX_K_BLK), col(FOX_V_BLK),
                  pl.BlockSpec((seq, LANES), lambda b, p: (b, FOX_F_BLK)), vec, vec, vec],
        out_specs=pl.BlockSpec((seq, LANES), lambda b, p: (b, p)),
        out_shape=jax.ShapeDtypeStruct((m, FOX_WIDTH), BF16),
        scratch_shapes=[pltpu.VMEM((seq, LANES), F32),
                        pltpu.VMEM((2, seq, LANES), BF16),
                        pltpu.VMEM((2, seq, LANES), BF16),
                        pltpu.VMEM((2, seq // bk, LANES, bk), BF16),
                        pltpu.VMEM((2, bk, bq), F32),
                        pltpu.VMEM((2, bk, bq), F32)],
        compiler_params=pltpu.CompilerParams(
            dimension_semantics=("arbitrary", "arbitrary"), vmem_limit_bytes=VMEM_LIMIT),
        name="forgetting",
    )(proj, proj, proj, proj, fbias, gq, gk)


def _mlp_kernel(x_ref, ohg_ref, osb_ref, ofx_ref, wo1_ref, wo2_ref, wo3_ref, g2_ref,
                w1_ref, w2_ref, o_ref, *, tf):
    x1 = (x_ref[...] + _dot(ohg_ref[...], wo1_ref[...]) + _dot(osb_ref[...], wo2_ref[...])
          + _dot(ofx_ref[...], wo3_ref[...]))
    ms = jnp.mean(x1 * x1, axis=-1, keepdims=True)
    h2 = (x1 * lax.rsqrt(ms + EPS) * g2_ref[...]).astype(BF16)
    o_ref[...] = x1
    acc = None
    for f in range(w1_ref.shape[1] // tf):
        a = _dot(h2, w1_ref[:, f * tf:(f + 1) * tf])
        a = jnp.square(jnp.maximum(a, 0.0)).astype(BF16)
        y = _dot(a, w2_ref[f * tf:(f + 1) * tf, :])
        acc = y if acc is None else acc + y
    o_ref[...] = o_ref[...] + acc


def _mlp_call(x, ohg, osb, ofx, wo1, wo2, wo3, g2, w1, w2, l, tm, tf):
    m, d = x.shape
    row = lambda w: pl.BlockSpec((tm, w), lambda i: (i, 0))
    whole = lambda a: _layer(l, *a.shape[1:])
    return pl.pallas_call(
        functools.partial(_mlp_kernel, tf=tf),
        grid=(m // tm,),
        in_specs=[row(d), row(HG_WIDTH), row(SB_WIDTH), row(FOX_WIDTH),
                  whole(wo1), whole(wo2), whole(wo3), whole(g2), whole(w1), whole(w2)],
        out_specs=row(d),
        out_shape=jax.ShapeDtypeStruct((m, d), F32),
        compiler_params=pltpu.CompilerParams(
            dimension_semantics=("arbitrary",), vmem_limit_bytes=VMEM_LIMIT),
        name="outproj_mlp",
    )(x, ohg, osb, ofx, wo1, wo2, wo3, g2, w1, w2)


def _tile_sizes(batch, seq):
    m = batch * seq
    return dict(
        proj_tm=min(512, m),
        hg_tb=min(512, seq),
        sb_bq=min(256, seq),
        fox_bq=min(256, seq),
        fox_bk=min(512, seq),
        mlp_tm=min(512, m),
        mlp_tf=1024,
    )


def kernel(x, lb_logits, norm1_g, w_in, hg_norm_g, sb_q_norm_g, sb_k_norm_g, fox_q_norm_g,
           fox_k_norm_g, fox_f_bias, w_out, norm2_g, w_ff1, w_ff2):
    batch, seq, d = x.shape
    assert d == D_MODEL and seq % HG_ROWS == 0 and x.dtype == F32
    ts = _tile_sizes(batch, seq)
    assert seq % ts["sb_bq"] == 0 and seq % ts["fox_bk"] == 0 and ts["fox_bk"] % ts["fox_bq"] == 0
    assert seq % ts["hg_tb"] == 0 and seq % PRO_ROWS == 0
    m = batch * seq
    xf = x.reshape(m, d)
    row = lambda a: a.astype(F32)[:, None, :]
    pair = lambda g: row(jnp.tile(g, (1, 2)))
    w_in_p = jnp.pad(w_in, ((0, 0), (0, 0), (0, IN_COLS_PAD - IN_COLS))).astype(BF16)
    wo = w_out.astype(BF16)
    wo_hg, wo_sb, wo_fx = wo[:, :HG_WIDTH], wo[:, HG_WIDTH:HG_WIDTH + SB_WIDTH], wo[:, HG_WIDTH + SB_WIDTH:]
    w1, w2 = w_ff1.astype(BF16), w_ff2.astype(BF16)
    g1, g2, g_hg = row(norm1_g), row(norm2_g), row(jnp.tile(hg_norm_g, (1, HG_HEADS)))
    g_sq, g_sk, g_fq, g_fk = pair(sb_q_norm_g), pair(sb_k_norm_g), pair(fox_q_norm_g), pair(fox_k_norm_g)
    fbias = row(jnp.pad(fox_f_bias, ((0, 0), (0, LANES - FOX_HEADS))))
    lbl = lb_logits.astype(F32)

    for l in range(DEPTH):
        proj = _proj_call(xf, g1, w_in_p, l, ts["proj_tm"])
        o_hg = _hgrn_call(l, lbl, proj, g_hg, batch, seq, ts["hg_tb"])
        o_sb = _sb_call(proj, g_sq, g_sk, l, batch, seq, ts["sb_bq"])
        o_fx = _fox_call(proj, fbias, g_fq, g_fk, l, batch, seq, ts["fox_bq"], ts["fox_bk"])
        xf = _mlp_call(xf, o_hg, o_sb, o_fx, wo_hg, wo_sb, wo_fx, g2, w1, w2, l,
                       ts["mlp_tm"], ts["mlp_tf"])
    return xf.reshape(batch, seq, d)
```

```python
import functools
import math

import jax
import jax.numpy as jnp
from jax import lax
from jax.experimental import pallas as pl
from jax.experimental.pallas import tpu as pltpu

F32 = jnp.float32
BF16 = jnp.bfloat16

D_MODEL = 1024
DEPTH = 4
HEAD_DIM = 64
HG_HEADS = 4
HG_KW = 256
HG_WIDTH = 256
SB_WIDTH = 384
FOX_WIDTH = 384
FOX_HEADS = 6
IN_COLS = 4 * 256 + 3 * 384 + 3 * 384 + FOX_HEADS
D_FF = 4 * D_MODEL
EPS = 1e-6
LB_FLOOR = 1e-30
NEG_BIG = -1e30
LOG2E = math.log2(math.e)

LANES = 128
IN_COLS_PAD = 27 * LANES
SB_Q_BLK, SB_K_BLK, SB_V_BLK = 8, 11, 14
FOX_Q_BLK, FOX_K_BLK, FOX_V_BLK, FOX_F_BLK = 17, 20, 23, 26

SB_ZERO_LOG = -105.0
SUB = 16
HG_ROWS = 256
VMEM_LIMIT = 56 * 1024 * 1024


def _nt_dot(a, b):
    return lax.dot_general(a, b, (((1,), (1,)), ((), ())), preferred_element_type=F32)


def _dot(a, b):
    return jnp.dot(a, b, preferred_element_type=F32)


def _split2(x):
    hi = x.astype(BF16)
    lo = (x - hi.astype(F32)).astype(BF16)
    return hi, lo


def _split3(x):
    hi = x.astype(BF16)
    r1 = x - hi.astype(F32)
    mid = r1.astype(BF16)
    lo = (r1 - mid.astype(F32)).astype(BF16)
    return hi, mid, lo


def _softplus(z):
    return jnp.maximum(z, 0.0) + jnp.log(1.0 + jnp.exp(-jnp.abs(z)))


def _softplus2(z2):
    return jnp.maximum(z2, 0.0) + jnp.log2(1.0 + jnp.exp2(-jnp.abs(z2)))


def _proj_kernel(x_ref, g_ref, w_ref, o_ref):
    x = x_ref[...]
    ms = jnp.mean(x * x, axis=-1, keepdims=True)
    h = (x * lax.rsqrt(ms + EPS) * g_ref[...]).astype(BF16)
    o_ref[...] = _dot(h, w_ref[...])


def _layer(l, *shape):
    return pl.BlockSpec((None,) + shape, lambda *_: (l,) + (0,) * len(shape))


def _proj_call(x, g, w, l, tm):
    m, d = x.shape
    n = w.shape[2]
    return pl.pallas_call(
        _proj_kernel,
        grid=(m // tm,),
        in_specs=[
            pl.BlockSpec((tm, d), lambda i: (i, 0)),
            _layer(l, 1, d),
            _layer(l, d, n),
        ],
        out_specs=pl.BlockSpec((tm, n), lambda i: (i, 0)),
        out_shape=jax.ShapeDtypeStruct((m, n), F32),
        compiler_params=pltpu.CompilerParams(
            dimension_semantics=("arbitrary",), vmem_limit_bytes=VMEM_LIMIT),
        name="proj",
    )(x, g, w)


def _hgrn_kernel(lbl_ref, q_ref, f_ref, i_ref, g_ref, gn_ref, o_ref, st_scr, oi_scr, *, layer, tb):
    @pl.when(pl.program_id(1) == 0)
    def _():
        st_scr[...] = jnp.zeros_like(st_scr)

    rows = [lbl_ref[j:j + 1, :] for j in range(DEPTH)]
    mx = functools.reduce(jnp.maximum, rows)
    ex = [jnp.exp(r - mx) for r in rows]
    den = functools.reduce(lambda a, b: a + b, ex)
    lb = jnp.zeros_like(mx)
    for j in range(1, layer + 1):
        lb = lb + ex[j] / den
    log_lb = jnp.log(jnp.maximum(lb, LB_FLOOR))
    one_m_lb = 1.0 - lb
    log_one_m_lb = jnp.log(one_m_lb)
    gn = gn_ref[...]

    r_i = lax.broadcasted_iota(jnp.int32, (HG_KW, HG_KW), 0)
    c_i = lax.broadcasted_iota(jnp.int32, (HG_KW, HG_KW), 1)
    same_head = (r_i >> 6) == (c_i >> 6)
    ones_bd = jnp.where(same_head, 1.0, 0.0).astype(BF16)
    t_mod = lax.broadcasted_iota(jnp.int32, (HG_ROWS, HG_KW), 0) & (SUB - 1)
    lane_sub = lax.broadcasted_iota(jnp.int32, (HG_KW, HG_ROWS), 1) // SUB
    t_r = lax.broadcasted_iota(jnp.int32, (HG_ROWS, HG_ROWS), 0)
    t_c = lax.broadcasted_iota(jnp.int32, (HG_ROWS, HG_ROWS), 1)
    in_sub = (t_r // SUB) == (t_c // SUB)
    same_sub = jnp.where(in_sub, 1.0, 0.0).astype(BF16)
    upto_sub = jnp.where(jnp.logical_and(in_sub, t_c <= t_r), 1.0, 0.0).astype(BF16)

    def chunk(c, carry):
        rs = pl.ds(pl.multiple_of(c * HG_ROWS, HG_ROWS), HG_ROWS)
        q = q_ref[rs, :]
        fl = f_ref[rs, :]
        v = i_ref[rs, :]
        g = g_ref[rs, :]

        sp = _softplus(fl)
        lf = jnp.maximum(fl, log_lb) + jnp.log(1.0 + jnp.exp(-jnp.abs(fl - log_lb))) - sp
        kk = one_m_lb * jnp.exp(-sp)

        parts = _split3(lf)
        b = functools.reduce(lambda x, y: x + y, [_dot(upto_sub, part) for part in parts])
        beta = functools.reduce(lambda x, y: x + y, [_dot(same_sub, part) for part in parts])
        rest = beta - b

        b2 = b * LOG2E
        w2 = (log_one_m_lb - sp) * LOG2E - b2
        acc = _dot((q * kk).astype(BF16), ones_bd) * v
        for d in range(1, SUB):
            e = jnp.where(t_mod >= d, b2 + pltpu.roll(w2, d, 0), NEG_BIG)
            p = q * jnp.exp2(e)
            acc = acc + _dot(p.astype(BF16), ones_bd) * pltpu.roll(v, d, 0)

        qt = (q * jnp.exp2(b2)).astype(BF16)
        kt = (kk * jnp.exp(rest)).astype(BF16)
        v_t = v.T
        for j in range(HG_ROWS // SUB):
            st = st_scr[...]
            oi_scr[j * SUB:(j + 1) * SUB, :] = _nt_dot(qt[j * SUB:(j + 1) * SUB, :], st.astype(BF16))
            vm = jnp.where(lane_sub == j, v_t, 0.0).astype(BF16)
            ut = _dot(vm, kt)
            dec = jnp.exp(beta[j * SUB:j * SUB + 1, :])
            st_scr[...] = st * dec + jnp.where(same_head, ut, 0.0)

        o = acc + oi_scr[...]
        hi, lo = _split2(o * o)
        msq = (_dot(hi, ones_bd) + _dot(lo, ones_bd)) * (1.0 / HEAD_DIM)
        y = o * lax.rsqrt(msq + EPS) * gn
        o_ref[rs, :] = (y * (g / (1.0 + jnp.exp(-g)))).astype(o_ref.dtype)
        return carry

    lax.fori_loop(0, tb // HG_ROWS, chunk, 0)


def _hgrn_call(layer, lb_logits, proj, gn, batch, seq, tb):
    m = proj.shape[0]
    nt = seq // tb

    def col(j):
        return pl.BlockSpec((tb, HG_KW), lambda b, t, j=j: (b * nt + t, j))

    return pl.pallas_call(
        functools.partial(_hgrn_kernel, layer=layer, tb=tb),
        grid=(batch, nt),
        in_specs=[pl.BlockSpec((DEPTH, HG_KW), lambda b, t: (0, 0)),
                  col(0), col(1), col(2), col(3), _layer(layer, 1, HG_WIDTH)],
        out_specs=pl.BlockSpec((tb, HG_WIDTH), lambda b, t: (b * nt + t, 0)),
        out_shape=jax.ShapeDtypeStruct((m, HG_WIDTH), BF16),
        scratch_shapes=[pltpu.VMEM((HG_WIDTH, HG_KW), F32),
                        pltpu.VMEM((HG_ROWS, HG_WIDTH), F32)],
        compiler_params=pltpu.CompilerParams(
            dimension_semantics=("arbitrary", "arbitrary"), vmem_limit_bytes=VMEM_LIMIT),
        name="hgrn2",
    )(lb_logits, proj, proj, proj, proj, gn)


def _pair_rms(x, gain, lane_lo):
    x2 = x * x
    s0 = jnp.sum(jnp.where(lane_lo, x2, 0.0), axis=1, keepdims=True)
    s1 = jnp.sum(jnp.where(lane_lo, 0.0, x2), axis=1, keepdims=True)
    ms = jnp.where(lane_lo, s0, s1) * (1.0 / HEAD_DIM)
    return x * lax.rsqrt(ms + EPS) * gain


PRO_ROWS = 512


def _sb_kernel(q0_ref, q1_ref, q2_ref, k0_ref, k1_ref, k2_ref, v0_ref, v1_ref, v2_ref,
               gq_ref, gk_ref, o_ref, kn_scr, vb_scr, acc_scr, car_scr, *, bq, n_sub, seq):
    g = pl.program_id(1)
    q_refs = (q0_ref, q1_ref, q2_ref)
    k_refs = (k0_ref, k1_ref, k2_ref)
    v_refs = (v0_ref, v1_ref, v2_ref)
    n_pairs = len(q_refs)

    @pl.when(g == 0)
    def _():
        lane_lo = lax.broadcasted_iota(jnp.int32, (PRO_ROWS, LANES), 1) < HEAD_DIM

        def body(c, carry):
            rs = pl.ds(pl.multiple_of(c * PRO_ROWS, PRO_ROWS), PRO_ROWS)
            for p in range(n_pairs):
                kn_scr[p, rs, :] = _pair_rms(k_refs[p][rs, :], gk_ref[...], lane_lo).astype(BF16)
                vb_scr[p, rs, :] = v_refs[p][rs, :].astype(BF16)
            return carry

        lax.fori_loop(0, seq // PRO_ROWS, body, 0)

    lane_lo = lax.broadcasted_iota(jnp.int32, (bq, LANES), 1) < HEAD_DIM
    scale = 1.0 / math.sqrt(HEAD_DIM)
    r_i = lax.broadcasted_iota(jnp.int32, (bq, bq), 0)
    c_i = lax.broadcasted_iota(jnp.int32, (bq, bq), 1)
    before = c_i < r_i
    later = jnp.where(r_i > c_i, 1.0, 0.0).astype(BF16)

    heads = range(2 * n_pairs)

    def q_block(sub, carry_unused):
        i = g * n_sub + sub
        qs = pl.ds(pl.multiple_of(sub * bq, bq), bq)
        qms = []
        for p in range(n_pairs):
            qn = _pair_rms(q_refs[p][qs, :], gq_ref[...], lane_lo) * (scale * LOG2E)
            qms.append(jnp.where(lane_lo, qn, 0.0).astype(BF16))
            qms.append(jnp.where(lane_lo, 0.0, qn).astype(BF16))

        def absorb(tiles, fresh):
            rows = [pl.ds(pl.multiple_of(kb * bq, bq), bq) for kb, _ in tiles]
            masks = [mask for _, mask in tiles]
            nt = range(len(tiles))
            zs = [[_nt_dot(qms[h], kn_scr[h // 2, rows[t], :]) for h in heads] for t in nt]
            sps = [[_softplus2(z) for z in zt] for zt in zs]
            us = [spt if masks[t] is None else [jnp.where(masks[t], sp, 0.0) for sp in spt]
                  for t, spt in enumerate(sps)]
            tails = [[_dot(u.astype(BF16), later) for u in ut] for ut in us]
            low = None
            for h in heads:
                carry = None if fresh else car_scr[h]
                acc = None if fresh else acc_scr[h]
                for t in nt:
                    row = jnp.broadcast_to(jnp.sum(us[t][h], axis=1, keepdims=True), (bq, LANES))
                    e = zs[t][h] - sps[t][h] - tails[t][h]
                    if carry is not None:
                        e = e - jnp.concatenate([carry] * (bq // LANES), axis=1)
                        row = row + carry
                    a = jnp.exp2(e)
                    if masks[t] is not None:
                        a = jnp.where(masks[t], a, 0.0)
                    pv = _dot(a.astype(BF16), vb_scr[h // 2, rows[t], :])
                    acc = pv if acc is None else acc + pv
                    carry = row
                acc_scr[h] = acc
                car_scr[h] = carry
                low = carry if low is None else jnp.minimum(low, carry)
            return jnp.min(low)

        low = lax.cond(i > 0,
                       lambda: absorb([(i, before), (i - 1, None)], True),
                       lambda: absorb([(i, before)], True))

        def live(st):
            return jnp.logical_and(st[0] >= 0, st[1] < -SB_ZERO_LOG * LOG2E)

        lax.while_loop(live, lambda st: (st[0] - 1, absorb([(st[0], None)], False)), (i - 2, low))
        for p in range(n_pairs):
            o_ref[qs, p * LANES:(p + 1) * LANES] = jnp.where(
                lane_lo, acc_scr[2 * p], acc_scr[2 * p + 1]).astype(o_ref.dtype)
        return carry_unused

    lax.fori_loop(0, n_sub, q_block, 0)


def _sb_call(proj, gq, gk, l, batch, seq, bq, n_sub):
    m = proj.shape[0]
    rows = bq * n_sub
    nq = seq // rows
    n_pairs = SB_WIDTH // LANES
    q_spec = lambda p: pl.BlockSpec((rows, LANES), lambda b, i, p=p: (b * nq + i, SB_Q_BLK + p))
    k_spec = lambda blk, p: pl.BlockSpec((seq, LANES), lambda b, i, p=p: (b, blk + p))
    gain = _layer(l, 1, LANES)
    return pl.pallas_call(
        functools.partial(_sb_kernel, bq=bq, n_sub=n_sub, seq=seq),
        grid=(batch, nq),
        in_specs=([q_spec(p) for p in range(n_pairs)]
                  + [k_spec(SB_K_BLK, p) for p in range(n_pairs)]
                  + [k_spec(SB_V_BLK, p) for p in range(n_pairs)] + [gain, gain]),
        out_specs=pl.BlockSpec((rows, SB_WIDTH), lambda b, i: (b * nq + i, 0)),
        out_shape=jax.ShapeDtypeStruct((m, SB_WIDTH), BF16),
        scratch_shapes=[pltpu.VMEM((n_pairs, seq, LANES), BF16),
                        pltpu.VMEM((n_pairs, seq, LANES), BF16),
                        pltpu.VMEM((2 * n_pairs, bq, LANES), F32),
                        pltpu.VMEM((2 * n_pairs, bq, LANES), F32)],
        compiler_params=pltpu.CompilerParams(
            dimension_semantics=("arbitrary", "arbitrary"), vmem_limit_bytes=VMEM_LIMIT),
        name="stickbreak",
    )(*([proj] * (3 * n_pairs)), gq, gk)


def _aug_lanes(lane, base, first, second):
    out = jnp.zeros(lane.shape, F32)
    for n in range(3):
        out = jnp.where(lane == base + n, first[n], out)
        out = jnp.where(lane == base + 3 + n, second[n], out)
    return out


def _fox_kernel(q_ref, k_ref, v_ref, f_ref, fb_ref, gq_ref, gk_ref, o_ref,
                c_scr, qa_scr, ka_scr, vt_scr, sa_scr, sb_scr, *, bq, bk, seq):
    p_idx = pl.program_id(1)
    one = jnp.ones((1, 1), F32)
    scale = 1.0 / math.sqrt(HEAD_DIM)

    @pl.when(p_idx == 0)
    def _():
        r_i = lax.broadcasted_iota(jnp.int32, (bk, bk), 0)
        c_i = lax.broadcasted_iota(jnp.int32, (bk, bk), 1)
        upto = jnp.where(c_i <= r_i, 1.0, 0.0).astype(BF16)

        def body(n, run):
            rs = pl.ds(pl.multiple_of(n * bk, bk), bk)
            y = f_ref[rs, :] + fb_ref[...]
            lg = jnp.minimum(y, 0.0) - jnp.log(1.0 + jnp.exp(-jnp.abs(y)))
            hi, mid, lo = _split3(lg)
            c = _dot(upto, hi) + _dot(upto, mid) + _dot(upto, lo) + run
            c_scr[rs, :] = c
            return c[bk - 1:bk, :]

        lax.fori_loop(0, seq // bk, body, jnp.zeros((1, LANES), F32))

    lane = lax.broadcasted_iota(jnp.int32, (bk, LANES), 1)
    lane_lo = lane < HEAD_DIM
    row_lo_k = lax.broadcasted_iota(jnp.int32, (LANES, bk), 0) < HEAD_DIM

    def prep(n, carry):
        rs = pl.ds(pl.multiple_of(n * bk, bk), bk)
        kn = _pair_rms(k_ref[rs, :], gk_ref[...], lane_lo)
        qn = _pair_rms(q_ref[rs, :], gq_ref[...], lane_lo) * (scale * LOG2E)
        c = c_scr[rs, :] * LOG2E
        v_t = v_ref[rs, :].T
        for h in range(2):
            ch = jnp.sum(jnp.where(lane == 2 * p_idx + h, c, 0.0), axis=1, keepdims=True)
            parts = [x.astype(F32) for x in _split3(ch)]
            mine = lane_lo if h == 0 else jnp.logical_not(lane_lo)
            base = HEAD_DIM * (1 - h)
            k_aug = _aug_lanes(lane, base, (one, one, one), [-x for x in parts])
            q_aug = _aug_lanes(lane, base, parts, (one, one, one))
            ka_scr[h, rs, :] = jnp.where(mine, kn, k_aug).astype(BF16)
            qa_scr[h, rs, :] = jnp.where(mine, qn, q_aug).astype(BF16)
            vt_scr[h, n] = jnp.where(row_lo_k if h == 0 else jnp.logical_not(row_lo_k), v_t, 1.0).astype(BF16)
        return carry

    lax.fori_loop(0, seq // bk, prep, 0)

    key_minus_qry = (lax.broadcasted_iota(jnp.int32, (bk, bq), 0)
                     - lax.broadcasted_iota(jnp.int32, (bk, bq), 1))
    row_lo = lax.broadcasted_iota(jnp.int32, (LANES, bq), 0) < HEAD_DIM

    nq = seq // bq
    n_items = sum((qi * bq + bq - 1) // bk + 1 for qi in range(nq))
    assert n_items % 2 == 0
    no_mask = jnp.int32(bk)

    def unpack(item):
        qi = jnp.minimum(item[0], nq - 1)
        last = (qi * bq + bq - 1) // bk
        first = item[1] == 0
        kb = jnp.where(first, last, item[1] - 1)
        return qi, last, first, kb

    def advance(item):
        qi, last, _, _ = unpack(item)
        wrap = item[1] >= last
        return jnp.where(wrap, item[0] + 1, item[0]), jnp.where(wrap, 0, item[1] + 1)

    def scores_into(dst, item):
        qi, _, first, kb = unpack(item)
        qs = pl.ds(pl.multiple_of(qi * bq, bq), bq)
        rs = pl.ds(pl.multiple_of(kb * bk, bk), bk)
        limit = jnp.where(first, qi * bq - kb * bk, no_mask)
        for h in range(2):
            s = _nt_dot(ka_scr[h, rs, :], qa_scr[h, qs, :])
            dst[h] = jnp.where(key_minus_qry <= limit, s, NEG_BIG)

    def write_out(qi, sts):
        acc0, acc1 = sts[0][1], sts[1][1]
        o_t = jnp.where(row_lo, acc0 / acc0[HEAD_DIM:HEAD_DIM + 1, :], acc1 / acc1[0:1, :])
        o_ref[pl.ds(pl.multiple_of(qi * bq, bq), bq), :] = o_t.T.astype(o_ref.dtype)

    def absorb(src, item, prev_qi, sts):
        qi, _, first, kb = unpack(item)
        write_out(prev_qi, sts)
        out = []
        for h in range(2):
            m_run, acc = sts[h]
            m_run = jnp.where(first, NEG_BIG, m_run)
            s = src[h]
            m_new = jnp.maximum(m_run, jnp.max(s, axis=0, keepdims=True))
            p = jnp.exp2(s - m_new)
            acc = jnp.exp2(m_run - m_new) * acc + _dot(vt_scr[h, kb], p.astype(BF16))
            out.append((m_new, acc))
        return qi, tuple(out)

    unroll = 4 if n_items % 4 == 0 else 2
    bufs = (sa_scr, sb_scr)

    def group(u, ca):
        held, prev_qi, sts = ca
        for k in range(unroll):
            nxt = advance(held)
            scores_into(bufs[(k + 1) % 2], nxt)
            prev_qi, sts = absorb(bufs[k % 2], held, prev_qi, sts)
            held = nxt
        return held, prev_qi, sts

    start = (jnp.int32(0), jnp.int32(0))
    init = (jnp.full((1, bq), NEG_BIG, F32), jnp.ones((LANES, bq), F32))
    scores_into(sa_scr, start)
    _, last_qi, sts = lax.fori_loop(0, n_items // unroll, group, (start, jnp.int32(0), (init, init)))
    write_out(last_qi, sts)


def _fox_call(proj, fbias, gq, gk, l, batch, seq, bq, bk):
    m = proj.shape[0]
    col = lambda blk: pl.BlockSpec((seq, LANES), lambda b, p, blk=blk: (b, blk + p))
    vec = _layer(l, 1, LANES)
    return pl.pallas_call(
        functools.partial(_fox_kernel, bq=bq, bk=bk, seq=seq),
        grid=(batch, FOX_WIDTH // LANES),
        in_specs=[col(FOX_Q_BLK), col(FOX_K_BLK), col(FOX_V_BLK),
                  pl.BlockSpec((seq, LANES), lambda b, p: (b, FOX_F_BLK)), vec, vec, vec],
        out_specs=pl.BlockSpec((seq, LANES), lambda b, p: (b, p)),
        out_shape=jax.ShapeDtypeStruct((m, FOX_WIDTH), BF16),
        scratch_shapes=[pltpu.VMEM((seq, LANES), F32),
                        pltpu.VMEM((2, seq, LANES), BF16),
                        pltpu.VMEM((2, seq, LANES), BF16),
                        pltpu.VMEM((2, seq // bk, LANES, bk), BF16),
                        pltpu.VMEM((2, bk, bq), F32),
                        pltpu.VMEM((2, bk, bq), F32)],
        compiler_params=pltpu.CompilerParams(
            dimension_semantics=("arbitrary", "arbitrary"), vmem_limit_bytes=VMEM_LIMIT),
        name="forgetting",
    )(proj, proj, proj, proj, fbias, gq, gk)


def _mlp_kernel(x_ref, ohg_ref, osb_ref, ofx_ref, wo1_ref, wo2_ref, wo3_ref, g2_ref,
                w1_ref, w2_ref, o_ref, *, tf):
    x1 = (x_ref[...] + _dot(ohg_ref[...], wo1_ref[...]) + _dot(osb_ref[...], wo2_ref[...])
          + _dot(ofx_ref[...], wo3_ref[...]))
    ms = jnp.mean(x1 * x1, axis=-1, keepdims=True)
    h2 = (x1 * lax.rsqrt(ms + EPS) * g2_ref[...]).astype(BF16)
    o_ref[...] = x1
    acc = None
    for f in range(w1_ref.shape[1] // tf):
        a = _dot(h2, w1_ref[:, f * tf:(f + 1) * tf])
        a = jnp.square(jnp.maximum(a, 0.0)).astype(BF16)
        y = _dot(a, w2_ref[f * tf:(f + 1) * tf, :])
        acc = y if acc is None else acc + y
    o_ref[...] = o_ref[...] + acc


def _mlp_call(x, ohg, osb, ofx, wo1, wo2, wo3, g2, w1, w2, l, tm, tf):
    m, d = x.shape
    row = lambda w: pl.BlockSpec((tm, w), lambda i: (i, 0))
    whole = lambda a: _layer(l, *a.shape[1:])
    return pl.pallas_call(
        functools.partial(_mlp_kernel, tf=tf),
        grid=(m // tm,),
        in_specs=[row(d), row(HG_WIDTH), row(SB_WIDTH), row(FOX_WIDTH),
                  whole(wo1), whole(wo2), whole(wo3), whole(g2), whole(w1), whole(w2)],
        out_specs=row(d),
        out_shape=jax.ShapeDtypeStruct((m, d), F32),
        compiler_params=pltpu.CompilerParams(
            dimension_semantics=("arbitrary",), vmem_limit_bytes=VMEM_LIMIT),
        name="outproj_mlp",
    )(x, ohg, osb, ofx, wo1, wo2, wo3, g2, w1, w2)


def _tile_sizes(batch, seq):
    m = batch * seq
    return dict(
        proj_tm=min(512, m),
        hg_tb=min(512, seq),
        sb_bq=min(256, seq),
        sb_sub=max(1, min(4, seq // 256)),
        fox_bq=min(256, seq),
        fox_bk=min(512, seq),
        mlp_tm=min(512, m),
        mlp_tf=1024,
    )


def kernel(x, lb_logits, norm1_g, w_in, hg_norm_g, sb_q_norm_g, sb_k_norm_g, fox_q_norm_g,
           fox_k_norm_g, fox_f_bias, w_out, norm2_g, w_ff1, w_ff2):
    batch, seq, d = x.shape
    assert d == D_MODEL and seq % HG_ROWS == 0 and x.dtype == F32
    ts = _tile_sizes(batch, seq)
    assert seq % (ts["sb_bq"] * ts["sb_sub"]) == 0 and seq % ts["fox_bk"] == 0 and ts["fox_bk"] % ts["fox_bq"] == 0
    assert seq % ts["hg_tb"] == 0 and seq % PRO_ROWS == 0
    m = batch * seq
    xf = x.reshape(m, d)
    row = lambda a: a.astype(F32)[:, None, :]
    pair = lambda g: row(jnp.tile(g, (1, 2)))
    w_in_p = jnp.pad(w_in, ((0, 0), (0, 0), (0, IN_COLS_PAD - IN_COLS))).astype(BF16)
    wo = w_out.astype(BF16)
    wo_hg, wo_sb, wo_fx = wo[:, :HG_WIDTH], wo[:, HG_WIDTH:HG_WIDTH + SB_WIDTH], wo[:, HG_WIDTH + SB_WIDTH:]
    w1, w2 = w_ff1.astype(BF16), w_ff2.astype(BF16)
    g1, g2, g_hg = row(norm1_g), row(norm2_g), row(jnp.tile(hg_norm_g, (1, HG_HEADS)))
    g_sq, g_sk, g_fq, g_fk = pair(sb_q_norm_g), pair(sb_k_norm_g), pair(fox_q_norm_g), pair(fox_k_norm_g)
    fbias = row(jnp.pad(fox_f_bias, ((0, 0), (0, LANES - FOX_HEADS))))
    lbl = lb_logits.astype(F32)

    for l in range(DEPTH):
        proj = _proj_call(xf, g1, w_in_p, l, ts["proj_tm"])
        o_hg = _hgrn_call(l, lbl, proj, g_hg, batch, seq, ts["hg_tb"])
        o_sb = _sb_call(proj, g_sq, g_sk, l, batch, seq, ts["sb_bq"], ts["sb_sub"])
        o_fx = _fox_call(proj, fbias, g_fq, g_fk, l, batch, seq, ts["fox_bq"], ts["fox_bk"])
        xf = _mlp_call(xf, o_hg, o_sb, o_fx, wo_hg, wo_sb, wo_fx, g2, w1, w2, l,
                       ts["mlp_tm"], ts["mlp_tf"])
    return xf.reshape(batch, seq, d)
```

```python
import functools
import math

import jax
import jax.numpy as jnp
from jax import lax
from jax.experimental import pallas as pl
from jax.experimental.pallas import tpu as pltpu

F32 = jnp.float32
BF16 = jnp.bfloat16

D_MODEL = 1024
DEPTH = 4
HEAD_DIM = 64
HG_HEADS = 4
HG_KW = 256
HG_WIDTH = 256
SB_WIDTH = 384
FOX_WIDTH = 384
FOX_HEADS = 6
IN_COLS = 4 * 256 + 3 * 384 + 3 * 384 + FOX_HEADS
D_FF = 4 * D_MODEL
EPS = 1e-6
LB_FLOOR = 1e-30
NEG_BIG = -1e30
LOG2E = math.log2(math.e)

LANES = 128
IN_COLS_PAD = 27 * LANES
SB_Q_BLK, SB_K_BLK, SB_V_BLK = 8, 11, 14
FOX_Q_BLK, FOX_K_BLK, FOX_V_BLK, FOX_F_BLK = 17, 20, 23, 26

SB_ZERO_LOG = -105.0
SUB = 16
HG_ROWS = 256
VMEM_LIMIT = 56 * 1024 * 1024


def _nt_dot(a, b):
    return lax.dot_general(a, b, (((1,), (1,)), ((), ())), preferred_element_type=F32)


def _dot(a, b):
    return jnp.dot(a, b, preferred_element_type=F32)


def _split2(x):
    hi = x.astype(BF16)
    lo = (x - hi.astype(F32)).astype(BF16)
    return hi, lo


def _split3(x):
    hi = x.astype(BF16)
    r1 = x - hi.astype(F32)
    mid = r1.astype(BF16)
    lo = (r1 - mid.astype(F32)).astype(BF16)
    return hi, mid, lo


def _softplus(z):
    return jnp.maximum(z, 0.0) + jnp.log(1.0 + jnp.exp(-jnp.abs(z)))


def _softplus2(z2):
    return jnp.maximum(z2, 0.0) + jnp.log2(1.0 + jnp.exp2(-jnp.abs(z2)))


def _proj_kernel(x_ref, g_ref, w_ref, o_ref):
    x = x_ref[...]
    ms = jnp.mean(x * x, axis=-1, keepdims=True)
    h = (x * lax.rsqrt(ms + EPS) * g_ref[...]).astype(BF16)
    o_ref[...] = _dot(h, w_ref[...])


def _layer(l, *shape):
    return pl.BlockSpec((None,) + shape, lambda *_: (l,) + (0,) * len(shape))


def _proj_call(x, g, w, l, tm):
    m, d = x.shape
    n = w.shape[2]
    return pl.pallas_call(
        _proj_kernel,
        grid=(m // tm,),
        in_specs=[
            pl.BlockSpec((tm, d), lambda i: (i, 0)),
            _layer(l, 1, d),
            _layer(l, d, n),
        ],
        out_specs=pl.BlockSpec((tm, n), lambda i: (i, 0)),
        out_shape=jax.ShapeDtypeStruct((m, n), F32),
        compiler_params=pltpu.CompilerParams(
            dimension_semantics=("arbitrary",), vmem_limit_bytes=VMEM_LIMIT),
        name="proj",
    )(x, g, w)


def _hgrn_kernel(lbl_ref, q_ref, f_ref, i_ref, g_ref, gn_ref, o_ref, st_scr, oi_scr, *, layer, tb):
    @pl.when(pl.program_id(1) == 0)
    def _():
        st_scr[...] = jnp.zeros_like(st_scr)

    rows = [lbl_ref[j:j + 1, :] for j in range(DEPTH)]
    mx = functools.reduce(jnp.maximum, rows)
    ex = [jnp.exp(r - mx) for r in rows]
    den = functools.reduce(lambda a, b: a + b, ex)
    lb = jnp.zeros_like(mx)
    for j in range(1, layer + 1):
        lb = lb + ex[j] / den
    log_lb = jnp.log(jnp.maximum(lb, LB_FLOOR))
    one_m_lb = 1.0 - lb
    log_one_m_lb = jnp.log(one_m_lb)
    gn = gn_ref[...]

    r_i = lax.broadcasted_iota(jnp.int32, (HG_KW, HG_KW), 0)
    c_i = lax.broadcasted_iota(jnp.int32, (HG_KW, HG_KW), 1)
    same_head = (r_i >> 6) == (c_i >> 6)
    ones_bd = jnp.where(same_head, 1.0, 0.0).astype(BF16)
    t_mod = lax.broadcasted_iota(jnp.int32, (HG_ROWS, HG_KW), 0) & (SUB - 1)
    lane_sub = lax.broadcasted_iota(jnp.int32, (HG_KW, HG_ROWS), 1) // SUB
    t_r = lax.broadcasted_iota(jnp.int32, (HG_ROWS, HG_ROWS), 0)
    t_c = lax.broadcasted_iota(jnp.int32, (HG_ROWS, HG_ROWS), 1)
    in_sub = (t_r // SUB) == (t_c // SUB)
    same_sub = jnp.where(in_sub, 1.0, 0.0).astype(BF16)
    upto_sub = jnp.where(jnp.logical_and(in_sub, t_c <= t_r), 1.0, 0.0).astype(BF16)

    def chunk(c, carry):
        rs = pl.ds(pl.multiple_of(c * HG_ROWS, HG_ROWS), HG_ROWS)
        q = q_ref[rs, :]
        fl = f_ref[rs, :]
        v = i_ref[rs, :]
        g = g_ref[rs, :]

        sp = _softplus(fl)
        lf = jnp.maximum(fl, log_lb) + jnp.log(1.0 + jnp.exp(-jnp.abs(fl - log_lb))) - sp
        kk = one_m_lb * jnp.exp(-sp)

        parts = _split3(lf)
        b = functools.reduce(lambda x, y: x + y, [_dot(upto_sub, part) for part in parts])
        beta = functools.reduce(lambda x, y: x + y, [_dot(same_sub, part) for part in parts])
        rest = beta - b

        b2 = b * LOG2E
        w2 = (log_one_m_lb - sp) * LOG2E - b2
        acc = _dot((q * kk).astype(BF16), ones_bd) * v
        for d in range(1, SUB):
            e = jnp.where(t_mod >= d, b2 + pltpu.roll(w2, d, 0), NEG_BIG)
            p = q * jnp.exp2(e)
            acc = acc + _dot(p.astype(BF16), ones_bd) * pltpu.roll(v, d, 0)

        qt = (q * jnp.exp2(b2)).astype(BF16)
        kt = (kk * jnp.exp(rest)).astype(BF16)
        v_t = v.T
        for j in range(HG_ROWS // SUB):
            st = st_scr[...]
            oi_scr[j * SUB:(j + 1) * SUB, :] = _nt_dot(qt[j * SUB:(j + 1) * SUB, :], st.astype(BF16))
            vm = jnp.where(lane_sub == j, v_t, 0.0).astype(BF16)
            ut = _dot(vm, kt)
            dec = jnp.exp(beta[j * SUB:j * SUB + 1, :])
            st_scr[...] = st * dec + jnp.where(same_head, ut, 0.0)

        o = acc + oi_scr[...]
        hi, lo = _split2(o * o)
        msq = (_dot(hi, ones_bd) + _dot(lo, ones_bd)) * (1.0 / HEAD_DIM)
        y = o * lax.rsqrt(msq + EPS) * gn
        o_ref[rs, :] = (y * (g / (1.0 + jnp.exp(-g)))).astype(o_ref.dtype)
        return carry

    lax.fori_loop(0, tb // HG_ROWS, chunk, 0)


def _hgrn_call(layer, lb_logits, proj, gn, batch, seq, tb):
    m = proj.shape[0]
    nt = seq // tb

    def col(j):
        return pl.BlockSpec((tb, HG_KW), lambda b, t, j=j: (b * nt + t, j))

    return pl.pallas_call(
        functools.partial(_hgrn_kernel, layer=layer, tb=tb),
        grid=(batch, nt),
        in_specs=[pl.BlockSpec((DEPTH, HG_KW), lambda b, t: (0, 0)),
                  col(0), col(1), col(2), col(3), _layer(layer, 1, HG_WIDTH)],
        out_specs=pl.BlockSpec((tb, HG_WIDTH), lambda b, t: (b * nt + t, 0)),
        out_shape=jax.ShapeDtypeStruct((m, HG_WIDTH), BF16),
        scratch_shapes=[pltpu.VMEM((HG_WIDTH, HG_KW), F32),
                        pltpu.VMEM((HG_ROWS, HG_WIDTH), F32)],
        compiler_params=pltpu.CompilerParams(
            dimension_semantics=("arbitrary", "arbitrary"), vmem_limit_bytes=VMEM_LIMIT),
        name="hgrn2",
    )(lb_logits, proj, proj, proj, proj, gn)


def _pair_rms(x, gain, lane_lo):
    x2 = x * x
    s0 = jnp.sum(jnp.where(lane_lo, x2, 0.0), axis=1, keepdims=True)
    s1 = jnp.sum(jnp.where(lane_lo, 0.0, x2), axis=1, keepdims=True)
    ms = jnp.where(lane_lo, s0, s1) * (1.0 / HEAD_DIM)
    return x * lax.rsqrt(ms + EPS) * gain


PRO_ROWS = 512


def _sb_kernel(q0_ref, q1_ref, q2_ref, k0_ref, k1_ref, k2_ref, v0_ref, v1_ref, v2_ref,
               gq_ref, gk_ref, o_ref, kn_scr, vb_scr, acc_scr, car_scr, *, bq, n_sub, seq):
    g = pl.program_id(1)
    q_refs = (q0_ref, q1_ref, q2_ref)
    k_refs = (k0_ref, k1_ref, k2_ref)
    v_refs = (v0_ref, v1_ref, v2_ref)
    n_pairs = len(q_refs)

    @pl.when(g == 0)
    def _():
        lane_lo = lax.broadcasted_iota(jnp.int32, (PRO_ROWS, LANES), 1) < HEAD_DIM

        def body(c, carry):
            rs = pl.ds(pl.multiple_of(c * PRO_ROWS, PRO_ROWS), PRO_ROWS)
            for p in range(n_pairs):
                kn_scr[p, rs, :] = _pair_rms(k_refs[p][rs, :], gk_ref[...], lane_lo).astype(BF16)
                vb_scr[p, rs, :] = v_refs[p][rs, :].astype(BF16)
            return carry

        lax.fori_loop(0, seq // PRO_ROWS, body, 0)

    lane_lo = lax.broadcasted_iota(jnp.int32, (bq, LANES), 1) < HEAD_DIM
    scale = 1.0 / math.sqrt(HEAD_DIM)
    r_i = lax.broadcasted_iota(jnp.int32, (bq, bq), 0)
    c_i = lax.broadcasted_iota(jnp.int32, (bq, bq), 1)
    before = c_i < r_i
    later = jnp.where(r_i > c_i, 1.0, 0.0).astype(BF16)

    heads = range(2 * n_pairs)

    def q_block(sub, carry_unused):
        i = g * n_sub + sub
        qs = pl.ds(pl.multiple_of(sub * bq, bq), bq)
        qms = []
        for p in range(n_pairs):
            qn = _pair_rms(q_refs[p][qs, :], gq_ref[...], lane_lo) * (scale * LOG2E)
            qms.append(jnp.where(lane_lo, qn, 0.0).astype(BF16))
            qms.append(jnp.where(lane_lo, 0.0, qn).astype(BF16))

        def absorb(tiles, fresh):
            rows = [pl.ds(pl.multiple_of(kb * bq, bq), bq) for kb, _ in tiles]
            masks = [mask for _, mask in tiles]
            nt = range(len(tiles))
            zs = [[_nt_dot(qms[h], kn_scr[h // 2, rows[t], :]) for h in heads] for t in nt]
            sps = [[_softplus2(z) for z in zt] for zt in zs]
            us = [spt if masks[t] is None else [jnp.where(masks[t], sp, 0.0) for sp in spt]
                  for t, spt in enumerate(sps)]
            tails = [[_dot(u.astype(BF16), later) for u in ut] for ut in us]
            low = None
            for h in heads:
                carry = None if fresh else car_scr[h]
                acc = None if fresh else acc_scr[h]
                for t in nt:
                    row = jnp.broadcast_to(jnp.sum(us[t][h], axis=1, keepdims=True), (bq, LANES))
                    e = zs[t][h] - sps[t][h] - tails[t][h]
                    if carry is not None:
                        e = e - jnp.concatenate([carry] * (bq // LANES), axis=1)
                        row = row + carry
                    a = jnp.exp2(e)
                    if masks[t] is not None:
                        a = jnp.where(masks[t], a, 0.0)
                    pv = _dot(a.astype(BF16), vb_scr[h // 2, rows[t], :])
                    acc = pv if acc is None else acc + pv
                    carry = row
                acc_scr[h] = acc
                car_scr[h] = carry
                low = carry if low is None else jnp.minimum(low, carry)
            return jnp.min(low)

        low = lax.cond(i > 0,
                       lambda: absorb([(i, before), (i - 1, None)], True),
                       lambda: absorb([(i, before)], True))

        def live(st):
            return jnp.logical_and(st[0] >= 0, st[1] < -SB_ZERO_LOG * LOG2E)

        lax.while_loop(live, lambda st: (st[0] - 1, absorb([(st[0], None)], False)), (i - 2, low))
        for p in range(n_pairs):
            o_ref[qs, p * LANES:(p + 1) * LANES] = jnp.where(
                lane_lo, acc_scr[2 * p], acc_scr[2 * p + 1]).astype(o_ref.dtype)
        return carry_unused

    lax.fori_loop(0, n_sub, q_block, 0)


def _sb_call(proj, gq, gk, l, batch, seq, bq, n_sub):
    m = proj.shape[0]
    rows = bq * n_sub
    nq = seq // rows
    n_pairs = SB_WIDTH // LANES
    q_spec = lambda p: pl.BlockSpec((rows, LANES), lambda b, i, p=p: (b * nq + i, SB_Q_BLK + p))
    k_spec = lambda blk, p: pl.BlockSpec((seq, LANES), lambda b, i, p=p: (b, blk + p))
    gain = _layer(l, 1, LANES)
    return pl.pallas_call(
        functools.partial(_sb_kernel, bq=bq, n_sub=n_sub, seq=seq),
        grid=(batch, nq),
        in_specs=([q_spec(p) for p in range(n_pairs)]
                  + [k_spec(SB_K_BLK, p) for p in range(n_pairs)]
                  + [k_spec(SB_V_BLK, p) for p in range(n_pairs)] + [gain, gain]),
        out_specs=pl.BlockSpec((rows, SB_WIDTH), lambda b, i: (b * nq + i, 0)),
        out_shape=jax.ShapeDtypeStruct((m, SB_WIDTH), BF16),
        scratch_shapes=[pltpu.VMEM((n_pairs, seq, LANES), BF16),
                        pltpu.VMEM((n_pairs, seq, LANES), BF16),
                        pltpu.VMEM((2 * n_pairs, bq, LANES), F32),
                        pltpu.VMEM((2 * n_pairs, bq, LANES), F32)],
        compiler_params=pltpu.CompilerParams(
            dimension_semantics=("arbitrary", "arbitrary"), vmem_limit_bytes=VMEM_LIMIT),
        name="stickbreak",
    )(*([proj] * (3 * n_pairs)), gq, gk)


def _aug_lanes(lane, base, first, second):
    out = jnp.zeros(lane.shape, F32)
    for n in range(3):
        out = jnp.where(lane == base + n, first[n], out)
        out = jnp.where(lane == base + 3 + n, second[n], out)
    return out


def _fox_kernel(q_ref, k_ref, v_ref, f_ref, fb_ref, gq_ref, gk_ref, o_ref,
                c_scr, qa_scr, ka_scr, vt_scr, sa_scr, sb_scr, mb_scr, *, bq, bk, seq):
    p_idx = pl.program_id(1)
    one = jnp.ones((1, 1), F32)
    scale = 1.0 / math.sqrt(HEAD_DIM)

    @pl.when(p_idx == 0)
    def _():
        r_i = lax.broadcasted_iota(jnp.int32, (bk, bk), 0)
        c_i = lax.broadcasted_iota(jnp.int32, (bk, bk), 1)
        upto = jnp.where(c_i <= r_i, 1.0, 0.0).astype(BF16)

        def body(n, run):
            rs = pl.ds(pl.multiple_of(n * bk, bk), bk)
            y = f_ref[rs, :] + fb_ref[...]
            lg = jnp.minimum(y, 0.0) - jnp.log(1.0 + jnp.exp(-jnp.abs(y)))
            hi, mid, lo = _split3(lg)
            c = _dot(upto, hi) + _dot(upto, mid) + _dot(upto, lo) + run
            c_scr[rs, :] = c
            return c[bk - 1:bk, :]

        lax.fori_loop(0, seq // bk, body, jnp.zeros((1, LANES), F32))

    lane = lax.broadcasted_iota(jnp.int32, (bk, LANES), 1)
    lane_lo = lane < HEAD_DIM
    row_lo_k = lax.broadcasted_iota(jnp.int32, (LANES, bk), 0) < HEAD_DIM

    def prep(n, carry):
        rs = pl.ds(pl.multiple_of(n * bk, bk), bk)
        kn = _pair_rms(k_ref[rs, :], gk_ref[...], lane_lo)
        qn = _pair_rms(q_ref[rs, :], gq_ref[...], lane_lo) * (scale * LOG2E)
        c = c_scr[rs, :] * LOG2E
        v_t = v_ref[rs, :].T
        for h in range(2):
            ch = jnp.sum(jnp.where(lane == 2 * p_idx + h, c, 0.0), axis=1, keepdims=True)
            parts = [x.astype(F32) for x in _split3(ch)]
            mine = lane_lo if h == 0 else jnp.logical_not(lane_lo)
            base = HEAD_DIM * (1 - h)
            k_aug = _aug_lanes(lane, base, (one, one, one), [-x for x in parts])
            q_aug = _aug_lanes(lane, base, parts, (one, one, one))
            ka_scr[h, rs, :] = jnp.where(mine, kn, k_aug).astype(BF16)
            qa_scr[h, rs, :] = jnp.where(mine, qn, q_aug).astype(BF16)
            vt_scr[h, n] = jnp.where(row_lo_k if h == 0 else jnp.logical_not(row_lo_k), v_t, 1.0).astype(BF16)
        return carry

    lax.fori_loop(0, seq // bk, prep, 0)

    key_minus_qry = (lax.broadcasted_iota(jnp.int32, (bk, bq), 0)
                     - lax.broadcasted_iota(jnp.int32, (bk, bq), 1))
    row_lo = lax.broadcasted_iota(jnp.int32, (LANES, bq), 0) < HEAD_DIM

    nq = seq // bq
    n_items = sum((qi * bq + bq - 1) // bk + 1 for qi in range(nq))
    mb_scr[0] = jnp.zeros((bk, bq), F32)
    for r in range(bk // bq):
        mb_scr[r + 1] = jnp.where(key_minus_qry <= r * bq, 0.0, NEG_BIG)

    def unpack(item):
        qi = jnp.minimum(item[0], nq - 1)
        last = (qi * bq + bq - 1) // bk
        first = item[1] == 0
        kb = jnp.where(first, last, item[1] - 1)
        return qi, last, first, kb

    def advance(item):
        qi, last, _, _ = unpack(item)
        wrap = item[1] >= last
        return jnp.where(wrap, item[0] + 1, item[0]), jnp.where(wrap, 0, item[1] + 1)

    def scores_into(dst, item):
        qi, _, first, kb = unpack(item)
        qs = pl.ds(pl.multiple_of(qi * bq, bq), bq)
        rs = pl.ds(pl.multiple_of(kb * bk, bk), bk)
        bias = mb_scr[jnp.where(first, 1 + (qi * bq - kb * bk) // bq, 0)]
        for h in range(2):
            dst[h] = _nt_dot(ka_scr[h, rs, :], qa_scr[h, qs, :]) + bias

    def write_out(qi, sts):
        acc0, acc1 = sts[0][1], sts[1][1]
        o_t = jnp.where(row_lo, acc0 / acc0[HEAD_DIM:HEAD_DIM + 1, :], acc1 / acc1[0:1, :])
        o_ref[pl.ds(pl.multiple_of(qi * bq, bq), bq), :] = o_t.T.astype(o_ref.dtype)

    def absorb(src, item, prev_qi, sts):
        qi, _, first, kb = unpack(item)
        write_out(prev_qi, sts)
        out = []
        for h in range(2):
            m_run, acc = sts[h]
            m_run = jnp.where(first, NEG_BIG, m_run)
            s = src[h]
            m_new = jnp.maximum(m_run, jnp.max(s, axis=0, keepdims=True))
            p = jnp.exp2(s - m_new)
            acc = jnp.exp2(m_run - m_new) * acc + _dot(vt_scr[h, kb], p.astype(BF16))
            out.append((m_new, acc))
        return qi, tuple(out)

    unroll = max([u for u in (8, 4) if n_items % u == 0 and u * bk <= 2048] + [2])
    bufs = (sa_scr, sb_scr)

    def run(count, ca):
        held, prev_qi, sts = ca
        for k in range(count):
            nxt = advance(held)
            scores_into(bufs[(k + 1) % 2], nxt)
            prev_qi, sts = absorb(bufs[k % 2], held, prev_qi, sts)
            held = nxt
        return held, prev_qi, sts

    start = (jnp.int32(0), jnp.int32(0))
    init = (jnp.full((1, bq), NEG_BIG, F32), jnp.ones((LANES, bq), F32))
    scores_into(sa_scr, start)
    ca = lax.fori_loop(0, n_items // unroll, lambda u, ca: run(unroll, ca),
                       (start, jnp.int32(0), (init, init)))
    _, last_qi, sts = run(n_items % unroll, ca)
    write_out(last_qi, sts)


def _fox_call(proj, fbias, gq, gk, l, batch, seq, bq, bk):
    m = proj.shape[0]
    col = lambda blk: pl.BlockSpec((seq, LANES), lambda b, p, blk=blk: (b, blk + p))
    vec = _layer(l, 1, LANES)
    return pl.pallas_call(
        functools.partial(_fox_kernel, bq=bq, bk=bk, seq=seq),
        grid=(batch, FOX_WIDTH // LANES),
        in_specs=[col(FOX_Q_BLK), col(FOX_K_BLK), col(FOX_V_BLK),
                  pl.BlockSpec((seq, LANES), lambda b, p: (b, FOX_F_BLK)), vec, vec, vec],
        out_specs=pl.BlockSpec((seq, LANES), lambda b, p: (b, p)),
        out_shape=jax.ShapeDtypeStruct((m, FOX_WIDTH), BF16),
        scratch_shapes=[pltpu.VMEM((seq, LANES), F32),
                        pltpu.VMEM((2, seq, LANES), BF16),
                        pltpu.VMEM((2, seq, LANES), BF16),
                        pltpu.VMEM((2, seq // bk, LANES, bk), BF16),
                        pltpu.VMEM((2, bk, bq), F32),
                        pltpu.VMEM((2, bk, bq), F32),
                        pltpu.VMEM((bk // bq + 1, bk, bq), F32)],
        compiler_params=pltpu.CompilerParams(
            dimension_semantics=("arbitrary", "arbitrary"), vmem_limit_bytes=VMEM_LIMIT),
        name="forgetting",
    )(proj, proj, proj, proj, fbias, gq, gk)


def _mlp_kernel(x_ref, ohg_ref, osb_ref, ofx_ref, wo1_ref, wo2_ref, wo3_ref, g2_ref,
                w1_ref, w2_ref, o_ref, *, tf):
    x1 = (x_ref[...] + _dot(ohg_ref[...], wo1_ref[...]) + _dot(osb_ref[...], wo2_ref[...])
          + _dot(ofx_ref[...], wo3_ref[...]))
    ms = jnp.mean(x1 * x1, axis=-1, keepdims=True)
    h2 = (x1 * lax.rsqrt(ms + EPS) * g2_ref[...]).astype(BF16)
    o_ref[...] = x1
    acc = None
    for f in range(w1_ref.shape[1] // tf):
        a = _dot(h2, w1_ref[:, f * tf:(f + 1) * tf])
        a = jnp.square(jnp.maximum(a, 0.0)).astype(BF16)
        y = _dot(a, w2_ref[f * tf:(f + 1) * tf, :])
        acc = y if acc is None else acc + y
    o_ref[...] = o_ref[...] + acc


def _mlp_call(x, ohg, osb, ofx, wo1, wo2, wo3, g2, w1, w2, l, tm, tf):
    m, d = x.shape
    row = lambda w: pl.BlockSpec((tm, w), lambda i: (i, 0))
    whole = lambda a: _layer(l, *a.shape[1:])
    return pl.pallas_call(
        functools.partial(_mlp_kernel, tf=tf),
        grid=(m // tm,),
        in_specs=[row(d), row(HG_WIDTH), row(SB_WIDTH), row(FOX_WIDTH),
                  whole(wo1), whole(wo2), whole(wo3), whole(g2), whole(w1), whole(w2)],
        out_specs=row(d),
        out_shape=jax.ShapeDtypeStruct((m, d), F32),
        compiler_params=pltpu.CompilerParams(
            dimension_semantics=("arbitrary",), vmem_limit_bytes=VMEM_LIMIT),
        name="outproj_mlp",
    )(x, ohg, osb, ofx, wo1, wo2, wo3, g2, w1, w2)


def _tile_sizes(batch, seq):
    m = batch * seq
    return dict(
        proj_tm=min(512, m),
        hg_tb=min(512, seq),
        sb_bq=min(256, seq),
        sb_sub=max(1, min(4, seq // 256)),
        fox_bq=min(256, seq),
        fox_bk=min(512, seq),
        mlp_tm=min(512, m),
        mlp_tf=1024,
    )


def kernel(x, lb_logits, norm1_g, w_in, hg_norm_g, sb_q_norm_g, sb_k_norm_g, fox_q_norm_g,
           fox_k_norm_g, fox_f_bias, w_out, norm2_g, w_ff1, w_ff2):
    batch, seq, d = x.shape
    assert d == D_MODEL and seq % HG_ROWS == 0 and x.dtype == F32
    ts = _tile_sizes(batch, seq)
    assert seq % (ts["sb_bq"] * ts["sb_sub"]) == 0 and seq % ts["fox_bk"] == 0 and ts["fox_bk"] % ts["fox_bq"] == 0
    assert seq % ts["hg_tb"] == 0 and seq % PRO_ROWS == 0
    m = batch * seq
    xf = x.reshape(m, d)
    row = lambda a: a.astype(F32)[:, None, :]
    pair = lambda g: row(jnp.tile(g, (1, 2)))
    w_in_p = jnp.pad(w_in, ((0, 0), (0, 0), (0, IN_COLS_PAD - IN_COLS))).astype(BF16)
    wo = w_out.astype(BF16)
    wo_hg, wo_sb, wo_fx = wo[:, :HG_WIDTH], wo[:, HG_WIDTH:HG_WIDTH + SB_WIDTH], wo[:, HG_WIDTH + SB_WIDTH:]
    w1, w2 = w_ff1.astype(BF16), w_ff2.astype(BF16)
    g1, g2, g_hg = row(norm1_g), row(norm2_g), row(jnp.tile(hg_norm_g, (1, HG_HEADS)))
    g_sq, g_sk, g_fq, g_fk = pair(sb_q_norm_g), pair(sb_k_norm_g), pair(fox_q_norm_g), pair(fox_k_norm_g)
    fbias = row(jnp.pad(fox_f_bias, ((0, 0), (0, LANES - FOX_HEADS))))
    lbl = lb_logits.astype(F32)

    for l in range(DEPTH):
        proj = _proj_call(xf, g1, w_in_p, l, ts["proj_tm"])
        o_hg = _hgrn_call(l, lbl, proj, g_hg, batch, seq, ts["hg_tb"])
        o_sb = _sb_call(proj, g_sq, g_sk, l, batch, seq, ts["sb_bq"], ts["sb_sub"])
        o_fx = _fox_call(proj, fbias, g_fq, g_fk, l, batch, seq, ts["fox_bq"], ts["fox_bk"])
        xf = _mlp_call(xf, o_hg, o_sb, o_fx, wo_hg, wo_sb, wo_fx, g2, w1, w2, l,
                       ts["mlp_tm"], ts["mlp_tf"])
    return xf.reshape(batch, seq, d)
```

```python
import functools
import math

import jax
import jax.numpy as jnp
from jax import lax
from jax.experimental import pallas as pl
from jax.experimental.pallas import tpu as pltpu

F32 = jnp.float32
BF16 = jnp.bfloat16

D_MODEL = 1024
DEPTH = 4
HEAD_DIM = 64
HG_HEADS = 4
HG_KW = 256
HG_WIDTH = 256
SB_WIDTH = 384
FOX_WIDTH = 384
FOX_HEADS = 6
IN_COLS = 4 * 256 + 3 * 384 + 3 * 384 + FOX_HEADS
D_FF = 4 * D_MODEL
EPS = 1e-6
LB_FLOOR = 1e-30
NEG_BIG = -1e30
LOG2E = math.log2(math.e)

LANES = 128
IN_COLS_PAD = 27 * LANES
SB_Q_BLK, SB_K_BLK, SB_V_BLK = 8, 11, 14
FOX_Q_BLK, FOX_K_BLK, FOX_V_BLK, FOX_F_BLK = 17, 20, 23, 26

SB_ZERO_LOG = -105.0
SUB = 16
HG_ROWS = 256
VMEM_LIMIT = 56 * 1024 * 1024


def _nt_dot(a, b):
    return lax.dot_general(a, b, (((1,), (1,)), ((), ())), preferred_element_type=F32)


def _dot(a, b):
    return jnp.dot(a, b, preferred_element_type=F32)


def _split2(x):
    hi = x.astype(BF16)
    lo = (x - hi.astype(F32)).astype(BF16)
    return hi, lo


def _split3(x):
    hi = x.astype(BF16)
    r1 = x - hi.astype(F32)
    mid = r1.astype(BF16)
    lo = (r1 - mid.astype(F32)).astype(BF16)
    return hi, mid, lo


def _softplus(z):
    return jnp.maximum(z, 0.0) + jnp.log(1.0 + jnp.exp(-jnp.abs(z)))


def _softplus2(z2):
    return jnp.maximum(z2, 0.0) + jnp.log2(1.0 + jnp.exp2(-jnp.abs(z2)))


def _proj_kernel(x_ref, g_ref, w_ref, o_ref):
    x = x_ref[...]
    ms = jnp.mean(x * x, axis=-1, keepdims=True)
    h = (x * lax.rsqrt(ms + EPS) * g_ref[...]).astype(BF16)
    o_ref[...] = _dot(h, w_ref[...])


def _layer(l, *shape):
    return pl.BlockSpec((None,) + shape, lambda *_: (l,) + (0,) * len(shape))


def _proj_call(x, g, w, l, tm):
    m, d = x.shape
    n = w.shape[2]
    return pl.pallas_call(
        _proj_kernel,
        grid=(m // tm,),
        in_specs=[
            pl.BlockSpec((tm, d), lambda i: (i, 0)),
            _layer(l, 1, d),
            _layer(l, d, n),
        ],
        out_specs=pl.BlockSpec((tm, n), lambda i: (i, 0)),
        out_shape=jax.ShapeDtypeStruct((m, n), F32),
        compiler_params=pltpu.CompilerParams(
            dimension_semantics=("arbitrary",), vmem_limit_bytes=VMEM_LIMIT),
        name="proj",
    )(x, g, w)


def _hgrn_kernel(lbl_ref, q_ref, f_ref, i_ref, g_ref, gn_ref, o_ref, st_scr, oi_scr, *, layer, tb):
    @pl.when(pl.program_id(1) == 0)
    def _():
        st_scr[...] = jnp.zeros_like(st_scr)

    rows = [lbl_ref[j:j + 1, :] for j in range(DEPTH)]
    mx = functools.reduce(jnp.maximum, rows)
    ex = [jnp.exp(r - mx) for r in rows]
    den = functools.reduce(lambda a, b: a + b, ex)
    lb = jnp.zeros_like(mx)
    for j in range(1, layer + 1):
        lb = lb + ex[j] / den
    log_lb = jnp.log(jnp.maximum(lb, LB_FLOOR))
    one_m_lb = 1.0 - lb
    gn = gn_ref[...]

    r_i = lax.broadcasted_iota(jnp.int32, (HG_KW, HG_KW), 0)
    c_i = lax.broadcasted_iota(jnp.int32, (HG_KW, HG_KW), 1)
    same_head = (r_i >> 6) == (c_i >> 6)
    ones_bd = jnp.where(same_head, 1.0, 0.0).astype(BF16)
    t_mod = lax.broadcasted_iota(jnp.int32, (HG_ROWS, HG_KW), 0) & (SUB - 1)
    lane_sub = lax.broadcasted_iota(jnp.int32, (HG_KW, HG_ROWS), 1) // SUB
    t_r = lax.broadcasted_iota(jnp.int32, (HG_ROWS, HG_ROWS), 0)
    t_c = lax.broadcasted_iota(jnp.int32, (HG_ROWS, HG_ROWS), 1)
    in_sub = (t_r // SUB) == (t_c // SUB)
    same_sub = jnp.where(in_sub, 1.0, 0.0).astype(BF16)
    upto_sub = jnp.where(jnp.logical_and(in_sub, t_c <= t_r), 1.0, 0.0).astype(BF16)

    def chunk(c, carry):
        rs = pl.ds(pl.multiple_of(c * HG_ROWS, HG_ROWS), HG_ROWS)
        q = q_ref[rs, :]
        fl = f_ref[rs, :]
        v = i_ref[rs, :]
        g = g_ref[rs, :]

        sp = _softplus(fl)
        lf = jnp.maximum(fl, log_lb) + jnp.log(1.0 + jnp.exp(-jnp.abs(fl - log_lb))) - sp
        kk = one_m_lb * jnp.exp(-sp)

        parts = _split3(lf)
        b = functools.reduce(lambda x, y: x + y, [_dot(upto_sub, part) for part in parts])
        beta = functools.reduce(lambda x, y: x + y, [_dot(same_sub, part) for part in parts])
        rest = beta - b

        b2 = b * LOG2E
        w2 = -sp * LOG2E - b2
        q_lb = q * one_m_lb
        acc = _dot((q * kk).astype(BF16), ones_bd) * v
        for d in range(1, SUB):
            e = jnp.where(t_mod >= d, b2 + pltpu.roll(w2, d, 0), NEG_BIG)
            p = q_lb * jnp.exp2(e)
            acc = acc + _dot(p.astype(BF16), ones_bd) * pltpu.roll(v, d, 0)

        qt = (q * jnp.exp2(b2)).astype(BF16)
        kt = (kk * jnp.exp(rest)).astype(BF16)
        v_t = v.T
        for j in range(HG_ROWS // SUB):
            st = st_scr[...]
            oi_scr[j * SUB:(j + 1) * SUB, :] = _nt_dot(qt[j * SUB:(j + 1) * SUB, :], st.astype(BF16))
            vm = jnp.where(lane_sub == j, v_t, 0.0).astype(BF16)
            ut = _dot(vm, kt)
            dec = jnp.exp(beta[j * SUB:j * SUB + 1, :])
            st_scr[...] = st * dec + jnp.where(same_head, ut, 0.0)

        o = acc + oi_scr[...]
        hi, lo = _split2(o * o)
        msq = (_dot(hi, ones_bd) + _dot(lo, ones_bd)) * (1.0 / HEAD_DIM)
        y = o * lax.rsqrt(msq + EPS) * gn
        o_ref[rs, :] = (y * (g / (1.0 + jnp.exp(-g)))).astype(o_ref.dtype)
        return carry

    lax.fori_loop(0, tb // HG_ROWS, chunk, 0)


def _hgrn_call(layer, lb_logits, proj, gn, batch, seq, tb):
    m = proj.shape[0]
    nt = seq // tb

    def col(j):
        return pl.BlockSpec((tb, HG_KW), lambda b, t, j=j: (b * nt + t, j))

    return pl.pallas_call(
        functools.partial(_hgrn_kernel, layer=layer, tb=tb),
        grid=(batch, nt),
        in_specs=[pl.BlockSpec((DEPTH, HG_KW), lambda b, t: (0, 0)),
                  col(0), col(1), col(2), col(3), _layer(layer, 1, HG_WIDTH)],
        out_specs=pl.BlockSpec((tb, HG_WIDTH), lambda b, t: (b * nt + t, 0)),
        out_shape=jax.ShapeDtypeStruct((m, HG_WIDTH), BF16),
        scratch_shapes=[pltpu.VMEM((HG_WIDTH, HG_KW), F32),
                        pltpu.VMEM((HG_ROWS, HG_WIDTH), F32)],
        compiler_params=pltpu.CompilerParams(
            dimension_semantics=("arbitrary", "arbitrary"), vmem_limit_bytes=VMEM_LIMIT),
        name="hgrn2",
    )(lb_logits, proj, proj, proj, proj, gn)


def _pair_rms(x, gain, lane_lo):
    x2 = x * x
    s0 = jnp.sum(jnp.where(lane_lo, x2, 0.0), axis=1, keepdims=True)
    s1 = jnp.sum(jnp.where(lane_lo, 0.0, x2), axis=1, keepdims=True)
    ms = jnp.where(lane_lo, s0, s1) * (1.0 / HEAD_DIM)
    return x * lax.rsqrt(ms + EPS) * gain


PRO_ROWS = 512


def _sb_kernel(q0_ref, q1_ref, q2_ref, k0_ref, k1_ref, k2_ref, v0_ref, v1_ref, v2_ref,
               gq_ref, gk_ref, o_ref, kn_scr, vb_scr, acc_scr, car_scr, *, bq, n_sub, seq):
    g = pl.program_id(1)
    q_refs = (q0_ref, q1_ref, q2_ref)
    k_refs = (k0_ref, k1_ref, k2_ref)
    v_refs = (v0_ref, v1_ref, v2_ref)
    n_pairs = len(q_refs)

    @pl.when(g == 0)
    def _():
        lane_lo = lax.broadcasted_iota(jnp.int32, (PRO_ROWS, LANES), 1) < HEAD_DIM

        def body(c, carry):
            rs = pl.ds(pl.multiple_of(c * PRO_ROWS, PRO_ROWS), PRO_ROWS)
            for p in range(n_pairs):
                kn_scr[p, rs, :] = _pair_rms(k_refs[p][rs, :], gk_ref[...], lane_lo).astype(BF16)
                vb_scr[p, rs, :] = v_refs[p][rs, :].astype(BF16)
            return carry

        lax.fori_loop(0, seq // PRO_ROWS, body, 0)

    lane_lo = lax.broadcasted_iota(jnp.int32, (bq, LANES), 1) < HEAD_DIM
    scale = 1.0 / math.sqrt(HEAD_DIM)
    r_i = lax.broadcasted_iota(jnp.int32, (bq, bq), 0)
    c_i = lax.broadcasted_iota(jnp.int32, (bq, bq), 1)
    before = c_i < r_i
    later = jnp.where(r_i > c_i, 1.0, 0.0).astype(BF16)

    heads = range(2 * n_pairs)

    def q_block(sub, carry_unused):
        i = g * n_sub + sub
        qs = pl.ds(pl.multiple_of(sub * bq, bq), bq)
        qms = []
        for p in range(n_pairs):
            qn = _pair_rms(q_refs[p][qs, :], gq_ref[...], lane_lo) * (scale * LOG2E)
            qms.append(jnp.where(lane_lo, qn, 0.0).astype(BF16))
            qms.append(jnp.where(lane_lo, 0.0, qn).astype(BF16))

        def absorb(tiles, fresh):
            rows = [pl.ds(pl.multiple_of(kb * bq, bq), bq) for kb, _ in tiles]
            masks = [mask for _, mask in tiles]
            nt = range(len(tiles))
            zs = [[_nt_dot(qms[h], kn_scr[h // 2, rows[t], :]) for h in heads] for t in nt]
            sps = [[_softplus2(z) for z in zt] for zt in zs]
            us = [spt if masks[t] is None else [jnp.where(masks[t], sp, 0.0) for sp in spt]
                  for t, spt in enumerate(sps)]
            tails = [[_dot(u.astype(BF16), later) for u in ut] for ut in us]
            low = None
            for h in heads:
                carry = None if fresh else car_scr[h]
                acc = None if fresh else acc_scr[h]
                for t in nt:
                    row = jnp.broadcast_to(jnp.sum(us[t][h], axis=1, keepdims=True), (bq, LANES))
                    e = zs[t][h] - sps[t][h] - tails[t][h]
                    if carry is not None:
                        e = e - jnp.concatenate([carry] * (bq // LANES), axis=1)
                        row = row + carry
                    a = jnp.exp2(e)
                    if masks[t] is not None:
                        a = jnp.where(masks[t], a, 0.0)
                    pv = _dot(a.astype(BF16), vb_scr[h // 2, rows[t], :])
                    acc = pv if acc is None else acc + pv
                    carry = row
                acc_scr[h] = acc
                car_scr[h] = carry
                low = carry if low is None else jnp.minimum(low, carry)
            return jnp.min(low)

        low = lax.cond(i > 0,
                       lambda: absorb([(i, before), (i - 1, None)], True),
                       lambda: absorb([(i, before)], True))

        def live(st):
            return jnp.logical_and(st[0] >= 0, st[1] < -SB_ZERO_LOG * LOG2E)

        lax.while_loop(live, lambda st: (st[0] - 1, absorb([(st[0], None)], False)), (i - 2, low))
        for p in range(n_pairs):
            o_ref[qs, p * LANES:(p + 1) * LANES] = jnp.where(
                lane_lo, acc_scr[2 * p], acc_scr[2 * p + 1]).astype(o_ref.dtype)
        return carry_unused

    lax.fori_loop(0, n_sub, q_block, 0)


def _sb_call(proj, gq, gk, l, batch, seq, bq, n_sub):
    m = proj.shape[0]
    rows = bq * n_sub
    nq = seq // rows
    n_pairs = SB_WIDTH // LANES
    q_spec = lambda p: pl.BlockSpec((rows, LANES), lambda b, i, p=p: (b * nq + i, SB_Q_BLK + p))
    k_spec = lambda blk, p: pl.BlockSpec((seq, LANES), lambda b, i, p=p: (b, blk + p))
    gain = _layer(l, 1, LANES)
    return pl.pallas_call(
        functools.partial(_sb_kernel, bq=bq, n_sub=n_sub, seq=seq),
        grid=(batch, nq),
        in_specs=([q_spec(p) for p in range(n_pairs)]
                  + [k_spec(SB_K_BLK, p) for p in range(n_pairs)]
                  + [k_spec(SB_V_BLK, p) for p in range(n_pairs)] + [gain, gain]),
        out_specs=pl.BlockSpec((rows, SB_WIDTH), lambda b, i: (b * nq + i, 0)),
        out_shape=jax.ShapeDtypeStruct((m, SB_WIDTH), BF16),
        scratch_shapes=[pltpu.VMEM((n_pairs, seq, LANES), BF16),
                        pltpu.VMEM((n_pairs, seq, LANES), BF16),
                        pltpu.VMEM((2 * n_pairs, bq, LANES), F32),
                        pltpu.VMEM((2 * n_pairs, bq, LANES), F32)],
        compiler_params=pltpu.CompilerParams(
            dimension_semantics=("arbitrary", "arbitrary"), vmem_limit_bytes=VMEM_LIMIT),
        name="stickbreak",
    )(*([proj] * (3 * n_pairs)), gq, gk)


def _aug_lanes(lane, base, first, second):
    out = jnp.zeros(lane.shape, F32)
    for n in range(3):
        out = jnp.where(lane == base + n, first[n], out)
        out = jnp.where(lane == base + 3 + n, second[n], out)
    return out


def _fox_kernel(q_ref, k_ref, v_ref, f_ref, fb_ref, gq_ref, gk_ref, o_ref,
                c_scr, qa_scr, ka_scr, vt_scr, sa_scr, sb_scr, mb_scr, *, bq, bk, seq):
    p_idx = pl.program_id(1)
    one = jnp.ones((1, 1), F32)
    scale = 1.0 / math.sqrt(HEAD_DIM)

    @pl.when(p_idx == 0)
    def _():
        r_i = lax.broadcasted_iota(jnp.int32, (bk, bk), 0)
        c_i = lax.broadcasted_iota(jnp.int32, (bk, bk), 1)
        upto = jnp.where(c_i <= r_i, 1.0, 0.0).astype(BF16)

        def body(n, run):
            rs = pl.ds(pl.multiple_of(n * bk, bk), bk)
            y = f_ref[rs, :] + fb_ref[...]
            lg = jnp.minimum(y, 0.0) - jnp.log(1.0 + jnp.exp(-jnp.abs(y)))
            hi, mid, lo = _split3(lg)
            c = _dot(upto, hi) + _dot(upto, mid) + _dot(upto, lo) + run
            c_scr[rs, :] = c
            return c[bk - 1:bk, :]

        lax.fori_loop(0, seq // bk, body, jnp.zeros((1, LANES), F32))

    lane = lax.broadcasted_iota(jnp.int32, (bk, LANES), 1)
    lane_lo = lane < HEAD_DIM
    row_lo_k = lax.broadcasted_iota(jnp.int32, (LANES, bk), 0) < HEAD_DIM

    def prep(n, carry):
        rs = pl.ds(pl.multiple_of(n * bk, bk), bk)
        kn = _pair_rms(k_ref[rs, :], gk_ref[...], lane_lo)
        qn = _pair_rms(q_ref[rs, :], gq_ref[...], lane_lo) * (scale * LOG2E)
        c = c_scr[rs, :] * LOG2E
        v_t = v_ref[rs, :].T
        for h in range(2):
            ch = jnp.sum(jnp.where(lane == 2 * p_idx + h, c, 0.0), axis=1, keepdims=True)
            parts = [x.astype(F32) for x in _split3(ch)]
            mine = lane_lo if h == 0 else jnp.logical_not(lane_lo)
            base = HEAD_DIM * (1 - h)
            k_aug = _aug_lanes(lane, base, (one, one, one), [-x for x in parts])
            q_aug = _aug_lanes(lane, base, parts, (one, one, one))
            ka_scr[h, rs, :] = jnp.where(mine, kn, k_aug).astype(BF16)
            qa_scr[h, rs, :] = jnp.where(mine, qn, q_aug).astype(BF16)
            vt_scr[h, n] = jnp.where(row_lo_k if h == 0 else jnp.logical_not(row_lo_k), v_t, 1.0).astype(BF16)
        return carry

    lax.fori_loop(0, seq // bk, prep, 0)

    key_minus_qry = (lax.broadcasted_iota(jnp.int32, (bk, bq), 0)
                     - lax.broadcasted_iota(jnp.int32, (bk, bq), 1))
    row_lo = lax.broadcasted_iota(jnp.int32, (LANES, bq), 0) < HEAD_DIM

    nq = seq // bq
    n_items = sum((qi * bq + bq - 1) // bk + 1 for qi in range(nq))
    mb_scr[0] = jnp.zeros((bk, bq), F32)
    for r in range(bk // bq):
        mb_scr[r + 1] = jnp.where(key_minus_qry <= r * bq, 0.0, NEG_BIG)

    def unpack(item):
        qi = jnp.minimum(item[0], nq - 1)
        last = (qi * bq + bq - 1) // bk
        first = item[1] == 0
        kb = jnp.where(first, last, item[1] - 1)
        return qi, last, first, kb

    def advance(item):
        qi, last, _, _ = unpack(item)
        wrap = item[1] >= last
        return jnp.where(wrap, item[0] + 1, item[0]), jnp.where(wrap, 0, item[1] + 1)

    def scores_into(dst, item):
        qi, _, first, kb = unpack(item)
        qs = pl.ds(pl.multiple_of(qi * bq, bq), bq)
        rs = pl.ds(pl.multiple_of(kb * bk, bk), bk)
        bias = mb_scr[jnp.where(first, 1 + (qi * bq - kb * bk) // bq, 0)]
        for h in range(2):
            dst[h] = _nt_dot(ka_scr[h, rs, :], qa_scr[h, qs, :]) + bias

    def write_out(qi, sts):
        acc0, acc1 = sts[0][1], sts[1][1]
        o_t = jnp.where(row_lo, acc0 / acc0[HEAD_DIM:HEAD_DIM + 1, :], acc1 / acc1[0:1, :])
        o_ref[pl.ds(pl.multiple_of(qi * bq, bq), bq), :] = o_t.T.astype(o_ref.dtype)

    def absorb(src, item, prev_qi, sts):
        qi, _, first, kb = unpack(item)
        write_out(prev_qi, sts)
        out = []
        for h in range(2):
            m_run, acc = sts[h]
            m_run = jnp.where(first, NEG_BIG, m_run)
            s = src[h]
            m_new = jnp.maximum(m_run, jnp.max(s, axis=0, keepdims=True))
            p = jnp.exp2(s - m_new)
            acc = jnp.exp2(m_run - m_new) * acc + _dot(vt_scr[h, kb], p.astype(BF16))
            out.append((m_new, acc))
        return qi, tuple(out)

    unroll = max([u for u in (8, 4) if n_items % u == 0 and u * bk <= 2048] + [2])
    bufs = (sa_scr, sb_scr)

    def run(count, ca):
        held, prev_qi, sts = ca
        for k in range(count):
            nxt = advance(held)
            scores_into(bufs[(k + 1) % 2], nxt)
            prev_qi, sts = absorb(bufs[k % 2], held, prev_qi, sts)
            held = nxt
        return held, prev_qi, sts

    start = (jnp.int32(0), jnp.int32(0))
    init = (jnp.full((1, bq), NEG_BIG, F32), jnp.ones((LANES, bq), F32))
    scores_into(sa_scr, start)
    ca = lax.fori_loop(0, n_items // unroll, lambda u, ca: run(unroll, ca),
                       (start, jnp.int32(0), (init, init)))
    _, last_qi, sts = run(n_items % unroll, ca)
    write_out(last_qi, sts)


def _fox_call(proj, fbias, gq, gk, l, batch, seq, bq, bk):
    m = proj.shape[0]
    col = lambda blk: pl.BlockSpec((seq, LANES), lambda b, p, blk=blk: (b, blk + p))
    vec = _layer(l, 1, LANES)
    return pl.pallas_call(
        functools.partial(_fox_kernel, bq=bq, bk=bk, seq=seq),
        grid=(batch, FOX_WIDTH // LANES),
        in_specs=[col(FOX_Q_BLK), col(FOX_K_BLK), col(FOX_V_BLK),
                  pl.BlockSpec((seq, LANES), lambda b, p: (b, FOX_F_BLK)), vec, vec, vec],
        out_specs=pl.BlockSpec((seq, LANES), lambda b, p: (b, p)),
        out_shape=jax.ShapeDtypeStruct((m, FOX_WIDTH), BF16),
        scratch_shapes=[pltpu.VMEM((seq, LANES), F32),
                        pltpu.VMEM((2, seq, LANES), BF16),
                        pltpu.VMEM((2, seq, LANES), BF16),
                        pltpu.VMEM((2, seq // bk, LANES, bk), BF16),
                        pltpu.VMEM((2, bk, bq), F32),
                        pltpu.VMEM((2, bk, bq), F32),
                        pltpu.VMEM((bk // bq + 1, bk, bq), F32)],
        compiler_params=pltpu.CompilerParams(
            dimension_semantics=("arbitrary", "arbitrary"), vmem_limit_bytes=VMEM_LIMIT),
        name="forgetting",
    )(proj, proj, proj, proj, fbias, gq, gk)


def _mlp_kernel(x_ref, ohg_ref, osb_ref, ofx_ref, wo1_ref, wo2_ref, wo3_ref, g2_ref,
                w1_ref, w2_ref, o_ref, *, tf):
    x1 = (x_ref[...] + _dot(ohg_ref[...], wo1_ref[...]) + _dot(osb_ref[...], wo2_ref[...])
          + _dot(ofx_ref[...], wo3_ref[...]))
    ms = jnp.mean(x1 * x1, axis=-1, keepdims=True)
    h2 = (x1 * lax.rsqrt(ms + EPS) * g2_ref[...]).astype(BF16)
    o_ref[...] = x1
    acc = None
    for f in range(w1_ref.shape[1] // tf):
        a = _dot(h2, w1_ref[:, f * tf:(f + 1) * tf])
        a = jnp.square(jnp.maximum(a, 0.0)).astype(BF16)
        y = _dot(a, w2_ref[f * tf:(f + 1) * tf, :])
        acc = y if acc is None else acc + y
    o_ref[...] = o_ref[...] + acc


def _mlp_call(x, ohg, osb, ofx, wo1, wo2, wo3, g2, w1, w2, l, tm, tf):
    m, d = x.shape
    row = lambda w: pl.BlockSpec((tm, w), lambda i: (i, 0))
    whole = lambda a: _layer(l, *a.shape[1:])
    return pl.pallas_call(
        functools.partial(_mlp_kernel, tf=tf),
        grid=(m // tm,),
        in_specs=[row(d), row(HG_WIDTH), row(SB_WIDTH), row(FOX_WIDTH),
                  whole(wo1), whole(wo2), whole(wo3), whole(g2), whole(w1), whole(w2)],
        out_specs=row(d),
        out_shape=jax.ShapeDtypeStruct((m, d), F32),
        compiler_params=pltpu.CompilerParams(
            dimension_semantics=("arbitrary",), vmem_limit_bytes=VMEM_LIMIT),
        name="outproj_mlp",
    )(x, ohg, osb, ofx, wo1, wo2, wo3, g2, w1, w2)


def _tile_sizes(batch, seq):
    m = batch * seq
    return dict(
        proj_tm=min(512, m),
        hg_tb=min(512, seq),
        sb_bq=min(256, seq),
        sb_sub=max(1, min(4, seq // 256)),
        fox_bq=min(256, seq),
        fox_bk=min(512, seq),
        mlp_tm=min(512, m),
        mlp_tf=1024,
    )


def kernel(x, lb_logits, norm1_g, w_in, hg_norm_g, sb_q_norm_g, sb_k_norm_g, fox_q_norm_g,
           fox_k_norm_g, fox_f_bias, w_out, norm2_g, w_ff1, w_ff2):
    batch, seq, d = x.shape
    assert d == D_MODEL and seq % HG_ROWS == 0 and x.dtype == F32
    ts = _tile_sizes(batch, seq)
    assert seq % (ts["sb_bq"] * ts["sb_sub"]) == 0 and seq % ts["fox_bk"] == 0 and ts["fox_bk"] % ts["fox_bq"] == 0
    assert seq % ts["hg_tb"] == 0 and seq % PRO_ROWS == 0
    m = batch * seq
    xf = x.reshape(m, d)
    row = lambda a: a.astype(F32)[:, None, :]
    pair = lambda g: row(jnp.tile(g, (1, 2)))
    w_in_p = jnp.pad(w_in, ((0, 0), (0, 0), (0, IN_COLS_PAD - IN_COLS))).astype(BF16)
    wo = w_out.astype(BF16)
    wo_hg, wo_sb, wo_fx = wo[:, :HG_WIDTH], wo[:, HG_WIDTH:HG_WIDTH + SB_WIDTH], wo[:, HG_WIDTH + SB_WIDTH:]
    w1, w2 = w_ff1.astype(BF16), w_ff2.astype(BF16)
    g1, g2, g_hg = row(norm1_g), row(norm2_g), row(jnp.tile(hg_norm_g, (1, HG_HEADS)))
    g_sq, g_sk, g_fq, g_fk = pair(sb_q_norm_g), pair(sb_k_norm_g), pair(fox_q_norm_g), pair(fox_k_norm_g)
    fbias = row(jnp.pad(fox_f_bias, ((0, 0), (0, LANES - FOX_HEADS))))
    lbl = lb_logits.astype(F32)

    for l in range(DEPTH):
        proj = _proj_call(xf, g1, w_in_p, l, ts["proj_tm"])
        o_hg = _hgrn_call(l, lbl, proj, g_hg, batch, seq, ts["hg_tb"])
        o_sb = _sb_call(proj, g_sq, g_sk, l, batch, seq, ts["sb_bq"], ts["sb_sub"])
        o_fx = _fox_call(proj, fbias, g_fq, g_fk, l, batch, seq, ts["fox_bq"], ts["fox_bk"])
        xf = _mlp_call(xf, o_hg, o_sb, o_fx, wo_hg, wo_sb, wo_fx, g2, w1, w2, l,
                       ts["mlp_tm"], ts["mlp_tf"])
    return xf.reshape(batch, seq, d)
```

```python
import functools
import math

import jax
import jax.numpy as jnp
from jax import lax
from jax.experimental import pallas as pl
from jax.experimental.pallas import tpu as pltpu

F32 = jnp.float32
BF16 = jnp.bfloat16

D_MODEL = 1024
DEPTH = 4
HEAD_DIM = 64
HG_HEADS = 4
HG_KW = 256
HG_WIDTH = 256
SB_WIDTH = 384
FOX_WIDTH = 384
FOX_HEADS = 6
IN_COLS = 4 * 256 + 3 * 384 + 3 * 384 + FOX_HEADS
D_FF = 4 * D_MODEL
EPS = 1e-6
LB_FLOOR = 1e-30
NEG_BIG = -1e30
LOG2E = math.log2(math.e)

LANES = 128
IN_COLS_PAD = 27 * LANES
SB_Q_BLK, SB_K_BLK, SB_V_BLK = 8, 11, 14
FOX_Q_BLK, FOX_K_BLK, FOX_V_BLK, FOX_F_BLK = 17, 20, 23, 26

SB_ZERO_LOG = -105.0
SUB = 16
HG_ROWS = 256
VMEM_LIMIT = 56 * 1024 * 1024


def _nt_dot(a, b):
    return lax.dot_general(a, b, (((1,), (1,)), ((), ())), preferred_element_type=F32)


def _dot(a, b):
    return jnp.dot(a, b, preferred_element_type=F32)


def _split2(x):
    hi = x.astype(BF16)
    lo = (x - hi.astype(F32)).astype(BF16)
    return hi, lo


def _split3(x):
    hi = x.astype(BF16)
    r1 = x - hi.astype(F32)
    mid = r1.astype(BF16)
    lo = (r1 - mid.astype(F32)).astype(BF16)
    return hi, mid, lo


def _softplus(z):
    return jnp.maximum(z, 0.0) + jnp.log(1.0 + jnp.exp(-jnp.abs(z)))


def _softplus2(z2):
    return jnp.maximum(z2, 0.0) + jnp.log2(1.0 + jnp.exp2(-jnp.abs(z2)))


def _pair_rms(x, gain, lane_lo):
    x2 = x * x
    s0 = jnp.sum(jnp.where(lane_lo, x2, 0.0), axis=1, keepdims=True)
    s1 = jnp.sum(jnp.where(lane_lo, 0.0, x2), axis=1, keepdims=True)
    ms = jnp.where(lane_lo, s0, s1) * (1.0 / HEAD_DIM)
    return x * lax.rsqrt(ms + EPS) * gain


def _layer(l, *shape):
    return pl.BlockSpec((None,) + shape, lambda *_: (l,) + (0,) * len(shape))


def _proj_kernel(x_ref, g_ref, w_ref, gsq_ref, gsk_ref, gfq_ref, gfk_ref,
                 hg_ref, ff_ref, sq_ref, sk_ref, sv_ref, fq_ref, fk_ref, fv_ref):
    x = x_ref[...]
    ms = jnp.mean(x * x, axis=-1, keepdims=True)
    h = (x * lax.rsqrt(ms + EPS) * g_ref[...]).astype(BF16)
    y = _dot(h, w_ref[...])
    hg_ref[...] = y[:, :SB_Q_BLK * LANES]
    ff_ref[...] = y[:, FOX_F_BLK * LANES:]
    lane_lo = lax.broadcasted_iota(jnp.int32, (x.shape[0], LANES), 1) < HEAD_DIM
    q_mult = LOG2E / math.sqrt(HEAD_DIM)
    groups = ((SB_Q_BLK, gsq_ref, q_mult, sq_ref), (SB_K_BLK, gsk_ref, None, sk_ref),
              (SB_V_BLK, None, None, sv_ref),
              (FOX_Q_BLK, gfq_ref, q_mult, fq_ref), (FOX_K_BLK, gfk_ref, None, fk_ref),
              (FOX_V_BLK, None, None, fv_ref))
    for blk, gain_ref, mult, out_ref in groups:
        for p in range(SB_WIDTH // LANES):
            t = y[:, (blk + p) * LANES:(blk + p + 1) * LANES]
            if gain_ref is not None:
                t = _pair_rms(t, gain_ref[...], lane_lo)
            if mult is not None:
                t = t * mult
            out_ref[:, p * LANES:(p + 1) * LANES] = t.astype(BF16)


def _proj_call(x, g, w, gains, l, tm):
    m, d = x.shape
    n = w.shape[2]
    rows = lambda width: pl.BlockSpec((tm, width), lambda i: (i, 0))
    widths = (SB_Q_BLK * LANES, LANES) + (SB_WIDTH,) * 6
    dtypes = (F32, F32) + (BF16,) * 6
    return pl.pallas_call(
        _proj_kernel,
        grid=(m // tm,),
        in_specs=[rows(d), _layer(l, 1, d), _layer(l, d, n)] + [_layer(l, 1, LANES)] * 4,
        out_specs=[rows(wd) for wd in widths],
        out_shape=[jax.ShapeDtypeStruct((m, wd), dt) for wd, dt in zip(widths, dtypes)],
        compiler_params=pltpu.CompilerParams(
            dimension_semantics=("arbitrary",), vmem_limit_bytes=VMEM_LIMIT),
        name="proj",
    )(x, g, w, *gains)


def _hgrn_kernel(lbl_ref, q_ref, f_ref, i_ref, g_ref, gn_ref, o_ref, st_scr, oi_scr, *, layer, tb):
    @pl.when(pl.program_id(1) == 0)
    def _():
        st_scr[...] = jnp.zeros_like(st_scr)

    rows = [lbl_ref[j:j + 1, :] for j in range(DEPTH)]
    mx = functools.reduce(jnp.maximum, rows)
    ex = [jnp.exp(r - mx) for r in rows]
    den = functools.reduce(lambda a, b: a + b, ex)
    lb = jnp.zeros_like(mx)
    for j in range(1, layer + 1):
        lb = lb + ex[j] / den
    log_lb = jnp.log(jnp.maximum(lb, LB_FLOOR))
    one_m_lb = 1.0 - lb
    gn = gn_ref[...]

    r_i = lax.broadcasted_iota(jnp.int32, (HG_KW, HG_KW), 0)
    c_i = lax.broadcasted_iota(jnp.int32, (HG_KW, HG_KW), 1)
    same_head = (r_i >> 6) == (c_i >> 6)
    ones_bd = jnp.where(same_head, 1.0, 0.0).astype(BF16)
    t_mod = lax.broadcasted_iota(jnp.int32, (HG_ROWS, HG_KW), 0) & (SUB - 1)
    lane_sub = lax.broadcasted_iota(jnp.int32, (HG_KW, HG_ROWS), 1) // SUB
    t_r = lax.broadcasted_iota(jnp.int32, (HG_ROWS, HG_ROWS), 0)
    t_c = lax.broadcasted_iota(jnp.int32, (HG_ROWS, HG_ROWS), 1)
    in_sub = (t_r // SUB) == (t_c // SUB)
    same_sub = jnp.where(in_sub, 1.0, 0.0).astype(BF16)
    upto_sub = jnp.where(jnp.logical_and(in_sub, t_c <= t_r), 1.0, 0.0).astype(BF16)

    def chunk(c, carry):
        rs = pl.ds(pl.multiple_of(c * HG_ROWS, HG_ROWS), HG_ROWS)
        q = q_ref[rs, :]
        fl = f_ref[rs, :]
        v = i_ref[rs, :]
        g = g_ref[rs, :]

        sp = _softplus(fl)
        lf = jnp.maximum(fl, log_lb) + jnp.log(1.0 + jnp.exp(-jnp.abs(fl - log_lb))) - sp
        kk = one_m_lb * jnp.exp(-sp)

        parts = _split3(lf)
        b = functools.reduce(lambda x, y: x + y, [_dot(upto_sub, part) for part in parts])
        beta = functools.reduce(lambda x, y: x + y, [_dot(same_sub, part) for part in parts])
        rest = beta - b

        b2 = b * LOG2E
        w2 = -sp * LOG2E - b2
        q_lb = q * one_m_lb
        acc = _dot((q * kk).astype(BF16), ones_bd) * v
        for d in range(1, SUB):
            e = jnp.where(t_mod >= d, b2 + pltpu.roll(w2, d, 0), NEG_BIG)
            p = q_lb * jnp.exp2(e)
            acc = acc + _dot(p.astype(BF16), ones_bd) * pltpu.roll(v, d, 0)

        qt = (q * jnp.exp2(b2)).astype(BF16)
        kt = (kk * jnp.exp(rest)).astype(BF16)
        v_t = v.T
        for j in range(HG_ROWS // SUB):
            st = st_scr[...]
            oi_scr[j * SUB:(j + 1) * SUB, :] = _nt_dot(qt[j * SUB:(j + 1) * SUB, :], st.astype(BF16))
            vm = jnp.where(lane_sub == j, v_t, 0.0).astype(BF16)
            ut = _dot(vm, kt)
            dec = jnp.exp(beta[j * SUB:j * SUB + 1, :])
            st_scr[...] = st * dec + jnp.where(same_head, ut, 0.0)

        o = acc + oi_scr[...]
        hi, lo = _split2(o * o)
        msq = (_dot(hi, ones_bd) + _dot(lo, ones_bd)) * (1.0 / HEAD_DIM)
        y = o * lax.rsqrt(msq + EPS) * gn
        o_ref[rs, :] = (y * (g / (1.0 + jnp.exp(-g)))).astype(o_ref.dtype)
        return carry

    lax.fori_loop(0, tb // HG_ROWS, chunk, 0)


def _hgrn_call(layer, lb_logits, proj, gn, batch, seq, tb):
    m = proj.shape[0]
    nt = seq // tb

    def col(j):
        return pl.BlockSpec((tb, HG_KW), lambda b, t, j=j: (b * nt + t, j))

    return pl.pallas_call(
        functools.partial(_hgrn_kernel, layer=layer, tb=tb),
        grid=(batch, nt),
        in_specs=[pl.BlockSpec((DEPTH, HG_KW), lambda b, t: (0, 0)),
                  col(0), col(1), col(2), col(3), _layer(layer, 1, HG_WIDTH)],
        out_specs=pl.BlockSpec((tb, HG_WIDTH), lambda b, t: (b * nt + t, 0)),
        out_shape=jax.ShapeDtypeStruct((m, HG_WIDTH), BF16),
        scratch_shapes=[pltpu.VMEM((HG_WIDTH, HG_KW), F32),
                        pltpu.VMEM((HG_ROWS, HG_WIDTH), F32)],
        compiler_params=pltpu.CompilerParams(
            dimension_semantics=("arbitrary", "arbitrary"), vmem_limit_bytes=VMEM_LIMIT),
        name="hgrn2",
    )(lb_logits, proj, proj, proj, proj, gn)


def _sb_kernel(q0_ref, q1_ref, q2_ref, k0_ref, k1_ref, k2_ref, v0_ref, v1_ref, v2_ref,
               o_ref, acc_scr, car_scr, *, bq, n_sub):
    g = pl.program_id(1)
    q_refs = (q0_ref, q1_ref, q2_ref)
    k_refs = (k0_ref, k1_ref, k2_ref)
    v_refs = (v0_ref, v1_ref, v2_ref)
    n_pairs = len(q_refs)

    lane_lo = lax.broadcasted_iota(jnp.int32, (bq, LANES), 1) < HEAD_DIM
    r_i = lax.broadcasted_iota(jnp.int32, (bq, bq), 0)
    c_i = lax.broadcasted_iota(jnp.int32, (bq, bq), 1)
    before = c_i < r_i
    later = jnp.where(r_i > c_i, 1.0, 0.0).astype(BF16)

    heads = range(2 * n_pairs)

    def q_block(sub, carry_unused):
        i = g * n_sub + sub
        qs = pl.ds(pl.multiple_of(sub * bq, bq), bq)
        qms = []
        for p in range(n_pairs):
            qn = q_refs[p][qs, :]
            zero = jnp.zeros_like(qn)
            qms.append(jnp.where(lane_lo, qn, zero))
            qms.append(jnp.where(lane_lo, zero, qn))

        def absorb(tiles, fresh):
            rows = [pl.ds(pl.multiple_of(kb * bq, bq), bq) for kb, _ in tiles]
            masks = [mask for _, mask in tiles]
            nt = range(len(tiles))
            zs = [[_nt_dot(qms[h], k_refs[h // 2][rows[t], :]) for h in heads] for t in nt]
            sps = [[_softplus2(z) for z in zt] for zt in zs]
            us = [spt if masks[t] is None else [jnp.where(masks[t], sp, 0.0) for sp in spt]
                  for t, spt in enumerate(sps)]
            tails = [[_dot(u.astype(BF16), later) for u in ut] for ut in us]
            low = None
            for h in heads:
                carry = None if fresh else car_scr[h]
                acc = None if fresh else acc_scr[h]
                for t in nt:
                    row = jnp.broadcast_to(jnp.sum(us[t][h], axis=1, keepdims=True), (bq, LANES))
                    e = zs[t][h] - sps[t][h] - tails[t][h]
                    if carry is not None:
                        e = e - jnp.concatenate([carry] * (bq // LANES), axis=1)
                        row = row + carry
                    a = jnp.exp2(e)
                    if masks[t] is not None:
                        a = jnp.where(masks[t], a, 0.0)
                    pv = _dot(a.astype(BF16), v_refs[h // 2][rows[t], :])
                    acc = pv if acc is None else acc + pv
                    carry = row
                acc_scr[h] = acc
                car_scr[h] = carry
                low = carry if low is None else jnp.minimum(low, carry)
            return jnp.min(low)

        low = lax.cond(i > 0,
                       lambda: absorb([(i, before), (i - 1, None)], True),
                       lambda: absorb([(i, before)], True))

        def live(st):
            return jnp.logical_and(st[0] >= 0, st[1] < -SB_ZERO_LOG * LOG2E)

        lax.while_loop(live, lambda st: (st[0] - 1, absorb([(st[0], None)], False)), (i - 2, low))
        for p in range(n_pairs):
            o_ref[qs, p * LANES:(p + 1) * LANES] = jnp.where(
                lane_lo, acc_scr[2 * p], acc_scr[2 * p + 1]).astype(o_ref.dtype)
        return carry_unused

    lax.fori_loop(0, n_sub, q_block, 0)


def _sb_call(q, k, v, batch, seq, bq, n_sub):
    m = q.shape[0]
    rows = bq * n_sub
    nq = seq // rows
    n_pairs = SB_WIDTH // LANES
    q_spec = lambda p: pl.BlockSpec((rows, LANES), lambda b, i, p=p: (b * nq + i, p))
    k_spec = lambda p: pl.BlockSpec((seq, LANES), lambda b, i, p=p: (b, p))
    pairs = range(n_pairs)
    return pl.pallas_call(
        functools.partial(_sb_kernel, bq=bq, n_sub=n_sub),
        grid=(batch, nq),
        in_specs=[q_spec(p) for p in pairs] + [k_spec(p) for p in pairs] * 2,
        out_specs=pl.BlockSpec((rows, SB_WIDTH), lambda b, i: (b * nq + i, 0)),
        out_shape=jax.ShapeDtypeStruct((m, SB_WIDTH), BF16),
        scratch_shapes=[pltpu.VMEM((2 * n_pairs, bq, LANES), F32),
                        pltpu.VMEM((2 * n_pairs, bq, LANES), F32)],
        compiler_params=pltpu.CompilerParams(
            dimension_semantics=("arbitrary", "arbitrary"), vmem_limit_bytes=VMEM_LIMIT),
        name="stickbreak",
    )(*([q] * n_pairs + [k] * n_pairs + [v] * n_pairs))


def _aug_lanes(lane, base, first, second):
    out = jnp.zeros(lane.shape, F32)
    for n in range(3):
        out = jnp.where(lane == base + n, first[n], out)
        out = jnp.where(lane == base + 3 + n, second[n], out)
    return out


def _fox_kernel(q_ref, k_ref, v_ref, f_ref, fb_ref, o_ref,
                c_scr, qa_scr, ka_scr, vt_scr, sa_scr, sb_scr, mb_scr, *, bq, bk, seq):
    p_idx = pl.program_id(1)
    one = jnp.ones((1, 1), F32)

    @pl.when(p_idx == 0)
    def _():
        r_i = lax.broadcasted_iota(jnp.int32, (bk, bk), 0)
        c_i = lax.broadcasted_iota(jnp.int32, (bk, bk), 1)
        upto = jnp.where(c_i <= r_i, 1.0, 0.0).astype(BF16)

        def body(n, run):
            rs = pl.ds(pl.multiple_of(n * bk, bk), bk)
            y = f_ref[rs, :] + fb_ref[...]
            lg = jnp.minimum(y, 0.0) - jnp.log(1.0 + jnp.exp(-jnp.abs(y)))
            hi, mid, lo = _split3(lg)
            c = _dot(upto, hi) + _dot(upto, mid) + _dot(upto, lo) + run
            c_scr[rs, :] = c
            return c[bk - 1:bk, :]

        lax.fori_loop(0, seq // bk, body, jnp.zeros((1, LANES), F32))

    lane = lax.broadcasted_iota(jnp.int32, (bk, LANES), 1)
    lane_lo = lane < HEAD_DIM
    row_lo_k = lax.broadcasted_iota(jnp.int32, (LANES, bk), 0) < HEAD_DIM

    def prep(n, carry):
        rs = pl.ds(pl.multiple_of(n * bk, bk), bk)
        kn = k_ref[rs, :]
        qn = q_ref[rs, :]
        c = c_scr[rs, :] * LOG2E
        v_t = v_ref[rs, :].T
        for h in range(2):
            ch = jnp.sum(jnp.where(lane == 2 * p_idx + h, c, 0.0), axis=1, keepdims=True)
            parts = [x.astype(F32) for x in _split3(ch)]
            mine = lane_lo if h == 0 else jnp.logical_not(lane_lo)
            base = HEAD_DIM * (1 - h)
            k_aug = _aug_lanes(lane, base, (one, one, one), [-x for x in parts])
            q_aug = _aug_lanes(lane, base, parts, (one, one, one))
            ka_scr[h, rs, :] = jnp.where(mine, kn, k_aug.astype(BF16))
            qa_scr[h, rs, :] = jnp.where(mine, qn, q_aug.astype(BF16))
            vt_scr[h, n] = jnp.where(row_lo_k if h == 0 else jnp.logical_not(row_lo_k), v_t,
                                     jnp.ones_like(v_t))
        return carry

    lax.fori_loop(0, seq // bk, prep, 0)

    key_minus_qry = (lax.broadcasted_iota(jnp.int32, (bk, bq), 0)
                     - lax.broadcasted_iota(jnp.int32, (bk, bq), 1))
    row_lo = lax.broadcasted_iota(jnp.int32, (LANES, bq), 0) < HEAD_DIM

    nq = seq // bq
    n_items = sum((qi * bq + bq - 1) // bk + 1 for qi in range(nq))
    mb_scr[0] = jnp.zeros((bk, bq), F32)
    for r in range(bk // bq):
        mb_scr[r + 1] = jnp.where(key_minus_qry <= r * bq, 0.0, NEG_BIG)

    def unpack(item):
        qi = jnp.minimum(item[0], nq - 1)
        last = (qi * bq + bq - 1) // bk
        first = item[1] == 0
        kb = jnp.where(first, last, item[1] - 1)
        return qi, last, first, kb

    def advance(item):
        qi, last, _, _ = unpack(item)
        wrap = item[1] >= last
        return jnp.where(wrap, item[0] + 1, item[0]), jnp.where(wrap, 0, item[1] + 1)

    def scores_into(dst, item):
        qi, _, first, kb = unpack(item)
        qs = pl.ds(pl.multiple_of(qi * bq, bq), bq)
        rs = pl.ds(pl.multiple_of(kb * bk, bk), bk)
        bias = mb_scr[jnp.where(first, 1 + (qi * bq - kb * bk) // bq, 0)]
        for h in range(2):
            dst[h] = _nt_dot(ka_scr[h, rs, :], qa_scr[h, qs, :]) + bias

    def write_out(qi, sts):
        acc0, acc1 = sts[0][1], sts[1][1]
        o_t = jnp.where(row_lo, acc0 / acc0[HEAD_DIM:HEAD_DIM + 1, :], acc1 / acc1[0:1, :])
        o_ref[pl.ds(pl.multiple_of(qi * bq, bq), bq), :] = o_t.T.astype(o_ref.dtype)

    def absorb(src, item, prev_qi, sts):
        qi, _, first, kb = unpack(item)
        write_out(prev_qi, sts)
        out = []
        for h in range(2):
            m_run, acc = sts[h]
            m_run = jnp.where(first, NEG_BIG, m_run)
            s = src[h]
            m_new = jnp.maximum(m_run, jnp.max(s, axis=0, keepdims=True))
            p = jnp.exp2(s - m_new)
            acc = jnp.exp2(m_run - m_new) * acc + _dot(vt_scr[h, kb], p.astype(BF16))
            out.append((m_new, acc))
        return qi, tuple(out)

    unroll = max([u for u in (8, 4) if n_items % u == 0 and u * bk <= 2048] + [2])
    bufs = (sa_scr, sb_scr)

    def run(count, ca):
        held, prev_qi, sts = ca
        for k in range(count):
            nxt = advance(held)
            scores_into(bufs[(k + 1) % 2], nxt)
            prev_qi, sts = absorb(bufs[k % 2], held, prev_qi, sts)
            held = nxt
        return held, prev_qi, sts

    start = (jnp.int32(0), jnp.int32(0))
    init = (jnp.full((1, bq), NEG_BIG, F32), jnp.ones((LANES, bq), F32))
    scores_into(sa_scr, start)
    ca = lax.fori_loop(0, n_items // unroll, lambda u, ca: run(unroll, ca),
                       (start, jnp.int32(0), (init, init)))
    _, last_qi, sts = run(n_items % unroll, ca)
    write_out(last_qi, sts)


def _fox_call(q, k, v, ff, fbias, l, batch, seq, bq, bk):
    m = q.shape[0]
    col = pl.BlockSpec((seq, LANES), lambda b, p: (b, p))
    return pl.pallas_call(
        functools.partial(_fox_kernel, bq=bq, bk=bk, seq=seq),
        grid=(batch, FOX_WIDTH // LANES),
        in_specs=[col, col, col, pl.BlockSpec((seq, LANES), lambda b, p: (b, 0)), _layer(l, 1, LANES)],
        out_specs=pl.BlockSpec((seq, LANES), lambda b, p: (b, p)),
        out_shape=jax.ShapeDtypeStruct((m, FOX_WIDTH), BF16),
        scratch_shapes=[pltpu.VMEM((seq, LANES), F32),
                        pltpu.VMEM((2, seq, LANES), BF16),
                        pltpu.VMEM((2, seq, LANES), BF16),
                        pltpu.VMEM((2, seq // bk, LANES, bk), BF16),
                        pltpu.VMEM((2, bk, bq), F32),
                        pltpu.VMEM((2, bk, bq), F32),
                        pltpu.VMEM((bk // bq + 1, bk, bq), F32)],
        compiler_params=pltpu.CompilerParams(
            dimension_semantics=("arbitrary", "arbitrary"), vmem_limit_bytes=VMEM_LIMIT),
        name="forgetting",
    )(q, k, v, ff, fbias)


def _mlp_kernel(x_ref, ohg_ref, osb_ref, ofx_ref, wo1_ref, wo2_ref, wo3_ref, g2_ref,
                w1_ref, w2_ref, o_ref, *, tf):
    x1 = (x_ref[...] + _dot(ohg_ref[...], wo1_ref[...]) + _dot(osb_ref[...], wo2_ref[...])
          + _dot(ofx_ref[...], wo3_ref[...]))
    ms = jnp.mean(x1 * x1, axis=-1, keepdims=True)
    h2 = (x1 * lax.rsqrt(ms + EPS) * g2_ref[...]).astype(BF16)
    o_ref[...] = x1
    acc = None
    for f in range(w1_ref.shape[1] // tf):
        a = _dot(h2, w1_ref[:, f * tf:(f + 1) * tf])
        a = jnp.square(jnp.maximum(a, 0.0)).astype(BF16)
        y = _dot(a, w2_ref[f * tf:(f + 1) * tf, :])
        acc = y if acc is None else acc + y
    o_ref[...] = o_ref[...] + acc


def _mlp_call(x, ohg, osb, ofx, wo1, wo2, wo3, g2, w1, w2, l, tm, tf):
    m, d = x.shape
    row = lambda w: pl.BlockSpec((tm, w), lambda i: (i, 0))
    whole = lambda a: _layer(l, *a.shape[1:])
    return pl.pallas_call(
        functools.partial(_mlp_kernel, tf=tf),
        grid=(m // tm,),
        in_specs=[row(d), row(HG_WIDTH), row(SB_WIDTH), row(FOX_WIDTH),
                  whole(wo1), whole(wo2), whole(wo3), whole(g2), whole(w1), whole(w2)],
        out_specs=row(d),
        out_shape=jax.ShapeDtypeStruct((m, d), F32),
        compiler_params=pltpu.CompilerParams(
            dimension_semantics=("arbitrary",), vmem_limit_bytes=VMEM_LIMIT),
        name="outproj_mlp",
    )(x, ohg, osb, ofx, wo1, wo2, wo3, g2, w1, w2)


def _tile_sizes(batch, seq):
    m = batch * seq
    return dict(
        proj_tm=min(512, m),
        hg_tb=min(512, seq),
        sb_bq=min(256, seq),
        sb_sub=max(1, min(4, seq // 256)),
        fox_bq=min(256, seq),
        fox_bk=min(512, seq),
        mlp_tm=min(512, m),
        mlp_tf=1024,
    )


def kernel(x, lb_logits, norm1_g, w_in, hg_norm_g, sb_q_norm_g, sb_k_norm_g, fox_q_norm_g,
           fox_k_norm_g, fox_f_bias, w_out, norm2_g, w_ff1, w_ff2):
    batch, seq, d = x.shape
    assert d == D_MODEL and seq % HG_ROWS == 0 and x.dtype == F32
    ts = _tile_sizes(batch, seq)
    assert seq % (ts["sb_bq"] * ts["sb_sub"]) == 0 and seq % ts["fox_bk"] == 0 and ts["fox_bk"] % ts["fox_bq"] == 0
    assert seq % ts["hg_tb"] == 0
    m = batch * seq
    xf = x.reshape(m, d)
    row = lambda a: a.astype(F32)[:, None, :]
    pair = lambda g: row(jnp.tile(g, (1, 2)))
    w_in_p = jnp.pad(w_in, ((0, 0), (0, 0), (0, IN_COLS_PAD - IN_COLS))).astype(BF16)
    wo = w_out.astype(BF16)
    wo_hg, wo_sb, wo_fx = wo[:, :HG_WIDTH], wo[:, HG_WIDTH:HG_WIDTH + SB_WIDTH], wo[:, HG_WIDTH + SB_WIDTH:]
    w1, w2 = w_ff1.astype(BF16), w_ff2.astype(BF16)
    g1, g2, g_hg = row(norm1_g), row(norm2_g), row(jnp.tile(hg_norm_g, (1, HG_HEADS)))
    qk_gains = (pair(sb_q_norm_g), pair(sb_k_norm_g), pair(fox_q_norm_g), pair(fox_k_norm_g))
    fbias = row(jnp.pad(fox_f_bias, ((0, 0), (0, LANES - FOX_HEADS))))
    lbl = lb_logits.astype(F32)

    for l in range(DEPTH):
        hg, ff, sq, sk, sv, fq, fk, fv = _proj_call(xf, g1, w_in_p, qk_gains, l, ts["proj_tm"])
        o_hg = _hgrn_call(l, lbl, hg, g_hg, batch, seq, ts["hg_tb"])
        o_sb = _sb_call(sq, sk, sv, batch, seq, ts["sb_bq"], ts["sb_sub"])
        o_fx = _fox_call(fq, fk, fv, ff, fbias, l, batch, seq, ts["fox_bq"], ts["fox_bk"])
        xf = _mlp_call(xf, o_hg, o_sb, o_fx, wo_hg, wo_sb, wo_fx, g2, w1, w2, l,
                       ts["mlp_tm"], ts["mlp_tf"])
    return xf.reshape(batch, seq, d)
```

```python
import functools
import math

import jax
import jax.numpy as jnp
from jax import lax
from jax.experimental import pallas as pl
from jax.experimental.pallas import tpu as pltpu

F32 = jnp.float32
BF16 = jnp.bfloat16

D_MODEL = 1024
DEPTH = 4
HEAD_DIM = 64
HG_HEADS = 4
HG_KW = 256
HG_WIDTH = 256
SB_WIDTH = 384
FOX_WIDTH = 384
FOX_HEADS = 6
IN_COLS = 4 * 256 + 3 * 384 + 3 * 384 + FOX_HEADS
D_FF = 4 * D_MODEL
EPS = 1e-6
LB_FLOOR = 1e-30
NEG_BIG = -1e30
LOG2E = math.log2(math.e)

LANES = 128
IN_COLS_PAD = 27 * LANES
SB_Q_BLK, SB_K_BLK, SB_V_BLK = 8, 11, 14
FOX_Q_BLK, FOX_K_BLK, FOX_V_BLK, FOX_F_BLK = 17, 20, 23, 26

SB_ZERO_LOG = -105.0
SUB = 16
HG_ROWS = 256
VMEM_LIMIT = 56 * 1024 * 1024


def _nt_dot(a, b):
    return lax.dot_general(a, b, (((1,), (1,)), ((), ())), preferred_element_type=F32)


def _dot(a, b):
    return jnp.dot(a, b, preferred_element_type=F32)


def _split2(x):
    hi = x.astype(BF16)
    lo = (x - hi.astype(F32)).astype(BF16)
    return hi, lo


def _split3(x):
    hi = x.astype(BF16)
    r1 = x - hi.astype(F32)
    mid = r1.astype(BF16)
    lo = (r1 - mid.astype(F32)).astype(BF16)
    return hi, mid, lo


def _softplus(z):
    return jnp.maximum(z, 0.0) + jnp.log(1.0 + jnp.exp(-jnp.abs(z)))


def _softplus2(z2):
    return jnp.maximum(z2, 0.0) + jnp.log2(1.0 + jnp.exp2(-jnp.abs(z2)))


def _pair_rms(x, gain, lane_lo):
    x2 = x * x
    s0 = jnp.sum(jnp.where(lane_lo, x2, 0.0), axis=1, keepdims=True)
    s1 = jnp.sum(jnp.where(lane_lo, 0.0, x2), axis=1, keepdims=True)
    ms = jnp.where(lane_lo, s0, s1) * (1.0 / HEAD_DIM)
    return x * lax.rsqrt(ms + EPS) * gain


def _layer(l, *shape):
    return pl.BlockSpec((None,) + shape, lambda *_: (l,) + (0,) * len(shape))


def _lower_bound(lbl_ref, layer):
    rows = [lbl_ref[j:j + 1, :] for j in range(DEPTH)]
    mx = functools.reduce(jnp.maximum, rows)
    ex = [jnp.exp(r - mx) for r in rows]
    den = functools.reduce(lambda a, b: a + b, ex)
    lb = jnp.zeros_like(mx)
    for j in range(1, layer + 1):
        lb = lb + ex[j] / den
    return lb


def _proj_kernel(x_ref, g_ref, w_ref, lbl_ref, gsq_ref, gsk_ref, gfq_ref, gfk_ref,
                 hg_ref, ff_ref, sq_ref, sk_ref, sv_ref, fq_ref, fk_ref, fv_ref, *, layer):
    x = x_ref[...]
    ms = jnp.mean(x * x, axis=-1, keepdims=True)
    h = (x * lax.rsqrt(ms + EPS) * g_ref[...]).astype(BF16)
    y = _dot(h, w_ref[...])
    fl = y[:, HG_KW:2 * HG_KW]
    gt = y[:, 3 * HG_KW:4 * HG_KW]
    log_lb = jnp.log(jnp.maximum(_lower_bound(lbl_ref, layer), LB_FLOOR))
    sp = _softplus(fl)
    hg_ref[:, :HG_KW] = y[:, :HG_KW]
    hg_ref[:, HG_KW:2 * HG_KW] = sp
    hg_ref[:, 2 * HG_KW:3 * HG_KW] = y[:, 2 * HG_KW:3 * HG_KW]
    hg_ref[:, 3 * HG_KW:4 * HG_KW] = gt / (1.0 + jnp.exp(-gt))
    hg_ref[:, 4 * HG_KW:] = (jnp.maximum(fl, log_lb) + jnp.log(1.0 + jnp.exp(-jnp.abs(fl - log_lb)))
                             - sp)
    ff_ref[...] = y[:, FOX_F_BLK * LANES:]
    lane_lo = lax.broadcasted_iota(jnp.int32, (x.shape[0], LANES), 1) < HEAD_DIM
    q_mult = LOG2E / math.sqrt(HEAD_DIM)
    groups = ((SB_Q_BLK, gsq_ref, q_mult, sq_ref), (SB_K_BLK, gsk_ref, None, sk_ref),
              (SB_V_BLK, None, None, sv_ref),
              (FOX_Q_BLK, gfq_ref, q_mult, fq_ref), (FOX_K_BLK, gfk_ref, None, fk_ref),
              (FOX_V_BLK, None, None, fv_ref))
    for blk, gain_ref, mult, out_ref in groups:
        for p in range(SB_WIDTH // LANES):
            t = y[:, (blk + p) * LANES:(blk + p + 1) * LANES]
            if gain_ref is not None:
                t = _pair_rms(t, gain_ref[...], lane_lo)
            if mult is not None:
                t = t * mult
            out_ref[:, p * LANES:(p + 1) * LANES] = t.astype(BF16)


def _proj_call(x, g, w, lbl, gains, l, tm):
    m, d = x.shape
    n = w.shape[2]
    rows = lambda width: pl.BlockSpec((tm, width), lambda i: (i, 0))
    widths = (5 * HG_KW, LANES) + (SB_WIDTH,) * 6
    dtypes = (F32, F32) + (BF16,) * 6
    return pl.pallas_call(
        functools.partial(_proj_kernel, layer=l),
        grid=(m // tm,),
        in_specs=([rows(d), _layer(l, 1, d), _layer(l, d, n),
                   pl.BlockSpec((DEPTH, HG_KW), lambda i: (0, 0))] + [_layer(l, 1, LANES)] * 4),
        out_specs=[rows(wd) for wd in widths],
        out_shape=[jax.ShapeDtypeStruct((m, wd), dt) for wd, dt in zip(widths, dtypes)],
        compiler_params=pltpu.CompilerParams(
            dimension_semantics=("arbitrary",), vmem_limit_bytes=VMEM_LIMIT),
        name="proj",
    )(x, g, w, lbl, *gains)


def _hgrn_kernel(lbl_ref, q_ref, sp_ref, i_ref, gate_ref, lf_ref, gn_ref, o_ref, st_scr, oi_scr,
                 *, layer, tb):
    @pl.when(pl.program_id(1) == 0)
    def _():
        st_scr[...] = jnp.zeros_like(st_scr)

    one_m_lb = 1.0 - _lower_bound(lbl_ref, layer)
    gn = gn_ref[...]

    r_i = lax.broadcasted_iota(jnp.int32, (HG_KW, HG_KW), 0)
    c_i = lax.broadcasted_iota(jnp.int32, (HG_KW, HG_KW), 1)
    same_head = (r_i >> 6) == (c_i >> 6)
    ones_bd = jnp.where(same_head, 1.0, 0.0).astype(BF16)
    t_mod = lax.broadcasted_iota(jnp.int32, (HG_ROWS, HG_KW), 0) & (SUB - 1)
    lane_sub = lax.broadcasted_iota(jnp.int32, (HG_KW, HG_ROWS), 1) // SUB
    t_r = lax.broadcasted_iota(jnp.int32, (HG_ROWS, HG_ROWS), 0)
    t_c = lax.broadcasted_iota(jnp.int32, (HG_ROWS, HG_ROWS), 1)
    in_sub = (t_r // SUB) == (t_c // SUB)
    same_sub = jnp.where(in_sub, 1.0, 0.0).astype(BF16)
    upto_sub = jnp.where(jnp.logical_and(in_sub, t_c <= t_r), 1.0, 0.0).astype(BF16)

    def chunk(c, carry):
        rs = pl.ds(pl.multiple_of(c * HG_ROWS, HG_ROWS), HG_ROWS)
        q = q_ref[rs, :]
        sp = sp_ref[rs, :]
        v = i_ref[rs, :]
        lf = lf_ref[rs, :]
        kk = one_m_lb * jnp.exp(-sp)

        parts = _split3(lf)
        b = functools.reduce(lambda x, y: x + y, [_dot(upto_sub, part) for part in parts])
        beta = functools.reduce(lambda x, y: x + y, [_dot(same_sub, part) for part in parts])
        rest = beta - b

        b2 = b * LOG2E
        w2 = -sp * LOG2E - b2
        q_lb = q * one_m_lb
        acc = _dot((q * kk).astype(BF16), ones_bd) * v
        for d in range(1, SUB):
            e = jnp.where(t_mod >= d, b2 + pltpu.roll(w2, d, 0), NEG_BIG)
            p = q_lb * jnp.exp2(e)
            acc = acc + _dot(p.astype(BF16), ones_bd) * pltpu.roll(v, d, 0)

        qt = (q * jnp.exp2(b2)).astype(BF16)
        kt = (kk * jnp.exp(rest)).astype(BF16)
        v_t = v.T
        for j in range(HG_ROWS // SUB):
            st = st_scr[...]
            oi_scr[j * SUB:(j + 1) * SUB, :] = _nt_dot(qt[j * SUB:(j + 1) * SUB, :], st.astype(BF16))
            vm = jnp.where(lane_sub == j, v_t, 0.0).astype(BF16)
            ut = _dot(vm, kt)
            dec = jnp.exp(beta[j * SUB:j * SUB + 1, :])
            st_scr[...] = st * dec + jnp.where(same_head, ut, 0.0)

        o = acc + oi_scr[...]
        hi, lo = _split2(o * o)
        msq = (_dot(hi, ones_bd) + _dot(lo, ones_bd)) * (1.0 / HEAD_DIM)
        y = o * lax.rsqrt(msq + EPS) * gn
        o_ref[rs, :] = (y * gate_ref[rs, :]).astype(o_ref.dtype)
        return carry

    lax.fori_loop(0, tb // HG_ROWS, chunk, 0)


def _hgrn_call(layer, lb_logits, proj, gn, batch, seq, tb):
    m = proj.shape[0]
    nt = seq // tb

    def col(j):
        return pl.BlockSpec((tb, HG_KW), lambda b, t, j=j: (b * nt + t, j))

    return pl.pallas_call(
        functools.partial(_hgrn_kernel, layer=layer, tb=tb),
        grid=(batch, nt),
        in_specs=[pl.BlockSpec((DEPTH, HG_KW), lambda b, t: (0, 0)),
                  col(0), col(1), col(2), col(3), col(4), _layer(layer, 1, HG_WIDTH)],
        out_specs=pl.BlockSpec((tb, HG_WIDTH), lambda b, t: (b * nt + t, 0)),
        out_shape=jax.ShapeDtypeStruct((m, HG_WIDTH), BF16),
        scratch_shapes=[pltpu.VMEM((HG_WIDTH, HG_KW), F32),
                        pltpu.VMEM((HG_ROWS, HG_WIDTH), F32)],
        compiler_params=pltpu.CompilerParams(
            dimension_semantics=("arbitrary", "arbitrary"), vmem_limit_bytes=VMEM_LIMIT),
        name="hgrn2",
    )(lb_logits, proj, proj, proj, proj, proj, gn)


def _sb_kernel(q0_ref, q1_ref, q2_ref, k0_ref, k1_ref, k2_ref, v0_ref, v1_ref, v2_ref,
               o_ref, acc_scr, car_scr, *, bq, n_sub):
    g = pl.program_id(1)
    q_refs = (q0_ref, q1_ref, q2_ref)
    k_refs = (k0_ref, k1_ref, k2_ref)
    v_refs = (v0_ref, v1_ref, v2_ref)
    n_pairs = len(q_refs)

    lane_lo = lax.broadcasted_iota(jnp.int32, (bq, LANES), 1) < HEAD_DIM
    r_i = lax.broadcasted_iota(jnp.int32, (bq, bq), 0)
    c_i = lax.broadcasted_iota(jnp.int32, (bq, bq), 1)
    before = c_i < r_i
    later = jnp.where(r_i > c_i, 1.0, 0.0).astype(BF16)

    heads = range(2 * n_pairs)

    def q_block(sub, carry_unused):
        i = g * n_sub + sub
        qs = pl.ds(pl.multiple_of(sub * bq, bq), bq)
        qms = []
        for p in range(n_pairs):
            qn = q_refs[p][qs, :]
            zero = jnp.zeros_like(qn)
            qms.append(jnp.where(lane_lo, qn, zero))
            qms.append(jnp.where(lane_lo, zero, qn))

        def absorb(tiles, fresh):
            rows = [pl.ds(pl.multiple_of(kb * bq, bq), bq) for kb, _ in tiles]
            masks = [mask for _, mask in tiles]
            nt = range(len(tiles))
            zs = [[_nt_dot(qms[h], k_refs[h // 2][rows[t], :]) for h in heads] for t in nt]
            sps = [[_softplus2(z) for z in zt] for zt in zs]
            us = [spt if masks[t] is None else [jnp.where(masks[t], sp, 0.0) for sp in spt]
                  for t, spt in enumerate(sps)]
            tails = [[_dot(u.astype(BF16), later) for u in ut] for ut in us]
            low = None
            for h in heads:
                carry = None if fresh else car_scr[h]
                acc = None if fresh else acc_scr[h]
                for t in nt:
                    row = jnp.broadcast_to(jnp.sum(us[t][h], axis=1, keepdims=True), (bq, LANES))
                    e = zs[t][h] - sps[t][h] - tails[t][h]
                    if carry is not None:
                        e = e - jnp.concatenate([carry] * (bq // LANES), axis=1)
                        row = row + carry
                    a = jnp.exp2(e)
                    if masks[t] is not None:
                        a = jnp.where(masks[t], a, 0.0)
                    pv = _dot(a.astype(BF16), v_refs[h // 2][rows[t], :])
                    acc = pv if acc is None else acc + pv
                    carry = row
                acc_scr[h] = acc
                car_scr[h] = carry
                low = carry if low is None else jnp.minimum(low, carry)
            return jnp.min(low)

        low = lax.cond(i > 0,
                       lambda: absorb([(i, before), (i - 1, None)], True),
                       lambda: absorb([(i, before)], True))

        def live(st):
            return jnp.logical_and(st[0] >= 0, st[1] < -SB_ZERO_LOG * LOG2E)

        lax.while_loop(live, lambda st: (st[0] - 1, absorb([(st[0], None)], False)), (i - 2, low))
        for p in range(n_pairs):
            o_ref[qs, p * LANES:(p + 1) * LANES] = jnp.where(
                lane_lo, acc_scr[2 * p], acc_scr[2 * p + 1]).astype(o_ref.dtype)
        return carry_unused

    lax.fori_loop(0, n_sub, q_block, 0)


def _sb_call(q, k, v, batch, seq, bq, n_sub):
    m = q.shape[0]
    rows = bq * n_sub
    nq = seq // rows
    n_pairs = SB_WIDTH // LANES
    q_spec = lambda p: pl.BlockSpec((rows, LANES), lambda b, i, p=p: (b * nq + i, p))
    k_spec = lambda p: pl.BlockSpec((seq, LANES), lambda b, i, p=p: (b, p))
    pairs = range(n_pairs)
    return pl.pallas_call(
        functools.partial(_sb_kernel, bq=bq, n_sub=n_sub),
        grid=(batch, nq),
        in_specs=[q_spec(p) for p in pairs] + [k_spec(p) for p in pairs] * 2,
        out_specs=pl.BlockSpec((rows, SB_WIDTH), lambda b, i: (b * nq + i, 0)),
        out_shape=jax.ShapeDtypeStruct((m, SB_WIDTH), BF16),
        scratch_shapes=[pltpu.VMEM((2 * n_pairs, bq, LANES), F32),
                        pltpu.VMEM((2 * n_pairs, bq, LANES), F32)],
        compiler_params=pltpu.CompilerParams(
            dimension_semantics=("arbitrary", "arbitrary"), vmem_limit_bytes=VMEM_LIMIT),
        name="stickbreak",
    )(*([q] * n_pairs + [k] * n_pairs + [v] * n_pairs))


def _fox_kernel(q_ref, k_ref, v_ref, f_ref, fb_ref, o_ref,
                c_scr, qa_scr, ka_scr, vt_scr, sa_scr, sb_scr, mb_scr, *, bq, bk, seq):
    p_idx = pl.program_id(1)

    @pl.when(p_idx == 0)
    def _():
        r_i = lax.broadcasted_iota(jnp.int32, (bk, bk), 0)
        c_i = lax.broadcasted_iota(jnp.int32, (bk, bk), 1)
        upto = jnp.where(c_i <= r_i, 1.0, 0.0).astype(BF16)

        def body(n, run):
            rs = pl.ds(pl.multiple_of(n * bk, bk), bk)
            y = f_ref[rs, :] + fb_ref[...]
            lg = jnp.minimum(y, 0.0) - jnp.log(1.0 + jnp.exp(-jnp.abs(y)))
            hi, mid, lo = _split3(lg)
            c = _dot(upto, hi) + _dot(upto, mid) + _dot(upto, lo) + run
            c_scr[rs, :] = c
            return c[bk - 1:bk, :]

        lax.fori_loop(0, seq // bk, body, jnp.zeros((1, LANES), F32))

    lane = lax.broadcasted_iota(jnp.int32, (bk, LANES), 1)
    lane_lo = lane < HEAD_DIM
    row_lo_k = lax.broadcasted_iota(jnp.int32, (LANES, bk), 0) < HEAD_DIM

    pr = lax.broadcasted_iota(jnp.int32, (LANES, LANES), 0)
    pc = lax.broadcasted_iota(jnp.int32, (LANES, LANES), 1)
    place, ones_q, ones_k = [], [], []
    for h in range(2):
        base = HEAD_DIM * (1 - h)
        place.append([jnp.where(jnp.logical_and(pr == 2 * p_idx + h,
                                                jnp.logical_or(pc == base + n, pc == base + 3 + n)),
                                1.0, 0.0).astype(BF16) for n in range(3)])
        ones_q.append(jnp.logical_and(lane >= base + 3, lane < base + 6))
        ones_k.append(jnp.logical_and(lane >= base, lane < base + 3))

    def prep(n, carry):
        rs = pl.ds(pl.multiple_of(n * bk, bk), bk)
        kn = k_ref[rs, :]
        qn = q_ref[rs, :]
        parts = _split3(c_scr[rs, :] * LOG2E)
        v_t = v_ref[rs, :].T
        for h in range(2):
            c_aug = functools.reduce(lambda x, y: x + y,
                                     [_dot(part, sel) for part, sel in zip(parts, place[h])])
            mine = lane_lo if h == 0 else jnp.logical_not(lane_lo)
            k_aug = jnp.where(ones_k[h], 1.0, -c_aug)
            q_aug = jnp.where(ones_q[h], 1.0, c_aug)
            ka_scr[h, rs, :] = jnp.where(mine, kn, k_aug.astype(BF16))
            qa_scr[h, rs, :] = jnp.where(mine, qn, q_aug.astype(BF16))
            vt_scr[h, n] = jnp.where(row_lo_k if h == 0 else jnp.logical_not(row_lo_k), v_t,
                                     jnp.ones_like(v_t))
        return carry

    lax.fori_loop(0, seq // bk, prep, 0)

    key_minus_qry = (lax.broadcasted_iota(jnp.int32, (bk, bq), 0)
                     - lax.broadcasted_iota(jnp.int32, (bk, bq), 1))
    row_lo = lax.broadcasted_iota(jnp.int32, (LANES, bq), 0) < HEAD_DIM

    nq = seq // bq
    n_items = sum((qi * bq + bq - 1) // bk + 1 for qi in range(nq))
    mb_scr[0] = jnp.zeros((bk, bq), F32)
    for r in range(bk // bq):
        mb_scr[r + 1] = jnp.where(key_minus_qry <= r * bq, 0.0, NEG_BIG)

    def unpack(item):
        qi = jnp.minimum(item[0], nq - 1)
        last = (qi * bq + bq - 1) // bk
        first = item[1] == 0
        kb = jnp.where(first, last, item[1] - 1)
        return qi, last, first, kb

    def advance(item):
        qi, last, _, _ = unpack(item)
        wrap = item[1] >= last
        return jnp.where(wrap, item[0] + 1, item[0]), jnp.where(wrap, 0, item[1] + 1)

    def scores_into(dst, item):
        qi, _, first, kb = unpack(item)
        qs = pl.ds(pl.multiple_of(qi * bq, bq), bq)
        rs = pl.ds(pl.multiple_of(kb * bk, bk), bk)
        bias = mb_scr[jnp.where(first, 1 + (qi * bq - kb * bk) // bq, 0)]
        for h in range(2):
            dst[h] = _nt_dot(ka_scr[h, rs, :], qa_scr[h, qs, :]) + bias

    def write_out(qi, sts):
        acc0, acc1 = sts[0][1], sts[1][1]
        o_t = jnp.where(row_lo, acc0 / acc0[HEAD_DIM:HEAD_DIM + 1, :], acc1 / acc1[0:1, :])
        o_ref[pl.ds(pl.multiple_of(qi * bq, bq), bq), :] = o_t.T.astype(o_ref.dtype)

    def absorb(src, item, prev_qi, sts):
        qi, _, first, kb = unpack(item)
        write_out(prev_qi, sts)
        out = []
        for h in range(2):
            m_run, acc = sts[h]
            m_run = jnp.where(first, NEG_BIG, m_run)
            s = src[h]
            m_new = jnp.maximum(m_run, jnp.max(s, axis=0, keepdims=True))
            p = jnp.exp2(s - m_new)
            acc = jnp.exp2(m_run - m_new) * acc + _dot(vt_scr[h, kb], p.astype(BF16))
            out.append((m_new, acc))
        return qi, tuple(out)

    unroll = max([u for u in (8, 4) if n_items % u == 0 and u * bk <= 2048] + [2])
    bufs = (sa_scr, sb_scr)

    def run(count, ca):
        held, prev_qi, sts = ca
        for k in range(count):
            nxt = advance(held)
            scores_into(bufs[(k + 1) % 2], nxt)
            prev_qi, sts = absorb(bufs[k % 2], held, prev_qi, sts)
            held = nxt
        return held, prev_qi, sts

    start = (jnp.int32(0), jnp.int32(0))
    init = (jnp.full((1, bq), NEG_BIG, F32), jnp.ones((LANES, bq), F32))
    scores_into(sa_scr, start)
    ca = lax.fori_loop(0, n_items // unroll, lambda u, ca: run(unroll, ca),
                       (start, jnp.int32(0), (init, init)))
    _, last_qi, sts = run(n_items % unroll, ca)
    write_out(last_qi, sts)


def _fox_call(q, k, v, ff, fbias, l, batch, seq, bq, bk):
    m = q.shape[0]
    col = pl.BlockSpec((seq, LANES), lambda b, p: (b, p))
    return pl.pallas_call(
        functools.partial(_fox_kernel, bq=bq, bk=bk, seq=seq),
        grid=(batch, FOX_WIDTH // LANES),
        in_specs=[col, col, col, pl.BlockSpec((seq, LANES), lambda b, p: (b, 0)), _layer(l, 1, LANES)],
        out_specs=pl.BlockSpec((seq, LANES), lambda b, p: (b, p)),
        out_shape=jax.ShapeDtypeStruct((m, FOX_WIDTH), BF16),
        scratch_shapes=[pltpu.VMEM((seq, LANES), F32),
                        pltpu.VMEM((2, seq, LANES), BF16),
                        pltpu.VMEM((2, seq, LANES), BF16),
                        pltpu.VMEM((2, seq // bk, LANES, bk), BF16),
                        pltpu.VMEM((2, bk, bq), F32),
                        pltpu.VMEM((2, bk, bq), F32),
                        pltpu.VMEM((bk // bq + 1, bk, bq), F32)],
        compiler_params=pltpu.CompilerParams(
            dimension_semantics=("arbitrary", "arbitrary"), vmem_limit_bytes=VMEM_LIMIT),
        name="forgetting",
    )(q, k, v, ff, fbias)


def _mlp_kernel(x_ref, ohg_ref, osb_ref, ofx_ref, wo1_ref, wo2_ref, wo3_ref, g2_ref,
                w1_ref, w2_ref, o_ref, *, tf):
    x1 = (x_ref[...] + _dot(ohg_ref[...], wo1_ref[...]) + _dot(osb_ref[...], wo2_ref[...])
          + _dot(ofx_ref[...], wo3_ref[...]))
    ms = jnp.mean(x1 * x1, axis=-1, keepdims=True)
    h2 = (x1 * lax.rsqrt(ms + EPS) * g2_ref[...]).astype(BF16)
    o_ref[...] = x1
    acc = None
    for f in range(w1_ref.shape[1] // tf):
        a = _dot(h2, w1_ref[:, f * tf:(f + 1) * tf])
        a = jnp.square(jnp.maximum(a, 0.0)).astype(BF16)
        y = _dot(a, w2_ref[f * tf:(f + 1) * tf, :])
        acc = y if acc is None else acc + y
    o_ref[...] = o_ref[...] + acc


def _mlp_call(x, ohg, osb, ofx, wo1, wo2, wo3, g2, w1, w2, l, tm, tf):
    m, d = x.shape
    row = lambda w: pl.BlockSpec((tm, w), lambda i: (i, 0))
    whole = lambda a: _layer(l, *a.shape[1:])
    return pl.pallas_call(
        functools.partial(_mlp_kernel, tf=tf),
        grid=(m // tm,),
        in_specs=[row(d), row(HG_WIDTH), row(SB_WIDTH), row(FOX_WIDTH),
                  whole(wo1), whole(wo2), whole(wo3), whole(g2), whole(w1), whole(w2)],
        out_specs=row(d),
        out_shape=jax.ShapeDtypeStruct((m, d), F32),
        compiler_params=pltpu.CompilerParams(
            dimension_semantics=("arbitrary",), vmem_limit_bytes=VMEM_LIMIT),
        name="outproj_mlp",
    )(x, ohg, osb, ofx, wo1, wo2, wo3, g2, w1, w2)


def _tile_sizes(batch, seq):
    m = batch * seq
    return dict(
        proj_tm=min(512, m),
        hg_tb=min(512, seq),
        sb_bq=min(256, seq),
        sb_sub=max(1, min(4, seq // 256)),
        fox_bq=min(256, seq),
        fox_bk=min(512, seq),
        mlp_tm=min(512, m),
        mlp_tf=1024,
    )


def kernel(x, lb_logits, norm1_g, w_in, hg_norm_g, sb_q_norm_g, sb_k_norm_g, fox_q_norm_g,
           fox_k_norm_g, fox_f_bias, w_out, norm2_g, w_ff1, w_ff2):
    batch, seq, d = x.shape
    assert d == D_MODEL and seq % HG_ROWS == 0 and x.dtype == F32
    ts = _tile_sizes(batch, seq)
    assert seq % (ts["sb_bq"] * ts["sb_sub"]) == 0 and seq % ts["fox_bk"] == 0 and ts["fox_bk"] % ts["fox_bq"] == 0
    assert seq % ts["hg_tb"] == 0
    m = batch * seq
    xf = x.reshape(m, d)
    row = lambda a: a.astype(F32)[:, None, :]
    pair = lambda g: row(jnp.tile(g, (1, 2)))
    w_in_p = jnp.pad(w_in, ((0, 0), (0, 0), (0, IN_COLS_PAD - IN_COLS))).astype(BF16)
    wo = w_out.astype(BF16)
    wo_hg, wo_sb, wo_fx = wo[:, :HG_WIDTH], wo[:, HG_WIDTH:HG_WIDTH + SB_WIDTH], wo[:, HG_WIDTH + SB_WIDTH:]
    w1, w2 = w_ff1.astype(BF16), w_ff2.astype(BF16)
    g1, g2, g_hg = row(norm1_g), row(norm2_g), row(jnp.tile(hg_norm_g, (1, HG_HEADS)))
    qk_gains = (pair(sb_q_norm_g), pair(sb_k_norm_g), pair(fox_q_norm_g), pair(fox_k_norm_g))
    fbias = row(jnp.pad(fox_f_bias, ((0, 0), (0, LANES - FOX_HEADS))))
    lbl = lb_logits.astype(F32)

    for l in range(DEPTH):
        hg, ff, sq, sk, sv, fq, fk, fv = _proj_call(xf, g1, w_in_p, lbl, qk_gains, l, ts["proj_tm"])
        o_hg = _hgrn_call(l, lbl, hg, g_hg, batch, seq, ts["hg_tb"])
        o_sb = _sb_call(sq, sk, sv, batch, seq, ts["sb_bq"], ts["sb_sub"])
        o_fx = _fox_call(fq, fk, fv, ff, fbias, l, batch, seq, ts["fox_bq"], ts["fox_bk"])
        xf = _mlp_call(xf, o_hg, o_sb, o_fx, wo_hg, wo_sb, wo_fx, g2, w1, w2, l,
                       ts["mlp_tm"], ts["mlp_tf"])
    return xf.reshape(batch, seq, d)
```

```python
import functools
import math

import jax
import jax.numpy as jnp
from jax import lax
from jax.experimental import pallas as pl
from jax.experimental.pallas import tpu as pltpu

F32 = jnp.float32
BF16 = jnp.bfloat16

D_MODEL = 1024
DEPTH = 4
HEAD_DIM = 64
HG_HEADS = 4
HG_KW = 256
HG_WIDTH = 256
SB_WIDTH = 384
FOX_WIDTH = 384
FOX_HEADS = 6
IN_COLS = 4 * 256 + 3 * 384 + 3 * 384 + FOX_HEADS
D_FF = 4 * D_MODEL
EPS = 1e-6
LB_FLOOR = 1e-30
NEG_BIG = -1e30
LOG2E = math.log2(math.e)

LANES = 128
IN_COLS_PAD = 27 * LANES
SB_Q_BLK, SB_K_BLK, SB_V_BLK = 8, 11, 14
FOX_Q_BLK, FOX_K_BLK, FOX_V_BLK, FOX_F_BLK = 17, 20, 23, 26

SB_ZERO_LOG = -105.0
SUB = 16
HG_ROWS = 256
VMEM_LIMIT = 56 * 1024 * 1024


def _nt_dot(a, b):
    return lax.dot_general(a, b, (((1,), (1,)), ((), ())), preferred_element_type=F32)


def _dot(a, b):
    return jnp.dot(a, b, preferred_element_type=F32)


def _split2(x):
    hi = x.astype(BF16)
    lo = (x - hi.astype(F32)).astype(BF16)
    return hi, lo


def _split3(x):
    hi = x.astype(BF16)
    r1 = x - hi.astype(F32)
    mid = r1.astype(BF16)
    lo = (r1 - mid.astype(F32)).astype(BF16)
    return hi, mid, lo


def _softplus(z):
    return jnp.maximum(z, 0.0) + jnp.log(1.0 + jnp.exp(-jnp.abs(z)))


def _softplus2(z2):
    return jnp.maximum(z2, 0.0) + jnp.log2(1.0 + jnp.exp2(-jnp.abs(z2)))


def _pair_rms(x, gain, lane_lo):
    x2 = x * x
    s0 = jnp.sum(jnp.where(lane_lo, x2, 0.0), axis=1, keepdims=True)
    s1 = jnp.sum(jnp.where(lane_lo, 0.0, x2), axis=1, keepdims=True)
    ms = jnp.where(lane_lo, s0, s1) * (1.0 / HEAD_DIM)
    return x * lax.rsqrt(ms + EPS) * gain


def _layer(l, *shape):
    return pl.BlockSpec((None,) + shape, lambda *_: (l,) + (0,) * len(shape))


def _lower_bound(lbl_ref, layer):
    rows = [lbl_ref[j:j + 1, :] for j in range(DEPTH)]
    mx = functools.reduce(jnp.maximum, rows)
    ex = [jnp.exp(r - mx) for r in rows]
    den = functools.reduce(lambda a, b: a + b, ex)
    lb = jnp.zeros_like(mx)
    for j in range(1, layer + 1):
        lb = lb + ex[j] / den
    return lb


def _proj_kernel(x_ref, g_ref, w_ref, lbl_ref, gsq_ref, gsk_ref, gfq_ref, gfk_ref,
                 hg_ref, ff_ref, sq_ref, sk_ref, sv_ref, fq_ref, fk_ref, fv_ref, *, layer):
    x = x_ref[...]
    ms = jnp.mean(x * x, axis=-1, keepdims=True)
    h = (x * lax.rsqrt(ms + EPS) * g_ref[...]).astype(BF16)
    y = _dot(h, w_ref[...])
    fl = y[:, HG_KW:2 * HG_KW]
    gt = y[:, 3 * HG_KW:4 * HG_KW]
    log_lb = jnp.log(jnp.maximum(_lower_bound(lbl_ref, layer), LB_FLOOR))
    sp = _softplus(fl)
    hg_ref[:, :HG_KW] = y[:, :HG_KW]
    hg_ref[:, HG_KW:2 * HG_KW] = sp
    hg_ref[:, 2 * HG_KW:3 * HG_KW] = y[:, 2 * HG_KW:3 * HG_KW]
    hg_ref[:, 3 * HG_KW:4 * HG_KW] = gt / (1.0 + jnp.exp(-gt))
    hg_ref[:, 4 * HG_KW:] = (jnp.maximum(fl, log_lb) + jnp.log(1.0 + jnp.exp(-jnp.abs(fl - log_lb)))
                             - sp)
    ff_ref[...] = y[:, FOX_F_BLK * LANES:]
    lane_lo = lax.broadcasted_iota(jnp.int32, (x.shape[0], LANES), 1) < HEAD_DIM
    q_mult = LOG2E / math.sqrt(HEAD_DIM)
    groups = ((SB_Q_BLK, gsq_ref, q_mult, sq_ref), (SB_K_BLK, gsk_ref, None, sk_ref),
              (SB_V_BLK, None, None, sv_ref),
              (FOX_Q_BLK, gfq_ref, q_mult, fq_ref), (FOX_K_BLK, gfk_ref, None, fk_ref),
              (FOX_V_BLK, None, None, fv_ref))
    for blk, gain_ref, mult, out_ref in groups:
        for p in range(SB_WIDTH // LANES):
            t = y[:, (blk + p) * LANES:(blk + p + 1) * LANES]
            if gain_ref is not None:
                t = _pair_rms(t, gain_ref[...], lane_lo)
            if mult is not None:
                t = t * mult
            out_ref[:, p * LANES:(p + 1) * LANES] = t.astype(BF16)


def _proj_call(x, g, w, lbl, gains, l, tm):
    m, d = x.shape
    n = w.shape[2]
    rows = lambda width: pl.BlockSpec((tm, width), lambda i: (i, 0))
    widths = (5 * HG_KW, LANES) + (SB_WIDTH,) * 6
    dtypes = (F32, F32) + (BF16,) * 6
    return pl.pallas_call(
        functools.partial(_proj_kernel, layer=l),
        grid=(m // tm,),
        in_specs=([rows(d), _layer(l, 1, d), _layer(l, d, n),
                   pl.BlockSpec((DEPTH, HG_KW), lambda i: (0, 0))] + [_layer(l, 1, LANES)] * 4),
        out_specs=[rows(wd) for wd in widths],
        out_shape=[jax.ShapeDtypeStruct((m, wd), dt) for wd, dt in zip(widths, dtypes)],
        compiler_params=pltpu.CompilerParams(
            dimension_semantics=("arbitrary",), vmem_limit_bytes=VMEM_LIMIT),
        name="proj",
    )(x, g, w, lbl, *gains)


def _hgrn_kernel(lbl_ref, q_ref, sp_ref, i_ref, gate_ref, lf_ref, gn_ref, o_ref, st_scr, oi_scr,
                 *, layer, tb):
    @pl.when(pl.program_id(1) == 0)
    def _():
        st_scr[...] = jnp.zeros_like(st_scr)

    one_m_lb = 1.0 - _lower_bound(lbl_ref, layer)
    gn = gn_ref[...]

    r_i = lax.broadcasted_iota(jnp.int32, (HG_KW, HG_KW), 0)
    c_i = lax.broadcasted_iota(jnp.int32, (HG_KW, HG_KW), 1)
    same_head = (r_i >> 6) == (c_i >> 6)
    ones_bd = jnp.where(same_head, 1.0, 0.0).astype(BF16)
    t_mod = lax.broadcasted_iota(jnp.int32, (HG_ROWS, HG_KW), 0) & (SUB - 1)
    lane_sub = lax.broadcasted_iota(jnp.int32, (HG_KW, HG_ROWS), 1) // SUB
    t_r = lax.broadcasted_iota(jnp.int32, (HG_ROWS, HG_ROWS), 0)
    t_c = lax.broadcasted_iota(jnp.int32, (HG_ROWS, HG_ROWS), 1)
    in_sub = (t_r // SUB) == (t_c // SUB)
    same_sub = jnp.where(in_sub, 1.0, 0.0).astype(BF16)
    upto_sub = jnp.where(jnp.logical_and(in_sub, t_c <= t_r), 1.0, 0.0).astype(BF16)

    def chunk(c, carry):
        rs = pl.ds(pl.multiple_of(c * HG_ROWS, HG_ROWS), HG_ROWS)
        q = q_ref[rs, :]
        sp = sp_ref[rs, :]
        v = i_ref[rs, :]
        lf = lf_ref[rs, :]
        kk = one_m_lb * jnp.exp(-sp)

        parts = _split3(lf)
        b = functools.reduce(lambda x, y: x + y, [_dot(upto_sub, part) for part in parts])
        beta = functools.reduce(lambda x, y: x + y, [_dot(same_sub, part) for part in parts])
        rest = beta - b

        b2 = b * LOG2E
        w2 = -sp * LOG2E - b2
        q_lb = q * one_m_lb
        acc = _dot((q * kk).astype(BF16), ones_bd) * v
        for d in range(1, SUB):
            e = jnp.where(t_mod >= d, b2 + pltpu.roll(w2, d, 0), NEG_BIG)
            p = q_lb * jnp.exp2(e)
            acc = acc + _dot(p.astype(BF16), ones_bd) * pltpu.roll(v, d, 0)

        qt = (q * jnp.exp2(b2)).astype(BF16)
        kt = (kk * jnp.exp(rest)).astype(BF16)
        v_t = v.T
        for j in range(HG_ROWS // SUB):
            st = st_scr[...]
            oi_scr[j * SUB:(j + 1) * SUB, :] = _nt_dot(qt[j * SUB:(j + 1) * SUB, :], st.astype(BF16))
            vm = jnp.where(lane_sub == j, v_t, 0.0).astype(BF16)
            ut = _dot(vm, kt)
            dec = jnp.exp(beta[j * SUB:j * SUB + 1, :])
            st_scr[...] = st * dec + jnp.where(same_head, ut, 0.0)

        o = acc + oi_scr[...]
        hi, lo = _split2(o * o)
        msq = (_dot(hi, ones_bd) + _dot(lo, ones_bd)) * (1.0 / HEAD_DIM)
        y = o * lax.rsqrt(msq + EPS) * gn
        o_ref[rs, :] = (y * gate_ref[rs, :]).astype(o_ref.dtype)
        return carry

    lax.fori_loop(0, tb // HG_ROWS, chunk, 0)


def _hgrn_call(layer, lb_logits, proj, gn, batch, seq, tb):
    m = proj.shape[0]
    nt = seq // tb

    def col(j):
        return pl.BlockSpec((tb, HG_KW), lambda b, t, j=j: (b * nt + t, j))

    return pl.pallas_call(
        functools.partial(_hgrn_kernel, layer=layer, tb=tb),
        grid=(batch, nt),
        in_specs=[pl.BlockSpec((DEPTH, HG_KW), lambda b, t: (0, 0)),
                  col(0), col(1), col(2), col(3), col(4), _layer(layer, 1, HG_WIDTH)],
        out_specs=pl.BlockSpec((tb, HG_WIDTH), lambda b, t: (b * nt + t, 0)),
        out_shape=jax.ShapeDtypeStruct((m, HG_WIDTH), BF16),
        scratch_shapes=[pltpu.VMEM((HG_WIDTH, HG_KW), F32),
                        pltpu.VMEM((HG_ROWS, HG_WIDTH), F32)],
        compiler_params=pltpu.CompilerParams(
            dimension_semantics=("arbitrary", "arbitrary"), vmem_limit_bytes=VMEM_LIMIT),
        name="hgrn2",
    )(lb_logits, proj, proj, proj, proj, proj, gn)


def _sb_kernel(q0_ref, q1_ref, q2_ref, k0_ref, k1_ref, k2_ref, v0_ref, v1_ref, v2_ref,
               o_ref, acc_scr, car_scr, *, bq, n_sub):
    g = pl.program_id(1)
    q_refs = (q0_ref, q1_ref, q2_ref)
    k_refs = (k0_ref, k1_ref, k2_ref)
    v_refs = (v0_ref, v1_ref, v2_ref)
    n_pairs = len(q_refs)

    lane_lo = lax.broadcasted_iota(jnp.int32, (bq, LANES), 1) < HEAD_DIM
    r_i = lax.broadcasted_iota(jnp.int32, (bq, bq), 0)
    c_i = lax.broadcasted_iota(jnp.int32, (bq, bq), 1)
    before = c_i < r_i
    later = jnp.where(r_i > c_i, 1.0, 0.0).astype(BF16)

    heads = range(2 * n_pairs)

    def q_block(sub, carry_unused):
        i = g * n_sub + sub
        qs = pl.ds(pl.multiple_of(sub * bq, bq), bq)
        qms = []
        for p in range(n_pairs):
            qn = q_refs[p][qs, :]
            zero = jnp.zeros_like(qn)
            qms.append(jnp.where(lane_lo, qn, zero))
            qms.append(jnp.where(lane_lo, zero, qn))

        def absorb(tiles, fresh):
            rows = [pl.ds(pl.multiple_of(kb * bq, bq), bq) for kb, _ in tiles]
            masks = [mask for _, mask in tiles]
            nt = range(len(tiles))
            zs = [[_nt_dot(qms[h], k_refs[h // 2][rows[t], :]) for h in heads] for t in nt]
            sps = [[_softplus2(z) for z in zt] for zt in zs]
            us = [spt if masks[t] is None else [jnp.where(masks[t], sp, 0.0) for sp in spt]
                  for t, spt in enumerate(sps)]
            tails = [[_dot(u.astype(BF16), later) for u in ut] for ut in us]
            low = None
            for h in heads:
                carry = None if fresh else car_scr[h]
                acc = None if fresh else acc_scr[h]
                for t in nt:
                    row = jnp.broadcast_to(jnp.sum(us[t][h], axis=1, keepdims=True), (bq, LANES))
                    e = zs[t][h] - sps[t][h] - tails[t][h]
                    if carry is not None:
                        e = e - jnp.concatenate([carry] * (bq // LANES), axis=1)
                        row = row + carry
                    a = jnp.exp2(e)
                    if masks[t] is not None:
                        a = jnp.where(masks[t], a, 0.0)
                    pv = _dot(a.astype(BF16), v_refs[h // 2][rows[t], :])
                    acc = pv if acc is None else acc + pv
                    carry = row
                acc_scr[h] = acc
                car_scr[h] = carry
                low = carry if low is None else jnp.minimum(low, carry)
            return jnp.min(low)

        low = lax.cond(i > 0,
                       lambda: absorb([(i, before), (i - 1, None)], True),
                       lambda: absorb([(i, before)], True))

        def live(st):
            return jnp.logical_and(st[0] >= 0, st[1] < -SB_ZERO_LOG * LOG2E)

        lax.while_loop(live, lambda st: (st[0] - 1, absorb([(st[0], None)], False)), (i - 2, low))
        for p in range(n_pairs):
            o_ref[qs, p * LANES:(p + 1) * LANES] = jnp.where(
                lane_lo, acc_scr[2 * p], acc_scr[2 * p + 1]).astype(o_ref.dtype)
        return carry_unused

    lax.fori_loop(0, n_sub, q_block, 0)


def _sb_call(q, k, v, batch, seq, bq, n_sub):
    m = q.shape[0]
    rows = bq * n_sub
    nq = seq // rows
    n_pairs = SB_WIDTH // LANES
    q_spec = lambda p: pl.BlockSpec((rows, LANES), lambda b, i, p=p: (b * nq + i, p))
    k_spec = lambda p: pl.BlockSpec((seq, LANES), lambda b, i, p=p: (b, p))
    pairs = range(n_pairs)
    return pl.pallas_call(
        functools.partial(_sb_kernel, bq=bq, n_sub=n_sub),
        grid=(batch, nq),
        in_specs=[q_spec(p) for p in pairs] + [k_spec(p) for p in pairs] * 2,
        out_specs=pl.BlockSpec((rows, SB_WIDTH), lambda b, i: (b * nq + i, 0)),
        out_shape=jax.ShapeDtypeStruct((m, SB_WIDTH), BF16),
        scratch_shapes=[pltpu.VMEM((2 * n_pairs, bq, LANES), F32),
                        pltpu.VMEM((2 * n_pairs, bq, LANES), F32)],
        compiler_params=pltpu.CompilerParams(
            dimension_semantics=("arbitrary", "arbitrary"), vmem_limit_bytes=VMEM_LIMIT),
        name="stickbreak",
    )(*([q] * n_pairs + [k] * n_pairs + [v] * n_pairs))


def _fox_kernel(q_ref, k_ref, v_ref, f_ref, fb_ref, o_ref,
                c_scr, qa_scr, ka_scr, vt_scr, sa_scr, sb_scr, mb_scr, *, bq, bk, seq):
    p_idx = pl.program_id(1)

    @pl.when(p_idx == 0)
    def _():
        r_i = lax.broadcasted_iota(jnp.int32, (bk, bk), 0)
        c_i = lax.broadcasted_iota(jnp.int32, (bk, bk), 1)
        upto = jnp.where(c_i <= r_i, 1.0, 0.0).astype(BF16)

        def body(n, run):
            rs = pl.ds(pl.multiple_of(n * bk, bk), bk)
            y = f_ref[rs, :] + fb_ref[...]
            lg = jnp.minimum(y, 0.0) - jnp.log(1.0 + jnp.exp(-jnp.abs(y)))
            hi, mid, lo = _split3(lg)
            c = _dot(upto, hi) + _dot(upto, mid) + _dot(upto, lo) + run
            c_scr[rs, :] = c
            return c[bk - 1:bk, :]

        lax.fori_loop(0, seq // bk, body, jnp.zeros((1, LANES), F32))

    lane = lax.broadcasted_iota(jnp.int32, (bk, LANES), 1)
    lane_lo = lane < HEAD_DIM
    row_lo_k = lax.broadcasted_iota(jnp.int32, (LANES, bk), 0) < HEAD_DIM

    pr = lax.broadcasted_iota(jnp.int32, (LANES, LANES), 0)
    pc = lax.broadcasted_iota(jnp.int32, (LANES, LANES), 1)
    place, ones_q, ones_k = [], [], []
    for h in range(2):
        base = HEAD_DIM * (1 - h)
        place.append([jnp.where(jnp.logical_and(pr == 2 * p_idx + h,
                                                jnp.logical_or(pc == base + n, pc == base + 3 + n)),
                                1.0, 0.0).astype(BF16) for n in range(3)])
        ones_q.append(jnp.logical_and(lane >= base + 3, lane < base + 6))
        ones_k.append(jnp.logical_and(lane >= base, lane < base + 3))

    def prep(n, carry):
        rs = pl.ds(pl.multiple_of(n * bk, bk), bk)
        kn = k_ref[rs, :]
        qn = q_ref[rs, :]
        parts = _split3(c_scr[rs, :] * LOG2E)
        v_t = v_ref[rs, :].T
        for h in range(2):
            c_aug = functools.reduce(lambda x, y: x + y,
                                     [_dot(part, sel) for part, sel in zip(parts, place[h])])
            mine = lane_lo if h == 0 else jnp.logical_not(lane_lo)
            k_aug = jnp.where(ones_k[h], 1.0, -c_aug)
            q_aug = jnp.where(ones_q[h], 1.0, c_aug)
            ka_scr[h, rs, :] = jnp.where(mine, kn, k_aug.astype(BF16))
            qa_scr[h, rs, :] = jnp.where(mine, qn, q_aug.astype(BF16))
            vt_scr[h, n] = jnp.where(row_lo_k if h == 0 else jnp.logical_not(row_lo_k), v_t,
                                     jnp.ones_like(v_t))
        return carry

    lax.fori_loop(0, seq // bk, prep, 0)

    key_minus_qry = (lax.broadcasted_iota(jnp.int32, (bk, bq), 0)
                     - lax.broadcasted_iota(jnp.int32, (bk, bq), 1))
    row_lo = lax.broadcasted_iota(jnp.int32, (LANES, bq), 0) < HEAD_DIM

    nq = seq // bq
    n_items = sum((qi * bq + bq - 1) // bk + 1 for qi in range(nq))
    mb_scr[0] = jnp.zeros((bk, bq), F32)
    for r in range(bk // bq):
        mb_scr[r + 1] = jnp.where(key_minus_qry <= r * bq, 0.0, NEG_BIG)

    def unpack(item):
        qi = jnp.minimum(item[0], nq - 1)
        last = (qi * bq + bq - 1) // bk
        first = item[1] == 0
        kb = jnp.where(first, last, item[1] - 1)
        return qi, last, first, kb

    def advance(item):
        qi, last, _, _ = unpack(item)
        wrap = item[1] >= last
        return jnp.where(wrap, item[0] + 1, item[0]), jnp.where(wrap, 0, item[1] + 1)

    def scores_into(dst, item):
        qi, _, first, kb = unpack(item)
        qs = pl.ds(pl.multiple_of(qi * bq, bq), bq)
        rs = pl.ds(pl.multiple_of(kb * bk, bk), bk)
        bias = mb_scr[jnp.where(first, 1 + (qi * bq - kb * bk) // bq, 0)]
        for h in range(2):
            dst[h] = _nt_dot(ka_scr[h, rs, :], qa_scr[h, qs, :]) + bias

    def write_out(qi, sts):
        acc0, acc1 = sts[0][1], sts[1][1]
        o_t = jnp.where(row_lo, acc0 / acc0[HEAD_DIM:HEAD_DIM + 1, :], acc1 / acc1[0:1, :])
        o_ref[pl.ds(pl.multiple_of(qi * bq, bq), bq), :] = o_t.T.astype(o_ref.dtype)

    def absorb(src, item, prev_qi, sts):
        qi, _, first, kb = unpack(item)
        write_out(prev_qi, sts)
        out = []
        for h in range(2):
            m_run, acc = sts[h]
            m_run = jnp.where(first, NEG_BIG, m_run)
            s = src[h]
            m_new = jnp.maximum(m_run, jnp.max(s, axis=0, keepdims=True))
            p = jnp.exp2(s - m_new)
            acc = jnp.exp2(m_run - m_new) * acc + _dot(vt_scr[h, kb], p.astype(BF16))
            out.append((m_new, acc))
        return qi, tuple(out)

    unroll = max([u for u in (8, 4) if n_items % u == 0 and u * bk <= 2048] + [2])
    bufs = (sa_scr, sb_scr)

    def run(count, ca):
        held, prev_qi, sts = ca
        for k in range(count):
            nxt = advance(held)
            scores_into(bufs[(k + 1) % 2], nxt)
            prev_qi, sts = absorb(bufs[k % 2], held, prev_qi, sts)
            held = nxt
        return held, prev_qi, sts

    start = (jnp.int32(0), jnp.int32(0))
    init = (jnp.full((1, bq), NEG_BIG, F32), jnp.ones((LANES, bq), F32))
    scores_into(sa_scr, start)
    ca = lax.fori_loop(0, n_items // unroll, lambda u, ca: run(unroll, ca),
                       (start, jnp.int32(0), (init, init)))
    _, last_qi, sts = run(n_items % unroll, ca)
    write_out(last_qi, sts)


def _fox_call(q, k, v, ff, fbias, l, batch, seq, bq, bk):
    m = q.shape[0]
    col = pl.BlockSpec((seq, LANES), lambda b, p: (b, p))
    return pl.pallas_call(
        functools.partial(_fox_kernel, bq=bq, bk=bk, seq=seq),
        grid=(batch, FOX_WIDTH // LANES),
        in_specs=[col, col, col, pl.BlockSpec((seq, LANES), lambda b, p: (b, 0)), _layer(l, 1, LANES)],
        out_specs=pl.BlockSpec((seq, LANES), lambda b, p: (b, p)),
        out_shape=jax.ShapeDtypeStruct((m, FOX_WIDTH), BF16),
        scratch_shapes=[pltpu.VMEM((seq, LANES), F32),
                        pltpu.VMEM((2, seq, LANES), BF16),
                        pltpu.VMEM((2, seq, LANES), BF16),
                        pltpu.VMEM((2, seq // bk, LANES, bk), BF16),
                        pltpu.VMEM((2, bk, bq), F32),
                        pltpu.VMEM((2, bk, bq), F32),
                        pltpu.VMEM((bk // bq + 1, bk, bq), F32)],
        compiler_params=pltpu.CompilerParams(
            dimension_semantics=("arbitrary", "arbitrary"), vmem_limit_bytes=VMEM_LIMIT),
        name="forgetting",
    )(q, k, v, ff, fbias)


def _mlp_kernel(x_ref, ohg_ref, osb_ref, ofx_ref, wo1_ref, wo2_ref, wo3_ref, g2_ref,
                w1_ref, w2_ref, o_ref, *, tf):
    x1 = (x_ref[...] + _dot(ohg_ref[...], wo1_ref[...]) + _dot(osb_ref[...], wo2_ref[...])
          + _dot(ofx_ref[...], wo3_ref[...]))
    ms = jnp.mean(x1 * x1, axis=-1, keepdims=True)
    h2 = (x1 * lax.rsqrt(ms + EPS) * g2_ref[...]).astype(BF16)
    o_ref[...] = x1
    acc = None
    for f in range(w1_ref.shape[1] // tf):
        a = _dot(h2, w1_ref[:, f * tf:(f + 1) * tf])
        a = jnp.square(jnp.maximum(a, 0.0)).astype(BF16)
        y = _dot(a, w2_ref[f * tf:(f + 1) * tf, :])
        acc = y if acc is None else acc + y
    o_ref[...] = o_ref[...] + acc


def _mlp_call(x, ohg, osb, ofx, wo1, wo2, wo3, g2, w1, w2, l, tm, tf):
    m, d = x.shape
    row = lambda w: pl.BlockSpec((tm, w), lambda i: (i, 0))
    whole = lambda a: _layer(l, *a.shape[1:])
    return pl.pallas_call(
        functools.partial(_mlp_kernel, tf=tf),
        grid=(m // tm,),
        in_specs=[row(d), row(HG_WIDTH), row(SB_WIDTH), row(FOX_WIDTH),
                  whole(wo1), whole(wo2), whole(wo3), whole(g2), whole(w1), whole(w2)],
        out_specs=row(d),
        out_shape=jax.ShapeDtypeStruct((m, d), F32),
        compiler_params=pltpu.CompilerParams(
            dimension_semantics=("arbitrary",), vmem_limit_bytes=VMEM_LIMIT),
        name="outproj_mlp",
    )(x, ohg, osb, ofx, wo1, wo2, wo3, g2, w1, w2)


def _tile_sizes(batch, seq):
    m = batch * seq
    return dict(
        proj_tm=min(512, m),
        hg_tb=min(2048, seq),
        sb_bq=min(256, seq),
        sb_sub=max(1, seq // 256),
        fox_bq=min(256, seq),
        fox_bk=min(512, seq),
        mlp_tm=min(512, m),
        mlp_tf=1024,
    )


def kernel(x, lb_logits, norm1_g, w_in, hg_norm_g, sb_q_norm_g, sb_k_norm_g, fox_q_norm_g,
           fox_k_norm_g, fox_f_bias, w_out, norm2_g, w_ff1, w_ff2):
    batch, seq, d = x.shape
    assert d == D_MODEL and seq % HG_ROWS == 0 and x.dtype == F32
    ts = _tile_sizes(batch, seq)
    assert seq % (ts["sb_bq"] * ts["sb_sub"]) == 0 and seq % ts["fox_bk"] == 0 and ts["fox_bk"] % ts["fox_bq"] == 0
    assert seq % ts["hg_tb"] == 0
    m = batch * seq
    xf = x.reshape(m, d)
    row = lambda a: a.astype(F32)[:, None, :]
    pair = lambda g: row(jnp.tile(g, (1, 2)))
    w_in_p = jnp.pad(w_in, ((0, 0), (0, 0), (0, IN_COLS_PAD - IN_COLS))).astype(BF16)
    wo = w_out.astype(BF16)
    wo_hg, wo_sb, wo_fx = wo[:, :HG_WIDTH], wo[:, HG_WIDTH:HG_WIDTH + SB_WIDTH], wo[:, HG_WIDTH + SB_WIDTH:]
    w1, w2 = w_ff1.astype(BF16), w_ff2.astype(BF16)
    g1, g2, g_hg = row(norm1_g), row(norm2_g), row(jnp.tile(hg_norm_g, (1, HG_HEADS)))
    qk_gains = (pair(sb_q_norm_g), pair(sb_k_norm_g), pair(fox_q_norm_g), pair(fox_k_norm_g))
    fbias = row(jnp.pad(fox_f_bias, ((0, 0), (0, LANES - FOX_HEADS))))
    lbl = lb_logits.astype(F32)

    for l in range(DEPTH):
        hg, ff, sq, sk, sv, fq, fk, fv = _proj_call(xf, g1, w_in_p, lbl, qk_gains, l, ts["proj_tm"])
        o_hg = _hgrn_call(l, lbl, hg, g_hg, batch, seq, ts["hg_tb"])
        o_sb = _sb_call(sq, sk, sv, batch, seq, ts["sb_bq"], ts["sb_sub"])
        o_fx = _fox_call(fq, fk, fv, ff, fbias, l, batch, seq, ts["fox_bq"], ts["fox_bk"])
        xf = _mlp_call(xf, o_hg, o_sb, o_fx, wo_hg, wo_sb, wo_fx, g2, w1, w2, l,
                       ts["mlp_tm"], ts["mlp_tf"])
    return xf.reshape(batch, seq, d)
```

```python
import functools
import math

import jax
import jax.numpy as jnp
from jax import lax
from jax.experimental import pallas as pl
from jax.experimental.pallas import tpu as pltpu

F32 = jnp.float32
BF16 = jnp.bfloat16

D_MODEL = 1024
DEPTH = 4
HEAD_DIM = 64
HG_HEADS = 4
HG_KW = 256
HG_WIDTH = 256
SB_WIDTH = 384
FOX_WIDTH = 384
FOX_HEADS = 6
IN_COLS = 4 * 256 + 3 * 384 + 3 * 384 + FOX_HEADS
D_FF = 4 * D_MODEL
EPS = 1e-6
LB_FLOOR = 1e-30
NEG_BIG = -1e30
LOG2E = math.log2(math.e)

LANES = 128
IN_COLS_PAD = 27 * LANES
SB_Q_BLK, SB_K_BLK, SB_V_BLK = 8, 11, 14
FOX_Q_BLK, FOX_K_BLK, FOX_V_BLK, FOX_F_BLK = 17, 20, 23, 26

SB_ZERO_LOG = -105.0
SUB = 16
HG_ROWS = 256
FOX_KEYS_PER_TRIP = 2048
VMEM_LIMIT = 56 * 1024 * 1024


def _nt_dot(a, b):
    return lax.dot_general(a, b, (((1,), (1,)), ((), ())), preferred_element_type=F32)


def _dot(a, b):
    return jnp.dot(a, b, preferred_element_type=F32)


def _split2(x):
    hi = x.astype(BF16)
    lo = (x - hi.astype(F32)).astype(BF16)
    return hi, lo


def _split3(x):
    hi = x.astype(BF16)
    r1 = x - hi.astype(F32)
    mid = r1.astype(BF16)
    lo = (r1 - mid.astype(F32)).astype(BF16)
    return hi, mid, lo


def _softplus(z):
    return jnp.maximum(z, 0.0) + jnp.log(1.0 + jnp.exp(-jnp.abs(z)))


def _softplus2(z2):
    return jnp.maximum(z2, 0.0) + jnp.log2(1.0 + jnp.exp2(-jnp.abs(z2)))


def _pair_rms(x, gain, lane_lo):
    x2 = x * x
    s0 = jnp.sum(jnp.where(lane_lo, x2, 0.0), axis=1, keepdims=True)
    s1 = jnp.sum(jnp.where(lane_lo, 0.0, x2), axis=1, keepdims=True)
    ms = jnp.where(lane_lo, s0, s1) * (1.0 / HEAD_DIM)
    return x * lax.rsqrt(ms + EPS) * gain


def _layer(l, *shape):
    return pl.BlockSpec((None,) + shape, lambda *_: (l,) + (0,) * len(shape))


def _lower_bound(lbl_ref, layer):
    rows = [lbl_ref[j:j + 1, :] for j in range(DEPTH)]
    mx = functools.reduce(jnp.maximum, rows)
    ex = [jnp.exp(r - mx) for r in rows]
    den = functools.reduce(lambda a, b: a + b, ex)
    lb = jnp.zeros_like(mx)
    for j in range(1, layer + 1):
        lb = lb + ex[j] / den
    return lb


def _proj_kernel(x_ref, g_ref, w_ref, lbl_ref, gsq_ref, gsk_ref, gfq_ref, gfk_ref,
                 hg_ref, ff_ref, sq_ref, sk_ref, sv_ref, fq_ref, fk_ref, fv_ref, *, layer):
    x = x_ref[...]
    ms = jnp.mean(x * x, axis=-1, keepdims=True)
    h = (x * lax.rsqrt(ms + EPS) * g_ref[...]).astype(BF16)
    y = _dot(h, w_ref[...])
    fl = y[:, HG_KW:2 * HG_KW]
    gt = y[:, 3 * HG_KW:4 * HG_KW]
    log_lb = jnp.log(jnp.maximum(_lower_bound(lbl_ref, layer), LB_FLOOR))
    sp = _softplus(fl)
    hg_ref[:, :HG_KW] = y[:, :HG_KW]
    hg_ref[:, HG_KW:2 * HG_KW] = sp
    hg_ref[:, 2 * HG_KW:3 * HG_KW] = y[:, 2 * HG_KW:3 * HG_KW]
    hg_ref[:, 3 * HG_KW:4 * HG_KW] = gt / (1.0 + jnp.exp(-gt))
    hg_ref[:, 4 * HG_KW:] = (jnp.maximum(fl, log_lb) + jnp.log(1.0 + jnp.exp(-jnp.abs(fl - log_lb)))
                             - sp)
    ff_ref[...] = y[:, FOX_F_BLK * LANES:]
    lane_lo = lax.broadcasted_iota(jnp.int32, (x.shape[0], LANES), 1) < HEAD_DIM
    q_mult = LOG2E / math.sqrt(HEAD_DIM)
    groups = ((SB_Q_BLK, gsq_ref, q_mult, sq_ref), (SB_K_BLK, gsk_ref, None, sk_ref),
              (SB_V_BLK, None, None, sv_ref),
              (FOX_Q_BLK, gfq_ref, q_mult, fq_ref), (FOX_K_BLK, gfk_ref, None, fk_ref),
              (FOX_V_BLK, None, None, fv_ref))
    for blk, gain_ref, mult, out_ref in groups:
        for p in range(SB_WIDTH // LANES):
            t = y[:, (blk + p) * LANES:(blk + p + 1) * LANES]
            if gain_ref is not None:
                t = _pair_rms(t, gain_ref[...], lane_lo)
            if mult is not None:
                t = t * mult
            out_ref[:, p * LANES:(p + 1) * LANES] = t.astype(BF16)


def _proj_call(x, g, w, lbl, gains, l, tm):
    m, d = x.shape
    n = w.shape[2]
    rows = lambda width: pl.BlockSpec((tm, width), lambda i: (i, 0))
    widths = (5 * HG_KW, LANES) + (SB_WIDTH,) * 6
    dtypes = (F32, F32) + (BF16,) * 6
    return pl.pallas_call(
        functools.partial(_proj_kernel, layer=l),
        grid=(m // tm,),
        in_specs=([rows(d), _layer(l, 1, d), _layer(l, d, n),
                   pl.BlockSpec((DEPTH, HG_KW), lambda i: (0, 0))] + [_layer(l, 1, LANES)] * 4),
        out_specs=[rows(wd) for wd in widths],
        out_shape=[jax.ShapeDtypeStruct((m, wd), dt) for wd, dt in zip(widths, dtypes)],
        compiler_params=pltpu.CompilerParams(
            dimension_semantics=("arbitrary",), vmem_limit_bytes=VMEM_LIMIT),
        name="proj",
    )(x, g, w, lbl, *gains)


def _hgrn_kernel(lbl_ref, q_ref, sp_ref, i_ref, gate_ref, lf_ref, gn_ref, o_ref, st_scr, oi_scr,
                 *, layer, tb):
    @pl.when(pl.program_id(1) == 0)
    def _():
        st_scr[...] = jnp.zeros_like(st_scr)

    one_m_lb = 1.0 - _lower_bound(lbl_ref, layer)
    gn = gn_ref[...]

    r_i = lax.broadcasted_iota(jnp.int32, (HG_KW, HG_KW), 0)
    c_i = lax.broadcasted_iota(jnp.int32, (HG_KW, HG_KW), 1)
    same_head = (r_i // HEAD_DIM) == (c_i // HEAD_DIM)
    ones_bd = jnp.where(same_head, 1.0, 0.0).astype(BF16)
    t_mod = lax.broadcasted_iota(jnp.int32, (HG_ROWS, HG_KW), 0) & (SUB - 1)
    lane_sub = lax.broadcasted_iota(jnp.int32, (HG_KW, HG_ROWS), 1) // SUB
    t_r = lax.broadcasted_iota(jnp.int32, (HG_ROWS, HG_ROWS), 0)
    t_c = lax.broadcasted_iota(jnp.int32, (HG_ROWS, HG_ROWS), 1)
    in_sub = (t_r // SUB) == (t_c // SUB)
    same_sub = jnp.where(in_sub, 1.0, 0.0).astype(BF16)
    upto_sub = jnp.where(jnp.logical_and(in_sub, t_c <= t_r), 1.0, 0.0).astype(BF16)

    def chunk(c, carry):
        rs = pl.ds(pl.multiple_of(c * HG_ROWS, HG_ROWS), HG_ROWS)
        q = q_ref[rs, :]
        sp = sp_ref[rs, :]
        v = i_ref[rs, :]
        lf = lf_ref[rs, :]
        kk = one_m_lb * jnp.exp(-sp)

        parts = _split3(lf)
        b = functools.reduce(lambda x, y: x + y, [_dot(upto_sub, part) for part in parts])
        beta = functools.reduce(lambda x, y: x + y, [_dot(same_sub, part) for part in parts])
        rest = beta - b

        b2 = b * LOG2E
        w2 = -sp * LOG2E - b2
        q_lb = q * one_m_lb
        acc = _dot((q * kk).astype(BF16), ones_bd) * v
        for d in range(1, SUB):
            e = jnp.where(t_mod >= d, b2 + pltpu.roll(w2, d, 0), NEG_BIG)
            p = q_lb * jnp.exp2(e)
            acc = acc + _dot(p.astype(BF16), ones_bd) * pltpu.roll(v, d, 0)

        qt = (q * jnp.exp2(b2)).astype(BF16)
        kt = (kk * jnp.exp(rest)).astype(BF16)
        v_t = v.T
        for j in range(HG_ROWS // SUB):
            st = st_scr[...]
            oi_scr[j * SUB:(j + 1) * SUB, :] = _nt_dot(qt[j * SUB:(j + 1) * SUB, :], st.astype(BF16))
            vm = jnp.where(lane_sub == j, v_t, 0.0).astype(BF16)
            ut = _dot(vm, kt)
            dec = jnp.exp(beta[j * SUB:j * SUB + 1, :])
            st_scr[...] = st * dec + jnp.where(same_head, ut, 0.0)

        o = acc + oi_scr[...]
        hi, lo = _split2(o * o)
        msq = (_dot(hi, ones_bd) + _dot(lo, ones_bd)) * (1.0 / HEAD_DIM)
        y = o * lax.rsqrt(msq + EPS) * gn
        o_ref[rs, :] = (y * gate_ref[rs, :]).astype(o_ref.dtype)
        return carry

    lax.fori_loop(0, tb // HG_ROWS, chunk, 0)


def _hgrn_call(layer, lb_logits, proj, gn, batch, seq, tb):
    m = proj.shape[0]
    nt = seq // tb

    def col(j):
        return pl.BlockSpec((tb, HG_KW), lambda b, t, j=j: (b * nt + t, j))

    return pl.pallas_call(
        functools.partial(_hgrn_kernel, layer=layer, tb=tb),
        grid=(batch, nt),
        in_specs=[pl.BlockSpec((DEPTH, HG_KW), lambda b, t: (0, 0)),
                  col(0), col(1), col(2), col(3), col(4), _layer(layer, 1, HG_WIDTH)],
        out_specs=pl.BlockSpec((tb, HG_WIDTH), lambda b, t: (b * nt + t, 0)),
        out_shape=jax.ShapeDtypeStruct((m, HG_WIDTH), BF16),
        scratch_shapes=[pltpu.VMEM((HG_WIDTH, HG_KW), F32),
                        pltpu.VMEM((HG_ROWS, HG_WIDTH), F32)],
        compiler_params=pltpu.CompilerParams(
            dimension_semantics=("arbitrary", "arbitrary"), vmem_limit_bytes=VMEM_LIMIT),
        name="hgrn2",
    )(lb_logits, proj, proj, proj, proj, proj, gn)


def _sb_kernel(q0_ref, q1_ref, q2_ref, k0_ref, k1_ref, k2_ref, v0_ref, v1_ref, v2_ref,
               o_ref, acc_scr, car_scr, *, bq, n_sub):
    g = pl.program_id(1)
    q_refs = (q0_ref, q1_ref, q2_ref)
    k_refs = (k0_ref, k1_ref, k2_ref)
    v_refs = (v0_ref, v1_ref, v2_ref)
    n_pairs = len(q_refs)

    lane_lo = lax.broadcasted_iota(jnp.int32, (bq, LANES), 1) < HEAD_DIM
    r_i = lax.broadcasted_iota(jnp.int32, (bq, bq), 0)
    c_i = lax.broadcasted_iota(jnp.int32, (bq, bq), 1)
    before = c_i < r_i
    later = jnp.where(r_i > c_i, 1.0, 0.0).astype(BF16)

    heads = range(2 * n_pairs)

    def q_block(sub, carry_unused):
        i = g * n_sub + sub
        qs = pl.ds(pl.multiple_of(sub * bq, bq), bq)
        qms = []
        for p in range(n_pairs):
            qn = q_refs[p][qs, :]
            zero = jnp.zeros_like(qn)
            qms.append(jnp.where(lane_lo, qn, zero))
            qms.append(jnp.where(lane_lo, zero, qn))

        def absorb(tiles, fresh):
            rows = [pl.ds(pl.multiple_of(kb * bq, bq), bq) for kb, _ in tiles]
            masks = [mask for _, mask in tiles]
            nt = range(len(tiles))
            zs = [[_nt_dot(qms[h], k_refs[h // 2][rows[t], :]) for h in heads] for t in nt]
            sps = [[_softplus2(z) for z in zt] for zt in zs]
            us = [spt if masks[t] is None else [jnp.where(masks[t], sp, 0.0) for sp in spt]
                  for t, spt in enumerate(sps)]
            tails = [[_dot(u.astype(BF16), later) for u in ut] for ut in us]
            low = None
            for h in heads:
                carry = None if fresh else car_scr[h]
                acc = None if fresh else acc_scr[h]
                for t in nt:
                    row = jnp.broadcast_to(jnp.sum(us[t][h], axis=1, keepdims=True), (bq, LANES))
                    e = zs[t][h] - sps[t][h] - tails[t][h]
                    if carry is not None:
                        e = e - jnp.concatenate([carry] * (bq // LANES), axis=1)
                        row = row + carry
                    a = jnp.exp2(e)
                    if masks[t] is not None:
                        a = jnp.where(masks[t], a, 0.0)
                    pv = _dot(a.astype(BF16), v_refs[h // 2][rows[t], :])
                    acc = pv if acc is None else acc + pv
                    carry = row
                acc_scr[h] = acc
                car_scr[h] = carry
                low = carry if low is None else jnp.minimum(low, carry)
            return jnp.min(low)

        low = lax.cond(i > 0,
                       lambda: absorb([(i, before), (i - 1, None)], True),
                       lambda: absorb([(i, before)], True))

        def live(st):
            return jnp.logical_and(st[0] >= 0, st[1] < -SB_ZERO_LOG * LOG2E)

        lax.while_loop(live, lambda st: (st[0] - 1, absorb([(st[0], None)], False)), (i - 2, low))
        for p in range(n_pairs):
            o_ref[qs, p * LANES:(p + 1) * LANES] = jnp.where(
                lane_lo, acc_scr[2 * p], acc_scr[2 * p + 1]).astype(o_ref.dtype)
        return carry_unused

    lax.fori_loop(0, n_sub, q_block, 0)


def _sb_call(q, k, v, batch, seq, bq, n_sub):
    m = q.shape[0]
    rows = bq * n_sub
    nq = seq // rows
    n_pairs = SB_WIDTH // LANES
    q_spec = lambda p: pl.BlockSpec((rows, LANES), lambda b, i, p=p: (b * nq + i, p))
    k_spec = lambda p: pl.BlockSpec((seq, LANES), lambda b, i, p=p: (b, p))
    pairs = range(n_pairs)
    return pl.pallas_call(
        functools.partial(_sb_kernel, bq=bq, n_sub=n_sub),
        grid=(batch, nq),
        in_specs=[q_spec(p) for p in pairs] + [k_spec(p) for p in pairs] * 2,
        out_specs=pl.BlockSpec((rows, SB_WIDTH), lambda b, i: (b * nq + i, 0)),
        out_shape=jax.ShapeDtypeStruct((m, SB_WIDTH), BF16),
        scratch_shapes=[pltpu.VMEM((2 * n_pairs, bq, LANES), F32),
                        pltpu.VMEM((2 * n_pairs, bq, LANES), F32)],
        compiler_params=pltpu.CompilerParams(
            dimension_semantics=("arbitrary", "arbitrary"), vmem_limit_bytes=VMEM_LIMIT),
        name="stickbreak",
    )(*([q] * n_pairs + [k] * n_pairs + [v] * n_pairs))


def _fox_kernel(q_ref, k_ref, v_ref, f_ref, fb_ref, o_ref,
                c_scr, qa_scr, ka_scr, vt_scr, sa_scr, sb_scr, mb_scr, *, bq, bk, seq):
    p_idx = pl.program_id(1)

    @pl.when(p_idx == 0)
    def _():
        r_i = lax.broadcasted_iota(jnp.int32, (bk, bk), 0)
        c_i = lax.broadcasted_iota(jnp.int32, (bk, bk), 1)
        upto = jnp.where(c_i <= r_i, 1.0, 0.0).astype(BF16)

        def body(n, run):
            rs = pl.ds(pl.multiple_of(n * bk, bk), bk)
            y = f_ref[rs, :] + fb_ref[...]
            lg = jnp.minimum(y, 0.0) - jnp.log(1.0 + jnp.exp(-jnp.abs(y)))
            hi, mid, lo = _split3(lg)
            c = _dot(upto, hi) + _dot(upto, mid) + _dot(upto, lo) + run
            c_scr[rs, :] = c
            return c[bk - 1:bk, :]

        lax.fori_loop(0, seq // bk, body, jnp.zeros((1, LANES), F32))

    lane = lax.broadcasted_iota(jnp.int32, (bk, LANES), 1)
    lane_lo = lane < HEAD_DIM
    row_lo_k = lax.broadcasted_iota(jnp.int32, (LANES, bk), 0) < HEAD_DIM

    pr = lax.broadcasted_iota(jnp.int32, (LANES, LANES), 0)
    pc = lax.broadcasted_iota(jnp.int32, (LANES, LANES), 1)
    place, ones_q, ones_k = [], [], []
    for h in range(2):
        base = HEAD_DIM * (1 - h)
        place.append([jnp.where(jnp.logical_and(pr == 2 * p_idx + h,
                                                jnp.logical_or(pc == base + n, pc == base + 3 + n)),
                                1.0, 0.0).astype(BF16) for n in range(3)])
        ones_q.append(jnp.logical_and(lane >= base + 3, lane < base + 6))
        ones_k.append(jnp.logical_and(lane >= base, lane < base + 3))

    def prep(n, carry):
        rs = pl.ds(pl.multiple_of(n * bk, bk), bk)
        kn = k_ref[rs, :]
        qn = q_ref[rs, :]
        parts = _split3(c_scr[rs, :] * LOG2E)
        v_t = v_ref[rs, :].T
        for h in range(2):
            c_aug = functools.reduce(lambda x, y: x + y,
                                     [_dot(part, sel) for part, sel in zip(parts, place[h])])
            mine = lane_lo if h == 0 else jnp.logical_not(lane_lo)
            k_aug = jnp.where(ones_k[h], 1.0, -c_aug)
            q_aug = jnp.where(ones_q[h], 1.0, c_aug)
            ka_scr[h, rs, :] = jnp.where(mine, kn, k_aug.astype(BF16))
            qa_scr[h, rs, :] = jnp.where(mine, qn, q_aug.astype(BF16))
            vt_scr[h, n] = jnp.where(row_lo_k if h == 0 else jnp.logical_not(row_lo_k), v_t,
                                     jnp.ones_like(v_t))
        return carry

    lax.fori_loop(0, seq // bk, prep, 0)

    key_minus_qry = (lax.broadcasted_iota(jnp.int32, (bk, bq), 0)
                     - lax.broadcasted_iota(jnp.int32, (bk, bq), 1))
    row_lo = lax.broadcasted_iota(jnp.int32, (LANES, bq), 0) < HEAD_DIM

    nq = seq // bq
    n_items = sum((qi * bq + bq - 1) // bk + 1 for qi in range(nq))
    mb_scr[0] = jnp.zeros((bk, bq), F32)
    for r in range(bk // bq):
        mb_scr[r + 1] = jnp.where(key_minus_qry <= r * bq, 0.0, NEG_BIG)

    def unpack(item):
        qi = jnp.minimum(item[0], nq - 1)
        last = (qi * bq + bq - 1) // bk
        first = item[1] == 0
        kb = jnp.where(first, last, item[1] - 1)
        return qi, last, first, kb

    def advance(item):
        qi, last, _, _ = unpack(item)
        wrap = item[1] >= last
        return jnp.where(wrap, item[0] + 1, item[0]), jnp.where(wrap, 0, item[1] + 1)

    def scores_into(dst, item):
        qi, _, first, kb = unpack(item)
        qs = pl.ds(pl.multiple_of(qi * bq, bq), bq)
        rs = pl.ds(pl.multiple_of(kb * bk, bk), bk)
        bias = mb_scr[jnp.where(first, 1 + (qi * bq - kb * bk) // bq, 0)]
        for h in range(2):
            dst[h] = _nt_dot(ka_scr[h, rs, :], qa_scr[h, qs, :]) + bias

    def write_out(qi, sts):
        acc0, acc1 = sts[0][1], sts[1][1]
        o_t = jnp.where(row_lo, acc0 / acc0[HEAD_DIM:HEAD_DIM + 1, :], acc1 / acc1[0:1, :])
        o_ref[pl.ds(pl.multiple_of(qi * bq, bq), bq), :] = o_t.T.astype(o_ref.dtype)

    def absorb(src, item, prev_qi, sts):
        qi, _, first, kb = unpack(item)
        write_out(prev_qi, sts)
        out = []
        for h in range(2):
            m_run, acc = sts[h]
            m_run = jnp.where(first, NEG_BIG, m_run)
            s = src[h]
            m_new = jnp.maximum(m_run, jnp.max(s, axis=0, keepdims=True))
            p = jnp.exp2(s - m_new)
            acc = jnp.exp2(m_run - m_new) * acc + _dot(vt_scr[h, kb], p.astype(BF16))
            out.append((m_new, acc))
        return qi, tuple(out)

    unroll = max([u for u in (8, 4) if n_items % u == 0 and u * bk <= FOX_KEYS_PER_TRIP] + [2])
    bufs = (sa_scr, sb_scr)

    def run(count, ca):
        held, prev_qi, sts = ca
        for k in range(count):
            nxt = advance(held)
            scores_into(bufs[(k + 1) % 2], nxt)
            prev_qi, sts = absorb(bufs[k % 2], held, prev_qi, sts)
            held = nxt
        return held, prev_qi, sts

    start = (jnp.int32(0), jnp.int32(0))
    init = (jnp.full((1, bq), NEG_BIG, F32), jnp.ones((LANES, bq), F32))
    scores_into(sa_scr, start)
    ca = lax.fori_loop(0, n_items // unroll, lambda u, ca: run(unroll, ca),
                       (start, jnp.int32(0), (init, init)))
    _, last_qi, sts = run(n_items % unroll, ca)
    write_out(last_qi, sts)


def _fox_call(q, k, v, ff, fbias, l, batch, seq, bq, bk):
    m = q.shape[0]
    col = pl.BlockSpec((seq, LANES), lambda b, p: (b, p))
    return pl.pallas_call(
        functools.partial(_fox_kernel, bq=bq, bk=bk, seq=seq),
        grid=(batch, FOX_WIDTH // LANES),
        in_specs=[col, col, col, pl.BlockSpec((seq, LANES), lambda b, p: (b, 0)), _layer(l, 1, LANES)],
        out_specs=pl.BlockSpec((seq, LANES), lambda b, p: (b, p)),
        out_shape=jax.ShapeDtypeStruct((m, FOX_WIDTH), BF16),
        scratch_shapes=[pltpu.VMEM((seq, LANES), F32),
                        pltpu.VMEM((2, seq, LANES), BF16),
                        pltpu.VMEM((2, seq, LANES), BF16),
                        pltpu.VMEM((2, seq // bk, LANES, bk), BF16),
                        pltpu.VMEM((2, bk, bq), F32),
                        pltpu.VMEM((2, bk, bq), F32),
                        pltpu.VMEM((bk // bq + 1, bk, bq), F32)],
        compiler_params=pltpu.CompilerParams(
            dimension_semantics=("arbitrary", "arbitrary"), vmem_limit_bytes=VMEM_LIMIT),
        name="forgetting",
    )(q, k, v, ff, fbias)


def _mlp_kernel(x_ref, ohg_ref, osb_ref, ofx_ref, wo1_ref, wo2_ref, wo3_ref, g2_ref,
                w1_ref, w2_ref, o_ref, *, tf):
    x1 = (x_ref[...] + _dot(ohg_ref[...], wo1_ref[...]) + _dot(osb_ref[...], wo2_ref[...])
          + _dot(ofx_ref[...], wo3_ref[...]))
    ms = jnp.mean(x1 * x1, axis=-1, keepdims=True)
    h2 = (x1 * lax.rsqrt(ms + EPS) * g2_ref[...]).astype(BF16)
    o_ref[...] = x1
    acc = None
    for f in range(w1_ref.shape[1] // tf):
        a = _dot(h2, w1_ref[:, f * tf:(f + 1) * tf])
        a = jnp.square(jnp.maximum(a, 0.0)).astype(BF16)
        y = _dot(a, w2_ref[f * tf:(f + 1) * tf, :])
        acc = y if acc is None else acc + y
    o_ref[...] = o_ref[...] + acc


def _mlp_call(x, ohg, osb, ofx, wo1, wo2, wo3, g2, w1, w2, l, tm, tf):
    m, d = x.shape
    row = lambda w: pl.BlockSpec((tm, w), lambda i: (i, 0))
    whole = lambda a: _layer(l, *a.shape[1:])
    return pl.pallas_call(
        functools.partial(_mlp_kernel, tf=tf),
        grid=(m // tm,),
        in_specs=[row(d), row(HG_WIDTH), row(SB_WIDTH), row(FOX_WIDTH),
                  whole(wo1), whole(wo2), whole(wo3), whole(g2), whole(w1), whole(w2)],
        out_specs=row(d),
        out_shape=jax.ShapeDtypeStruct((m, d), F32),
        compiler_params=pltpu.CompilerParams(
            dimension_semantics=("arbitrary",), vmem_limit_bytes=VMEM_LIMIT),
        name="outproj_mlp",
    )(x, ohg, osb, ofx, wo1, wo2, wo3, g2, w1, w2)


def _tile_sizes(batch, seq):
    m = batch * seq
    return dict(
        proj_tm=min(512, m),
        hg_tb=min(2048, seq),
        sb_bq=min(256, seq),
        sb_sub=max(1, seq // 256),
        fox_bq=min(256, seq),
        fox_bk=min(512, seq),
        mlp_tm=min(512, m),
        mlp_tf=1024,
    )


def kernel(x, lb_logits, norm1_g, w_in, hg_norm_g, sb_q_norm_g, sb_k_norm_g, fox_q_norm_g,
           fox_k_norm_g, fox_f_bias, w_out, norm2_g, w_ff1, w_ff2):
    batch, seq, d = x.shape
    assert d == D_MODEL and seq % HG_ROWS == 0 and x.dtype == F32
    ts = _tile_sizes(batch, seq)
    assert seq % (ts["sb_bq"] * ts["sb_sub"]) == 0 and seq % ts["fox_bk"] == 0 and ts["fox_bk"] % ts["fox_bq"] == 0
    assert seq % ts["hg_tb"] == 0
    m = batch * seq
    xf = x.reshape(m, d)
    row = lambda a: a.astype(F32)[:, None, :]
    pair = lambda g: row(jnp.tile(g, (1, 2)))
    w_in_p = jnp.pad(w_in, ((0, 0), (0, 0), (0, IN_COLS_PAD - IN_COLS))).astype(BF16)
    wo = w_out.astype(BF16)
    wo_hg, wo_sb, wo_fx = wo[:, :HG_WIDTH], wo[:, HG_WIDTH:HG_WIDTH + SB_WIDTH], wo[:, HG_WIDTH + SB_WIDTH:]
    w1, w2 = w_ff1.astype(BF16), w_ff2.astype(BF16)
    g1, g2, g_hg = row(norm1_g), row(norm2_g), row(jnp.tile(hg_norm_g, (1, HG_HEADS)))
    qk_gains = (pair(sb_q_norm_g), pair(sb_k_norm_g), pair(fox_q_norm_g), pair(fox_k_norm_g))
    fbias = row(jnp.pad(fox_f_bias, ((0, 0), (0, LANES - FOX_HEADS))))
    lbl = lb_logits.astype(F32)

    for l in range(DEPTH):
        hg, ff, sq, sk, sv, fq, fk, fv = _proj_call(xf, g1, w_in_p, lbl, qk_gains, l, ts["proj_tm"])
        o_hg = _hgrn_call(l, lbl, hg, g_hg, batch, seq, ts["hg_tb"])
        o_sb = _sb_call(sq, sk, sv, batch, seq, ts["sb_bq"], ts["sb_sub"])
        o_fx = _fox_call(fq, fk, fv, ff, fbias, l, batch, seq, ts["fox_bq"], ts["fox_bk"])
        xf = _mlp_call(xf, o_hg, o_sb, o_fx, wo_hg, wo_sb, wo_fx, g2, w1, w2, l,
                       ts["mlp_tm"], ts["mlp_tf"])
    return xf.reshape(batch, seq, d)
```

```python
import functools
import math

import jax
import jax.numpy as jnp
from jax import lax
from jax.experimental import pallas as pl
from jax.experimental.pallas import tpu as pltpu

F32 = jnp.float32
BF16 = jnp.bfloat16

D_MODEL = 1024
DEPTH = 4
HEAD_DIM = 64
HEAD_SHIFT = HEAD_DIM.bit_length() - 1
HG_HEADS = 4
HG_KW = 256
HG_WIDTH = 256
SB_WIDTH = 384
FOX_WIDTH = 384
FOX_HEADS = 6
IN_COLS = 4 * 256 + 3 * 384 + 3 * 384 + FOX_HEADS
D_FF = 4 * D_MODEL
EPS = 1e-6
LB_FLOOR = 1e-30
NEG_BIG = -1e30
LOG2E = math.log2(math.e)

LANES = 128
IN_COLS_PAD = 27 * LANES
SB_Q_BLK, SB_K_BLK, SB_V_BLK = 8, 11, 14
FOX_Q_BLK, FOX_K_BLK, FOX_V_BLK, FOX_F_BLK = 17, 20, 23, 26

SB_ZERO_LOG = -105.0
SUB = 16
HG_ROWS = 256
FOX_KEYS_PER_TRIP = 2048
VMEM_LIMIT = 56 * 1024 * 1024


def _nt_dot(a, b):
    return lax.dot_general(a, b, (((1,), (1,)), ((), ())), preferred_element_type=F32)


def _dot(a, b):
    return jnp.dot(a, b, preferred_element_type=F32)


def _split2(x):
    hi = x.astype(BF16)
    lo = (x - hi.astype(F32)).astype(BF16)
    return hi, lo


def _split3(x):
    hi = x.astype(BF16)
    r1 = x - hi.astype(F32)
    mid = r1.astype(BF16)
    lo = (r1 - mid.astype(F32)).astype(BF16)
    return hi, mid, lo


def _softplus(z):
    return jnp.maximum(z, 0.0) + jnp.log(1.0 + jnp.exp(-jnp.abs(z)))


def _softplus2(z2):
    return jnp.maximum(z2, 0.0) + jnp.log2(1.0 + jnp.exp2(-jnp.abs(z2)))


def _pair_rms(x, gain, lane_lo):
    x2 = x * x
    s0 = jnp.sum(jnp.where(lane_lo, x2, 0.0), axis=1, keepdims=True)
    s1 = jnp.sum(jnp.where(lane_lo, 0.0, x2), axis=1, keepdims=True)
    ms = jnp.where(lane_lo, s0, s1) * (1.0 / HEAD_DIM)
    return x * lax.rsqrt(ms + EPS) * gain


def _layer(l, *shape):
    return pl.BlockSpec((None,) + shape, lambda *_: (l,) + (0,) * len(shape))


def _lower_bound(lbl_ref, layer):
    rows = [lbl_ref[j:j + 1, :] for j in range(DEPTH)]
    mx = functools.reduce(jnp.maximum, rows)
    ex = [jnp.exp(r - mx) for r in rows]
    den = functools.reduce(lambda a, b: a + b, ex)
    lb = jnp.zeros_like(mx)
    for j in range(1, layer + 1):
        lb = lb + ex[j] / den
    return lb


def _proj_kernel(x_ref, g_ref, w_ref, lbl_ref, gsq_ref, gsk_ref, gfq_ref, gfk_ref,
                 hg_ref, ff_ref, sq_ref, sk_ref, sv_ref, fq_ref, fk_ref, fv_ref, *, layer):
    x = x_ref[...]
    ms = jnp.mean(x * x, axis=-1, keepdims=True)
    h = (x * lax.rsqrt(ms + EPS) * g_ref[...]).astype(BF16)
    y = _dot(h, w_ref[...])
    fl = y[:, HG_KW:2 * HG_KW]
    gt = y[:, 3 * HG_KW:4 * HG_KW]
    log_lb = jnp.log(jnp.maximum(_lower_bound(lbl_ref, layer), LB_FLOOR))
    sp = _softplus(fl)
    hg_ref[:, :HG_KW] = y[:, :HG_KW]
    hg_ref[:, HG_KW:2 * HG_KW] = sp
    hg_ref[:, 2 * HG_KW:3 * HG_KW] = y[:, 2 * HG_KW:3 * HG_KW]
    hg_ref[:, 3 * HG_KW:4 * HG_KW] = gt / (1.0 + jnp.exp(-gt))
    hg_ref[:, 4 * HG_KW:] = (jnp.maximum(fl, log_lb) + jnp.log(1.0 + jnp.exp(-jnp.abs(fl - log_lb)))
                             - sp)
    ff_ref[...] = y[:, FOX_F_BLK * LANES:]
    lane_lo = lax.broadcasted_iota(jnp.int32, (x.shape[0], LANES), 1) < HEAD_DIM
    q_mult = LOG2E / math.sqrt(HEAD_DIM)
    groups = ((SB_Q_BLK, gsq_ref, q_mult, sq_ref), (SB_K_BLK, gsk_ref, None, sk_ref),
              (SB_V_BLK, None, None, sv_ref),
              (FOX_Q_BLK, gfq_ref, q_mult, fq_ref), (FOX_K_BLK, gfk_ref, None, fk_ref),
              (FOX_V_BLK, None, None, fv_ref))
    for blk, gain_ref, mult, out_ref in groups:
        for p in range(SB_WIDTH // LANES):
            t = y[:, (blk + p) * LANES:(blk + p + 1) * LANES]
            if gain_ref is not None:
                t = _pair_rms(t, gain_ref[...], lane_lo)
            if mult is not None:
                t = t * mult
            out_ref[:, p * LANES:(p + 1) * LANES] = t.astype(BF16)


def _proj_call(x, g, w, lbl, gains, l, tm):
    m, d = x.shape
    n = w.shape[2]
    rows = lambda width: pl.BlockSpec((tm, width), lambda i: (i, 0))
    widths = (5 * HG_KW, LANES) + (SB_WIDTH,) * 6
    dtypes = (F32, F32) + (BF16,) * 6
    return pl.pallas_call(
        functools.partial(_proj_kernel, layer=l),
        grid=(m // tm,),
        in_specs=([rows(d), _layer(l, 1, d), _layer(l, d, n),
                   pl.BlockSpec((DEPTH, HG_KW), lambda i: (0, 0))] + [_layer(l, 1, LANES)] * 4),
        out_specs=[rows(wd) for wd in widths],
        out_shape=[jax.ShapeDtypeStruct((m, wd), dt) for wd, dt in zip(widths, dtypes)],
        compiler_params=pltpu.CompilerParams(
            dimension_semantics=("arbitrary",), vmem_limit_bytes=VMEM_LIMIT),
        name="proj",
    )(x, g, w, lbl, *gains)


def _hgrn_kernel(lbl_ref, q_ref, sp_ref, i_ref, gate_ref, lf_ref, gn_ref, o_ref, st_scr, oi_scr,
                 *, layer, tb):
    @pl.when(pl.program_id(1) == 0)
    def _():
        st_scr[...] = jnp.zeros_like(st_scr)

    one_m_lb = 1.0 - _lower_bound(lbl_ref, layer)
    gn = gn_ref[...]

    r_i = lax.broadcasted_iota(jnp.int32, (HG_KW, HG_KW), 0)
    c_i = lax.broadcasted_iota(jnp.int32, (HG_KW, HG_KW), 1)
    same_head = (r_i >> HEAD_SHIFT) == (c_i >> HEAD_SHIFT)
    ones_bd = jnp.where(same_head, 1.0, 0.0).astype(BF16)
    t_mod = lax.broadcasted_iota(jnp.int32, (HG_ROWS, HG_KW), 0) & (SUB - 1)
    lane_sub = lax.broadcasted_iota(jnp.int32, (HG_KW, HG_ROWS), 1) // SUB
    t_r = lax.broadcasted_iota(jnp.int32, (HG_ROWS, HG_ROWS), 0)
    t_c = lax.broadcasted_iota(jnp.int32, (HG_ROWS, HG_ROWS), 1)
    in_sub = (t_r // SUB) == (t_c // SUB)
    same_sub = jnp.where(in_sub, 1.0, 0.0).astype(BF16)
    upto_sub = jnp.where(jnp.logical_and(in_sub, t_c <= t_r), 1.0, 0.0).astype(BF16)

    def chunk(c, carry):
        rs = pl.ds(pl.multiple_of(c * HG_ROWS, HG_ROWS), HG_ROWS)
        q = q_ref[rs, :]
        sp = sp_ref[rs, :]
        v = i_ref[rs, :]
        lf = lf_ref[rs, :]
        kk = one_m_lb * jnp.exp(-sp)

        parts = _split3(lf)
        b = functools.reduce(lambda x, y: x + y, [_dot(upto_sub, part) for part in parts])
        beta = functools.reduce(lambda x, y: x + y, [_dot(same_sub, part) for part in parts])
        rest = beta - b

        b2 = b * LOG2E
        w2 = -sp * LOG2E - b2
        q_lb = q * one_m_lb
        acc = _dot((q * kk).astype(BF16), ones_bd) * v
        for d in range(1, SUB):
            e = jnp.where(t_mod >= d, b2 + pltpu.roll(w2, d, 0), NEG_BIG)
            p = q_lb * jnp.exp2(e)
            acc = acc + _dot(p.astype(BF16), ones_bd) * pltpu.roll(v, d, 0)

        qt = (q * jnp.exp2(b2)).astype(BF16)
        kt = (kk * jnp.exp(rest)).astype(BF16)
        v_t = v.T
        for j in range(HG_ROWS // SUB):
            st = st_scr[...]
            oi_scr[j * SUB:(j + 1) * SUB, :] = _nt_dot(qt[j * SUB:(j + 1) * SUB, :], st.astype(BF16))
            vm = jnp.where(lane_sub == j, v_t, 0.0).astype(BF16)
            ut = _dot(vm, kt)
            dec = jnp.exp(beta[j * SUB:j * SUB + 1, :])
            st_scr[...] = st * dec + jnp.where(same_head, ut, 0.0)

        o = acc + oi_scr[...]
        hi, lo = _split2(o * o)
        msq = (_dot(hi, ones_bd) + _dot(lo, ones_bd)) * (1.0 / HEAD_DIM)
        y = o * lax.rsqrt(msq + EPS) * gn
        o_ref[rs, :] = (y * gate_ref[rs, :]).astype(o_ref.dtype)
        return carry

    lax.fori_loop(0, tb // HG_ROWS, chunk, 0)


def _hgrn_call(layer, lb_logits, proj, gn, batch, seq, tb):
    m = proj.shape[0]
    nt = seq // tb

    def col(j):
        return pl.BlockSpec((tb, HG_KW), lambda b, t, j=j: (b * nt + t, j))

    return pl.pallas_call(
        functools.partial(_hgrn_kernel, layer=layer, tb=tb),
        grid=(batch, nt),
        in_specs=[pl.BlockSpec((DEPTH, HG_KW), lambda b, t: (0, 0)),
                  col(0), col(1), col(2), col(3), col(4), _layer(layer, 1, HG_WIDTH)],
        out_specs=pl.BlockSpec((tb, HG_WIDTH), lambda b, t: (b * nt + t, 0)),
        out_shape=jax.ShapeDtypeStruct((m, HG_WIDTH), BF16),
        scratch_shapes=[pltpu.VMEM((HG_WIDTH, HG_KW), F32),
                        pltpu.VMEM((HG_ROWS, HG_WIDTH), F32)],
        compiler_params=pltpu.CompilerParams(
            dimension_semantics=("arbitrary", "arbitrary"), vmem_limit_bytes=VMEM_LIMIT),
        name="hgrn2",
    )(lb_logits, proj, proj, proj, proj, proj, gn)


def _sb_kernel(q0_ref, q1_ref, q2_ref, k0_ref, k1_ref, k2_ref, v0_ref, v1_ref, v2_ref,
               o_ref, acc_scr, car_scr, *, bq, n_sub):
    g = pl.program_id(1)
    q_refs = (q0_ref, q1_ref, q2_ref)
    k_refs = (k0_ref, k1_ref, k2_ref)
    v_refs = (v0_ref, v1_ref, v2_ref)
    n_pairs = len(q_refs)

    lane_lo = lax.broadcasted_iota(jnp.int32, (bq, LANES), 1) < HEAD_DIM
    r_i = lax.broadcasted_iota(jnp.int32, (bq, bq), 0)
    c_i = lax.broadcasted_iota(jnp.int32, (bq, bq), 1)
    before = c_i < r_i
    later = jnp.where(r_i > c_i, 1.0, 0.0).astype(BF16)

    heads = range(2 * n_pairs)

    def q_block(sub, carry_unused):
        i = g * n_sub + sub
        qs = pl.ds(pl.multiple_of(sub * bq, bq), bq)
        qms = []
        for p in range(n_pairs):
            qn = q_refs[p][qs, :]
            zero = jnp.zeros_like(qn)
            qms.append(jnp.where(lane_lo, qn, zero))
            qms.append(jnp.where(lane_lo, zero, qn))

        def absorb(tiles, fresh):
            rows = [pl.ds(pl.multiple_of(kb * bq, bq), bq) for kb, _ in tiles]
            masks = [mask for _, mask in tiles]
            nt = range(len(tiles))
            zs = [[_nt_dot(qms[h], k_refs[h // 2][rows[t], :]) for h in heads] for t in nt]
            sps = [[_softplus2(z) for z in zt] for zt in zs]
            us = [spt if masks[t] is None else [jnp.where(masks[t], sp, 0.0) for sp in spt]
                  for t, spt in enumerate(sps)]
            tails = [[_dot(u.astype(BF16), later) for u in ut] for ut in us]
            low = None
            for h in heads:
                carry = None if fresh else car_scr[h]
                acc = None if fresh else acc_scr[h]
                for t in nt:
                    row = jnp.broadcast_to(jnp.sum(us[t][h], axis=1, keepdims=True), (bq, LANES))
                    e = zs[t][h] - sps[t][h] - tails[t][h]
                    if carry is not None:
                        e = e - jnp.concatenate([carry] * (bq // LANES), axis=1)
                        row = row + carry
                    a = jnp.exp2(e)
                    if masks[t] is not None:
                        a = jnp.where(masks[t], a, 0.0)
                    pv = _dot(a.astype(BF16), v_refs[h // 2][rows[t], :])
                    acc = pv if acc is None else acc + pv
                    carry = row
                acc_scr[h] = acc
                car_scr[h] = carry
                low = carry if low is None else jnp.minimum(low, carry)
            return jnp.min(low)

        low = lax.cond(i > 0,
                       lambda: absorb([(i, before), (i - 1, None)], True),
                       lambda: absorb([(i, before)], True))

        def live(st):
            return jnp.logical_and(st[0] >= 0, st[1] < -SB_ZERO_LOG * LOG2E)

        lax.while_loop(live, lambda st: (st[0] - 1, absorb([(st[0], None)], False)), (i - 2, low))
        for p in range(n_pairs):
            o_ref[qs, p * LANES:(p + 1) * LANES] = jnp.where(
                lane_lo, acc_scr[2 * p], acc_scr[2 * p + 1]).astype(o_ref.dtype)
        return carry_unused

    lax.fori_loop(0, n_sub, q_block, 0)


def _sb_call(q, k, v, batch, seq, bq, n_sub):
    m = q.shape[0]
    rows = bq * n_sub
    nq = seq // rows
    n_pairs = SB_WIDTH // LANES
    q_spec = lambda p: pl.BlockSpec((rows, LANES), lambda b, i, p=p: (b * nq + i, p))
    k_spec = lambda p: pl.BlockSpec((seq, LANES), lambda b, i, p=p: (b, p))
    pairs = range(n_pairs)
    return pl.pallas_call(
        functools.partial(_sb_kernel, bq=bq, n_sub=n_sub),
        grid=(batch, nq),
        in_specs=[q_spec(p) for p in pairs] + [k_spec(p) for p in pairs] * 2,
        out_specs=pl.BlockSpec((rows, SB_WIDTH), lambda b, i: (b * nq + i, 0)),
        out_shape=jax.ShapeDtypeStruct((m, SB_WIDTH), BF16),
        scratch_shapes=[pltpu.VMEM((2 * n_pairs, bq, LANES), F32),
                        pltpu.VMEM((2 * n_pairs, bq, LANES), F32)],
        compiler_params=pltpu.CompilerParams(
            dimension_semantics=("arbitrary", "arbitrary"), vmem_limit_bytes=VMEM_LIMIT),
        name="stickbreak",
    )(*([q] * n_pairs + [k] * n_pairs + [v] * n_pairs))


def _fox_kernel(q_ref, k_ref, v_ref, f_ref, fb_ref, o_ref,
                c_scr, qa_scr, ka_scr, vt_scr, sa_scr, sb_scr, mb_scr, *, bq, bk, seq):
    p_idx = pl.program_id(1)

    @pl.when(p_idx == 0)
    def _():
        r_i = lax.broadcasted_iota(jnp.int32, (bk, bk), 0)
        c_i = lax.broadcasted_iota(jnp.int32, (bk, bk), 1)
        upto = jnp.where(c_i <= r_i, 1.0, 0.0).astype(BF16)

        def body(n, run):
            rs = pl.ds(pl.multiple_of(n * bk, bk), bk)
            y = f_ref[rs, :] + fb_ref[...]
            lg = jnp.minimum(y, 0.0) - jnp.log(1.0 + jnp.exp(-jnp.abs(y)))
            hi, mid, lo = _split3(lg)
            c = _dot(upto, hi) + _dot(upto, mid) + _dot(upto, lo) + run
            c_scr[rs, :] = c
            return c[bk - 1:bk, :]

        lax.fori_loop(0, seq // bk, body, jnp.zeros((1, LANES), F32))

    lane = lax.broadcasted_iota(jnp.int32, (bk, LANES), 1)
    lane_lo = lane < HEAD_DIM
    row_lo_k = lax.broadcasted_iota(jnp.int32, (LANES, bk), 0) < HEAD_DIM

    pr = lax.broadcasted_iota(jnp.int32, (LANES, LANES), 0)
    pc = lax.broadcasted_iota(jnp.int32, (LANES, LANES), 1)
    place, ones_q, ones_k = [], [], []
    for h in range(2):
        base = HEAD_DIM * (1 - h)
        place.append([jnp.where(jnp.logical_and(pr == 2 * p_idx + h,
                                                jnp.logical_or(pc == base + n, pc == base + 3 + n)),
                                1.0, 0.0).astype(BF16) for n in range(3)])
        ones_q.append(jnp.logical_and(lane >= base + 3, lane < base + 6))
        ones_k.append(jnp.logical_and(lane >= base, lane < base + 3))

    def prep(n, carry):
        rs = pl.ds(pl.multiple_of(n * bk, bk), bk)
        kn = k_ref[rs, :]
        qn = q_ref[rs, :]
        parts = _split3(c_scr[rs, :] * LOG2E)
        v_t = v_ref[rs, :].T
        for h in range(2):
            c_aug = functools.reduce(lambda x, y: x + y,
                                     [_dot(part, sel) for part, sel in zip(parts, place[h])])
            mine = lane_lo if h == 0 else jnp.logical_not(lane_lo)
            k_aug = jnp.where(ones_k[h], 1.0, -c_aug)
            q_aug = jnp.where(ones_q[h], 1.0, c_aug)
            ka_scr[h, rs, :] = jnp.where(mine, kn, k_aug.astype(BF16))
            qa_scr[h, rs, :] = jnp.where(mine, qn, q_aug.astype(BF16))
            vt_scr[h, n] = jnp.where(row_lo_k if h == 0 else jnp.logical_not(row_lo_k), v_t,
                                     jnp.ones_like(v_t))
        return carry

    lax.fori_loop(0, seq // bk, prep, 0)

    key_minus_qry = (lax.broadcasted_iota(jnp.int32, (bk, bq), 0)
                     - lax.broadcasted_iota(jnp.int32, (bk, bq), 1))
    row_lo = lax.broadcasted_iota(jnp.int32, (LANES, bq), 0) < HEAD_DIM

    nq = seq // bq
    n_items = sum((qi * bq + bq - 1) // bk + 1 for qi in range(nq))
    mb_scr[0] = jnp.zeros((bk, bq), F32)
    for r in range(bk // bq):
        mb_scr[r + 1] = jnp.where(key_minus_qry <= r * bq, 0.0, NEG_BIG)

    def unpack(item):
        qi = jnp.minimum(item[0], nq - 1)
        last = (qi * bq + bq - 1) // bk
        first = item[1] == 0
        kb = jnp.where(first, last, item[1] - 1)
        return qi, last, first, kb

    def advance(item):
        qi, last, _, _ = unpack(item)
        wrap = item[1] >= last
        return jnp.where(wrap, item[0] + 1, item[0]), jnp.where(wrap, 0, item[1] + 1)

    def scores_into(dst, item):
        qi, _, first, kb = unpack(item)
        qs = pl.ds(pl.multiple_of(qi * bq, bq), bq)
        rs = pl.ds(pl.multiple_of(kb * bk, bk), bk)
        bias = mb_scr[jnp.where(first, 1 + (qi * bq - kb * bk) // bq, 0)]
        for h in range(2):
            dst[h] = _nt_dot(ka_scr[h, rs, :], qa_scr[h, qs, :]) + bias

    def write_out(qi, sts):
        acc0, acc1 = sts[0][1], sts[1][1]
        o_t = jnp.where(row_lo, acc0 / acc0[HEAD_DIM:HEAD_DIM + 1, :], acc1 / acc1[0:1, :])
        o_ref[pl.ds(pl.multiple_of(qi * bq, bq), bq), :] = o_t.T.astype(o_ref.dtype)

    def absorb(src, item, prev_qi, sts):
        qi, _, first, kb = unpack(item)
        write_out(prev_qi, sts)
        out = []
        for h in range(2):
            m_run, acc = sts[h]
            m_run = jnp.where(first, NEG_BIG, m_run)
            s = src[h]
            m_new = jnp.maximum(m_run, jnp.max(s, axis=0, keepdims=True))
            p = jnp.exp2(s - m_new)
            acc = jnp.exp2(m_run - m_new) * acc + _dot(vt_scr[h, kb], p.astype(BF16))
            out.append((m_new, acc))
        return qi, tuple(out)

    unroll = max([u for u in (8, 4) if n_items % u == 0 and u * bk <= FOX_KEYS_PER_TRIP] + [2])
    bufs = (sa_scr, sb_scr)

    def run(count, ca):
        held, prev_qi, sts = ca
        for k in range(count):
            nxt = advance(held)
            scores_into(bufs[(k + 1) % 2], nxt)
            prev_qi, sts = absorb(bufs[k % 2], held, prev_qi, sts)
            held = nxt
        return held, prev_qi, sts

    start = (jnp.int32(0), jnp.int32(0))
    init = (jnp.full((1, bq), NEG_BIG, F32), jnp.ones((LANES, bq), F32))
    scores_into(sa_scr, start)
    ca = lax.fori_loop(0, n_items // unroll, lambda u, ca: run(unroll, ca),
                       (start, jnp.int32(0), (init, init)))
    _, last_qi, sts = run(n_items % unroll, ca)
    write_out(last_qi, sts)


def _fox_call(q, k, v, ff, fbias, l, batch, seq, bq, bk):
    m = q.shape[0]
    col = pl.BlockSpec((seq, LANES), lambda b, p: (b, p))
    return pl.pallas_call(
        functools.partial(_fox_kernel, bq=bq, bk=bk, seq=seq),
        grid=(batch, FOX_WIDTH // LANES),
        in_specs=[col, col, col, pl.BlockSpec((seq, LANES), lambda b, p: (b, 0)), _layer(l, 1, LANES)],
        out_specs=pl.BlockSpec((seq, LANES), lambda b, p: (b, p)),
        out_shape=jax.ShapeDtypeStruct((m, FOX_WIDTH), BF16),
        scratch_shapes=[pltpu.VMEM((seq, LANES), F32),
                        pltpu.VMEM((2, seq, LANES), BF16),
                        pltpu.VMEM((2, seq, LANES), BF16),
                        pltpu.VMEM((2, seq // bk, LANES, bk), BF16),
                        pltpu.VMEM((2, bk, bq), F32),
                        pltpu.VMEM((2, bk, bq), F32),
                        pltpu.VMEM((bk // bq + 1, bk, bq), F32)],
        compiler_params=pltpu.CompilerParams(
            dimension_semantics=("arbitrary", "arbitrary"), vmem_limit_bytes=VMEM_LIMIT),
        name="forgetting",
    )(q, k, v, ff, fbias)


def _mlp_kernel(x_ref, ohg_ref, osb_ref, ofx_ref, wo1_ref, wo2_ref, wo3_ref, g2_ref,
                w1_ref, w2_ref, o_ref, *, tf):
    x1 = (x_ref[...] + _dot(ohg_ref[...], wo1_ref[...]) + _dot(osb_ref[...], wo2_ref[...])
          + _dot(ofx_ref[...], wo3_ref[...]))
    ms = jnp.mean(x1 * x1, axis=-1, keepdims=True)
    h2 = (x1 * lax.rsqrt(ms + EPS) * g2_ref[...]).astype(BF16)
    o_ref[...] = x1
    acc = None
    for f in range(w1_ref.shape[1] // tf):
        a = _dot(h2, w1_ref[:, f * tf:(f + 1) * tf])
        a = jnp.square(jnp.maximum(a, 0.0)).astype(BF16)
        y = _dot(a, w2_ref[f * tf:(f + 1) * tf, :])
        acc = y if acc is None else acc + y
    o_ref[...] = o_ref[...] + acc


def _mlp_call(x, ohg, osb, ofx, wo1, wo2, wo3, g2, w1, w2, l, tm, tf):
    m, d = x.shape
    row = lambda w: pl.BlockSpec((tm, w), lambda i: (i, 0))
    whole = lambda a: _layer(l, *a.shape[1:])
    return pl.pallas_call(
        functools.partial(_mlp_kernel, tf=tf),
        grid=(m // tm,),
        in_specs=[row(d), row(HG_WIDTH), row(SB_WIDTH), row(FOX_WIDTH),
                  whole(wo1), whole(wo2), whole(wo3), whole(g2), whole(w1), whole(w2)],
        out_specs=row(d),
        out_shape=jax.ShapeDtypeStruct((m, d), F32),
        compiler_params=pltpu.CompilerParams(
            dimension_semantics=("arbitrary",), vmem_limit_bytes=VMEM_LIMIT),
        name="outproj_mlp",
    )(x, ohg, osb, ofx, wo1, wo2, wo3, g2, w1, w2)


def _tile_sizes(batch, seq):
    m = batch * seq
    return dict(
        proj_tm=min(512, m),
        hg_tb=min(2048, seq),
        sb_bq=min(256, seq),
        sb_sub=max(1, seq // 256),
        fox_bq=min(256, seq),
        fox_bk=min(512, seq),
        mlp_tm=min(512, m),
        mlp_tf=1024,
    )


def kernel(x, lb_logits, norm1_g, w_in, hg_norm_g, sb_q_norm_g, sb_k_norm_g, fox_q_norm_g,
           fox_k_norm_g, fox_f_bias, w_out, norm2_g, w_ff1, w_ff2):
    batch, seq, d = x.shape
    assert d == D_MODEL and seq % HG_ROWS == 0 and x.dtype == F32
    ts = _tile_sizes(batch, seq)
    assert seq % (ts["sb_bq"] * ts["sb_sub"]) == 0 and seq % ts["fox_bk"] == 0 and ts["fox_bk"] % ts["fox_bq"] == 0
    assert seq % ts["hg_tb"] == 0
    m = batch * seq
    xf = x.reshape(m, d)
    row = lambda a: a.astype(F32)[:, None, :]
    pair = lambda g: row(jnp.tile(g, (1, 2)))
    w_in_p = jnp.pad(w_in, ((0, 0), (0, 0), (0, IN_COLS_PAD - IN_COLS))).astype(BF16)
    wo = w_out.astype(BF16)
    wo_hg, wo_sb, wo_fx = wo[:, :HG_WIDTH], wo[:, HG_WIDTH:HG_WIDTH + SB_WIDTH], wo[:, HG_WIDTH + SB_WIDTH:]
    w1, w2 = w_ff1.astype(BF16), w_ff2.astype(BF16)
    g1, g2, g_hg = row(norm1_g), row(norm2_g), row(jnp.tile(hg_norm_g, (1, HG_HEADS)))
    qk_gains = (pair(sb_q_norm_g), pair(sb_k_norm_g), pair(fox_q_norm_g), pair(fox_k_norm_g))
    fbias = row(jnp.pad(fox_f_bias, ((0, 0), (0, LANES - FOX_HEADS))))
    lbl = lb_logits.astype(F32)

    for l in range(DEPTH):
        hg, ff, sq, sk, sv, fq, fk, fv = _proj_call(xf, g1, w_in_p, lbl, qk_gains, l, ts["proj_tm"])
        o_hg = _hgrn_call(l, lbl, hg, g_hg, batch, seq, ts["hg_tb"])
        o_sb = _sb_call(sq, sk, sv, batch, seq, ts["sb_bq"], ts["sb_sub"])
        o_fx = _fox_call(fq, fk, fv, ff, fbias, l, batch, seq, ts["fox_bq"], ts["fox_bk"])
        xf = _mlp_call(xf, o_hg, o_sb, o_fx, wo_hg, wo_sb, wo_fx, g2, w1, w2, l,
                       ts["mlp_tm"], ts["mlp_tf"])
    return xf.reshape(batch, seq, d)
```

```python
import functools
import math

import jax
import jax.numpy as jnp
from jax import lax
from jax.experimental import pallas as pl
from jax.experimental.pallas import tpu as pltpu

F32 = jnp.float32
BF16 = jnp.bfloat16

D_MODEL = 1024
DEPTH = 4
HEAD_DIM = 64
HEAD_SHIFT = HEAD_DIM.bit_length() - 1
HG_HEADS = 4
HG_KW = 256
HG_WIDTH = 256
SB_WIDTH = 384
FOX_WIDTH = 384
FOX_HEADS = 6
IN_COLS = 4 * 256 + 3 * 384 + 3 * 384 + FOX_HEADS
D_FF = 4 * D_MODEL
EPS = 1e-6
LB_FLOOR = 1e-30
NEG_BIG = -1e30
LOG2E = math.log2(math.e)

LANES = 128
IN_COLS_PAD = 27 * LANES
SB_Q_BLK, SB_K_BLK, SB_V_BLK = 8, 11, 14
FOX_Q_BLK, FOX_K_BLK, FOX_V_BLK, FOX_F_BLK = 17, 20, 23, 26

SB_ZERO_LOG = -105.0
SUB = 16
HG_ROWS = 256
FOX_KEYS_PER_TRIP = 2048
VMEM_LIMIT = 56 * 1024 * 1024


def _nt_dot(a, b):
    return lax.dot_general(a, b, (((1,), (1,)), ((), ())), preferred_element_type=F32)


def _dot(a, b):
    return jnp.dot(a, b, preferred_element_type=F32)


def _split2(x):
    hi = x.astype(BF16)
    lo = (x - hi.astype(F32)).astype(BF16)
    return hi, lo


def _split3(x):
    hi = x.astype(BF16)
    r1 = x - hi.astype(F32)
    mid = r1.astype(BF16)
    lo = (r1 - mid.astype(F32)).astype(BF16)
    return hi, mid, lo


def _softplus(z):
    return jnp.maximum(z, 0.0) + jnp.log(1.0 + jnp.exp(-jnp.abs(z)))


def _softplus2(z2):
    return jnp.maximum(z2, 0.0) + jnp.log2(1.0 + jnp.exp2(-jnp.abs(z2)))


def _pair_rms(x, gain, lane_lo):
    x2 = x * x
    s0 = jnp.sum(jnp.where(lane_lo, x2, 0.0), axis=1, keepdims=True)
    s1 = jnp.sum(jnp.where(lane_lo, 0.0, x2), axis=1, keepdims=True)
    ms = jnp.where(lane_lo, s0, s1) * (1.0 / HEAD_DIM)
    return x * lax.rsqrt(ms + EPS) * gain


def _layer(l, *shape):
    return pl.BlockSpec((None,) + shape, lambda *_: (l,) + (0,) * len(shape))


def _lower_bound(lbl_ref, layer):
    rows = [lbl_ref[j:j + 1, :] for j in range(DEPTH)]
    mx = functools.reduce(jnp.maximum, rows)
    ex = [jnp.exp(r - mx) for r in rows]
    den = functools.reduce(lambda a, b: a + b, ex)
    lb = jnp.zeros_like(mx)
    for j in range(1, layer + 1):
        lb = lb + ex[j] / den
    return lb


def _proj_kernel(x_ref, g_ref, w_ref, lbl_ref, gsq_ref, gsk_ref, gfq_ref, gfk_ref,
                 hg_ref, ff_ref, sq_ref, sk_ref, sv_ref, fq_ref, fk_ref, fv_ref, *, layer):
    x = x_ref[...]
    ms = jnp.mean(x * x, axis=-1, keepdims=True)
    h = (x * lax.rsqrt(ms + EPS) * g_ref[...]).astype(BF16)
    y = _dot(h, w_ref[...])
    fl = y[:, HG_KW:2 * HG_KW]
    gt = y[:, 3 * HG_KW:4 * HG_KW]
    log_lb = jnp.log(jnp.maximum(_lower_bound(lbl_ref, layer), LB_FLOOR))
    sp = _softplus(fl)
    hg_ref[:, :HG_KW] = y[:, :HG_KW]
    hg_ref[:, HG_KW:2 * HG_KW] = sp
    hg_ref[:, 2 * HG_KW:3 * HG_KW] = y[:, 2 * HG_KW:3 * HG_KW]
    hg_ref[:, 3 * HG_KW:4 * HG_KW] = gt / (1.0 + jnp.exp(-gt))
    hg_ref[:, 4 * HG_KW:] = (jnp.maximum(fl, log_lb) + jnp.log(1.0 + jnp.exp(-jnp.abs(fl - log_lb)))
                             - sp)
    ff_ref[...] = y[:, FOX_F_BLK * LANES:]
    lane_lo = lax.broadcasted_iota(jnp.int32, (x.shape[0], LANES), 1) < HEAD_DIM
    q_mult = LOG2E / math.sqrt(HEAD_DIM)
    groups = ((SB_Q_BLK, gsq_ref, q_mult, sq_ref), (SB_K_BLK, gsk_ref, None, sk_ref),
              (SB_V_BLK, None, None, sv_ref),
              (FOX_Q_BLK, gfq_ref, q_mult, fq_ref), (FOX_K_BLK, gfk_ref, None, fk_ref),
              (FOX_V_BLK, None, None, fv_ref))
    for blk, gain_ref, mult, out_ref in groups:
        for p in range(SB_WIDTH // LANES):
            t = y[:, (blk + p) * LANES:(blk + p + 1) * LANES]
            if gain_ref is not None:
                t = _pair_rms(t, gain_ref[...], lane_lo)
            if mult is not None:
                t = t * mult
            out_ref[:, p * LANES:(p + 1) * LANES] = t.astype(BF16)


def _proj_call(x, g, w, lbl, gains, l, tm):
    m, d = x.shape
    n = w.shape[2]
    rows = lambda width: pl.BlockSpec((tm, width), lambda i: (i, 0))
    widths = (5 * HG_KW, LANES) + (SB_WIDTH,) * 6
    dtypes = (F32, F32) + (BF16,) * 6
    return pl.pallas_call(
        functools.partial(_proj_kernel, layer=l),
        grid=(m // tm,),
        in_specs=([rows(d), _layer(l, 1, d), _layer(l, d, n),
                   pl.BlockSpec((DEPTH, HG_KW), lambda i: (0, 0))] + [_layer(l, 1, LANES)] * 4),
        out_specs=[rows(wd) for wd in widths],
        out_shape=[jax.ShapeDtypeStruct((m, wd), dt) for wd, dt in zip(widths, dtypes)],
        compiler_params=pltpu.CompilerParams(
            dimension_semantics=("arbitrary",), vmem_limit_bytes=VMEM_LIMIT),
        name="proj",
    )(x, g, w, lbl, *gains)


def _hgrn_kernel(lbl_ref, q_ref, sp_ref, i_ref, gate_ref, lf_ref, gn_ref, o_ref, st_scr, oi_scr,
                 *, layer, tb):
    @pl.when(pl.program_id(1) == 0)
    def _():
        st_scr[...] = jnp.zeros_like(st_scr)

    one_m_lb = 1.0 - _lower_bound(lbl_ref, layer)
    gn = gn_ref[...]

    r_i = lax.broadcasted_iota(jnp.int32, (HG_KW, HG_KW), 0)
    c_i = lax.broadcasted_iota(jnp.int32, (HG_KW, HG_KW), 1)
    same_head = (r_i >> HEAD_SHIFT) == (c_i >> HEAD_SHIFT)
    ones_bd = jnp.where(same_head, 1.0, 0.0).astype(BF16)
    t_mod = lax.broadcasted_iota(jnp.int32, (HG_ROWS, HG_KW), 0) & (SUB - 1)
    lane_sub = lax.broadcasted_iota(jnp.int32, (HG_KW, HG_ROWS), 1) // SUB
    t_r = lax.broadcasted_iota(jnp.int32, (HG_ROWS, HG_ROWS), 0)
    t_c = lax.broadcasted_iota(jnp.int32, (HG_ROWS, HG_ROWS), 1)
    in_sub = (t_r // SUB) == (t_c // SUB)
    same_sub = jnp.where(in_sub, 1.0, 0.0).astype(BF16)
    upto_sub = jnp.where(jnp.logical_and(in_sub, t_c <= t_r), 1.0, 0.0).astype(BF16)

    def chunk(c, carry):
        rs = pl.ds(pl.multiple_of(c * HG_ROWS, HG_ROWS), HG_ROWS)
        q = q_ref[rs, :]
        sp = sp_ref[rs, :]
        v = i_ref[rs, :]
        lf = lf_ref[rs, :]
        kk = one_m_lb * jnp.exp(-sp)

        parts = _split3(lf)
        b = functools.reduce(lambda x, y: x + y, [_dot(upto_sub, part) for part in parts])
        beta = functools.reduce(lambda x, y: x + y, [_dot(same_sub, part) for part in parts])
        rest = beta - b

        b2 = b * LOG2E
        w2 = -sp * LOG2E - b2
        q_lb = q * one_m_lb
        acc = _dot((q * kk).astype(BF16), ones_bd) * v
        for d in range(1, SUB):
            e = jnp.where(t_mod >= d, b2 + pltpu.roll(w2, d, 0), NEG_BIG)
            p = q_lb * jnp.exp2(e)
            acc = acc + _dot(p.astype(BF16), ones_bd) * pltpu.roll(v, d, 0)

        qt = (q * jnp.exp2(b2)).astype(BF16)
        kt = (kk * jnp.exp(rest)).astype(BF16)
        v_t = v.T
        for j in range(HG_ROWS // SUB):
            st = st_scr[...]
            oi_scr[j * SUB:(j + 1) * SUB, :] = _nt_dot(qt[j * SUB:(j + 1) * SUB, :], st.astype(BF16))
            vm = jnp.where(lane_sub == j, v_t, 0.0).astype(BF16)
            ut = _dot(vm, kt)
            dec = jnp.exp(beta[j * SUB:j * SUB + 1, :])
            st_scr[...] = st * dec + jnp.where(same_head, ut, 0.0)

        o = acc + oi_scr[...]
        hi, lo = _split2(o * o)
        msq = (_dot(hi, ones_bd) + _dot(lo, ones_bd)) * (1.0 / HEAD_DIM)
        y = o * lax.rsqrt(msq + EPS) * gn
        o_ref[rs, :] = (y * gate_ref[rs, :]).astype(o_ref.dtype)
        return carry

    lax.fori_loop(0, tb // HG_ROWS, chunk, 0)


def _hgrn_call(layer, lb_logits, proj, gn, batch, seq, tb):
    m = proj.shape[0]
    nt = seq // tb

    def col(j):
        return pl.BlockSpec((tb, HG_KW), lambda b, t, j=j: (b * nt + t, j))

    return pl.pallas_call(
        functools.partial(_hgrn_kernel, layer=layer, tb=tb),
        grid=(batch, nt),
        in_specs=[pl.BlockSpec((DEPTH, HG_KW), lambda b, t: (0, 0)),
                  col(0), col(1), col(2), col(3), col(4), _layer(layer, 1, HG_WIDTH)],
        out_specs=pl.BlockSpec((tb, HG_WIDTH), lambda b, t: (b * nt + t, 0)),
        out_shape=jax.ShapeDtypeStruct((m, HG_WIDTH), BF16),
        scratch_shapes=[pltpu.VMEM((HG_WIDTH, HG_KW), F32),
                        pltpu.VMEM((HG_ROWS, HG_WIDTH), F32)],
        compiler_params=pltpu.CompilerParams(
            dimension_semantics=("arbitrary", "arbitrary"), vmem_limit_bytes=VMEM_LIMIT),
        name="hgrn2",
    )(lb_logits, proj, proj, proj, proj, proj, gn)


def _sb_kernel(q0_ref, q1_ref, q2_ref, k0_ref, k1_ref, k2_ref, v0_ref, v1_ref, v2_ref,
               o_ref, acc_scr, car_scr, *, bq, n_sub):
    g = pl.program_id(1)
    q_refs = (q0_ref, q1_ref, q2_ref)
    k_refs = (k0_ref, k1_ref, k2_ref)
    v_refs = (v0_ref, v1_ref, v2_ref)
    n_pairs = len(q_refs)

    lane_lo = lax.broadcasted_iota(jnp.int32, (bq, LANES), 1) < HEAD_DIM
    r_i = lax.broadcasted_iota(jnp.int32, (bq, bq), 0)
    c_i = lax.broadcasted_iota(jnp.int32, (bq, bq), 1)
    before = c_i < r_i
    later = jnp.where(r_i > c_i, 1.0, 0.0).astype(BF16)

    heads = range(2 * n_pairs)

    def q_block(sub, carry_unused):
        i = g * n_sub + sub
        qs = pl.ds(pl.multiple_of(sub * bq, bq), bq)
        qms = []
        for p in range(n_pairs):
            qn = q_refs[p][qs, :]
            zero = jnp.zeros_like(qn)
            qms.append(jnp.where(lane_lo, qn, zero))
            qms.append(jnp.where(lane_lo, zero, qn))

        def absorb(tiles, fresh):
            rows = [pl.ds(pl.multiple_of(kb * bq, bq), bq) for kb, _ in tiles]
            masks = [mask for _, mask in tiles]
            nt = range(len(tiles))
            zs = [[_nt_dot(qms[h], k_refs[h // 2][rows[t], :]) for h in heads] for t in nt]
            sps = [[_softplus2(z) for z in zt] for zt in zs]
            us = [spt if masks[t] is None else [jnp.where(masks[t], sp, 0.0) for sp in spt]
                  for t, spt in enumerate(sps)]
            tails = [[_dot(u.astype(BF16), later) for u in ut] for ut in us]
            low = None
            for h in heads:
                carry = None if fresh else car_scr[h]
                acc = None if fresh else acc_scr[h]
                for t in nt:
                    row = jnp.broadcast_to(jnp.sum(us[t][h], axis=1, keepdims=True), (bq, LANES))
                    e = zs[t][h] - sps[t][h] - tails[t][h]
                    if carry is not None:
                        e = e - jnp.concatenate([carry] * (bq // LANES), axis=1)
                        row = row + carry
                    a = jnp.exp2(e)
                    if masks[t] is not None:
                        a = jnp.where(masks[t], a, 0.0)
                    pv = _dot(a.astype(BF16), v_refs[h // 2][rows[t], :])
                    acc = pv if acc is None else acc + pv
                    carry = row
                acc_scr[h] = acc
                car_scr[h] = carry
                low = carry if low is None else jnp.minimum(low, carry)
            return jnp.min(low)

        low = lax.cond(i > 0,
                       lambda: absorb([(i, before), (i - 1, None)], True),
                       lambda: absorb([(i, before)], True))

        def live(st):
            return jnp.logical_and(st[0] >= 0, st[1] < -SB_ZERO_LOG * LOG2E)

        lax.while_loop(live, lambda st: (st[0] - 1, absorb([(st[0], None)], False)), (i - 2, low))
        for p in range(n_pairs):
            o_ref[qs, p * LANES:(p + 1) * LANES] = jnp.where(
                lane_lo, acc_scr[2 * p], acc_scr[2 * p + 1]).astype(o_ref.dtype)
        return carry_unused

    lax.fori_loop(0, n_sub, q_block, 0)


def _sb_call(q, k, v, batch, seq, bq, n_sub):
    m = q.shape[0]
    rows = bq * n_sub
    nq = seq // rows
    n_pairs = SB_WIDTH // LANES
    q_spec = lambda p: pl.BlockSpec((rows, LANES), lambda b, i, p=p: (b * nq + i, p))
    k_spec = lambda p: pl.BlockSpec((seq, LANES), lambda b, i, p=p: (b, p))
    pairs = range(n_pairs)
    return pl.pallas_call(
        functools.partial(_sb_kernel, bq=bq, n_sub=n_sub),
        grid=(batch, nq),
        in_specs=[q_spec(p) for p in pairs] + [k_spec(p) for p in pairs] * 2,
        out_specs=pl.BlockSpec((rows, SB_WIDTH), lambda b, i: (b * nq + i, 0)),
        out_shape=jax.ShapeDtypeStruct((m, SB_WIDTH), BF16),
        scratch_shapes=[pltpu.VMEM((2 * n_pairs, bq, LANES), F32),
                        pltpu.VMEM((2 * n_pairs, bq, LANES), F32)],
        compiler_params=pltpu.CompilerParams(
            dimension_semantics=("arbitrary", "arbitrary"), vmem_limit_bytes=VMEM_LIMIT),
        name="stickbreak",
    )(*([q] * n_pairs + [k] * n_pairs + [v] * n_pairs))


def _fox_kernel(q_ref, k_ref, v_ref, f_ref, fb_ref, o_ref,
                c_scr, qa_scr, ka_scr, vt_scr, sa_scr, sb_scr, mb_scr, *, bq, bk, seq):
    p_idx = pl.program_id(1)

    @pl.when(p_idx == 0)
    def _():
        cb = min(bk, 2 * LANES)
        r_i = lax.broadcasted_iota(jnp.int32, (cb, cb), 0)
        c_i = lax.broadcasted_iota(jnp.int32, (cb, cb), 1)
        upto = jnp.where(c_i <= r_i, 1.0, 0.0).astype(BF16)

        per_trip = max(u for u in (4, 2, 1) if (seq // cb) % u == 0)

        def body(n, run):
            for j in range(per_trip):
                rs = pl.ds(pl.multiple_of((n * per_trip + j) * cb, cb), cb)
                y = f_ref[rs, :] + fb_ref[...]
                lg = jnp.minimum(y, 0.0) - jnp.log(1.0 + jnp.exp(-jnp.abs(y)))
                hi, mid, lo = _split3(lg)
                c = _dot(upto, hi) + _dot(upto, mid) + _dot(upto, lo) + run
                c_scr[rs, :] = c
                run = c[cb - 1:cb, :]
            return run

        lax.fori_loop(0, seq // (cb * per_trip), body, jnp.zeros((1, LANES), F32))

    lane = lax.broadcasted_iota(jnp.int32, (bk, LANES), 1)
    lane_lo = lane < HEAD_DIM
    row_lo_k = lax.broadcasted_iota(jnp.int32, (LANES, bk), 0) < HEAD_DIM

    pr = lax.broadcasted_iota(jnp.int32, (LANES, LANES), 0)
    pc = lax.broadcasted_iota(jnp.int32, (LANES, LANES), 1)
    place, ones_q, ones_k = [], [], []
    for h in range(2):
        base = HEAD_DIM * (1 - h)
        place.append([jnp.where(jnp.logical_and(pr == 2 * p_idx + h,
                                                jnp.logical_or(pc == base + n, pc == base + 3 + n)),
                                1.0, 0.0).astype(BF16) for n in range(3)])
        ones_q.append(jnp.logical_and(lane >= base + 3, lane < base + 6))
        ones_k.append(jnp.logical_and(lane >= base, lane < base + 3))

    def prep(n, carry):
        rs = pl.ds(pl.multiple_of(n * bk, bk), bk)
        kn = k_ref[rs, :]
        qn = q_ref[rs, :]
        parts = _split3(c_scr[rs, :] * LOG2E)
        v_t = v_ref[rs, :].T
        for h in range(2):
            c_aug = functools.reduce(lambda x, y: x + y,
                                     [_dot(part, sel) for part, sel in zip(parts, place[h])])
            mine = lane_lo if h == 0 else jnp.logical_not(lane_lo)
            k_aug = jnp.where(ones_k[h], 1.0, -c_aug)
            q_aug = jnp.where(ones_q[h], 1.0, c_aug)
            ka_scr[h, rs, :] = jnp.where(mine, kn, k_aug.astype(BF16))
            qa_scr[h, rs, :] = jnp.where(mine, qn, q_aug.astype(BF16))
            vt_scr[h, n] = jnp.where(row_lo_k if h == 0 else jnp.logical_not(row_lo_k), v_t,
                                     jnp.ones_like(v_t))
        return carry

    lax.fori_loop(0, seq // bk, prep, 0)

    key_minus_qry = (lax.broadcasted_iota(jnp.int32, (bk, bq), 0)
                     - lax.broadcasted_iota(jnp.int32, (bk, bq), 1))
    row_lo = lax.broadcasted_iota(jnp.int32, (LANES, bq), 0) < HEAD_DIM

    nq = seq // bq
    n_items = sum((qi * bq + bq - 1) // bk + 1 for qi in range(nq))
    mb_scr[0] = jnp.zeros((bk, bq), F32)
    for r in range(bk // bq):
        mb_scr[r + 1] = jnp.where(key_minus_qry <= r * bq, 0.0, NEG_BIG)

    def unpack(item):
        qi = jnp.minimum(item[0], nq - 1)
        last = (qi * bq + bq - 1) // bk
        first = item[1] == 0
        kb = jnp.where(first, last, item[1] - 1)
        return qi, last, first, kb

    def advance(item):
        qi, last, _, _ = unpack(item)
        wrap = item[1] >= last
        return jnp.where(wrap, item[0] + 1, item[0]), jnp.where(wrap, 0, item[1] + 1)

    def scores_into(dst, item):
        qi, _, first, kb = unpack(item)
        qs = pl.ds(pl.multiple_of(qi * bq, bq), bq)
        rs = pl.ds(pl.multiple_of(kb * bk, bk), bk)
        bias = mb_scr[jnp.where(first, 1 + (qi * bq - kb * bk) // bq, 0)]
        for h in range(2):
            dst[h] = _nt_dot(ka_scr[h, rs, :], qa_scr[h, qs, :]) + bias

    def write_out(qi, sts):
        acc0, acc1 = sts[0][1], sts[1][1]
        o_t = jnp.where(row_lo, acc0 / acc0[HEAD_DIM:HEAD_DIM + 1, :], acc1 / acc1[0:1, :])
        o_ref[pl.ds(pl.multiple_of(qi * bq, bq), bq), :] = o_t.T.astype(o_ref.dtype)

    def absorb(src, item, prev_qi, sts):
        qi, _, first, kb = unpack(item)
        write_out(prev_qi, sts)
        out = []
        for h in range(2):
            m_run, acc = sts[h]
            m_run = jnp.where(first, NEG_BIG, m_run)
            s = src[h]
            m_new = jnp.maximum(m_run, jnp.max(s, axis=0, keepdims=True))
            p = jnp.exp2(s - m_new)
            acc = jnp.exp2(m_run - m_new) * acc + _dot(vt_scr[h, kb], p.astype(BF16))
            out.append((m_new, acc))
        return qi, tuple(out)

    unroll = max([u for u in (8, 4) if n_items % u == 0 and u * bk <= FOX_KEYS_PER_TRIP] + [2])
    bufs = (sa_scr, sb_scr)

    def run(count, ca):
        held, prev_qi, sts = ca
        for k in range(count):
            nxt = advance(held)
            scores_into(bufs[(k + 1) % 2], nxt)
            prev_qi, sts = absorb(bufs[k % 2], held, prev_qi, sts)
            held = nxt
        return held, prev_qi, sts

    start = (jnp.int32(0), jnp.int32(0))
    init = (jnp.full((1, bq), NEG_BIG, F32), jnp.ones((LANES, bq), F32))
    scores_into(sa_scr, start)
    ca = lax.fori_loop(0, n_items // unroll, lambda u, ca: run(unroll, ca),
                       (start, jnp.int32(0), (init, init)))
    _, last_qi, sts = run(n_items % unroll, ca)
    write_out(last_qi, sts)


def _fox_call(q, k, v, ff, fbias, l, batch, seq, bq, bk):
    m = q.shape[0]
    col = pl.BlockSpec((seq, LANES), lambda b, p: (b, p))
    return pl.pallas_call(
        functools.partial(_fox_kernel, bq=bq, bk=bk, seq=seq),
        grid=(batch, FOX_WIDTH // LANES),
        in_specs=[col, col, col, pl.BlockSpec((seq, LANES), lambda b, p: (b, 0)), _layer(l, 1, LANES)],
        out_specs=pl.BlockSpec((seq, LANES), lambda b, p: (b, p)),
        out_shape=jax.ShapeDtypeStruct((m, FOX_WIDTH), BF16),
        scratch_shapes=[pltpu.VMEM((seq, LANES), F32),
                        pltpu.VMEM((2, seq, LANES), BF16),
                        pltpu.VMEM((2, seq, LANES), BF16),
                        pltpu.VMEM((2, seq // bk, LANES, bk), BF16),
                        pltpu.VMEM((2, bk, bq), F32),
                        pltpu.VMEM((2, bk, bq), F32),
                        pltpu.VMEM((bk // bq + 1, bk, bq), F32)],
        compiler_params=pltpu.CompilerParams(
            dimension_semantics=("arbitrary", "arbitrary"), vmem_limit_bytes=VMEM_LIMIT),
        name="forgetting",
    )(q, k, v, ff, fbias)


def _mlp_kernel(x_ref, ohg_ref, osb_ref, ofx_ref, wo1_ref, wo2_ref, wo3_ref, g2_ref,
                w1_ref, w2_ref, o_ref, *, tf):
    x1 = (x_ref[...] + _dot(ohg_ref[...], wo1_ref[...]) + _dot(osb_ref[...], wo2_ref[...])
          + _dot(ofx_ref[...], wo3_ref[...]))
    ms = jnp.mean(x1 * x1, axis=-1, keepdims=True)
    h2 = (x1 * lax.rsqrt(ms + EPS) * g2_ref[...]).astype(BF16)
    o_ref[...] = x1
    acc = None
    for f in range(w1_ref.shape[1] // tf):
        a = _dot(h2, w1_ref[:, f * tf:(f + 1) * tf])
        a = jnp.square(jnp.maximum(a, 0.0)).astype(BF16)
        y = _dot(a, w2_ref[f * tf:(f + 1) * tf, :])
        acc = y if acc is None else acc + y
    o_ref[...] = o_ref[...] + acc


def _mlp_call(x, ohg, osb, ofx, wo1, wo2, wo3, g2, w1, w2, l, tm, tf):
    m, d = x.shape
    row = lambda w: pl.BlockSpec((tm, w), lambda i: (i, 0))
    whole = lambda a: _layer(l, *a.shape[1:])
    return pl.pallas_call(
        functools.partial(_mlp_kernel, tf=tf),
        grid=(m // tm,),
        in_specs=[row(d), row(HG_WIDTH), row(SB_WIDTH), row(FOX_WIDTH),
                  whole(wo1), whole(wo2), whole(wo3), whole(g2), whole(w1), whole(w2)],
        out_specs=row(d),
        out_shape=jax.ShapeDtypeStruct((m, d), F32),
        compiler_params=pltpu.CompilerParams(
            dimension_semantics=("arbitrary",), vmem_limit_bytes=VMEM_LIMIT),
        name="outproj_mlp",
    )(x, ohg, osb, ofx, wo1, wo2, wo3, g2, w1, w2)


def _tile_sizes(batch, seq):
    m = batch * seq
    return dict(
        proj_tm=min(512, m),
        hg_tb=min(2048, seq),
        sb_bq=min(256, seq),
        sb_sub=max(1, seq // 256),
        fox_bq=min(256, seq),
        fox_bk=min(512, seq),
        mlp_tm=min(512, m),
        mlp_tf=1024,
    )


def kernel(x, lb_logits, norm1_g, w_in, hg_norm_g, sb_q_norm_g, sb_k_norm_g, fox_q_norm_g,
           fox_k_norm_g, fox_f_bias, w_out, norm2_g, w_ff1, w_ff2):
    batch, seq, d = x.shape
    assert d == D_MODEL and seq % HG_ROWS == 0 and x.dtype == F32
    ts = _tile_sizes(batch, seq)
    assert seq % (ts["sb_bq"] * ts["sb_sub"]) == 0 and seq % ts["fox_bk"] == 0 and ts["fox_bk"] % ts["fox_bq"] == 0
    assert seq % ts["hg_tb"] == 0
    m = batch * seq
    xf = x.reshape(m, d)
    row = lambda a: a.astype(F32)[:, None, :]
    pair = lambda g: row(jnp.tile(g, (1, 2)))
    w_in_p = jnp.pad(w_in, ((0, 0), (0, 0), (0, IN_COLS_PAD - IN_COLS))).astype(BF16)
    wo = w_out.astype(BF16)
    wo_hg, wo_sb, wo_fx = wo[:, :HG_WIDTH], wo[:, HG_WIDTH:HG_WIDTH + SB_WIDTH], wo[:, HG_WIDTH + SB_WIDTH:]
    w1, w2 = w_ff1.astype(BF16), w_ff2.astype(BF16)
    g1, g2, g_hg = row(norm1_g), row(norm2_g), row(jnp.tile(hg_norm_g, (1, HG_HEADS)))
    qk_gains = (pair(sb_q_norm_g), pair(sb_k_norm_g), pair(fox_q_norm_g), pair(fox_k_norm_g))
    fbias = row(jnp.pad(fox_f_bias, ((0, 0), (0, LANES - FOX_HEADS))))
    lbl = lb_logits.astype(F32)

    for l in range(DEPTH):
        hg, ff, sq, sk, sv, fq, fk, fv = _proj_call(xf, g1, w_in_p, lbl, qk_gains, l, ts["proj_tm"])
        o_hg = _hgrn_call(l, lbl, hg, g_hg, batch, seq, ts["hg_tb"])
        o_sb = _sb_call(sq, sk, sv, batch, seq, ts["sb_bq"], ts["sb_sub"])
        o_fx = _fox_call(fq, fk, fv, ff, fbias, l, batch, seq, ts["fox_bq"], ts["fox_bk"])
        xf = _mlp_call(xf, o_hg, o_sb, o_fx, wo_hg, wo_sb, wo_fx, g2, w1, w2, l,
                       ts["mlp_tm"], ts["mlp_tf"])
    return xf.reshape(batch, seq, d)
```

```python
import functools
import math

import jax
import jax.numpy as jnp
from jax import lax
from jax.experimental import pallas as pl
from jax.experimental.pallas import tpu as pltpu

F32 = jnp.float32
BF16 = jnp.bfloat16

D_MODEL = 1024
DEPTH = 4
HEAD_DIM = 64
HEAD_SHIFT = HEAD_DIM.bit_length() - 1
HG_HEADS = 4
HG_KW = 256
HG_WIDTH = 256
SB_WIDTH = 384
FOX_WIDTH = 384
FOX_HEADS = 6
IN_COLS = 4 * 256 + 3 * 384 + 3 * 384 + FOX_HEADS
D_FF = 4 * D_MODEL
EPS = 1e-6
LB_FLOOR = 1e-30
NEG_BIG = -1e30
LOG2E = math.log2(math.e)

LANES = 128
IN_COLS_PAD = 27 * LANES
SB_Q_BLK, SB_K_BLK, SB_V_BLK = 8, 11, 14
FOX_Q_BLK, FOX_K_BLK, FOX_V_BLK, FOX_F_BLK = 17, 20, 23, 26

SB_ZERO_LOG = -105.0
SUB = 16
HG_ROWS = 256
FOX_KEYS_PER_TRIP = 2048
VMEM_LIMIT = 56 * 1024 * 1024


def _nt_dot(a, b):
    return lax.dot_general(a, b, (((1,), (1,)), ((), ())), preferred_element_type=F32)


def _dot(a, b):
    return jnp.dot(a, b, preferred_element_type=F32)


def _split2(x):
    hi = x.astype(BF16)
    lo = (x - hi.astype(F32)).astype(BF16)
    return hi, lo


def _split3(x):
    hi = x.astype(BF16)
    r1 = x - hi.astype(F32)
    mid = r1.astype(BF16)
    lo = (r1 - mid.astype(F32)).astype(BF16)
    return hi, mid, lo


def _softplus(z):
    return jnp.maximum(z, 0.0) + jnp.log(1.0 + jnp.exp(-jnp.abs(z)))


def _softplus2(z2):
    return jnp.maximum(z2, 0.0) + jnp.log2(1.0 + jnp.exp2(-jnp.abs(z2)))


def _pair_rms(x, gain, lane_lo):
    x2 = x * x
    s0 = jnp.sum(jnp.where(lane_lo, x2, 0.0), axis=1, keepdims=True)
    s1 = jnp.sum(jnp.where(lane_lo, 0.0, x2), axis=1, keepdims=True)
    ms = jnp.where(lane_lo, s0, s1) * (1.0 / HEAD_DIM)
    return x * lax.rsqrt(ms + EPS) * gain


def _layer(l, *shape):
    return pl.BlockSpec((None,) + shape, lambda *_: (l,) + (0,) * len(shape))


def _lower_bound(lbl_ref, layer):
    rows = [lbl_ref[j:j + 1, :] for j in range(DEPTH)]
    mx = functools.reduce(jnp.maximum, rows)
    ex = [jnp.exp(r - mx) for r in rows]
    den = functools.reduce(lambda a, b: a + b, ex)
    lb = jnp.zeros_like(mx)
    for j in range(1, layer + 1):
        lb = lb + ex[j] / den
    return lb


def _proj_kernel(x_ref, g_ref, w_ref, lbl_ref, gsq_ref, gsk_ref, gfq_ref, gfk_ref,
                 hg_ref, ff_ref, sq_ref, sk_ref, sv_ref, fq_ref, fk_ref, fv_ref, *, layer):
    x = x_ref[...]
    ms = jnp.mean(x * x, axis=-1, keepdims=True)
    h = (x * lax.rsqrt(ms + EPS) * g_ref[...]).astype(BF16)
    y = _dot(h, w_ref[...])
    fl = y[:, HG_KW:2 * HG_KW]
    gt = y[:, 3 * HG_KW:4 * HG_KW]
    log_lb = jnp.log(jnp.maximum(_lower_bound(lbl_ref, layer), LB_FLOOR))
    sp = _softplus(fl)
    hg_ref[:, :HG_KW] = y[:, :HG_KW]
    hg_ref[:, HG_KW:2 * HG_KW] = sp
    hg_ref[:, 2 * HG_KW:3 * HG_KW] = y[:, 2 * HG_KW:3 * HG_KW]
    hg_ref[:, 3 * HG_KW:4 * HG_KW] = gt / (1.0 + jnp.exp(-gt))
    hg_ref[:, 4 * HG_KW:] = (jnp.maximum(fl, log_lb) + jnp.log(1.0 + jnp.exp(-jnp.abs(fl - log_lb)))
                             - sp)
    ff_ref[...] = y[:, FOX_F_BLK * LANES:]
    lane_lo = lax.broadcasted_iota(jnp.int32, (x.shape[0], LANES), 1) < HEAD_DIM
    q_mult = LOG2E / math.sqrt(HEAD_DIM)
    groups = ((SB_Q_BLK, gsq_ref, q_mult, sq_ref), (SB_K_BLK, gsk_ref, None, sk_ref),
              (SB_V_BLK, None, None, sv_ref),
              (FOX_Q_BLK, gfq_ref, q_mult, fq_ref), (FOX_K_BLK, gfk_ref, None, fk_ref),
              (FOX_V_BLK, None, None, fv_ref))
    for blk, gain_ref, mult, out_ref in groups:
        for p in range(SB_WIDTH // LANES):
            t = y[:, (blk + p) * LANES:(blk + p + 1) * LANES]
            if gain_ref is not None:
                t = _pair_rms(t, gain_ref[...], lane_lo)
            if mult is not None:
                t = t * mult
            out_ref[:, p * LANES:(p + 1) * LANES] = t.astype(BF16)


def _proj_call(x, g, w, lbl, gains, l, tm):
    m, d = x.shape
    n = w.shape[2]
    rows = lambda width: pl.BlockSpec((tm, width), lambda i: (i, 0))
    widths = (5 * HG_KW, LANES) + (SB_WIDTH,) * 6
    dtypes = (F32, F32) + (BF16,) * 6
    return pl.pallas_call(
        functools.partial(_proj_kernel, layer=l),
        grid=(m // tm,),
        in_specs=([rows(d), _layer(l, 1, d), _layer(l, d, n),
                   pl.BlockSpec((DEPTH, HG_KW), lambda i: (0, 0))] + [_layer(l, 1, LANES)] * 4),
        out_specs=[rows(wd) for wd in widths],
        out_shape=[jax.ShapeDtypeStruct((m, wd), dt) for wd, dt in zip(widths, dtypes)],
        compiler_params=pltpu.CompilerParams(
            dimension_semantics=("arbitrary",), vmem_limit_bytes=VMEM_LIMIT),
        name="proj",
    )(x, g, w, lbl, *gains)


def _hgrn_kernel(lbl_ref, q_ref, sp_ref, i_ref, gate_ref, lf_ref, gn_ref, o_ref, st_scr, oi_scr,
                 *, layer, tb):
    @pl.when(pl.program_id(1) == 0)
    def _():
        st_scr[...] = jnp.zeros_like(st_scr)

    one_m_lb = 1.0 - _lower_bound(lbl_ref, layer)
    gn = gn_ref[...]

    r_i = lax.broadcasted_iota(jnp.int32, (HG_KW, HG_KW), 0)
    c_i = lax.broadcasted_iota(jnp.int32, (HG_KW, HG_KW), 1)
    same_head = (r_i >> HEAD_SHIFT) == (c_i >> HEAD_SHIFT)
    ones_bd = jnp.where(same_head, 1.0, 0.0).astype(BF16)
    t_mod = lax.broadcasted_iota(jnp.int32, (HG_ROWS, HG_KW), 0) & (SUB - 1)
    lane_sub = lax.broadcasted_iota(jnp.int32, (HG_KW, HG_ROWS), 1) // SUB
    t_r = lax.broadcasted_iota(jnp.int32, (HG_ROWS, HG_ROWS), 0)
    t_c = lax.broadcasted_iota(jnp.int32, (HG_ROWS, HG_ROWS), 1)
    in_sub = (t_r // SUB) == (t_c // SUB)
    same_sub = jnp.where(in_sub, 1.0, 0.0).astype(BF16)
    upto_sub = jnp.where(jnp.logical_and(in_sub, t_c <= t_r), 1.0, 0.0).astype(BF16)

    def chunk(c, carry):
        rs = pl.ds(pl.multiple_of(c * HG_ROWS, HG_ROWS), HG_ROWS)
        q = q_ref[rs, :]
        sp = sp_ref[rs, :]
        v = i_ref[rs, :]
        lf = lf_ref[rs, :]
        kk = one_m_lb * jnp.exp(-sp)

        parts = _split3(lf)
        b = functools.reduce(lambda x, y: x + y, [_dot(upto_sub, part) for part in parts])
        beta = functools.reduce(lambda x, y: x + y, [_dot(same_sub, part) for part in parts])
        rest = beta - b

        b2 = b * LOG2E
        w2 = -sp * LOG2E - b2
        q_lb = q * one_m_lb
        acc = _dot((q * kk).astype(BF16), ones_bd) * v
        for d in range(1, SUB):
            e = jnp.where(t_mod >= d, b2 + pltpu.roll(w2, d, 0), NEG_BIG)
            p = q_lb * jnp.exp2(e)
            acc = acc + _dot(p.astype(BF16), ones_bd) * pltpu.roll(v, d, 0)

        qt = (q * jnp.exp2(b2)).astype(BF16)
        kt = (kk * jnp.exp(rest)).astype(BF16)
        v_t = v.T
        for j in range(HG_ROWS // SUB):
            st = st_scr[...]
            oi_scr[j * SUB:(j + 1) * SUB, :] = _nt_dot(qt[j * SUB:(j + 1) * SUB, :], st.astype(BF16))
            vm = jnp.where(lane_sub == j, v_t, 0.0).astype(BF16)
            ut = _dot(vm, kt)
            dec = jnp.exp(beta[j * SUB:j * SUB + 1, :])
            st_scr[...] = st * dec + jnp.where(same_head, ut, 0.0)

        o = acc + oi_scr[...]
        hi, lo = _split2(o * o)
        msq = (_dot(hi, ones_bd) + _dot(lo, ones_bd)) * (1.0 / HEAD_DIM)
        y = o * lax.rsqrt(msq + EPS) * gn
        o_ref[rs, :] = (y * gate_ref[rs, :]).astype(o_ref.dtype)
        return carry

    lax.fori_loop(0, tb // HG_ROWS, chunk, 0)


def _hgrn_call(layer, lb_logits, proj, gn, batch, seq, tb):
    m = proj.shape[0]
    nt = seq // tb

    def col(j):
        return pl.BlockSpec((tb, HG_KW), lambda b, t, j=j: (b * nt + t, j))

    return pl.pallas_call(
        functools.partial(_hgrn_kernel, layer=layer, tb=tb),
        grid=(batch, nt),
        in_specs=[pl.BlockSpec((DEPTH, HG_KW), lambda b, t: (0, 0)),
                  col(0), col(1), col(2), col(3), col(4), _layer(layer, 1, HG_WIDTH)],
        out_specs=pl.BlockSpec((tb, HG_WIDTH), lambda b, t: (b * nt + t, 0)),
        out_shape=jax.ShapeDtypeStruct((m, HG_WIDTH), BF16),
        scratch_shapes=[pltpu.VMEM((HG_WIDTH, HG_KW), F32),
                        pltpu.VMEM((HG_ROWS, HG_WIDTH), F32)],
        compiler_params=pltpu.CompilerParams(
            dimension_semantics=("arbitrary", "arbitrary"), vmem_limit_bytes=VMEM_LIMIT),
        name="hgrn2",
    )(lb_logits, proj, proj, proj, proj, proj, gn)


def _sb_kernel(q0_ref, q1_ref, q2_ref, k0_ref, k1_ref, k2_ref, v0_ref, v1_ref, v2_ref,
               o_ref, acc_scr, car_scr, *, bq, n_sub):
    g = pl.program_id(1)
    q_refs = (q0_ref, q1_ref, q2_ref)
    k_refs = (k0_ref, k1_ref, k2_ref)
    v_refs = (v0_ref, v1_ref, v2_ref)
    n_pairs = len(q_refs)

    lane_lo = lax.broadcasted_iota(jnp.int32, (bq, LANES), 1) < HEAD_DIM
    r_i = lax.broadcasted_iota(jnp.int32, (bq, bq), 0)
    c_i = lax.broadcasted_iota(jnp.int32, (bq, bq), 1)
    before = c_i < r_i
    later = jnp.where(r_i > c_i, 1.0, 0.0).astype(BF16)

    heads = range(2 * n_pairs)

    def q_block(sub, carry_unused):
        i = g * n_sub + sub
        qs = pl.ds(pl.multiple_of(sub * bq, bq), bq)
        qms = []
        for p in range(n_pairs):
            qn = q_refs[p][qs, :]
            zero = jnp.zeros_like(qn)
            qms.append(jnp.where(lane_lo, qn, zero))
            qms.append(jnp.where(lane_lo, zero, qn))

        def absorb(tiles, fresh):
            rows = [pl.ds(pl.multiple_of(kb * bq, bq), bq) for kb, _ in tiles]
            masks = [mask for _, mask in tiles]
            nt = range(len(tiles))
            zs = [[_nt_dot(qms[h], k_refs[h // 2][rows[t], :]) for h in heads] for t in nt]
            sps = [[_softplus2(z) for z in zt] for zt in zs]
            us = [spt if masks[t] is None else [jnp.where(masks[t], sp, 0.0) for sp in spt]
                  for t, spt in enumerate(sps)]
            tails = [[_dot(u.astype(BF16), later) for u in ut] for ut in us]
            low = None
            for h in heads:
                carry = None if fresh else car_scr[h]
                acc = None if fresh else acc_scr[h]
                for t in nt:
                    row = jnp.broadcast_to(jnp.sum(us[t][h], axis=1, keepdims=True), (bq, LANES))
                    e = zs[t][h] - sps[t][h] - tails[t][h]
                    if carry is not None:
                        e = e - jnp.concatenate([carry] * (bq // LANES), axis=1)
                        row = row + carry
                    a = jnp.exp2(e)
                    if masks[t] is not None:
                        a = jnp.where(masks[t], a, 0.0)
                    pv = _dot(a.astype(BF16), v_refs[h // 2][rows[t], :])
                    acc = pv if acc is None else acc + pv
                    carry = row
                acc_scr[h] = acc
                car_scr[h] = carry
                low = carry if low is None else jnp.minimum(low, carry)
            return jnp.min(low)

        low = lax.cond(i > 0,
                       lambda: absorb([(i, before), (i - 1, None)], True),
                       lambda: absorb([(i, before)], True))

        def live(st):
            return jnp.logical_and(st[0] >= 0, st[1] < -SB_ZERO_LOG * LOG2E)

        lax.while_loop(live, lambda st: (st[0] - 1, absorb([(st[0], None)], False)), (i - 2, low))
        for p in range(n_pairs):
            o_ref[qs, p * LANES:(p + 1) * LANES] = jnp.where(
                lane_lo, acc_scr[2 * p], acc_scr[2 * p + 1]).astype(o_ref.dtype)
        return carry_unused

    lax.fori_loop(0, n_sub, q_block, 0)


def _sb_call(q, k, v, batch, seq, bq, n_sub):
    m = q.shape[0]
    rows = bq * n_sub
    nq = seq // rows
    n_pairs = SB_WIDTH // LANES
    q_spec = lambda p: pl.BlockSpec((rows, LANES), lambda b, i, p=p: (b * nq + i, p))
    k_spec = lambda p: pl.BlockSpec((seq, LANES), lambda b, i, p=p: (b, p))
    pairs = range(n_pairs)
    return pl.pallas_call(
        functools.partial(_sb_kernel, bq=bq, n_sub=n_sub),
        grid=(batch, nq),
        in_specs=[q_spec(p) for p in pairs] + [k_spec(p) for p in pairs] * 2,
        out_specs=pl.BlockSpec((rows, SB_WIDTH), lambda b, i: (b * nq + i, 0)),
        out_shape=jax.ShapeDtypeStruct((m, SB_WIDTH), BF16),
        scratch_shapes=[pltpu.VMEM((2 * n_pairs, bq, LANES), F32),
                        pltpu.VMEM((2 * n_pairs, bq, LANES), F32)],
        compiler_params=pltpu.CompilerParams(
            dimension_semantics=("arbitrary", "arbitrary"), vmem_limit_bytes=VMEM_LIMIT),
        name="stickbreak",
    )(*([q] * n_pairs + [k] * n_pairs + [v] * n_pairs))


def _fox_kernel(q_ref, k_ref, v_ref, f_ref, fb_ref, o_ref,
                c_scr, qa_scr, ka_scr, vt_scr, sa_scr, sb_scr, mb_scr, *, bq, bk, seq):
    p_idx = pl.program_id(1)

    @pl.when(p_idx == 0)
    def _():
        cb = min(bk, 2 * LANES)
        r_i = lax.broadcasted_iota(jnp.int32, (cb, cb), 0)
        c_i = lax.broadcasted_iota(jnp.int32, (cb, cb), 1)
        upto = jnp.where(c_i <= r_i, 1.0, 0.0).astype(BF16)

        per_trip = max(u for u in (4, 2, 1) if (seq // cb) % u == 0)

        def body(n, run):
            for j in range(per_trip):
                rs = pl.ds(pl.multiple_of((n * per_trip + j) * cb, cb), cb)
                y = f_ref[rs, :] + fb_ref[...]
                lg = jnp.minimum(y, 0.0) - jnp.log(1.0 + jnp.exp(-jnp.abs(y)))
                hi, mid, lo = _split3(lg)
                c = _dot(upto, hi) + _dot(upto, mid) + _dot(upto, lo) + run
                c_scr[rs, :] = c
                run = c[cb - 1:cb, :]
            return run

        lax.fori_loop(0, seq // (cb * per_trip), body, jnp.zeros((1, LANES), F32))

    lane = lax.broadcasted_iota(jnp.int32, (bk, LANES), 1)
    lane_lo = lane < HEAD_DIM
    row_lo_k = lax.broadcasted_iota(jnp.int32, (LANES, bk), 0) < HEAD_DIM

    pr = lax.broadcasted_iota(jnp.int32, (LANES, LANES), 0)
    pc = lax.broadcasted_iota(jnp.int32, (LANES, LANES), 1)
    place, ones_q, ones_k = [], [], []
    for h in range(2):
        base = HEAD_DIM * (1 - h)
        place.append([jnp.where(jnp.logical_and(pr == 2 * p_idx + h,
                                                jnp.logical_or(pc == base + n, pc == base + 3 + n)),
                                1.0, 0.0).astype(BF16) for n in range(3)])
        ones_q.append(jnp.logical_and(lane >= base + 3, lane < base + 6))
        ones_k.append(jnp.logical_and(lane >= base, lane < base + 3))

    def prep(n, carry):
        rs = pl.ds(pl.multiple_of(n * bk, bk), bk)
        kn = k_ref[rs, :]
        qn = q_ref[rs, :]
        parts = _split3(c_scr[rs, :] * LOG2E)
        v_t = v_ref[rs, :].T
        for h in range(2):
            c_aug = functools.reduce(lambda x, y: x + y,
                                     [_dot(part, sel) for part, sel in zip(parts, place[h])])
            mine = lane_lo if h == 0 else jnp.logical_not(lane_lo)
            k_aug = jnp.where(ones_k[h], 1.0, -c_aug)
            q_aug = jnp.where(ones_q[h], 1.0, c_aug)
            ka_scr[h, rs, :] = jnp.where(mine, kn, k_aug.astype(BF16))
            qa_scr[h, rs, :] = jnp.where(mine, qn, q_aug.astype(BF16))
            vt_scr[h, n] = jnp.where(row_lo_k if h == 0 else jnp.logical_not(row_lo_k), v_t,
                                     jnp.ones_like(v_t))
        return carry

    chunks = max(u for u in (4, 2, 1) if (seq // bk) % u == 0)

    def prep_trip(t, carry):
        for j in range(chunks):
            prep(t * chunks + j, carry)
        return carry

    lax.fori_loop(0, seq // (bk * chunks), prep_trip, 0)

    key_minus_qry = (lax.broadcasted_iota(jnp.int32, (bk, bq), 0)
                     - lax.broadcasted_iota(jnp.int32, (bk, bq), 1))
    row_lo = lax.broadcasted_iota(jnp.int32, (LANES, bq), 0) < HEAD_DIM

    nq = seq // bq
    n_items = sum((qi * bq + bq - 1) // bk + 1 for qi in range(nq))
    mb_scr[0] = jnp.zeros((bk, bq), F32)
    for r in range(bk // bq):
        mb_scr[r + 1] = jnp.where(key_minus_qry <= r * bq, 0.0, NEG_BIG)

    def unpack(item):
        qi = jnp.minimum(item[0], nq - 1)
        last = (qi * bq + bq - 1) // bk
        first = item[1] == 0
        kb = jnp.where(first, last, item[1] - 1)
        return qi, last, first, kb

    def advance(item):
        qi, last, _, _ = unpack(item)
        wrap = item[1] >= last
        return jnp.where(wrap, item[0] + 1, item[0]), jnp.where(wrap, 0, item[1] + 1)

    def scores_into(dst, item):
        qi, _, first, kb = unpack(item)
        qs = pl.ds(pl.multiple_of(qi * bq, bq), bq)
        rs = pl.ds(pl.multiple_of(kb * bk, bk), bk)
        bias = mb_scr[jnp.where(first, 1 + (qi * bq - kb * bk) // bq, 0)]
        for h in range(2):
            dst[h] = _nt_dot(ka_scr[h, rs, :], qa_scr[h, qs, :]) + bias

    def write_out(qi, sts):
        acc0, acc1 = sts[0][1], sts[1][1]
        o_t = jnp.where(row_lo, acc0 / acc0[HEAD_DIM:HEAD_DIM + 1, :], acc1 / acc1[0:1, :])
        o_ref[pl.ds(pl.multiple_of(qi * bq, bq), bq), :] = o_t.T.astype(o_ref.dtype)

    def absorb(src, item, prev_qi, sts):
        qi, _, first, kb = unpack(item)
        write_out(prev_qi, sts)
        out = []
        for h in range(2):
            m_run, acc = sts[h]
            m_run = jnp.where(first, NEG_BIG, m_run)
            s = src[h]
            m_new = jnp.maximum(m_run, jnp.max(s, axis=0, keepdims=True))
            p = jnp.exp2(s - m_new)
            acc = jnp.exp2(m_run - m_new) * acc + _dot(vt_scr[h, kb], p.astype(BF16))
            out.append((m_new, acc))
        return qi, tuple(out)

    unroll = max([u for u in (8, 4) if n_items % u == 0 and u * bk <= FOX_KEYS_PER_TRIP] + [2])
    bufs = (sa_scr, sb_scr)

    def run(count, ca):
        held, prev_qi, sts = ca
        for k in range(count):
            nxt = advance(held)
            scores_into(bufs[(k + 1) % 2], nxt)
            prev_qi, sts = absorb(bufs[k % 2], held, prev_qi, sts)
            held = nxt
        return held, prev_qi, sts

    start = (jnp.int32(0), jnp.int32(0))
    init = (jnp.full((1, bq), NEG_BIG, F32), jnp.ones((LANES, bq), F32))
    scores_into(sa_scr, start)
    ca = lax.fori_loop(0, n_items // unroll, lambda u, ca: run(unroll, ca),
                       (start, jnp.int32(0), (init, init)))
    _, last_qi, sts = run(n_items % unroll, ca)
    write_out(last_qi, sts)


def _fox_call(q, k, v, ff, fbias, l, batch, seq, bq, bk):
    m = q.shape[0]
    col = pl.BlockSpec((seq, LANES), lambda b, p: (b, p))
    return pl.pallas_call(
        functools.partial(_fox_kernel, bq=bq, bk=bk, seq=seq),
        grid=(batch, FOX_WIDTH // LANES),
        in_specs=[col, col, col, pl.BlockSpec((seq, LANES), lambda b, p: (b, 0)), _layer(l, 1, LANES)],
        out_specs=pl.BlockSpec((seq, LANES), lambda b, p: (b, p)),
        out_shape=jax.ShapeDtypeStruct((m, FOX_WIDTH), BF16),
        scratch_shapes=[pltpu.VMEM((seq, LANES), F32),
                        pltpu.VMEM((2, seq, LANES), BF16),
                        pltpu.VMEM((2, seq, LANES), BF16),
                        pltpu.VMEM((2, seq // bk, LANES, bk), BF16),
                        pltpu.VMEM((2, bk, bq), F32),
                        pltpu.VMEM((2, bk, bq), F32),
                        pltpu.VMEM((bk // bq + 1, bk, bq), F32)],
        compiler_params=pltpu.CompilerParams(
            dimension_semantics=("arbitrary", "arbitrary"), vmem_limit_bytes=VMEM_LIMIT),
        name="forgetting",
    )(q, k, v, ff, fbias)


def _mlp_kernel(x_ref, ohg_ref, osb_ref, ofx_ref, wo1_ref, wo2_ref, wo3_ref, g2_ref,
                w1_ref, w2_ref, o_ref, *, tf):
    x1 = (x_ref[...] + _dot(ohg_ref[...], wo1_ref[...]) + _dot(osb_ref[...], wo2_ref[...])
          + _dot(ofx_ref[...], wo3_ref[...]))
    ms = jnp.mean(x1 * x1, axis=-1, keepdims=True)
    h2 = (x1 * lax.rsqrt(ms + EPS) * g2_ref[...]).astype(BF16)
    o_ref[...] = x1
    acc = None
    for f in range(w1_ref.shape[1] // tf):
        a = _dot(h2, w1_ref[:, f * tf:(f + 1) * tf])
        a = jnp.square(jnp.maximum(a, 0.0)).astype(BF16)
        y = _dot(a, w2_ref[f * tf:(f + 1) * tf, :])
        acc = y if acc is None else acc + y
    o_ref[...] = o_ref[...] + acc


def _mlp_call(x, ohg, osb, ofx, wo1, wo2, wo3, g2, w1, w2, l, tm, tf):
    m, d = x.shape
    row = lambda w: pl.BlockSpec((tm, w), lambda i: (i, 0))
    whole = lambda a: _layer(l, *a.shape[1:])
    return pl.pallas_call(
        functools.partial(_mlp_kernel, tf=tf),
        grid=(m // tm,),
        in_specs=[row(d), row(HG_WIDTH), row(SB_WIDTH), row(FOX_WIDTH),
                  whole(wo1), whole(wo2), whole(wo3), whole(g2), whole(w1), whole(w2)],
        out_specs=row(d),
        out_shape=jax.ShapeDtypeStruct((m, d), F32),
        compiler_params=pltpu.CompilerParams(
            dimension_semantics=("arbitrary",), vmem_limit_bytes=VMEM_LIMIT),
        name="outproj_mlp",
    )(x, ohg, osb, ofx, wo1, wo2, wo3, g2, w1, w2)


def _tile_sizes(batch, seq):
    m = batch * seq
    return dict(
        proj_tm=min(512, m),
        hg_tb=min(2048, seq),
        sb_bq=min(256, seq),
        sb_sub=max(1, seq // 256),
        fox_bq=min(256, seq),
        fox_bk=min(512, seq),
        mlp_tm=min(512, m),
        mlp_tf=1024,
    )


def kernel(x, lb_logits, norm1_g, w_in, hg_norm_g, sb_q_norm_g, sb_k_norm_g, fox_q_norm_g,
           fox_k_norm_g, fox_f_bias, w_out, norm2_g, w_ff1, w_ff2):
    batch, seq, d = x.shape
    assert d == D_MODEL and seq % HG_ROWS == 0 and x.dtype == F32
    ts = _tile_sizes(batch, seq)
    assert seq % (ts["sb_bq"] * ts["sb_sub"]) == 0 and seq % ts["fox_bk"] == 0 and ts["fox_bk"] % ts["fox_bq"] == 0
    assert seq % ts["hg_tb"] == 0
    m = batch * seq
    xf = x.reshape(m, d)
    row = lambda a: a.astype(F32)[:, None, :]
    pair = lambda g: row(jnp.tile(g, (1, 2)))
    w_in_p = jnp.pad(w_in, ((0, 0), (0, 0), (0, IN_COLS_PAD - IN_COLS))).astype(BF16)
    wo = w_out.astype(BF16)
    wo_hg, wo_sb, wo_fx = wo[:, :HG_WIDTH], wo[:, HG_WIDTH:HG_WIDTH + SB_WIDTH], wo[:, HG_WIDTH + SB_WIDTH:]
    w1, w2 = w_ff1.astype(BF16), w_ff2.astype(BF16)
    g1, g2, g_hg = row(norm1_g), row(norm2_g), row(jnp.tile(hg_norm_g, (1, HG_HEADS)))
    qk_gains = (pair(sb_q_norm_g), pair(sb_k_norm_g), pair(fox_q_norm_g), pair(fox_k_norm_g))
    fbias = row(jnp.pad(fox_f_bias, ((0, 0), (0, LANES - FOX_HEADS))))
    lbl = lb_logits.astype(F32)

    for l in range(DEPTH):
        hg, ff, sq, sk, sv, fq, fk, fv = _proj_call(xf, g1, w_in_p, lbl, qk_gains, l, ts["proj_tm"])
        o_hg = _hgrn_call(l, lbl, hg, g_hg, batch, seq, ts["hg_tb"])
        o_sb = _sb_call(sq, sk, sv, batch, seq, ts["sb_bq"], ts["sb_sub"])
        o_fx = _fox_call(fq, fk, fv, ff, fbias, l, batch, seq, ts["fox_bq"], ts["fox_bk"])
        xf = _mlp_call(xf, o_hg, o_sb, o_fx, wo_hg, wo_sb, wo_fx, g2, w1, w2, l,
                       ts["mlp_tm"], ts["mlp_tf"])
    return xf.reshape(batch, seq, d)
```

```python
import functools
import math

import jax
import jax.numpy as jnp
from jax import lax
from jax.experimental import pallas as pl
from jax.experimental.pallas import tpu as pltpu

F32 = jnp.float32
BF16 = jnp.bfloat16

D_MODEL = 1024
DEPTH = 4
HEAD_DIM = 64
HEAD_SHIFT = HEAD_DIM.bit_length() - 1
HG_HEADS = 4
HG_KW = 256
HG_WIDTH = 256
SB_WIDTH = 384
FOX_WIDTH = 384
FOX_HEADS = 6
IN_COLS = 4 * 256 + 3 * 384 + 3 * 384 + FOX_HEADS
D_FF = 4 * D_MODEL
EPS = 1e-6
LB_FLOOR = 1e-30
NEG_BIG = -1e30
LOG2E = math.log2(math.e)

LANES = 128
IN_COLS_PAD = 27 * LANES
SB_Q_BLK, SB_K_BLK, SB_V_BLK = 8, 11, 14
FOX_Q_BLK, FOX_K_BLK, FOX_V_BLK, FOX_F_BLK = 17, 20, 23, 26

SB_ZERO_LOG = -105.0
SUB = 16
HG_ROWS = 256
FOX_KEYS_PER_TRIP = 4096
VMEM_LIMIT = 56 * 1024 * 1024


def _nt_dot(a, b):
    return lax.dot_general(a, b, (((1,), (1,)), ((), ())), preferred_element_type=F32)


def _dot(a, b):
    return jnp.dot(a, b, preferred_element_type=F32)


def _split2(x):
    hi = x.astype(BF16)
    lo = (x - hi.astype(F32)).astype(BF16)
    return hi, lo


def _split3(x):
    hi = x.astype(BF16)
    r1 = x - hi.astype(F32)
    mid = r1.astype(BF16)
    lo = (r1 - mid.astype(F32)).astype(BF16)
    return hi, mid, lo


def _softplus(z):
    return jnp.maximum(z, 0.0) + jnp.log(1.0 + jnp.exp(-jnp.abs(z)))


def _softplus2(z2):
    return jnp.maximum(z2, 0.0) + jnp.log2(1.0 + jnp.exp2(-jnp.abs(z2)))


def _pair_rms(x, gain, lane_lo):
    x2 = x * x
    s0 = jnp.sum(jnp.where(lane_lo, x2, 0.0), axis=1, keepdims=True)
    s1 = jnp.sum(jnp.where(lane_lo, 0.0, x2), axis=1, keepdims=True)
    ms = jnp.where(lane_lo, s0, s1) * (1.0 / HEAD_DIM)
    return x * lax.rsqrt(ms + EPS) * gain


def _layer(l, *shape):
    return pl.BlockSpec((None,) + shape, lambda *_: (l,) + (0,) * len(shape))


def _lower_bound(lbl_ref, layer):
    rows = [lbl_ref[j:j + 1, :] for j in range(DEPTH)]
    mx = functools.reduce(jnp.maximum, rows)
    ex = [jnp.exp(r - mx) for r in rows]
    den = functools.reduce(lambda a, b: a + b, ex)
    lb = jnp.zeros_like(mx)
    for j in range(1, layer + 1):
        lb = lb + ex[j] / den
    return lb


def _proj_kernel(x_ref, g_ref, w_ref, lbl_ref, gsq_ref, gsk_ref, gfq_ref, gfk_ref,
                 hg_ref, ff_ref, sq_ref, sk_ref, sv_ref, fq_ref, fk_ref, fv_ref, *, layer):
    x = x_ref[...]
    ms = jnp.mean(x * x, axis=-1, keepdims=True)
    h = (x * lax.rsqrt(ms + EPS) * g_ref[...]).astype(BF16)
    y = _dot(h, w_ref[...])
    fl = y[:, HG_KW:2 * HG_KW]
    gt = y[:, 3 * HG_KW:4 * HG_KW]
    log_lb = jnp.log(jnp.maximum(_lower_bound(lbl_ref, layer), LB_FLOOR))
    sp = _softplus(fl)
    hg_ref[:, :HG_KW] = y[:, :HG_KW]
    hg_ref[:, HG_KW:2 * HG_KW] = sp
    hg_ref[:, 2 * HG_KW:3 * HG_KW] = y[:, 2 * HG_KW:3 * HG_KW]
    hg_ref[:, 3 * HG_KW:4 * HG_KW] = gt / (1.0 + jnp.exp(-gt))
    hg_ref[:, 4 * HG_KW:] = (jnp.maximum(fl, log_lb) + jnp.log(1.0 + jnp.exp(-jnp.abs(fl - log_lb)))
                             - sp)
    ff_ref[...] = y[:, FOX_F_BLK * LANES:]
    lane_lo = lax.broadcasted_iota(jnp.int32, (x.shape[0], LANES), 1) < HEAD_DIM
    q_mult = LOG2E / math.sqrt(HEAD_DIM)
    groups = ((SB_Q_BLK, gsq_ref, q_mult, sq_ref), (SB_K_BLK, gsk_ref, None, sk_ref),
              (SB_V_BLK, None, None, sv_ref),
              (FOX_Q_BLK, gfq_ref, q_mult, fq_ref), (FOX_K_BLK, gfk_ref, None, fk_ref),
              (FOX_V_BLK, None, None, fv_ref))
    for blk, gain_ref, mult, out_ref in groups:
        for p in range(SB_WIDTH // LANES):
            t = y[:, (blk + p) * LANES:(blk + p + 1) * LANES]
            if gain_ref is not None:
                t = _pair_rms(t, gain_ref[...], lane_lo)
            if mult is not None:
                t = t * mult
            out_ref[:, p * LANES:(p + 1) * LANES] = t.astype(BF16)


def _proj_call(x, g, w, lbl, gains, l, tm):
    m, d = x.shape
    n = w.shape[2]
    rows = lambda width: pl.BlockSpec((tm, width), lambda i: (i, 0))
    widths = (5 * HG_KW, LANES) + (SB_WIDTH,) * 6
    dtypes = (F32, F32) + (BF16,) * 6
    return pl.pallas_call(
        functools.partial(_proj_kernel, layer=l),
        grid=(m // tm,),
        in_specs=([rows(d), _layer(l, 1, d), _layer(l, d, n),
                   pl.BlockSpec((DEPTH, HG_KW), lambda i: (0, 0))] + [_layer(l, 1, LANES)] * 4),
        out_specs=[rows(wd) for wd in widths],
        out_shape=[jax.ShapeDtypeStruct((m, wd), dt) for wd, dt in zip(widths, dtypes)],
        compiler_params=pltpu.CompilerParams(
            dimension_semantics=("arbitrary",), vmem_limit_bytes=VMEM_LIMIT),
        name="proj",
    )(x, g, w, lbl, *gains)


def _hgrn_kernel(lbl_ref, q_ref, sp_ref, i_ref, gate_ref, lf_ref, gn_ref, o_ref, st_scr, oi_scr,
                 *, layer, tb):
    @pl.when(pl.program_id(1) == 0)
    def _():
        st_scr[...] = jnp.zeros_like(st_scr)

    one_m_lb = 1.0 - _lower_bound(lbl_ref, layer)
    gn = gn_ref[...]

    r_i = lax.broadcasted_iota(jnp.int32, (HG_KW, HG_KW), 0)
    c_i = lax.broadcasted_iota(jnp.int32, (HG_KW, HG_KW), 1)
    same_head = (r_i >> HEAD_SHIFT) == (c_i >> HEAD_SHIFT)
    ones_bd = jnp.where(same_head, 1.0, 0.0).astype(BF16)
    t_mod = lax.broadcasted_iota(jnp.int32, (HG_ROWS, HG_KW), 0) & (SUB - 1)
    lane_sub = lax.broadcasted_iota(jnp.int32, (HG_KW, HG_ROWS), 1) // SUB
    t_r = lax.broadcasted_iota(jnp.int32, (HG_ROWS, HG_ROWS), 0)
    t_c = lax.broadcasted_iota(jnp.int32, (HG_ROWS, HG_ROWS), 1)
    in_sub = (t_r // SUB) == (t_c // SUB)
    same_sub = jnp.where(in_sub, 1.0, 0.0).astype(BF16)
    upto_sub = jnp.where(jnp.logical_and(in_sub, t_c <= t_r), 1.0, 0.0).astype(BF16)

    def chunk(c, carry):
        rs = pl.ds(pl.multiple_of(c * HG_ROWS, HG_ROWS), HG_ROWS)
        q = q_ref[rs, :]
        sp = sp_ref[rs, :]
        v = i_ref[rs, :]
        lf = lf_ref[rs, :]
        kk = one_m_lb * jnp.exp(-sp)

        parts = _split3(lf)
        b = functools.reduce(lambda x, y: x + y, [_dot(upto_sub, part) for part in parts])
        beta = functools.reduce(lambda x, y: x + y, [_dot(same_sub, part) for part in parts])
        rest = beta - b

        b2 = b * LOG2E
        w2 = -sp * LOG2E - b2
        q_lb = q * one_m_lb
        acc = _dot((q * kk).astype(BF16), ones_bd) * v
        for d in range(1, SUB):
            e = jnp.where(t_mod >= d, b2 + pltpu.roll(w2, d, 0), NEG_BIG)
            p = q_lb * jnp.exp2(e)
            acc = acc + _dot(p.astype(BF16), ones_bd) * pltpu.roll(v, d, 0)

        qt = (q * jnp.exp2(b2)).astype(BF16)
        kt = (kk * jnp.exp(rest)).astype(BF16)
        v_t = v.T
        for j in range(HG_ROWS // SUB):
            st = st_scr[...]
            oi_scr[j * SUB:(j + 1) * SUB, :] = _nt_dot(qt[j * SUB:(j + 1) * SUB, :], st.astype(BF16))
            vm = jnp.where(lane_sub == j, v_t, 0.0).astype(BF16)
            ut = _dot(vm, kt)
            dec = jnp.exp(beta[j * SUB:j * SUB + 1, :])
            st_scr[...] = st * dec + jnp.where(same_head, ut, 0.0)

        o = acc + oi_scr[...]
        hi, lo = _split2(o * o)
        msq = (_dot(hi, ones_bd) + _dot(lo, ones_bd)) * (1.0 / HEAD_DIM)
        y = o * lax.rsqrt(msq + EPS) * gn
        o_ref[rs, :] = (y * gate_ref[rs, :]).astype(o_ref.dtype)
        return carry

    lax.fori_loop(0, tb // HG_ROWS, chunk, 0)


def _hgrn_call(layer, lb_logits, proj, gn, batch, seq, tb):
    m = proj.shape[0]
    nt = seq // tb

    def col(j):
        return pl.BlockSpec((tb, HG_KW), lambda b, t, j=j: (b * nt + t, j))

    return pl.pallas_call(
        functools.partial(_hgrn_kernel, layer=layer, tb=tb),
        grid=(batch, nt),
        in_specs=[pl.BlockSpec((DEPTH, HG_KW), lambda b, t: (0, 0)),
                  col(0), col(1), col(2), col(3), col(4), _layer(layer, 1, HG_WIDTH)],
        out_specs=pl.BlockSpec((tb, HG_WIDTH), lambda b, t: (b * nt + t, 0)),
        out_shape=jax.ShapeDtypeStruct((m, HG_WIDTH), BF16),
        scratch_shapes=[pltpu.VMEM((HG_WIDTH, HG_KW), F32),
                        pltpu.VMEM((HG_ROWS, HG_WIDTH), F32)],
        compiler_params=pltpu.CompilerParams(
            dimension_semantics=("arbitrary", "arbitrary"), vmem_limit_bytes=VMEM_LIMIT),
        name="hgrn2",
    )(lb_logits, proj, proj, proj, proj, proj, gn)


def _sb_kernel(q0_ref, q1_ref, q2_ref, k0_ref, k1_ref, k2_ref, v0_ref, v1_ref, v2_ref,
               o_ref, acc_scr, car_scr, *, bq, n_sub):
    g = pl.program_id(1)
    q_refs = (q0_ref, q1_ref, q2_ref)
    k_refs = (k0_ref, k1_ref, k2_ref)
    v_refs = (v0_ref, v1_ref, v2_ref)
    n_pairs = len(q_refs)

    lane_lo = lax.broadcasted_iota(jnp.int32, (bq, LANES), 1) < HEAD_DIM
    r_i = lax.broadcasted_iota(jnp.int32, (bq, bq), 0)
    c_i = lax.broadcasted_iota(jnp.int32, (bq, bq), 1)
    before = c_i < r_i
    later = jnp.where(r_i > c_i, 1.0, 0.0).astype(BF16)

    heads = range(2 * n_pairs)

    def q_block(sub, carry_unused):
        i = g * n_sub + sub
        qs = pl.ds(pl.multiple_of(sub * bq, bq), bq)
        qms = []
        for p in range(n_pairs):
            qn = q_refs[p][qs, :]
            zero = jnp.zeros_like(qn)
            qms.append(jnp.where(lane_lo, qn, zero))
            qms.append(jnp.where(lane_lo, zero, qn))

        def absorb(tiles, fresh):
            rows = [pl.ds(pl.multiple_of(kb * bq, bq), bq) for kb, _ in tiles]
            masks = [mask for _, mask in tiles]
            nt = range(len(tiles))
            zs = [[_nt_dot(qms[h], k_refs[h // 2][rows[t], :]) for h in heads] for t in nt]
            sps = [[_softplus2(z) for z in zt] for zt in zs]
            us = [spt if masks[t] is None else [jnp.where(masks[t], sp, 0.0) for sp in spt]
                  for t, spt in enumerate(sps)]
            tails = [[_dot(u.astype(BF16), later) for u in ut] for ut in us]
            low = None
            for h in heads:
                carry = None if fresh else car_scr[h]
                acc = None if fresh else acc_scr[h]
                for t in nt:
                    row = jnp.broadcast_to(jnp.sum(us[t][h], axis=1, keepdims=True), (bq, LANES))
                    e = zs[t][h] - sps[t][h] - tails[t][h]
                    if carry is not None:
                        e = e - jnp.concatenate([carry] * (bq // LANES), axis=1)
                        row = row + carry
                    a = jnp.exp2(e)
                    if masks[t] is not None:
                        a = jnp.where(masks[t], a, 0.0)
                    pv = _dot(a.astype(BF16), v_refs[h // 2][rows[t], :])
                    acc = pv if acc is None else acc + pv
                    carry = row
                acc_scr[h] = acc
                car_scr[h] = carry
                low = carry if low is None else jnp.minimum(low, carry)
            return jnp.min(low)

        low = lax.cond(i > 0,
                       lambda: absorb([(i, before), (i - 1, None)], True),
                       lambda: absorb([(i, before)], True))

        def live(st):
            return jnp.logical_and(st[0] >= 0, st[1] < -SB_ZERO_LOG * LOG2E)

        lax.while_loop(live, lambda st: (st[0] - 1, absorb([(st[0], None)], False)), (i - 2, low))
        for p in range(n_pairs):
            o_ref[qs, p * LANES:(p + 1) * LANES] = jnp.where(
                lane_lo, acc_scr[2 * p], acc_scr[2 * p + 1]).astype(o_ref.dtype)
        return carry_unused

    lax.fori_loop(0, n_sub, q_block, 0)


def _sb_call(q, k, v, batch, seq, bq, n_sub):
    m = q.shape[0]
    rows = bq * n_sub
    nq = seq // rows
    n_pairs = SB_WIDTH // LANES
    q_spec = lambda p: pl.BlockSpec((rows, LANES), lambda b, i, p=p: (b * nq + i, p))
    k_spec = lambda p: pl.BlockSpec((seq, LANES), lambda b, i, p=p: (b, p))
    pairs = range(n_pairs)
    return pl.pallas_call(
        functools.partial(_sb_kernel, bq=bq, n_sub=n_sub),
        grid=(batch, nq),
        in_specs=[q_spec(p) for p in pairs] + [k_spec(p) for p in pairs] * 2,
        out_specs=pl.BlockSpec((rows, SB_WIDTH), lambda b, i: (b * nq + i, 0)),
        out_shape=jax.ShapeDtypeStruct((m, SB_WIDTH), BF16),
        scratch_shapes=[pltpu.VMEM((2 * n_pairs, bq, LANES), F32),
                        pltpu.VMEM((2 * n_pairs, bq, LANES), F32)],
        compiler_params=pltpu.CompilerParams(
            dimension_semantics=("arbitrary", "arbitrary"), vmem_limit_bytes=VMEM_LIMIT),
        name="stickbreak",
    )(*([q] * n_pairs + [k] * n_pairs + [v] * n_pairs))


def _fox_kernel(q_ref, k_ref, v_ref, f_ref, fb_ref, o_ref,
                c_scr, qa_scr, ka_scr, vt_scr, sa_scr, sb_scr, mb_scr, *, bq, bk, seq):
    p_idx = pl.program_id(1)

    @pl.when(p_idx == 0)
    def _():
        cb = min(bk, 2 * LANES)
        r_i = lax.broadcasted_iota(jnp.int32, (cb, cb), 0)
        c_i = lax.broadcasted_iota(jnp.int32, (cb, cb), 1)
        upto = jnp.where(c_i <= r_i, 1.0, 0.0).astype(BF16)

        per_trip = max(u for u in (4, 2, 1) if (seq // cb) % u == 0)

        def body(n, run):
            for j in range(per_trip):
                rs = pl.ds(pl.multiple_of((n * per_trip + j) * cb, cb), cb)
                y = f_ref[rs, :] + fb_ref[...]
                lg = jnp.minimum(y, 0.0) - jnp.log(1.0 + jnp.exp(-jnp.abs(y)))
                hi, mid, lo = _split3(lg)
                c = _dot(upto, hi) + _dot(upto, mid) + _dot(upto, lo) + run
                c_scr[rs, :] = c
                run = c[cb - 1:cb, :]
            return run

        lax.fori_loop(0, seq // (cb * per_trip), body, jnp.zeros((1, LANES), F32))

    lane = lax.broadcasted_iota(jnp.int32, (bk, LANES), 1)
    lane_lo = lane < HEAD_DIM
    row_lo_k = lax.broadcasted_iota(jnp.int32, (LANES, bk), 0) < HEAD_DIM

    pr = lax.broadcasted_iota(jnp.int32, (LANES, LANES), 0)
    pc = lax.broadcasted_iota(jnp.int32, (LANES, LANES), 1)
    place, ones_q, ones_k = [], [], []
    for h in range(2):
        base = HEAD_DIM * (1 - h)
        place.append([jnp.where(jnp.logical_and(pr == 2 * p_idx + h,
                                                jnp.logical_or(pc == base + n, pc == base + 3 + n)),
                                1.0, 0.0).astype(BF16) for n in range(3)])
        ones_q.append(jnp.logical_and(lane >= base + 3, lane < base + 6))
        ones_k.append(jnp.logical_and(lane >= base, lane < base + 3))

    def prep(n, carry):
        rs = pl.ds(pl.multiple_of(n * bk, bk), bk)
        kn = k_ref[rs, :]
        qn = q_ref[rs, :]
        parts = _split3(c_scr[rs, :] * LOG2E)
        v_t = v_ref[rs, :].T
        for h in range(2):
            c_aug = functools.reduce(lambda x, y: x + y,
                                     [_dot(part, sel) for part, sel in zip(parts, place[h])])
            mine = lane_lo if h == 0 else jnp.logical_not(lane_lo)
            k_aug = jnp.where(ones_k[h], 1.0, -c_aug)
            q_aug = jnp.where(ones_q[h], 1.0, c_aug)
            ka_scr[h, rs, :] = jnp.where(mine, kn, k_aug.astype(BF16))
            qa_scr[h, rs, :] = jnp.where(mine, qn, q_aug.astype(BF16))
            vt_scr[h, n] = jnp.where(row_lo_k if h == 0 else jnp.logical_not(row_lo_k), v_t,
                                     jnp.ones_like(v_t))
        return carry

    chunks = max(u for u in (4, 2, 1) if (seq // bk) % u == 0)

    def prep_trip(t, carry):
        for j in range(chunks):
            prep(t * chunks + j, carry)
        return carry

    lax.fori_loop(0, seq // (bk * chunks), prep_trip, 0)

    key_minus_qry = (lax.broadcasted_iota(jnp.int32, (bk, bq), 0)
                     - lax.broadcasted_iota(jnp.int32, (bk, bq), 1))
    row_lo = lax.broadcasted_iota(jnp.int32, (LANES, bq), 0) < HEAD_DIM

    nq = seq // bq
    n_items = sum((qi * bq + bq - 1) // bk + 1 for qi in range(nq))
    mb_scr[0] = jnp.zeros((bk, bq), F32)
    for r in range(bk // bq):
        mb_scr[r + 1] = jnp.where(key_minus_qry <= r * bq, 0.0, NEG_BIG)

    def unpack(item):
        qi = jnp.minimum(item[0], nq - 1)
        last = (qi * bq + bq - 1) // bk
        first = item[1] == 0
        kb = jnp.where(first, last, item[1] - 1)
        return qi, last, first, kb

    def advance(item):
        qi, last, _, _ = unpack(item)
        wrap = item[1] >= last
        return jnp.where(wrap, item[0] + 1, item[0]), jnp.where(wrap, 0, item[1] + 1)

    def scores_into(dst, item):
        qi, _, first, kb = unpack(item)
        qs = pl.ds(pl.multiple_of(qi * bq, bq), bq)
        rs = pl.ds(pl.multiple_of(kb * bk, bk), bk)
        bias = mb_scr[jnp.where(first, 1 + (qi * bq - kb * bk) // bq, 0)]
        for h in range(2):
            dst[h] = _nt_dot(ka_scr[h, rs, :], qa_scr[h, qs, :]) + bias

    def write_out(qi, sts):
        acc0, acc1 = sts[0][1], sts[1][1]
        o_t = jnp.where(row_lo, acc0 / acc0[HEAD_DIM:HEAD_DIM + 1, :], acc1 / acc1[0:1, :])
        o_ref[pl.ds(pl.multiple_of(qi * bq, bq), bq), :] = o_t.T.astype(o_ref.dtype)

    def absorb(src, item, prev_qi, sts):
        qi, _, first, kb = unpack(item)
        write_out(prev_qi, sts)
        out = []
        for h in range(2):
            m_run, acc = sts[h]
            m_run = jnp.where(first, NEG_BIG, m_run)
            s = src[h]
            m_new = jnp.maximum(m_run, jnp.max(s, axis=0, keepdims=True))
            p = jnp.exp2(s - m_new)
            acc = jnp.exp2(m_run - m_new) * acc + _dot(vt_scr[h, kb], p.astype(BF16))
            out.append((m_new, acc))
        return qi, tuple(out)

    unroll = max([u for u in (8, 4) if u <= n_items and u * bk <= FOX_KEYS_PER_TRIP] + [2])
    bufs = (sa_scr, sb_scr)

    def run(count, ca):
        held, prev_qi, sts = ca
        for k in range(count):
            nxt = advance(held)
            scores_into(bufs[(k + 1) % 2], nxt)
            prev_qi, sts = absorb(bufs[k % 2], held, prev_qi, sts)
            held = nxt
        return held, prev_qi, sts

    start = (jnp.int32(0), jnp.int32(0))
    init = (jnp.full((1, bq), NEG_BIG, F32), jnp.ones((LANES, bq), F32))
    scores_into(sa_scr, start)
    ca = lax.fori_loop(0, n_items // unroll, lambda u, ca: run(unroll, ca),
                       (start, jnp.int32(0), (init, init)))
    _, last_qi, sts = run(n_items % unroll, ca)
    write_out(last_qi, sts)


def _fox_call(q, k, v, ff, fbias, l, batch, seq, bq, bk):
    m = q.shape[0]
    col = pl.BlockSpec((seq, LANES), lambda b, p: (b, p))
    return pl.pallas_call(
        functools.partial(_fox_kernel, bq=bq, bk=bk, seq=seq),
        grid=(batch, FOX_WIDTH // LANES),
        in_specs=[col, col, col, pl.BlockSpec((seq, LANES), lambda b, p: (b, 0)), _layer(l, 1, LANES)],
        out_specs=pl.BlockSpec((seq, LANES), lambda b, p: (b, p)),
        out_shape=jax.ShapeDtypeStruct((m, FOX_WIDTH), BF16),
        scratch_shapes=[pltpu.VMEM((seq, LANES), F32),
                        pltpu.VMEM((2, seq, LANES), BF16),
                        pltpu.VMEM((2, seq, LANES), BF16),
                        pltpu.VMEM((2, seq // bk, LANES, bk), BF16),
                        pltpu.VMEM((2, bk, bq), F32),
                        pltpu.VMEM((2, bk, bq), F32),
                        pltpu.VMEM((bk // bq + 1, bk, bq), F32)],
        compiler_params=pltpu.CompilerParams(
            dimension_semantics=("arbitrary", "arbitrary"), vmem_limit_bytes=VMEM_LIMIT),
        name="forgetting",
    )(q, k, v, ff, fbias)


def _mlp_kernel(x_ref, ohg_ref, osb_ref, ofx_ref, wo1_ref, wo2_ref, wo3_ref, g2_ref,
                w1_ref, w2_ref, o_ref, *, tf):
    x1 = (x_ref[...] + _dot(ohg_ref[...], wo1_ref[...]) + _dot(osb_ref[...], wo2_ref[...])
          + _dot(ofx_ref[...], wo3_ref[...]))
    ms = jnp.mean(x1 * x1, axis=-1, keepdims=True)
    h2 = (x1 * lax.rsqrt(ms + EPS) * g2_ref[...]).astype(BF16)
    o_ref[...] = x1
    acc = None
    for f in range(w1_ref.shape[1] // tf):
        a = _dot(h2, w1_ref[:, f * tf:(f + 1) * tf])
        a = jnp.square(jnp.maximum(a, 0.0)).astype(BF16)
        y = _dot(a, w2_ref[f * tf:(f + 1) * tf, :])
        acc = y if acc is None else acc + y
    o_ref[...] = o_ref[...] + acc


def _mlp_call(x, ohg, osb, ofx, wo1, wo2, wo3, g2, w1, w2, l, tm, tf):
    m, d = x.shape
    row = lambda w: pl.BlockSpec((tm, w), lambda i: (i, 0))
    whole = lambda a: _layer(l, *a.shape[1:])
    return pl.pallas_call(
        functools.partial(_mlp_kernel, tf=tf),
        grid=(m // tm,),
        in_specs=[row(d), row(HG_WIDTH), row(SB_WIDTH), row(FOX_WIDTH),
                  whole(wo1), whole(wo2), whole(wo3), whole(g2), whole(w1), whole(w2)],
        out_specs=row(d),
        out_shape=jax.ShapeDtypeStruct((m, d), F32),
        compiler_params=pltpu.CompilerParams(
            dimension_semantics=("arbitrary",), vmem_limit_bytes=VMEM_LIMIT),
        name="outproj_mlp",
    )(x, ohg, osb, ofx, wo1, wo2, wo3, g2, w1, w2)


def _tile_sizes(batch, seq):
    m = batch * seq
    return dict(
        proj_tm=min(512, m),
        hg_tb=min(2048, seq),
        sb_bq=min(256, seq),
        sb_sub=max(1, seq // 256),
        fox_bq=min(256, seq),
        fox_bk=min(512, seq),
        mlp_tm=min(512, m),
        mlp_tf=1024,
    )


def kernel(x, lb_logits, norm1_g, w_in, hg_norm_g, sb_q_norm_g, sb_k_norm_g, fox_q_norm_g,
           fox_k_norm_g, fox_f_bias, w_out, norm2_g, w_ff1, w_ff2):
    batch, seq, d = x.shape
    assert d == D_MODEL and seq % HG_ROWS == 0 and x.dtype == F32
    ts = _tile_sizes(batch, seq)
    assert seq % (ts["sb_bq"] * ts["sb_sub"]) == 0 and seq % ts["fox_bk"] == 0 and ts["fox_bk"] % ts["fox_bq"] == 0
    assert seq % ts["hg_tb"] == 0
    m = batch * seq
    xf = x.reshape(m, d)
    row = lambda a: a.astype(F32)[:, None, :]
    pair = lambda g: row(jnp.tile(g, (1, 2)))
    w_in_p = jnp.pad(w_in, ((0, 0), (0, 0), (0, IN_COLS_PAD - IN_COLS))).astype(BF16)
    wo = w_out.astype(BF16)
    wo_hg, wo_sb, wo_fx = wo[:, :HG_WIDTH], wo[:, HG_WIDTH:HG_WIDTH + SB_WIDTH], wo[:, HG_WIDTH + SB_WIDTH:]
    w1, w2 = w_ff1.astype(BF16), w_ff2.astype(BF16)
    g1, g2, g_hg = row(norm1_g), row(norm2_g), row(jnp.tile(hg_norm_g, (1, HG_HEADS)))
    qk_gains = (pair(sb_q_norm_g), pair(sb_k_norm_g), pair(fox_q_norm_g), pair(fox_k_norm_g))
    fbias = row(jnp.pad(fox_f_bias, ((0, 0), (0, LANES - FOX_HEADS))))
    lbl = lb_logits.astype(F32)

    for l in range(DEPTH):
        hg, ff, sq, sk, sv, fq, fk, fv = _proj_call(xf, g1, w_in_p, lbl, qk_gains, l, ts["proj_tm"])
        o_hg = _hgrn_call(l, lbl, hg, g_hg, batch, seq, ts["hg_tb"])
        o_sb = _sb_call(sq, sk, sv, batch, seq, ts["sb_bq"], ts["sb_sub"])
        o_fx = _fox_call(fq, fk, fv, ff, fbias, l, batch, seq, ts["fox_bq"], ts["fox_bk"])
        xf = _mlp_call(xf, o_hg, o_sb, o_fx, wo_hg, wo_sb, wo_fx, g2, w1, w2, l,
                       ts["mlp_tm"], ts["mlp_tf"])
    return xf.reshape(batch, seq, d)
```

```python
import functools
import math

import jax
import jax.numpy as jnp
from jax import lax
from jax.experimental import pallas as pl
from jax.experimental.pallas import tpu as pltpu

F32 = jnp.float32
BF16 = jnp.bfloat16

D_MODEL = 1024
DEPTH = 4
HEAD_DIM = 64
HEAD_SHIFT = HEAD_DIM.bit_length() - 1
HG_HEADS = 4
HG_KW = 256
HG_WIDTH = 256
SB_WIDTH = 384
FOX_WIDTH = 384
FOX_HEADS = 6
IN_COLS = 4 * 256 + 3 * 384 + 3 * 384 + FOX_HEADS
D_FF = 4 * D_MODEL
EPS = 1e-6
LB_FLOOR = 1e-30
NEG_BIG = -1e30
LOG2E = math.log2(math.e)

LANES = 128
IN_COLS_PAD = 27 * LANES
SB_Q_BLK, SB_K_BLK, SB_V_BLK = 8, 11, 14
FOX_Q_BLK, FOX_K_BLK, FOX_V_BLK, FOX_F_BLK = 17, 20, 23, 26

SB_ZERO_LOG = -105.0
SUB = 16
HG_ROWS = 256
FOX_KEYS_PER_TRIP = 12288
VMEM_LIMIT = 56 * 1024 * 1024


def _nt_dot(a, b):
    return lax.dot_general(a, b, (((1,), (1,)), ((), ())), preferred_element_type=F32)


def _dot(a, b):
    return jnp.dot(a, b, preferred_element_type=F32)


def _split2(x):
    hi = x.astype(BF16)
    lo = (x - hi.astype(F32)).astype(BF16)
    return hi, lo


def _split3(x):
    hi = x.astype(BF16)
    r1 = x - hi.astype(F32)
    mid = r1.astype(BF16)
    lo = (r1 - mid.astype(F32)).astype(BF16)
    return hi, mid, lo


def _softplus(z):
    return jnp.maximum(z, 0.0) + jnp.log(1.0 + jnp.exp(-jnp.abs(z)))


def _softplus2(z2):
    return jnp.maximum(z2, 0.0) + jnp.log2(1.0 + jnp.exp2(-jnp.abs(z2)))


def _pair_rms(x, gain, lane_lo):
    x2 = x * x
    s0 = jnp.sum(jnp.where(lane_lo, x2, 0.0), axis=1, keepdims=True)
    s1 = jnp.sum(jnp.where(lane_lo, 0.0, x2), axis=1, keepdims=True)
    ms = jnp.where(lane_lo, s0, s1) * (1.0 / HEAD_DIM)
    return x * lax.rsqrt(ms + EPS) * gain


def _layer(l, *shape):
    return pl.BlockSpec((None,) + shape, lambda *_: (l,) + (0,) * len(shape))


def _lower_bound(lbl_ref, layer):
    rows = [lbl_ref[j:j + 1, :] for j in range(DEPTH)]
    mx = functools.reduce(jnp.maximum, rows)
    ex = [jnp.exp(r - mx) for r in rows]
    den = functools.reduce(lambda a, b: a + b, ex)
    lb = jnp.zeros_like(mx)
    for j in range(1, layer + 1):
        lb = lb + ex[j] / den
    return lb


def _proj_kernel(x_ref, g_ref, w_ref, lbl_ref, gsq_ref, gsk_ref, gfq_ref, gfk_ref,
                 hg_ref, ff_ref, sq_ref, sk_ref, sv_ref, fq_ref, fk_ref, fv_ref, *, layer):
    x = x_ref[...]
    ms = jnp.mean(x * x, axis=-1, keepdims=True)
    h = (x * lax.rsqrt(ms + EPS) * g_ref[...]).astype(BF16)
    y = _dot(h, w_ref[...])
    fl = y[:, HG_KW:2 * HG_KW]
    gt = y[:, 3 * HG_KW:4 * HG_KW]
    log_lb = jnp.log(jnp.maximum(_lower_bound(lbl_ref, layer), LB_FLOOR))
    sp = _softplus(fl)
    hg_ref[:, :HG_KW] = y[:, :HG_KW]
    hg_ref[:, HG_KW:2 * HG_KW] = sp
    hg_ref[:, 2 * HG_KW:3 * HG_KW] = y[:, 2 * HG_KW:3 * HG_KW]
    hg_ref[:, 3 * HG_KW:4 * HG_KW] = gt / (1.0 + jnp.exp(-gt))
    hg_ref[:, 4 * HG_KW:] = (jnp.maximum(fl, log_lb) + jnp.log(1.0 + jnp.exp(-jnp.abs(fl - log_lb)))
                             - sp)
    ff_ref[...] = y[:, FOX_F_BLK * LANES:]
    lane_lo = lax.broadcasted_iota(jnp.int32, (x.shape[0], LANES), 1) < HEAD_DIM
    q_mult = LOG2E / math.sqrt(HEAD_DIM)
    groups = ((SB_Q_BLK, gsq_ref, q_mult, sq_ref), (SB_K_BLK, gsk_ref, None, sk_ref),
              (SB_V_BLK, None, None, sv_ref),
              (FOX_Q_BLK, gfq_ref, q_mult, fq_ref), (FOX_K_BLK, gfk_ref, None, fk_ref),
              (FOX_V_BLK, None, None, fv_ref))
    for blk, gain_ref, mult, out_ref in groups:
        for p in range(SB_WIDTH // LANES):
            t = y[:, (blk + p) * LANES:(blk + p + 1) * LANES]
            if gain_ref is not None:
                t = _pair_rms(t, gain_ref[...], lane_lo)
            if mult is not None:
                t = t * mult
            out_ref[:, p * LANES:(p + 1) * LANES] = t.astype(BF16)


def _proj_call(x, g, w, lbl, gains, l, tm):
    m, d = x.shape
    n = w.shape[2]
    rows = lambda width: pl.BlockSpec((tm, width), lambda i: (i, 0))
    widths = (5 * HG_KW, LANES) + (SB_WIDTH,) * 6
    dtypes = (F32, F32) + (BF16,) * 6
    return pl.pallas_call(
        functools.partial(_proj_kernel, layer=l),
        grid=(m // tm,),
        in_specs=([rows(d), _layer(l, 1, d), _layer(l, d, n),
                   pl.BlockSpec((DEPTH, HG_KW), lambda i: (0, 0))] + [_layer(l, 1, LANES)] * 4),
        out_specs=[rows(wd) for wd in widths],
        out_shape=[jax.ShapeDtypeStruct((m, wd), dt) for wd, dt in zip(widths, dtypes)],
        compiler_params=pltpu.CompilerParams(
            dimension_semantics=("arbitrary",), vmem_limit_bytes=VMEM_LIMIT),
        name="proj",
    )(x, g, w, lbl, *gains)


def _hgrn_kernel(lbl_ref, q_ref, sp_ref, i_ref, gate_ref, lf_ref, gn_ref, o_ref, st_scr, oi_scr,
                 *, layer, tb):
    @pl.when(pl.program_id(1) == 0)
    def _():
        st_scr[...] = jnp.zeros_like(st_scr)

    one_m_lb = 1.0 - _lower_bound(lbl_ref, layer)
    gn = gn_ref[...]

    r_i = lax.broadcasted_iota(jnp.int32, (HG_KW, HG_KW), 0)
    c_i = lax.broadcasted_iota(jnp.int32, (HG_KW, HG_KW), 1)
    same_head = (r_i >> HEAD_SHIFT) == (c_i >> HEAD_SHIFT)
    ones_bd = jnp.where(same_head, 1.0, 0.0).astype(BF16)
    t_mod = lax.broadcasted_iota(jnp.int32, (HG_ROWS, HG_KW), 0) & (SUB - 1)
    lane_sub = lax.broadcasted_iota(jnp.int32, (HG_KW, HG_ROWS), 1) // SUB
    t_r = lax.broadcasted_iota(jnp.int32, (HG_ROWS, HG_ROWS), 0)
    t_c = lax.broadcasted_iota(jnp.int32, (HG_ROWS, HG_ROWS), 1)
    in_sub = (t_r // SUB) == (t_c // SUB)
    same_sub = jnp.where(in_sub, 1.0, 0.0).astype(BF16)
    upto_sub = jnp.where(jnp.logical_and(in_sub, t_c <= t_r), 1.0, 0.0).astype(BF16)

    def chunk(c, carry):
        rs = pl.ds(pl.multiple_of(c * HG_ROWS, HG_ROWS), HG_ROWS)
        q = q_ref[rs, :]
        sp = sp_ref[rs, :]
        v = i_ref[rs, :]
        lf = lf_ref[rs, :]
        kk = one_m_lb * jnp.exp(-sp)

        parts = _split3(lf)
        b = functools.reduce(lambda x, y: x + y, [_dot(upto_sub, part) for part in parts])
        beta = functools.reduce(lambda x, y: x + y, [_dot(same_sub, part) for part in parts])
        rest = beta - b

        b2 = b * LOG2E
        w2 = -sp * LOG2E - b2
        q_lb = q * one_m_lb
        acc = _dot((q * kk).astype(BF16), ones_bd) * v
        for d in range(1, SUB):
            e = jnp.where(t_mod >= d, b2 + pltpu.roll(w2, d, 0), NEG_BIG)
            p = q_lb * jnp.exp2(e)
            acc = acc + _dot(p.astype(BF16), ones_bd) * pltpu.roll(v, d, 0)

        qt = (q * jnp.exp2(b2)).astype(BF16)
        kt = (kk * jnp.exp(rest)).astype(BF16)
        v_t = v.T
        for j in range(HG_ROWS // SUB):
            st = st_scr[...]
            oi_scr[j * SUB:(j + 1) * SUB, :] = _nt_dot(qt[j * SUB:(j + 1) * SUB, :], st.astype(BF16))
            vm = jnp.where(lane_sub == j, v_t, 0.0).astype(BF16)
            ut = _dot(vm, kt)
            dec = jnp.exp(beta[j * SUB:j * SUB + 1, :])
            st_scr[...] = st * dec + jnp.where(same_head, ut, 0.0)

        o = acc + oi_scr[...]
        hi, lo = _split2(o * o)
        msq = (_dot(hi, ones_bd) + _dot(lo, ones_bd)) * (1.0 / HEAD_DIM)
        y = o * lax.rsqrt(msq + EPS) * gn
        o_ref[rs, :] = (y * gate_ref[rs, :]).astype(o_ref.dtype)
        return carry

    lax.fori_loop(0, tb // HG_ROWS, chunk, 0)


def _hgrn_call(layer, lb_logits, proj, gn, batch, seq, tb):
    m = proj.shape[0]
    nt = seq // tb

    def col(j):
        return pl.BlockSpec((tb, HG_KW), lambda b, t, j=j: (b * nt + t, j))

    return pl.pallas_call(
        functools.partial(_hgrn_kernel, layer=layer, tb=tb),
        grid=(batch, nt),
        in_specs=[pl.BlockSpec((DEPTH, HG_KW), lambda b, t: (0, 0)),
                  col(0), col(1), col(2), col(3), col(4), _layer(layer, 1, HG_WIDTH)],
        out_specs=pl.BlockSpec((tb, HG_WIDTH), lambda b, t: (b * nt + t, 0)),
        out_shape=jax.ShapeDtypeStruct((m, HG_WIDTH), BF16),
        scratch_shapes=[pltpu.VMEM((HG_WIDTH, HG_KW), F32),
                        pltpu.VMEM((HG_ROWS, HG_WIDTH), F32)],
        compiler_params=pltpu.CompilerParams(
            dimension_semantics=("arbitrary", "arbitrary"), vmem_limit_bytes=VMEM_LIMIT),
        name="hgrn2",
    )(lb_logits, proj, proj, proj, proj, proj, gn)


def _sb_kernel(q0_ref, q1_ref, q2_ref, k0_ref, k1_ref, k2_ref, v0_ref, v1_ref, v2_ref,
               o_ref, acc_scr, car_scr, *, bq, n_sub):
    g = pl.program_id(1)
    q_refs = (q0_ref, q1_ref, q2_ref)
    k_refs = (k0_ref, k1_ref, k2_ref)
    v_refs = (v0_ref, v1_ref, v2_ref)
    n_pairs = len(q_refs)

    lane_lo = lax.broadcasted_iota(jnp.int32, (bq, LANES), 1) < HEAD_DIM
    r_i = lax.broadcasted_iota(jnp.int32, (bq, bq), 0)
    c_i = lax.broadcasted_iota(jnp.int32, (bq, bq), 1)
    before = c_i < r_i
    later = jnp.where(r_i > c_i, 1.0, 0.0).astype(BF16)

    heads = range(2 * n_pairs)

    def q_block(sub, carry_unused):
        i = g * n_sub + sub
        qs = pl.ds(pl.multiple_of(sub * bq, bq), bq)
        qms = []
        for p in range(n_pairs):
            qn = q_refs[p][qs, :]
            zero = jnp.zeros_like(qn)
            qms.append(jnp.where(lane_lo, qn, zero))
            qms.append(jnp.where(lane_lo, zero, qn))

        def absorb(tiles, fresh):
            rows = [pl.ds(pl.multiple_of(kb * bq, bq), bq) for kb, _ in tiles]
            masks = [mask for _, mask in tiles]
            nt = range(len(tiles))
            zs = [[_nt_dot(qms[h], k_refs[h // 2][rows[t], :]) for h in heads] for t in nt]
            sps = [[_softplus2(z) for z in zt] for zt in zs]
            us = [spt if masks[t] is None else [jnp.where(masks[t], sp, 0.0) for sp in spt]
                  for t, spt in enumerate(sps)]
            tails = [[_dot(u.astype(BF16), later) for u in ut] for ut in us]
            low = None
            for h in heads:
                carry = None if fresh else car_scr[h]
                acc = None if fresh else acc_scr[h]
                for t in nt:
                    row = jnp.broadcast_to(jnp.sum(us[t][h], axis=1, keepdims=True), (bq, LANES))
                    e = zs[t][h] - sps[t][h] - tails[t][h]
                    if carry is not None:
                        e = e - jnp.concatenate([carry] * (bq // LANES), axis=1)
                        row = row + carry
                    a = jnp.exp2(e)
                    if masks[t] is not None:
                        a = jnp.where(masks[t], a, 0.0)
                    pv = _dot(a.astype(BF16), v_refs[h // 2][rows[t], :])
                    acc = pv if acc is None else acc + pv
                    carry = row
                acc_scr[h] = acc
                car_scr[h] = carry
                low = carry if low is None else jnp.minimum(low, carry)
            return jnp.min(low)

        low = lax.cond(i > 0,
                       lambda: absorb([(i, before), (i - 1, None)], True),
                       lambda: absorb([(i, before)], True))

        def live(st):
            return jnp.logical_and(st[0] >= 0, st[1] < -SB_ZERO_LOG * LOG2E)

        lax.while_loop(live, lambda st: (st[0] - 1, absorb([(st[0], None)], False)), (i - 2, low))
        for p in range(n_pairs):
            o_ref[qs, p * LANES:(p + 1) * LANES] = jnp.where(
                lane_lo, acc_scr[2 * p], acc_scr[2 * p + 1]).astype(o_ref.dtype)
        return carry_unused

    lax.fori_loop(0, n_sub, q_block, 0)


def _sb_call(q, k, v, batch, seq, bq, n_sub):
    m = q.shape[0]
    rows = bq * n_sub
    nq = seq // rows
    n_pairs = SB_WIDTH // LANES
    q_spec = lambda p: pl.BlockSpec((rows, LANES), lambda b, i, p=p: (b * nq + i, p))
    k_spec = lambda p: pl.BlockSpec((seq, LANES), lambda b, i, p=p: (b, p))
    pairs = range(n_pairs)
    return pl.pallas_call(
        functools.partial(_sb_kernel, bq=bq, n_sub=n_sub),
        grid=(batch, nq),
        in_specs=[q_spec(p) for p in pairs] + [k_spec(p) for p in pairs] * 2,
        out_specs=pl.BlockSpec((rows, SB_WIDTH), lambda b, i: (b * nq + i, 0)),
        out_shape=jax.ShapeDtypeStruct((m, SB_WIDTH), BF16),
        scratch_shapes=[pltpu.VMEM((2 * n_pairs, bq, LANES), F32),
                        pltpu.VMEM((2 * n_pairs, bq, LANES), F32)],
        compiler_params=pltpu.CompilerParams(
            dimension_semantics=("arbitrary", "arbitrary"), vmem_limit_bytes=VMEM_LIMIT),
        name="stickbreak",
    )(*([q] * n_pairs + [k] * n_pairs + [v] * n_pairs))


def _fox_kernel(q_ref, k_ref, v_ref, f_ref, fb_ref, o_ref,
                c_scr, qa_scr, ka_scr, vt_scr, sa_scr, sb_scr, mb_scr, *, bq, bk, seq):
    p_idx = pl.program_id(1)

    @pl.when(p_idx == 0)
    def _():
        cb = min(bk, 2 * LANES)
        r_i = lax.broadcasted_iota(jnp.int32, (cb, cb), 0)
        c_i = lax.broadcasted_iota(jnp.int32, (cb, cb), 1)
        upto = jnp.where(c_i <= r_i, 1.0, 0.0).astype(BF16)

        per_trip = max(u for u in (4, 2, 1) if (seq // cb) % u == 0)

        def body(n, run):
            for j in range(per_trip):
                rs = pl.ds(pl.multiple_of((n * per_trip + j) * cb, cb), cb)
                y = f_ref[rs, :] + fb_ref[...]
                lg = jnp.minimum(y, 0.0) - jnp.log(1.0 + jnp.exp(-jnp.abs(y)))
                hi, mid, lo = _split3(lg)
                c = _dot(upto, hi) + _dot(upto, mid) + _dot(upto, lo) + run
                c_scr[rs, :] = c
                run = c[cb - 1:cb, :]
            return run

        lax.fori_loop(0, seq // (cb * per_trip), body, jnp.zeros((1, LANES), F32))

    lane = lax.broadcasted_iota(jnp.int32, (bk, LANES), 1)
    lane_lo = lane < HEAD_DIM
    row_lo_k = lax.broadcasted_iota(jnp.int32, (LANES, bk), 0) < HEAD_DIM

    pr = lax.broadcasted_iota(jnp.int32, (LANES, LANES), 0)
    pc = lax.broadcasted_iota(jnp.int32, (LANES, LANES), 1)
    place, ones_q, ones_k = [], [], []
    for h in range(2):
        base = HEAD_DIM * (1 - h)
        place.append([jnp.where(jnp.logical_and(pr == 2 * p_idx + h,
                                                jnp.logical_or(pc == base + n, pc == base + 3 + n)),
                                1.0, 0.0).astype(BF16) for n in range(3)])
        ones_q.append(jnp.logical_and(lane >= base + 3, lane < base + 6))
        ones_k.append(jnp.logical_and(lane >= base, lane < base + 3))

    def prep(n, carry):
        rs = pl.ds(pl.multiple_of(n * bk, bk), bk)
        kn = k_ref[rs, :]
        qn = q_ref[rs, :]
        parts = _split3(c_scr[rs, :] * LOG2E)
        v_t = v_ref[rs, :].T
        for h in range(2):
            c_aug = functools.reduce(lambda x, y: x + y,
                                     [_dot(part, sel) for part, sel in zip(parts, place[h])])
            mine = lane_lo if h == 0 else jnp.logical_not(lane_lo)
            k_aug = jnp.where(ones_k[h], 1.0, -c_aug)
            q_aug = jnp.where(ones_q[h], 1.0, c_aug)
            ka_scr[h, rs, :] = jnp.where(mine, kn, k_aug.astype(BF16))
            qa_scr[h, rs, :] = jnp.where(mine, qn, q_aug.astype(BF16))
            vt_scr[h, n] = jnp.where(row_lo_k if h == 0 else jnp.logical_not(row_lo_k), v_t,
                                     jnp.ones_like(v_t))
        return carry

    chunks = max(u for u in (4, 2, 1) if (seq // bk) % u == 0)

    def prep_trip(t, carry):
        for j in range(chunks):
            prep(t * chunks + j, carry)
        return carry

    lax.fori_loop(0, seq // (bk * chunks), prep_trip, 0)

    key_minus_qry = (lax.broadcasted_iota(jnp.int32, (bk, bq), 0)
                     - lax.broadcasted_iota(jnp.int32, (bk, bq), 1))
    row_lo = lax.broadcasted_iota(jnp.int32, (LANES, bq), 0) < HEAD_DIM

    nq = seq // bq
    n_items = sum((qi * bq + bq - 1) // bk + 1 for qi in range(nq))
    mb_scr[0] = jnp.zeros((bk, bq), F32)
    for r in range(bk // bq):
        mb_scr[r + 1] = jnp.where(key_minus_qry <= r * bq, 0.0, NEG_BIG)

    def unpack(item):
        qi = jnp.minimum(item[0], nq - 1)
        last = (qi * bq + bq - 1) // bk
        first = item[1] == 0
        kb = jnp.where(first, last, item[1] - 1)
        return qi, last, first, kb

    def advance(item):
        qi, last, _, _ = unpack(item)
        wrap = item[1] >= last
        return jnp.where(wrap, item[0] + 1, item[0]), jnp.where(wrap, 0, item[1] + 1)

    def scores_into(dst, item):
        qi, _, first, kb = unpack(item)
        qs = pl.ds(pl.multiple_of(qi * bq, bq), bq)
        rs = pl.ds(pl.multiple_of(kb * bk, bk), bk)
        bias = mb_scr[jnp.where(first, 1 + (qi * bq - kb * bk) // bq, 0)]
        for h in range(2):
            dst[h] = _nt_dot(ka_scr[h, rs, :], qa_scr[h, qs, :]) + bias

    def write_out(qi, sts):
        acc0, acc1 = sts[0][1], sts[1][1]
        o_t = jnp.where(row_lo, acc0 / acc0[HEAD_DIM:HEAD_DIM + 1, :], acc1 / acc1[0:1, :])
        o_ref[pl.ds(pl.multiple_of(qi * bq, bq), bq), :] = o_t.T.astype(o_ref.dtype)

    def absorb(src, item, prev_qi, sts):
        qi, _, first, kb = unpack(item)
        write_out(prev_qi, sts)
        out = []
        for h in range(2):
            m_run, acc = sts[h]
            m_run = jnp.where(first, NEG_BIG, m_run)
            s = src[h]
            m_new = jnp.maximum(m_run, jnp.max(s, axis=0, keepdims=True))
            p = jnp.exp2(s - m_new)
            acc = jnp.exp2(m_run - m_new) * acc + _dot(vt_scr[h, kb], p.astype(BF16))
            out.append((m_new, acc))
        return qi, tuple(out)

    unroll = max([u for u in (24, 12, 8, 4) if u <= n_items and u * bk <= FOX_KEYS_PER_TRIP] + [2])
    bufs = (sa_scr, sb_scr)

    def run(count, ca):
        held, prev_qi, sts = ca
        for k in range(count):
            nxt = advance(held)
            scores_into(bufs[(k + 1) % 2], nxt)
            prev_qi, sts = absorb(bufs[k % 2], held, prev_qi, sts)
            held = nxt
        return held, prev_qi, sts

    start = (jnp.int32(0), jnp.int32(0))
    init = (jnp.full((1, bq), NEG_BIG, F32), jnp.ones((LANES, bq), F32))
    scores_into(sa_scr, start)
    ca = lax.fori_loop(0, n_items // unroll, lambda u, ca: run(unroll, ca),
                       (start, jnp.int32(0), (init, init)))
    _, last_qi, sts = run(n_items % unroll, ca)
    write_out(last_qi, sts)


def _fox_call(q, k, v, ff, fbias, l, batch, seq, bq, bk):
    m = q.shape[0]
    col = pl.BlockSpec((seq, LANES), lambda b, p: (b, p))
    return pl.pallas_call(
        functools.partial(_fox_kernel, bq=bq, bk=bk, seq=seq),
        grid=(batch, FOX_WIDTH // LANES),
        in_specs=[col, col, col, pl.BlockSpec((seq, LANES), lambda b, p: (b, 0)), _layer(l, 1, LANES)],
        out_specs=pl.BlockSpec((seq, LANES), lambda b, p: (b, p)),
        out_shape=jax.ShapeDtypeStruct((m, FOX_WIDTH), BF16),
        scratch_shapes=[pltpu.VMEM((seq, LANES), F32),
                        pltpu.VMEM((2, seq, LANES), BF16),
                        pltpu.VMEM((2, seq, LANES), BF16),
                        pltpu.VMEM((2, seq // bk, LANES, bk), BF16),
                        pltpu.VMEM((2, bk, bq), F32),
                        pltpu.VMEM((2, bk, bq), F32),
                        pltpu.VMEM((bk // bq + 1, bk, bq), F32)],
        compiler_params=pltpu.CompilerParams(
            dimension_semantics=("arbitrary", "arbitrary"), vmem_limit_bytes=VMEM_LIMIT),
        name="forgetting",
    )(q, k, v, ff, fbias)


def _mlp_kernel(x_ref, ohg_ref, osb_ref, ofx_ref, wo1_ref, wo2_ref, wo3_ref, g2_ref,
                w1_ref, w2_ref, o_ref, *, tf):
    x1 = (x_ref[...] + _dot(ohg_ref[...], wo1_ref[...]) + _dot(osb_ref[...], wo2_ref[...])
          + _dot(ofx_ref[...], wo3_ref[...]))
    ms = jnp.mean(x1 * x1, axis=-1, keepdims=True)
    h2 = (x1 * lax.rsqrt(ms + EPS) * g2_ref[...]).astype(BF16)
    o_ref[...] = x1
    acc = None
    for f in range(w1_ref.shape[1] // tf):
        a = _dot(h2, w1_ref[:, f * tf:(f + 1) * tf])
        a = jnp.square(jnp.maximum(a, 0.0)).astype(BF16)
        y = _dot(a, w2_ref[f * tf:(f + 1) * tf, :])
        acc = y if acc is None else acc + y
    o_ref[...] = o_ref[...] + acc


def _mlp_call(x, ohg, osb, ofx, wo1, wo2, wo3, g2, w1, w2, l, tm, tf):
    m, d = x.shape
    row = lambda w: pl.BlockSpec((tm, w), lambda i: (i, 0))
    whole = lambda a: _layer(l, *a.shape[1:])
    return pl.pallas_call(
        functools.partial(_mlp_kernel, tf=tf),
        grid=(m // tm,),
        in_specs=[row(d), row(HG_WIDTH), row(SB_WIDTH), row(FOX_WIDTH),
                  whole(wo1), whole(wo2), whole(wo3), whole(g2), whole(w1), whole(w2)],
        out_specs=row(d),
        out_shape=jax.ShapeDtypeStruct((m, d), F32),
        compiler_params=pltpu.CompilerParams(
            dimension_semantics=("arbitrary",), vmem_limit_bytes=VMEM_LIMIT),
        name="outproj_mlp",
    )(x, ohg, osb, ofx, wo1, wo2, wo3, g2, w1, w2)


def _tile_sizes(batch, seq):
    m = batch * seq
    return dict(
        proj_tm=min(512, m),
        hg_tb=min(2048, seq),
        sb_bq=min(256, seq),
        sb_sub=max(1, seq // 256),
        fox_bq=min(256, seq),
        fox_bk=min(512, seq),
        mlp_tm=min(512, m),
        mlp_tf=1024,
    )


def kernel(x, lb_logits, norm1_g, w_in, hg_norm_g, sb_q_norm_g, sb_k_norm_g, fox_q_norm_g,
           fox_k_norm_g, fox_f_bias, w_out, norm2_g, w_ff1, w_ff2):
    batch, seq, d = x.shape
    assert d == D_MODEL and seq % HG_ROWS == 0 and x.dtype == F32
    ts = _tile_sizes(batch, seq)
    assert seq % (ts["sb_bq"] * ts["sb_sub"]) == 0 and seq % ts["fox_bk"] == 0 and ts["fox_bk"] % ts["fox_bq"] == 0
    assert seq % ts["hg_tb"] == 0
    m = batch * seq
    xf = x.reshape(m, d)
    row = lambda a: a.astype(F32)[:, None, :]
    pair = lambda g: row(jnp.tile(g, (1, 2)))
    w_in_p = jnp.pad(w_in, ((0, 0), (0, 0), (0, IN_COLS_PAD - IN_COLS))).astype(BF16)
    wo = w_out.astype(BF16)
    wo_hg, wo_sb, wo_fx = wo[:, :HG_WIDTH], wo[:, HG_WIDTH:HG_WIDTH + SB_WIDTH], wo[:, HG_WIDTH + SB_WIDTH:]
    w1, w2 = w_ff1.astype(BF16), w_ff2.astype(BF16)
    g1, g2, g_hg = row(norm1_g), row(norm2_g), row(jnp.tile(hg_norm_g, (1, HG_HEADS)))
    qk_gains = (pair(sb_q_norm_g), pair(sb_k_norm_g), pair(fox_q_norm_g), pair(fox_k_norm_g))
    fbias = row(jnp.pad(fox_f_bias, ((0, 0), (0, LANES - FOX_HEADS))))
    lbl = lb_logits.astype(F32)

    for l in range(DEPTH):
        hg, ff, sq, sk, sv, fq, fk, fv = _proj_call(xf, g1, w_in_p, lbl, qk_gains, l, ts["proj_tm"])
        o_hg = _hgrn_call(l, lbl, hg, g_hg, batch, seq, ts["hg_tb"])
        o_sb = _sb_call(sq, sk, sv, batch, seq, ts["sb_bq"], ts["sb_sub"])
        o_fx = _fox_call(fq, fk, fv, ff, fbias, l, batch, seq, ts["fox_bq"], ts["fox_bk"])
        xf = _mlp_call(xf, o_hg, o_sb, o_fx, wo_hg, wo_sb, wo_fx, g2, w1, w2, l,
                       ts["mlp_tm"], ts["mlp_tf"])
    return xf.reshape(batch, seq, d)
```

```python
import functools
import math

import jax
import jax.numpy as jnp
from jax import lax
from jax.experimental import pallas as pl
from jax.experimental.pallas import tpu as pltpu

F32 = jnp.float32
BF16 = jnp.bfloat16

D_MODEL = 1024
DEPTH = 4
HEAD_DIM = 64
HEAD_SHIFT = HEAD_DIM.bit_length() - 1
HG_HEADS = 4
HG_KW = 256
HG_WIDTH = 256
SB_WIDTH = 384
FOX_WIDTH = 384
FOX_HEADS = 6
IN_COLS = 4 * 256 + 3 * 384 + 3 * 384 + FOX_HEADS
D_FF = 4 * D_MODEL
EPS = 1e-6
LB_FLOOR = 1e-30
NEG_BIG = -1e30
LOG2E = math.log2(math.e)

LANES = 128
IN_COLS_PAD = 27 * LANES
SB_Q_BLK, SB_K_BLK, SB_V_BLK = 8, 11, 14
FOX_Q_BLK, FOX_K_BLK, FOX_V_BLK, FOX_F_BLK = 17, 20, 23, 26

SB_ZERO_LOG = -105.0
SUB = 16
HG_ROWS = 256
FOX_KEYS_PER_TRIP = 18432
VMEM_LIMIT = 56 * 1024 * 1024


def _nt_dot(a, b):
    return lax.dot_general(a, b, (((1,), (1,)), ((), ())), preferred_element_type=F32)


def _dot(a, b):
    return jnp.dot(a, b, preferred_element_type=F32)


def _split2(x):
    hi = x.astype(BF16)
    lo = (x - hi.astype(F32)).astype(BF16)
    return hi, lo


def _split3(x):
    hi = x.astype(BF16)
    r1 = x - hi.astype(F32)
    mid = r1.astype(BF16)
    lo = (r1 - mid.astype(F32)).astype(BF16)
    return hi, mid, lo


def _softplus(z):
    return jnp.maximum(z, 0.0) + jnp.log(1.0 + jnp.exp(-jnp.abs(z)))


def _softplus2(z2):
    return jnp.maximum(z2, 0.0) + jnp.log2(1.0 + jnp.exp2(-jnp.abs(z2)))


def _pair_rms(x, gain, lane_lo):
    x2 = x * x
    s0 = jnp.sum(jnp.where(lane_lo, x2, 0.0), axis=1, keepdims=True)
    s1 = jnp.sum(jnp.where(lane_lo, 0.0, x2), axis=1, keepdims=True)
    ms = jnp.where(lane_lo, s0, s1) * (1.0 / HEAD_DIM)
    return x * lax.rsqrt(ms + EPS) * gain


def _layer(l, *shape):
    return pl.BlockSpec((None,) + shape, lambda *_: (l,) + (0,) * len(shape))


def _lower_bound(lbl_ref, layer):
    rows = [lbl_ref[j:j + 1, :] for j in range(DEPTH)]
    mx = functools.reduce(jnp.maximum, rows)
    ex = [jnp.exp(r - mx) for r in rows]
    den = functools.reduce(lambda a, b: a + b, ex)
    lb = jnp.zeros_like(mx)
    for j in range(1, layer + 1):
        lb = lb + ex[j] / den
    return lb


def _proj_kernel(x_ref, g_ref, w_ref, lbl_ref, gsq_ref, gsk_ref, gfq_ref, gfk_ref,
                 hg_ref, ff_ref, sq_ref, sk_ref, sv_ref, fq_ref, fk_ref, fv_ref, *, layer):
    x = x_ref[...]
    ms = jnp.mean(x * x, axis=-1, keepdims=True)
    h = (x * lax.rsqrt(ms + EPS) * g_ref[...]).astype(BF16)
    y = _dot(h, w_ref[...])
    fl = y[:, HG_KW:2 * HG_KW]
    gt = y[:, 3 * HG_KW:4 * HG_KW]
    log_lb = jnp.log(jnp.maximum(_lower_bound(lbl_ref, layer), LB_FLOOR))
    sp = _softplus(fl)
    hg_ref[:, :HG_KW] = y[:, :HG_KW]
    hg_ref[:, HG_KW:2 * HG_KW] = sp
    hg_ref[:, 2 * HG_KW:3 * HG_KW] = y[:, 2 * HG_KW:3 * HG_KW]
    hg_ref[:, 3 * HG_KW:4 * HG_KW] = gt / (1.0 + jnp.exp(-gt))
    hg_ref[:, 4 * HG_KW:] = (jnp.maximum(fl, log_lb) + jnp.log(1.0 + jnp.exp(-jnp.abs(fl - log_lb)))
                             - sp)
    ff_ref[...] = y[:, FOX_F_BLK * LANES:]
    lane_lo = lax.broadcasted_iota(jnp.int32, (x.shape[0], LANES), 1) < HEAD_DIM
    q_mult = LOG2E / math.sqrt(HEAD_DIM)
    groups = ((SB_Q_BLK, gsq_ref, q_mult, sq_ref), (SB_K_BLK, gsk_ref, None, sk_ref),
              (SB_V_BLK, None, None, sv_ref),
              (FOX_Q_BLK, gfq_ref, q_mult, fq_ref), (FOX_K_BLK, gfk_ref, None, fk_ref),
              (FOX_V_BLK, None, None, fv_ref))
    for blk, gain_ref, mult, out_ref in groups:
        for p in range(SB_WIDTH // LANES):
            t = y[:, (blk + p) * LANES:(blk + p + 1) * LANES]
            if gain_ref is not None:
                t = _pair_rms(t, gain_ref[...], lane_lo)
            if mult is not None:
                t = t * mult
            out_ref[:, p * LANES:(p + 1) * LANES] = t.astype(BF16)


def _proj_call(x, g, w, lbl, gains, l, tm):
    m, d = x.shape
    n = w.shape[2]
    rows = lambda width: pl.BlockSpec((tm, width), lambda i: (i, 0))
    widths = (5 * HG_KW, LANES) + (SB_WIDTH,) * 6
    dtypes = (F32, F32) + (BF16,) * 6
    return pl.pallas_call(
        functools.partial(_proj_kernel, layer=l),
        grid=(m // tm,),
        in_specs=([rows(d), _layer(l, 1, d), _layer(l, d, n),
                   pl.BlockSpec((DEPTH, HG_KW), lambda i: (0, 0))] + [_layer(l, 1, LANES)] * 4),
        out_specs=[rows(wd) for wd in widths],
        out_shape=[jax.ShapeDtypeStruct((m, wd), dt) for wd, dt in zip(widths, dtypes)],
        compiler_params=pltpu.CompilerParams(
            dimension_semantics=("arbitrary",), vmem_limit_bytes=VMEM_LIMIT),
        name="proj",
    )(x, g, w, lbl, *gains)


def _hgrn_kernel(lbl_ref, q_ref, sp_ref, i_ref, gate_ref, lf_ref, gn_ref, o_ref, st_scr, oi_scr,
                 *, layer, tb):
    @pl.when(pl.program_id(1) == 0)
    def _():
        st_scr[...] = jnp.zeros_like(st_scr)

    one_m_lb = 1.0 - _lower_bound(lbl_ref, layer)
    gn = gn_ref[...]

    r_i = lax.broadcasted_iota(jnp.int32, (HG_KW, HG_KW), 0)
    c_i = lax.broadcasted_iota(jnp.int32, (HG_KW, HG_KW), 1)
    same_head = (r_i >> HEAD_SHIFT) == (c_i >> HEAD_SHIFT)
    ones_bd = jnp.where(same_head, 1.0, 0.0).astype(BF16)
    t_mod = lax.broadcasted_iota(jnp.int32, (HG_ROWS, HG_KW), 0) & (SUB - 1)
    lane_sub = lax.broadcasted_iota(jnp.int32, (HG_KW, HG_ROWS), 1) // SUB
    t_r = lax.broadcasted_iota(jnp.int32, (HG_ROWS, HG_ROWS), 0)
    t_c = lax.broadcasted_iota(jnp.int32, (HG_ROWS, HG_ROWS), 1)
    in_sub = (t_r // SUB) == (t_c // SUB)
    same_sub = jnp.where(in_sub, 1.0, 0.0).astype(BF16)
    upto_sub = jnp.where(jnp.logical_and(in_sub, t_c <= t_r), 1.0, 0.0).astype(BF16)

    def chunk(c, carry):
        rs = pl.ds(pl.multiple_of(c * HG_ROWS, HG_ROWS), HG_ROWS)
        q = q_ref[rs, :]
        sp = sp_ref[rs, :]
        v = i_ref[rs, :]
        lf = lf_ref[rs, :]
        kk = one_m_lb * jnp.exp(-sp)

        parts = _split3(lf)
        b = functools.reduce(lambda x, y: x + y, [_dot(upto_sub, part) for part in parts])
        beta = functools.reduce(lambda x, y: x + y, [_dot(same_sub, part) for part in parts])
        rest = beta - b

        b2 = b * LOG2E
        w2 = -sp * LOG2E - b2
        q_lb = q * one_m_lb
        acc = _dot((q * kk).astype(BF16), ones_bd) * v
        for d in range(1, SUB):
            e = jnp.where(t_mod >= d, b2 + pltpu.roll(w2, d, 0), NEG_BIG)
            p = q_lb * jnp.exp2(e)
            acc = acc + _dot(p.astype(BF16), ones_bd) * pltpu.roll(v, d, 0)

        qt = (q * jnp.exp2(b2)).astype(BF16)
        kt = (kk * jnp.exp(rest)).astype(BF16)
        v_t = v.T
        for j in range(HG_ROWS // SUB):
            st = st_scr[...]
            oi_scr[j * SUB:(j + 1) * SUB, :] = _nt_dot(qt[j * SUB:(j + 1) * SUB, :], st.astype(BF16))
            vm = jnp.where(lane_sub == j, v_t, 0.0).astype(BF16)
            ut = _dot(vm, kt)
            dec = jnp.exp(beta[j * SUB:j * SUB + 1, :])
            st_scr[...] = st * dec + jnp.where(same_head, ut, 0.0)

        o = acc + oi_scr[...]
        hi, lo = _split2(o * o)
        msq = (_dot(hi, ones_bd) + _dot(lo, ones_bd)) * (1.0 / HEAD_DIM)
        y = o * lax.rsqrt(msq + EPS) * gn
        o_ref[rs, :] = (y * gate_ref[rs, :]).astype(o_ref.dtype)
        return carry

    lax.fori_loop(0, tb // HG_ROWS, chunk, 0)


def _hgrn_call(layer, lb_logits, proj, gn, batch, seq, tb):
    m = proj.shape[0]
    nt = seq // tb

    def col(j):
        return pl.BlockSpec((tb, HG_KW), lambda b, t, j=j: (b * nt + t, j))

    return pl.pallas_call(
        functools.partial(_hgrn_kernel, layer=layer, tb=tb),
        grid=(batch, nt),
        in_specs=[pl.BlockSpec((DEPTH, HG_KW), lambda b, t: (0, 0)),
                  col(0), col(1), col(2), col(3), col(4), _layer(layer, 1, HG_WIDTH)],
        out_specs=pl.BlockSpec((tb, HG_WIDTH), lambda b, t: (b * nt + t, 0)),
        out_shape=jax.ShapeDtypeStruct((m, HG_WIDTH), BF16),
        scratch_shapes=[pltpu.VMEM((HG_WIDTH, HG_KW), F32),
                        pltpu.VMEM((HG_ROWS, HG_WIDTH), F32)],
        compiler_params=pltpu.CompilerParams(
            dimension_semantics=("arbitrary", "arbitrary"), vmem_limit_bytes=VMEM_LIMIT),
        name="hgrn2",
    )(lb_logits, proj, proj, proj, proj, proj, gn)


def _sb_kernel(q0_ref, q1_ref, q2_ref, k0_ref, k1_ref, k2_ref, v0_ref, v1_ref, v2_ref,
               o_ref, acc_scr, car_scr, *, bq, n_sub):
    g = pl.program_id(1)
    q_refs = (q0_ref, q1_ref, q2_ref)
    k_refs = (k0_ref, k1_ref, k2_ref)
    v_refs = (v0_ref, v1_ref, v2_ref)
    n_pairs = len(q_refs)

    lane_lo = lax.broadcasted_iota(jnp.int32, (bq, LANES), 1) < HEAD_DIM
    r_i = lax.broadcasted_iota(jnp.int32, (bq, bq), 0)
    c_i = lax.broadcasted_iota(jnp.int32, (bq, bq), 1)
    before = c_i < r_i
    later = jnp.where(r_i > c_i, 1.0, 0.0).astype(BF16)

    heads = range(2 * n_pairs)

    def q_block(sub, carry_unused):
        i = g * n_sub + sub
        qs = pl.ds(pl.multiple_of(sub * bq, bq), bq)
        qms = []
        for p in range(n_pairs):
            qn = q_refs[p][qs, :]
            zero = jnp.zeros_like(qn)
            qms.append(jnp.where(lane_lo, qn, zero))
            qms.append(jnp.where(lane_lo, zero, qn))

        def absorb(tiles, fresh):
            rows = [pl.ds(pl.multiple_of(kb * bq, bq), bq) for kb, _ in tiles]
            masks = [mask for _, mask in tiles]
            nt = range(len(tiles))
            zs = [[_nt_dot(qms[h], k_refs[h // 2][rows[t], :]) for h in heads] for t in nt]
            sps = [[_softplus2(z) for z in zt] for zt in zs]
            us = [spt if masks[t] is None else [jnp.where(masks[t], sp, 0.0) for sp in spt]
                  for t, spt in enumerate(sps)]
            tails = [[_dot(u.astype(BF16), later) for u in ut] for ut in us]
            low = None
            for h in heads:
                carry = None if fresh else car_scr[h]
                acc = None if fresh else acc_scr[h]
                for t in nt:
                    row = jnp.broadcast_to(jnp.sum(us[t][h], axis=1, keepdims=True), (bq, LANES))
                    e = zs[t][h] - sps[t][h] - tails[t][h]
                    if carry is not None:
                        e = e - jnp.concatenate([carry] * (bq // LANES), axis=1)
                        row = row + carry
                    a = jnp.exp2(e)
                    if masks[t] is not None:
                        a = jnp.where(masks[t], a, 0.0)
                    pv = _dot(a.astype(BF16), v_refs[h // 2][rows[t], :])
                    acc = pv if acc is None else acc + pv
                    carry = row
                acc_scr[h] = acc
                car_scr[h] = carry
                low = carry if low is None else jnp.minimum(low, carry)
            return jnp.min(low)

        low = lax.cond(i > 0,
                       lambda: absorb([(i, before), (i - 1, None)], True),
                       lambda: absorb([(i, before)], True))

        def live(st):
            return jnp.logical_and(st[0] >= 0, st[1] < -SB_ZERO_LOG * LOG2E)

        lax.while_loop(live, lambda st: (st[0] - 1, absorb([(st[0], None)], False)), (i - 2, low))
        for p in range(n_pairs):
            o_ref[qs, p * LANES:(p + 1) * LANES] = jnp.where(
                lane_lo, acc_scr[2 * p], acc_scr[2 * p + 1]).astype(o_ref.dtype)
        return carry_unused

    lax.fori_loop(0, n_sub, q_block, 0)


def _sb_call(q, k, v, batch, seq, bq, n_sub):
    m = q.shape[0]
    rows = bq * n_sub
    nq = seq // rows
    n_pairs = SB_WIDTH // LANES
    q_spec = lambda p: pl.BlockSpec((rows, LANES), lambda b, i, p=p: (b * nq + i, p))
    k_spec = lambda p: pl.BlockSpec((seq, LANES), lambda b, i, p=p: (b, p))
    pairs = range(n_pairs)
    return pl.pallas_call(
        functools.partial(_sb_kernel, bq=bq, n_sub=n_sub),
        grid=(batch, nq),
        in_specs=[q_spec(p) for p in pairs] + [k_spec(p) for p in pairs] * 2,
        out_specs=pl.BlockSpec((rows, SB_WIDTH), lambda b, i: (b * nq + i, 0)),
        out_shape=jax.ShapeDtypeStruct((m, SB_WIDTH), BF16),
        scratch_shapes=[pltpu.VMEM((2 * n_pairs, bq, LANES), F32),
                        pltpu.VMEM((2 * n_pairs, bq, LANES), F32)],
        compiler_params=pltpu.CompilerParams(
            dimension_semantics=("arbitrary", "arbitrary"), vmem_limit_bytes=VMEM_LIMIT),
        name="stickbreak",
    )(*([q] * n_pairs + [k] * n_pairs + [v] * n_pairs))


def _fox_kernel(q_ref, k_ref, v_ref, f_ref, fb_ref, o_ref,
                c_scr, qa_scr, ka_scr, vt_scr, sa_scr, sb_scr, mb_scr, *, bq, bk, seq):
    p_idx = pl.program_id(1)

    @pl.when(p_idx == 0)
    def _():
        cb = min(bk, 2 * LANES)
        r_i = lax.broadcasted_iota(jnp.int32, (cb, cb), 0)
        c_i = lax.broadcasted_iota(jnp.int32, (cb, cb), 1)
        upto = jnp.where(c_i <= r_i, 1.0, 0.0).astype(BF16)

        per_trip = max(u for u in (8, 4, 2, 1) if (seq // cb) % u == 0)

        def body(n, run):
            for j in range(per_trip):
                rs = pl.ds(pl.multiple_of((n * per_trip + j) * cb, cb), cb)
                y = f_ref[rs, :] + fb_ref[...]
                lg = jnp.minimum(y, 0.0) - jnp.log(1.0 + jnp.exp(-jnp.abs(y)))
                hi, mid, lo = _split3(lg)
                c = _dot(upto, hi) + _dot(upto, mid) + _dot(upto, lo) + run
                c_scr[rs, :] = c
                run = c[cb - 1:cb, :]
            return run

        lax.fori_loop(0, seq // (cb * per_trip), body, jnp.zeros((1, LANES), F32))

    lane = lax.broadcasted_iota(jnp.int32, (bk, LANES), 1)
    lane_lo = lane < HEAD_DIM
    row_lo_k = lax.broadcasted_iota(jnp.int32, (LANES, bk), 0) < HEAD_DIM

    pr = lax.broadcasted_iota(jnp.int32, (LANES, LANES), 0)
    pc = lax.broadcasted_iota(jnp.int32, (LANES, LANES), 1)
    place, ones_q, ones_k = [], [], []
    for h in range(2):
        base = HEAD_DIM * (1 - h)
        place.append([jnp.where(jnp.logical_and(pr == 2 * p_idx + h,
                                                jnp.logical_or(pc == base + n, pc == base + 3 + n)),
                                1.0, 0.0).astype(BF16) for n in range(3)])
        ones_q.append(jnp.logical_and(lane >= base + 3, lane < base + 6))
        ones_k.append(jnp.logical_and(lane >= base, lane < base + 3))

    def prep(n, carry):
        rs = pl.ds(pl.multiple_of(n * bk, bk), bk)
        kn = k_ref[rs, :]
        qn = q_ref[rs, :]
        parts = _split3(c_scr[rs, :] * LOG2E)
        v_t = v_ref[rs, :].T
        for h in range(2):
            c_aug = functools.reduce(lambda x, y: x + y,
                                     [_dot(part, sel) for part, sel in zip(parts, place[h])])
            mine = lane_lo if h == 0 else jnp.logical_not(lane_lo)
            k_aug = jnp.where(ones_k[h], 1.0, -c_aug)
            q_aug = jnp.where(ones_q[h], 1.0, c_aug)
            ka_scr[h, rs, :] = jnp.where(mine, kn, k_aug.astype(BF16))
            qa_scr[h, rs, :] = jnp.where(mine, qn, q_aug.astype(BF16))
            vt_scr[h, n] = jnp.where(row_lo_k if h == 0 else jnp.logical_not(row_lo_k), v_t,
                                     jnp.ones_like(v_t))
        return carry

    chunks = max(u for u in (8, 4, 2, 1) if (seq // bk) % u == 0)

    def prep_trip(t, carry):
        for j in range(chunks):
            prep(t * chunks + j, carry)
        return carry

    lax.fori_loop(0, seq // (bk * chunks), prep_trip, 0)

    key_minus_qry = (lax.broadcasted_iota(jnp.int32, (bk, bq), 0)
                     - lax.broadcasted_iota(jnp.int32, (bk, bq), 1))
    row_lo = lax.broadcasted_iota(jnp.int32, (LANES, bq), 0) < HEAD_DIM

    nq = seq // bq
    n_items = sum((qi * bq + bq - 1) // bk + 1 for qi in range(nq))
    mb_scr[0] = jnp.zeros((bk, bq), F32)
    for r in range(bk // bq):
        mb_scr[r + 1] = jnp.where(key_minus_qry <= r * bq, 0.0, NEG_BIG)

    def unpack(item):
        qi = jnp.minimum(item[0], nq - 1)
        last = (qi * bq + bq - 1) // bk
        first = item[1] == 0
        kb = jnp.where(first, last, item[1] - 1)
        return qi, last, first, kb

    def advance(item):
        qi, last, _, _ = unpack(item)
        wrap = item[1] >= last
        return jnp.where(wrap, item[0] + 1, item[0]), jnp.where(wrap, 0, item[1] + 1)

    def scores_into(dst, item):
        qi, _, first, kb = unpack(item)
        qs = pl.ds(pl.multiple_of(qi * bq, bq), bq)
        rs = pl.ds(pl.multiple_of(kb * bk, bk), bk)
        bias = mb_scr[jnp.where(first, 1 + (qi * bq - kb * bk) // bq, 0)]
        for h in range(2):
            dst[h] = _nt_dot(ka_scr[h, rs, :], qa_scr[h, qs, :]) + bias

    def write_out(qi, sts):
        acc0, acc1 = sts[0][1], sts[1][1]
        o_t = jnp.where(row_lo, acc0 / acc0[HEAD_DIM:HEAD_DIM + 1, :], acc1 / acc1[0:1, :])
        o_ref[pl.ds(pl.multiple_of(qi * bq, bq), bq), :] = o_t.T.astype(o_ref.dtype)

    def absorb(src, item, prev_qi, sts):
        qi, _, first, kb = unpack(item)
        write_out(prev_qi, sts)
        out = []
        for h in range(2):
            m_run, acc = sts[h]
            m_run = jnp.where(first, NEG_BIG, m_run)
            s = src[h]
            m_new = jnp.maximum(m_run, jnp.max(s, axis=0, keepdims=True))
            p = jnp.exp2(s - m_new)
            acc = jnp.exp2(m_run - m_new) * acc + _dot(vt_scr[h, kb], p.astype(BF16))
            out.append((m_new, acc))
        return qi, tuple(out)

    unroll = max([u for u in (36, 24, 12, 8, 4) if u <= n_items and u * bk <= FOX_KEYS_PER_TRIP] + [2])
    bufs = (sa_scr, sb_scr)

    def run(count, ca):
        held, prev_qi, sts = ca
        for k in range(count):
            nxt = advance(held)
            scores_into(bufs[(k + 1) % 2], nxt)
            prev_qi, sts = absorb(bufs[k % 2], held, prev_qi, sts)
            held = nxt
        return held, prev_qi, sts

    start = (jnp.int32(0), jnp.int32(0))
    init = (jnp.full((1, bq), NEG_BIG, F32), jnp.ones((LANES, bq), F32))
    scores_into(sa_scr, start)
    ca = lax.fori_loop(0, n_items // unroll, lambda u, ca: run(unroll, ca),
                       (start, jnp.int32(0), (init, init)))
    _, last_qi, sts = run(n_items % unroll, ca)
    write_out(last_qi, sts)


def _fox_call(q, k, v, ff, fbias, l, batch, seq, bq, bk):
    m = q.shape[0]
    col = pl.BlockSpec((seq, LANES), lambda b, p: (b, p))
    return pl.pallas_call(
        functools.partial(_fox_kernel, bq=bq, bk=bk, seq=seq),
        grid=(batch, FOX_WIDTH // LANES),
        in_specs=[col, col, col, pl.BlockSpec((seq, LANES), lambda b, p: (b, 0)), _layer(l, 1, LANES)],
        out_specs=pl.BlockSpec((seq, LANES), lambda b, p: (b, p)),
        out_shape=jax.ShapeDtypeStruct((m, FOX_WIDTH), BF16),
        scratch_shapes=[pltpu.VMEM((seq, LANES), F32),
                        pltpu.VMEM((2, seq, LANES), BF16),
                        pltpu.VMEM((2, seq, LANES), BF16),
                        pltpu.VMEM((2, seq // bk, LANES, bk), BF16),
                        pltpu.VMEM((2, bk, bq), F32),
                        pltpu.VMEM((2, bk, bq), F32),
                        pltpu.VMEM((bk // bq + 1, bk, bq), F32)],
        compiler_params=pltpu.CompilerParams(
            dimension_semantics=("arbitrary", "arbitrary"), vmem_limit_bytes=VMEM_LIMIT),
        name="forgetting",
    )(q, k, v, ff, fbias)


def _mlp_kernel(x_ref, ohg_ref, osb_ref, ofx_ref, wo1_ref, wo2_ref, wo3_ref, g2_ref,
                w1_ref, w2_ref, o_ref, *, tf):
    x1 = (x_ref[...] + _dot(ohg_ref[...], wo1_ref[...]) + _dot(osb_ref[...], wo2_ref[...])
          + _dot(ofx_ref[...], wo3_ref[...]))
    ms = jnp.mean(x1 * x1, axis=-1, keepdims=True)
    h2 = (x1 * lax.rsqrt(ms + EPS) * g2_ref[...]).astype(BF16)
    o_ref[...] = x1
    acc = None
    for f in range(w1_ref.shape[1] // tf):
        a = _dot(h2, w1_ref[:, f * tf:(f + 1) * tf])
        a = jnp.square(jnp.maximum(a, 0.0)).astype(BF16)
        y = _dot(a, w2_ref[f * tf:(f + 1) * tf, :])
        acc = y if acc is None else acc + y
    o_ref[...] = o_ref[...] + acc


def _mlp_call(x, ohg, osb, ofx, wo1, wo2, wo3, g2, w1, w2, l, tm, tf):
    m, d = x.shape
    row = lambda w: pl.BlockSpec((tm, w), lambda i: (i, 0))
    whole = lambda a: _layer(l, *a.shape[1:])
    return pl.pallas_call(
        functools.partial(_mlp_kernel, tf=tf),
        grid=(m // tm,),
        in_specs=[row(d), row(HG_WIDTH), row(SB_WIDTH), row(FOX_WIDTH),
                  whole(wo1), whole(wo2), whole(wo3), whole(g2), whole(w1), whole(w2)],
        out_specs=row(d),
        out_shape=jax.ShapeDtypeStruct((m, d), F32),
        compiler_params=pltpu.CompilerParams(
            dimension_semantics=("arbitrary",), vmem_limit_bytes=VMEM_LIMIT),
        name="outproj_mlp",
    )(x, ohg, osb, ofx, wo1, wo2, wo3, g2, w1, w2)


def _tile_sizes(batch, seq):
    m = batch * seq
    return dict(
        proj_tm=min(512, m),
        hg_tb=min(2048, seq),
        sb_bq=min(256, seq),
        sb_sub=max(1, seq // 256),
        fox_bq=min(256, seq),
        fox_bk=min(512, seq),
        mlp_tm=min(512, m),
        mlp_tf=1024,
    )


def kernel(x, lb_logits, norm1_g, w_in, hg_norm_g, sb_q_norm_g, sb_k_norm_g, fox_q_norm_g,
           fox_k_norm_g, fox_f_bias, w_out, norm2_g, w_ff1, w_ff2):
    batch, seq, d = x.shape
    assert d == D_MODEL and seq % HG_ROWS == 0 and x.dtype == F32
    ts = _tile_sizes(batch, seq)
    assert seq % (ts["sb_bq"] * ts["sb_sub"]) == 0 and seq % ts["fox_bk"] == 0 and ts["fox_bk"] % ts["fox_bq"] == 0
    assert seq % ts["hg_tb"] == 0
    m = batch * seq
    xf = x.reshape(m, d)
    row = lambda a: a.astype(F32)[:, None, :]
    pair = lambda g: row(jnp.tile(g, (1, 2)))
    w_in_p = jnp.pad(w_in, ((0, 0), (0, 0), (0, IN_COLS_PAD - IN_COLS))).astype(BF16)
    wo = w_out.astype(BF16)
    wo_hg, wo_sb, wo_fx = wo[:, :HG_WIDTH], wo[:, HG_WIDTH:HG_WIDTH + SB_WIDTH], wo[:, HG_WIDTH + SB_WIDTH:]
    w1, w2 = w_ff1.astype(BF16), w_ff2.astype(BF16)
    g1, g2, g_hg = row(norm1_g), row(norm2_g), row(jnp.tile(hg_norm_g, (1, HG_HEADS)))
    qk_gains = (pair(sb_q_norm_g), pair(sb_k_norm_g), pair(fox_q_norm_g), pair(fox_k_norm_g))
    fbias = row(jnp.pad(fox_f_bias, ((0, 0), (0, LANES - FOX_HEADS))))
    lbl = lb_logits.astype(F32)

    for l in range(DEPTH):
        hg, ff, sq, sk, sv, fq, fk, fv = _proj_call(xf, g1, w_in_p, lbl, qk_gains, l, ts["proj_tm"])
        o_hg = _hgrn_call(l, lbl, hg, g_hg, batch, seq, ts["hg_tb"])
        o_sb = _sb_call(sq, sk, sv, batch, seq, ts["sb_bq"], ts["sb_sub"])
        o_fx = _fox_call(fq, fk, fv, ff, fbias, l, batch, seq, ts["fox_bq"], ts["fox_bk"])
        xf = _mlp_call(xf, o_hg, o_sb, o_fx, wo_hg, wo_sb, wo_fx, g2, w1, w2, l,
                       ts["mlp_tm"], ts["mlp_tf"])
    return xf.reshape(batch, seq, d)
```

```python
import functools
import math

import jax
import jax.numpy as jnp
from jax import lax
from jax.experimental import pallas as pl
from jax.experimental.pallas import tpu as pltpu

F32 = jnp.float32
BF16 = jnp.bfloat16

D_MODEL = 1024
DEPTH = 4
HEAD_DIM = 64
HEAD_SHIFT = HEAD_DIM.bit_length() - 1
HG_HEADS = 4
HG_KW = 256
HG_WIDTH = 256
SB_WIDTH = 384
FOX_WIDTH = 384
FOX_HEADS = 6
IN_COLS = 4 * 256 + 3 * 384 + 3 * 384 + FOX_HEADS
D_FF = 4 * D_MODEL
EPS = 1e-6
LB_FLOOR = 1e-30
NEG_BIG = -1e30
LOG2E = math.log2(math.e)

LANES = 128
IN_COLS_PAD = 27 * LANES
SB_Q_BLK, SB_K_BLK, SB_V_BLK = 8, 11, 14
FOX_Q_BLK, FOX_K_BLK, FOX_V_BLK, FOX_F_BLK = 17, 20, 23, 26

SB_ZERO_LOG = -105.0
SUB = 16
HG_ROWS = 256
FOX_KEYS_PER_TRIP = 18432
VMEM_LIMIT = 56 * 1024 * 1024


def _nt_dot(a, b):
    return lax.dot_general(a, b, (((1,), (1,)), ((), ())), preferred_element_type=F32)


def _dot(a, b):
    return jnp.dot(a, b, preferred_element_type=F32)


def _split2(x):
    hi = x.astype(BF16)
    lo = (x - hi.astype(F32)).astype(BF16)
    return hi, lo


def _split3(x):
    hi = x.astype(BF16)
    r1 = x - hi.astype(F32)
    mid = r1.astype(BF16)
    lo = (r1 - mid.astype(F32)).astype(BF16)
    return hi, mid, lo


def _softplus(z):
    return jnp.maximum(z, 0.0) + jnp.log(1.0 + jnp.exp(-jnp.abs(z)))


def _softplus2(z2):
    return jnp.maximum(z2, 0.0) + jnp.log2(1.0 + jnp.exp2(-jnp.abs(z2)))


def _pair_rms(x, gain, lane_lo):
    x2 = x * x
    s0 = jnp.sum(jnp.where(lane_lo, x2, 0.0), axis=1, keepdims=True)
    s1 = jnp.sum(jnp.where(lane_lo, 0.0, x2), axis=1, keepdims=True)
    ms = jnp.where(lane_lo, s0, s1) * (1.0 / HEAD_DIM)
    return x * lax.rsqrt(ms + EPS) * gain


def _layer(l, *shape):
    return pl.BlockSpec((None,) + shape, lambda *_: (l,) + (0,) * len(shape))


def _lower_bound(lbl_ref, layer):
    rows = [lbl_ref[j:j + 1, :] for j in range(DEPTH)]
    mx = functools.reduce(jnp.maximum, rows)
    ex = [jnp.exp(r - mx) for r in rows]
    den = functools.reduce(lambda a, b: a + b, ex)
    lb = jnp.zeros_like(mx)
    for j in range(1, layer + 1):
        lb = lb + ex[j] / den
    return lb


def _proj_kernel(x_ref, g_ref, w_ref, lbl_ref, gsq_ref, gsk_ref, gfq_ref, gfk_ref,
                 hg_ref, ff_ref, sq_ref, sk_ref, sv_ref, fq_ref, fk_ref, fv_ref, *, layer):
    x = x_ref[...]
    ms = jnp.mean(x * x, axis=-1, keepdims=True)
    h = (x * lax.rsqrt(ms + EPS) * g_ref[...]).astype(BF16)
    y = _dot(h, w_ref[...])
    fl = y[:, HG_KW:2 * HG_KW]
    gt = y[:, 3 * HG_KW:4 * HG_KW]
    log_lb = jnp.log(jnp.maximum(_lower_bound(lbl_ref, layer), LB_FLOOR))
    sp = _softplus(fl)
    hg_ref[:, :HG_KW] = y[:, :HG_KW]
    hg_ref[:, HG_KW:2 * HG_KW] = sp
    hg_ref[:, 2 * HG_KW:3 * HG_KW] = y[:, 2 * HG_KW:3 * HG_KW]
    hg_ref[:, 3 * HG_KW:4 * HG_KW] = gt / (1.0 + jnp.exp(-gt))
    hg_ref[:, 4 * HG_KW:] = (jnp.maximum(fl, log_lb) + jnp.log(1.0 + jnp.exp(-jnp.abs(fl - log_lb)))
                             - sp)
    ff_ref[...] = y[:, FOX_F_BLK * LANES:]
    lane_lo = lax.broadcasted_iota(jnp.int32, (x.shape[0], LANES), 1) < HEAD_DIM
    q_mult = LOG2E / math.sqrt(HEAD_DIM)
    groups = ((SB_Q_BLK, gsq_ref, q_mult, sq_ref), (SB_K_BLK, gsk_ref, None, sk_ref),
              (SB_V_BLK, None, None, sv_ref),
              (FOX_Q_BLK, gfq_ref, q_mult, fq_ref), (FOX_K_BLK, gfk_ref, None, fk_ref),
              (FOX_V_BLK, None, None, fv_ref))
    for blk, gain_ref, mult, out_ref in groups:
        for p in range(SB_WIDTH // LANES):
            t = y[:, (blk + p) * LANES:(blk + p + 1) * LANES]
            if gain_ref is not None:
                t = _pair_rms(t, gain_ref[...], lane_lo)
            if mult is not None:
                t = t * mult
            out_ref[:, p * LANES:(p + 1) * LANES] = t.astype(BF16)


def _proj_call(x, g, w, lbl, gains, l, tm):
    m, d = x.shape
    n = w.shape[2]
    rows = lambda width: pl.BlockSpec((tm, width), lambda i: (i, 0))
    widths = (5 * HG_KW, LANES) + (SB_WIDTH,) * 6
    dtypes = (F32, F32) + (BF16,) * 6
    return pl.pallas_call(
        functools.partial(_proj_kernel, layer=l),
        grid=(m // tm,),
        in_specs=([rows(d), _layer(l, 1, d), _layer(l, d, n),
                   pl.BlockSpec((DEPTH, HG_KW), lambda i: (0, 0))] + [_layer(l, 1, LANES)] * 4),
        out_specs=[rows(wd) for wd in widths],
        out_shape=[jax.ShapeDtypeStruct((m, wd), dt) for wd, dt in zip(widths, dtypes)],
        compiler_params=pltpu.CompilerParams(
            dimension_semantics=("arbitrary",), vmem_limit_bytes=VMEM_LIMIT),
        name="proj",
    )(x, g, w, lbl, *gains)


def _hgrn_kernel(lbl_ref, q_ref, sp_ref, i_ref, gate_ref, lf_ref, gn_ref, o_ref, st_scr, oi_scr,
                 *, layer, tb):
    @pl.when(pl.program_id(1) == 0)
    def _():
        st_scr[...] = jnp.zeros_like(st_scr)

    one_m_lb = 1.0 - _lower_bound(lbl_ref, layer)
    gn = gn_ref[...]

    r_i = lax.broadcasted_iota(jnp.int32, (HG_KW, HG_KW), 0)
    c_i = lax.broadcasted_iota(jnp.int32, (HG_KW, HG_KW), 1)
    same_head = (r_i >> HEAD_SHIFT) == (c_i >> HEAD_SHIFT)
    ones_bd = jnp.where(same_head, 1.0, 0.0).astype(BF16)
    t_mod = lax.broadcasted_iota(jnp.int32, (HG_ROWS, HG_KW), 0) & (SUB - 1)
    lane_sub = lax.broadcasted_iota(jnp.int32, (HG_KW, HG_ROWS), 1) // SUB
    t_r = lax.broadcasted_iota(jnp.int32, (HG_ROWS, HG_ROWS), 0)
    t_c = lax.broadcasted_iota(jnp.int32, (HG_ROWS, HG_ROWS), 1)
    in_sub = (t_r // SUB) == (t_c // SUB)
    same_sub = jnp.where(in_sub, 1.0, 0.0).astype(BF16)
    upto_sub = jnp.where(jnp.logical_and(in_sub, t_c <= t_r), 1.0, 0.0).astype(BF16)

    def chunk(c, carry):
        rs = pl.ds(pl.multiple_of(c * HG_ROWS, HG_ROWS), HG_ROWS)
        q = q_ref[rs, :]
        sp = sp_ref[rs, :]
        v = i_ref[rs, :]
        lf = lf_ref[rs, :]
        kk = one_m_lb * jnp.exp(-sp)

        parts = _split3(lf)
        b = functools.reduce(lambda x, y: x + y, [_dot(upto_sub, part) for part in parts])
        beta = functools.reduce(lambda x, y: x + y, [_dot(same_sub, part) for part in parts])
        rest = beta - b

        b2 = b * LOG2E
        w2 = -sp * LOG2E - b2
        q_lb = q * one_m_lb
        acc = _dot((q * kk).astype(BF16), ones_bd) * v
        for d in range(1, SUB):
            e = jnp.where(t_mod >= d, b2 + pltpu.roll(w2, d, 0), NEG_BIG)
            p = q_lb * jnp.exp2(e)
            acc = acc + _dot(p.astype(BF16), ones_bd) * pltpu.roll(v, d, 0)

        qt = (q * jnp.exp2(b2)).astype(BF16)
        kt = (kk * jnp.exp(rest)).astype(BF16)
        v_t = v.T
        for j in range(HG_ROWS // SUB):
            st = st_scr[...]
            oi_scr[j * SUB:(j + 1) * SUB, :] = _nt_dot(qt[j * SUB:(j + 1) * SUB, :], st.astype(BF16))
            vm = jnp.where(lane_sub == j, v_t, 0.0).astype(BF16)
            ut = _dot(vm, kt)
            dec = jnp.exp(beta[j * SUB:j * SUB + 1, :])
            st_scr[...] = st * dec + jnp.where(same_head, ut, 0.0)

        o = acc + oi_scr[...]
        hi, lo = _split2(o * o)
        msq = (_dot(hi, ones_bd) + _dot(lo, ones_bd)) * (1.0 / HEAD_DIM)
        y = o * lax.rsqrt(msq + EPS) * gn
        o_ref[rs, :] = (y * gate_ref[rs, :]).astype(o_ref.dtype)
        return carry

    lax.fori_loop(0, tb // HG_ROWS, chunk, 0)


def _hgrn_call(layer, lb_logits, proj, gn, batch, seq, tb):
    m = proj.shape[0]
    nt = seq // tb

    def col(j):
        return pl.BlockSpec((tb, HG_KW), lambda b, t, j=j: (b * nt + t, j))

    return pl.pallas_call(
        functools.partial(_hgrn_kernel, layer=layer, tb=tb),
        grid=(batch, nt),
        in_specs=[pl.BlockSpec((DEPTH, HG_KW), lambda b, t: (0, 0)),
                  col(0), col(1), col(2), col(3), col(4), _layer(layer, 1, HG_WIDTH)],
        out_specs=pl.BlockSpec((tb, HG_WIDTH), lambda b, t: (b * nt + t, 0)),
        out_shape=jax.ShapeDtypeStruct((m, HG_WIDTH), BF16),
        scratch_shapes=[pltpu.VMEM((HG_WIDTH, HG_KW), F32),
                        pltpu.VMEM((HG_ROWS, HG_WIDTH), F32)],
        compiler_params=pltpu.CompilerParams(
            dimension_semantics=("arbitrary", "arbitrary"), vmem_limit_bytes=VMEM_LIMIT),
        name="hgrn2",
    )(lb_logits, proj, proj, proj, proj, proj, gn)


def _sb_kernel(q0_ref, q1_ref, q2_ref, k0_ref, k1_ref, k2_ref, v0_ref, v1_ref, v2_ref,
               o_ref, acc_scr, car_scr, *, bq, n_sub):
    g = pl.program_id(1)
    q_refs = (q0_ref, q1_ref, q2_ref)
    k_refs = (k0_ref, k1_ref, k2_ref)
    v_refs = (v0_ref, v1_ref, v2_ref)
    n_pairs = len(q_refs)

    lane_lo = lax.broadcasted_iota(jnp.int32, (bq, LANES), 1) < HEAD_DIM
    row_lo = lax.broadcasted_iota(jnp.int32, (LANES, bq), 0) < HEAD_DIM
    k_i = lax.broadcasted_iota(jnp.int32, (bq, bq), 0)
    q_i = lax.broadcasted_iota(jnp.int32, (bq, bq), 1)
    before = k_i < q_i
    later = jnp.where(q_i > k_i, 1.0, 0.0).astype(BF16)

    heads = range(2 * n_pairs)

    def q_block(sub, carry_unused):
        i = g * n_sub + sub
        qs = pl.ds(pl.multiple_of(sub * bq, bq), bq)
        qms = []
        for p in range(n_pairs):
            qn = q_refs[p][qs, :]
            zero = jnp.zeros_like(qn)
            qms.append(jnp.where(lane_lo, qn, zero))
            qms.append(jnp.where(lane_lo, zero, qn))

        def absorb(tiles, fresh):
            rows = [pl.ds(pl.multiple_of(kb * bq, bq), bq) for kb, _ in tiles]
            masks = [mask for _, mask in tiles]
            nt = range(len(tiles))
            zs = [[_nt_dot(k_refs[h // 2][rows[t], :], qms[h]) for h in heads] for t in nt]
            vts = [[v_refs[p][rows[t], :].T for p in range(n_pairs)] for t in nt]
            sps = [[_softplus2(z) for z in zt] for zt in zs]
            us = [spt if masks[t] is None else [jnp.where(masks[t], sp, 0.0) for sp in spt]
                  for t, spt in enumerate(sps)]
            tails = [[_dot(later, u.astype(BF16)) for u in ut] for ut in us]
            low = None
            for h in heads:
                carry = None if fresh else car_scr[h]
                acc = None if fresh else acc_scr[h]
                for t in nt:
                    row = jnp.sum(us[t][h], axis=0, keepdims=True)
                    e = zs[t][h] - sps[t][h] - tails[t][h]
                    if carry is not None:
                        e = e - carry
                        row = row + carry
                    a = jnp.exp2(e)
                    if masks[t] is not None:
                        a = jnp.where(masks[t], a, 0.0)
                    pv = _dot(vts[t][h // 2], a.astype(BF16))
                    acc = pv if acc is None else acc + pv
                    carry = row
                acc_scr[h] = acc
                car_scr[h] = carry
                low = carry if low is None else jnp.minimum(low, carry)
            return jnp.min(low)

        low = lax.cond(i > 0,
                       lambda: absorb([(i, before), (i - 1, None)], True),
                       lambda: absorb([(i, before)], True))

        def live(st):
            return jnp.logical_and(st[0] >= 0, st[1] < -SB_ZERO_LOG * LOG2E)

        lax.while_loop(live, lambda st: (st[0] - 1, absorb([(st[0], None)], False)), (i - 2, low))
        for p in range(n_pairs):
            o_t = jnp.where(row_lo, acc_scr[2 * p], acc_scr[2 * p + 1])
            o_ref[qs, p * LANES:(p + 1) * LANES] = o_t.T.astype(o_ref.dtype)
        return carry_unused

    lax.fori_loop(0, n_sub, q_block, 0)


def _sb_call(q, k, v, batch, seq, bq, n_sub):
    m = q.shape[0]
    rows = bq * n_sub
    nq = seq // rows
    n_pairs = SB_WIDTH // LANES
    q_spec = lambda p: pl.BlockSpec((rows, LANES), lambda b, i, p=p: (b * nq + i, p))
    k_spec = lambda p: pl.BlockSpec((seq, LANES), lambda b, i, p=p: (b, p))
    pairs = range(n_pairs)
    return pl.pallas_call(
        functools.partial(_sb_kernel, bq=bq, n_sub=n_sub),
        grid=(batch, nq),
        in_specs=[q_spec(p) for p in pairs] + [k_spec(p) for p in pairs] * 2,
        out_specs=pl.BlockSpec((rows, SB_WIDTH), lambda b, i: (b * nq + i, 0)),
        out_shape=jax.ShapeDtypeStruct((m, SB_WIDTH), BF16),
        scratch_shapes=[pltpu.VMEM((2 * n_pairs, LANES, bq), F32),
                        pltpu.VMEM((2 * n_pairs, 1, bq), F32)],
        compiler_params=pltpu.CompilerParams(
            dimension_semantics=("arbitrary", "arbitrary"), vmem_limit_bytes=VMEM_LIMIT),
        name="stickbreak",
    )(*([q] * n_pairs + [k] * n_pairs + [v] * n_pairs))


def _fox_kernel(q_ref, k_ref, v_ref, f_ref, fb_ref, o_ref,
                c_scr, qa_scr, ka_scr, vt_scr, sa_scr, sb_scr, mb_scr, *, bq, bk, seq):
    p_idx = pl.program_id(1)

    @pl.when(p_idx == 0)
    def _():
        cb = min(bk, 2 * LANES)
        r_i = lax.broadcasted_iota(jnp.int32, (cb, cb), 0)
        c_i = lax.broadcasted_iota(jnp.int32, (cb, cb), 1)
        upto = jnp.where(c_i <= r_i, 1.0, 0.0).astype(BF16)

        per_trip = max(u for u in (8, 4, 2, 1) if (seq // cb) % u == 0)

        def body(n, run):
            for j in range(per_trip):
                rs = pl.ds(pl.multiple_of((n * per_trip + j) * cb, cb), cb)
                y = f_ref[rs, :] + fb_ref[...]
                lg = jnp.minimum(y, 0.0) - jnp.log(1.0 + jnp.exp(-jnp.abs(y)))
                hi, mid, lo = _split3(lg)
                c = _dot(upto, hi) + _dot(upto, mid) + _dot(upto, lo) + run
                c_scr[rs, :] = c
                run = c[cb - 1:cb, :]
            return run

        lax.fori_loop(0, seq // (cb * per_trip), body, jnp.zeros((1, LANES), F32))

    lane = lax.broadcasted_iota(jnp.int32, (bk, LANES), 1)
    lane_lo = lane < HEAD_DIM
    row_lo_k = lax.broadcasted_iota(jnp.int32, (LANES, bk), 0) < HEAD_DIM

    pr = lax.broadcasted_iota(jnp.int32, (LANES, LANES), 0)
    pc = lax.broadcasted_iota(jnp.int32, (LANES, LANES), 1)
    place, ones_q, ones_k = [], [], []
    for h in range(2):
        base = HEAD_DIM * (1 - h)
        place.append([jnp.where(jnp.logical_and(pr == 2 * p_idx + h,
                                                jnp.logical_or(pc == base + n, pc == base + 3 + n)),
                                1.0, 0.0).astype(BF16) for n in range(3)])
        ones_q.append(jnp.logical_and(lane >= base + 3, lane < base + 6))
        ones_k.append(jnp.logical_and(lane >= base, lane < base + 3))

    def prep(n, carry):
        rs = pl.ds(pl.multiple_of(n * bk, bk), bk)
        kn = k_ref[rs, :]
        qn = q_ref[rs, :]
        parts = _split3(c_scr[rs, :] * LOG2E)
        v_t = v_ref[rs, :].T
        for h in range(2):
            c_aug = functools.reduce(lambda x, y: x + y,
                                     [_dot(part, sel) for part, sel in zip(parts, place[h])])
            mine = lane_lo if h == 0 else jnp.logical_not(lane_lo)
            k_aug = jnp.where(ones_k[h], 1.0, -c_aug)
            q_aug = jnp.where(ones_q[h], 1.0, c_aug)
            ka_scr[h, rs, :] = jnp.where(mine, kn, k_aug.astype(BF16))
            qa_scr[h, rs, :] = jnp.where(mine, qn, q_aug.astype(BF16))
            vt_scr[h, n] = jnp.where(row_lo_k if h == 0 else jnp.logical_not(row_lo_k), v_t,
                                     jnp.ones_like(v_t))
        return carry

    chunks = max(u for u in (8, 4, 2, 1) if (seq // bk) % u == 0)

    def prep_trip(t, carry):
        for j in range(chunks):
            prep(t * chunks + j, carry)
        return carry

    lax.fori_loop(0, seq // (bk * chunks), prep_trip, 0)

    key_minus_qry = (lax.broadcasted_iota(jnp.int32, (bk, bq), 0)
                     - lax.broadcasted_iota(jnp.int32, (bk, bq), 1))
    row_lo = lax.broadcasted_iota(jnp.int32, (LANES, bq), 0) < HEAD_DIM

    nq = seq // bq
    n_items = sum((qi * bq + bq - 1) // bk + 1 for qi in range(nq))
    mb_scr[0] = jnp.zeros((bk, bq), F32)
    for r in range(bk // bq):
        mb_scr[r + 1] = jnp.where(key_minus_qry <= r * bq, 0.0, NEG_BIG)

    def unpack(item):
        qi = jnp.minimum(item[0], nq - 1)
        last = (qi * bq + bq - 1) // bk
        first = item[1] == 0
        kb = jnp.where(first, last, item[1] - 1)
        return qi, last, first, kb

    def advance(item):
        qi, last, _, _ = unpack(item)
        wrap = item[1] >= last
        return jnp.where(wrap, item[0] + 1, item[0]), jnp.where(wrap, 0, item[1] + 1)

    def scores_into(dst, item):
        qi, _, first, kb = unpack(item)
        qs = pl.ds(pl.multiple_of(qi * bq, bq), bq)
        rs = pl.ds(pl.multiple_of(kb * bk, bk), bk)
        bias = mb_scr[jnp.where(first, 1 + (qi * bq - kb * bk) // bq, 0)]
        for h in range(2):
            dst[h] = _nt_dot(ka_scr[h, rs, :], qa_scr[h, qs, :]) + bias

    def write_out(qi, sts):
        acc0, acc1 = sts[0][1], sts[1][1]
        o_t = jnp.where(row_lo, acc0 / acc0[HEAD_DIM:HEAD_DIM + 1, :], acc1 / acc1[0:1, :])
        o_ref[pl.ds(pl.multiple_of(qi * bq, bq), bq), :] = o_t.T.astype(o_ref.dtype)

    def absorb(src, item, prev_qi, sts):
        qi, _, first, kb = unpack(item)
        write_out(prev_qi, sts)
        out = []
        for h in range(2):
            m_run, acc = sts[h]
            m_run = jnp.where(first, NEG_BIG, m_run)
            s = src[h]
            m_new = jnp.maximum(m_run, jnp.max(s, axis=0, keepdims=True))
            p = jnp.exp2(s - m_new)
            acc = jnp.exp2(m_run - m_new) * acc + _dot(vt_scr[h, kb], p.astype(BF16))
            out.append((m_new, acc))
        return qi, tuple(out)

    unroll = max([u for u in (36, 24, 12, 8, 4) if u <= n_items and u * bk <= FOX_KEYS_PER_TRIP] + [2])
    bufs = (sa_scr, sb_scr)

    def run(count, ca):
        held, prev_qi, sts = ca
        for k in range(count):
            nxt = advance(held)
            scores_into(bufs[(k + 1) % 2], nxt)
            prev_qi, sts = absorb(bufs[k % 2], held, prev_qi, sts)
            held = nxt
        return held, prev_qi, sts

    start = (jnp.int32(0), jnp.int32(0))
    init = (jnp.full((1, bq), NEG_BIG, F32), jnp.ones((LANES, bq), F32))
    scores_into(sa_scr, start)
    ca = lax.fori_loop(0, n_items // unroll, lambda u, ca: run(unroll, ca),
                       (start, jnp.int32(0), (init, init)))
    _, last_qi, sts = run(n_items % unroll, ca)
    write_out(last_qi, sts)


def _fox_call(q, k, v, ff, fbias, l, batch, seq, bq, bk):
    m = q.shape[0]
    col = pl.BlockSpec((seq, LANES), lambda b, p: (b, p))
    return pl.pallas_call(
        functools.partial(_fox_kernel, bq=bq, bk=bk, seq=seq),
        grid=(batch, FOX_WIDTH // LANES),
        in_specs=[col, col, col, pl.BlockSpec((seq, LANES), lambda b, p: (b, 0)), _layer(l, 1, LANES)],
        out_specs=pl.BlockSpec((seq, LANES), lambda b, p: (b, p)),
        out_shape=jax.ShapeDtypeStruct((m, FOX_WIDTH), BF16),
        scratch_shapes=[pltpu.VMEM((seq, LANES), F32),
                        pltpu.VMEM((2, seq, LANES), BF16),
                        pltpu.VMEM((2, seq, LANES), BF16),
                        pltpu.VMEM((2, seq // bk, LANES, bk), BF16),
                        pltpu.VMEM((2, bk, bq), F32),
                        pltpu.VMEM((2, bk, bq), F32),
                        pltpu.VMEM((bk // bq + 1, bk, bq), F32)],
        compiler_params=pltpu.CompilerParams(
            dimension_semantics=("arbitrary", "arbitrary"), vmem_limit_bytes=VMEM_LIMIT),
        name="forgetting",
    )(q, k, v, ff, fbias)


def _mlp_kernel(x_ref, ohg_ref, osb_ref, ofx_ref, wo1_ref, wo2_ref, wo3_ref, g2_ref,
                w1_ref, w2_ref, o_ref, *, tf):
    x1 = (x_ref[...] + _dot(ohg_ref[...], wo1_ref[...]) + _dot(osb_ref[...], wo2_ref[...])
          + _dot(ofx_ref[...], wo3_ref[...]))
    ms = jnp.mean(x1 * x1, axis=-1, keepdims=True)
    h2 = (x1 * lax.rsqrt(ms + EPS) * g2_ref[...]).astype(BF16)
    o_ref[...] = x1
    acc = None
    for f in range(w1_ref.shape[1] // tf):
        a = _dot(h2, w1_ref[:, f * tf:(f + 1) * tf])
        a = jnp.square(jnp.maximum(a, 0.0)).astype(BF16)
        y = _dot(a, w2_ref[f * tf:(f + 1) * tf, :])
        acc = y if acc is None else acc + y
    o_ref[...] = o_ref[...] + acc


def _mlp_call(x, ohg, osb, ofx, wo1, wo2, wo3, g2, w1, w2, l, tm, tf):
    m, d = x.shape
    row = lambda w: pl.BlockSpec((tm, w), lambda i: (i, 0))
    whole = lambda a: _layer(l, *a.shape[1:])
    return pl.pallas_call(
        functools.partial(_mlp_kernel, tf=tf),
        grid=(m // tm,),
        in_specs=[row(d), row(HG_WIDTH), row(SB_WIDTH), row(FOX_WIDTH),
                  whole(wo1), whole(wo2), whole(wo3), whole(g2), whole(w1), whole(w2)],
        out_specs=row(d),
        out_shape=jax.ShapeDtypeStruct((m, d), F32),
        compiler_params=pltpu.CompilerParams(
            dimension_semantics=("arbitrary",), vmem_limit_bytes=VMEM_LIMIT),
        name="outproj_mlp",
    )(x, ohg, osb, ofx, wo1, wo2, wo3, g2, w1, w2)


def _tile_sizes(batch, seq):
    m = batch * seq
    return dict(
        proj_tm=min(512, m),
        hg_tb=min(2048, seq),
        sb_bq=min(256, seq),
        sb_sub=max(1, seq // 256),
        fox_bq=min(256, seq),
        fox_bk=min(512, seq),
        mlp_tm=min(512, m),
        mlp_tf=1024,
    )


def kernel(x, lb_logits, norm1_g, w_in, hg_norm_g, sb_q_norm_g, sb_k_norm_g, fox_q_norm_g,
           fox_k_norm_g, fox_f_bias, w_out, norm2_g, w_ff1, w_ff2):
    batch, seq, d = x.shape
    assert d == D_MODEL and seq % HG_ROWS == 0 and x.dtype == F32
    ts = _tile_sizes(batch, seq)
    assert seq % (ts["sb_bq"] * ts["sb_sub"]) == 0 and seq % ts["fox_bk"] == 0 and ts["fox_bk"] % ts["fox_bq"] == 0
    assert seq % ts["hg_tb"] == 0
    m = batch * seq
    xf = x.reshape(m, d)
    row = lambda a: a.astype(F32)[:, None, :]
    pair = lambda g: row(jnp.tile(g, (1, 2)))
    w_in_p = jnp.pad(w_in, ((0, 0), (0, 0), (0, IN_COLS_PAD - IN_COLS))).astype(BF16)
    wo = w_out.astype(BF16)
    wo_hg, wo_sb, wo_fx = wo[:, :HG_WIDTH], wo[:, HG_WIDTH:HG_WIDTH + SB_WIDTH], wo[:, HG_WIDTH + SB_WIDTH:]
    w1, w2 = w_ff1.astype(BF16), w_ff2.astype(BF16)
    g1, g2, g_hg = row(norm1_g), row(norm2_g), row(jnp.tile(hg_norm_g, (1, HG_HEADS)))
    qk_gains = (pair(sb_q_norm_g), pair(sb_k_norm_g), pair(fox_q_norm_g), pair(fox_k_norm_g))
    fbias = row(jnp.pad(fox_f_bias, ((0, 0), (0, LANES - FOX_HEADS))))
    lbl = lb_logits.astype(F32)

    for l in range(DEPTH):
        hg, ff, sq, sk, sv, fq, fk, fv = _proj_call(xf, g1, w_in_p, lbl, qk_gains, l, ts["proj_tm"])
        o_hg = _hgrn_call(l, lbl, hg, g_hg, batch, seq, ts["hg_tb"])
        o_sb = _sb_call(sq, sk, sv, batch, seq, ts["sb_bq"], ts["sb_sub"])
        o_fx = _fox_call(fq, fk, fv, ff, fbias, l, batch, seq, ts["fox_bq"], ts["fox_bk"])
        xf = _mlp_call(xf, o_hg, o_sb, o_fx, wo_hg, wo_sb, wo_fx, g2, w1, w2, l,
                       ts["mlp_tm"], ts["mlp_tf"])
    return xf.reshape(batch, seq, d)
```
